```python
import math
import jax
import jax.numpy as jnp
from jax import lax
import numpy as np

D_MODEL = 2048
BATCH = 2
SEQ = 4096
DEPTH = 4

GRID_W = 64
CTX_LEN = 256
EPS = 1e-6
ROPE_BASE = 10000.0
BLOCK = 128

GROUP_WIDTH = D_MODEL // 4
MIX_WIDTH = 4 * GROUP_WIDTH
N_MOD = 6

A_HEADS = 8
A_KV_HEADS = 2
A_HEAD_DIM = GROUP_WIDTH // A_HEADS
WINDOW = 128

HY_WIDTH = GROUP_WIDTH
HY_ORDER = 2
HY_SHORT = 3
HY_POS_FREQS = 8
HY_POS_DIM = 1 + 2 * HY_POS_FREQS
HY_FILTER_HIDDEN = 64
HY_DECAY_TARGET = 1e-2
HY_FAST_DECAY = 0.3
HY_SLOW_DECAY = 1.5

ML_HEADS = 4
ML_HEAD_DIM = GROUP_WIDTH // ML_HEADS
ML_CHUNK = 128

MLA_HEADS = 8
MLA_Q_RANK = D_MODEL // 4
MLA_KV_RANK = D_MODEL // 8
MLA_NOPE = 64
MLA_ROPE = 32
MLA_V = GROUP_WIDTH // MLA_HEADS

FFN_DIM = 5632
N_EXPERTS = 8
TOP_K = 2
EXPERT_DIM = FFN_DIM // TOP_K

IN_SPLITS = (A_HEADS * A_HEAD_DIM, A_KV_HEADS * A_HEAD_DIM, A_KV_HEADS * A_HEAD_DIM,
             (HY_ORDER + 1) * HY_WIDTH,
             GROUP_WIDTH, GROUP_WIDTH, GROUP_WIDTH, GROUP_WIDTH, 4 * ML_HEADS,
             MLA_Q_RANK, MLA_KV_RANK, MLA_ROPE)
IN_WIDTH = sum(IN_SPLITS)
N_DENSE = (DEPTH + 1) // 2
N_MOE = DEPTH // 2

kernel_name = 'hybrid_parallel_head_diffusion_trunk'


def rmsnorm(x, g):
    xf = x.astype(jnp.float32)
    y = xf * lax.rsqrt(jnp.mean(xf * xf, axis=-1, keepdims=True) + EPS)
    return (y * g.astype(jnp.float32)).astype(x.dtype)


def modulate(h, shift, scale):
    return h * (1 + scale) + shift


def split_cols(p):
    return jnp.split(p, np.cumsum(IN_SPLITS)[:-1].tolist(), axis=-1)


def merge_heads(o):
    b, h, n, d = o.shape
    return o.transpose(0, 2, 1, 3).reshape(b, n, h * d)


def swiglu(h, w_gate, w_up, w_down):
    return (jax.nn.silu(h @ w_gate) * (h @ w_up)) @ w_down


def axial_angles(rows, rot_dim):
    r = jnp.repeat(jnp.arange(rows, dtype=jnp.float32), GRID_W)
    col = jnp.tile(jnp.arange(GRID_W, dtype=jnp.float32), rows)
    half = rot_dim // 2
    inv = ROPE_BASE ** (-jnp.arange(0, half, 2, dtype=jnp.float32) / half)
    return r[:, None] * inv, col[:, None] * inv


def _rotate(x, ang):
    x1, x2 = jnp.split(x, 2, axis=-1)
    cos, sin = jnp.cos(ang), jnp.sin(ang)
    return jnp.concatenate([x1 * cos - x2 * sin, x2 * cos + x1 * sin], axis=-1)


def apply_axial_rope(x, ang):
    ang_r, ang_c = ang
    xf = x.astype(jnp.float32)
    half = xf.shape[-1] // 2
    out = jnp.concatenate([_rotate(xf[..., :half], ang_r), _rotate(xf[..., half:], ang_c)], axis=-1)
    return out.astype(x.dtype)


def dense_attend(q, k, v, scale):
    s = jnp.einsum('bhqd,bhkd->bhqk', q, k).astype(jnp.float32) * scale
    return jnp.einsum('bhqk,bhkd->bhqd', jax.nn.softmax(s, axis=-1).astype(v.dtype), v)


def window_gqa(q_l, k_l, v_l, q_c, k_c, v_c, sink, ang, with_ctx_out):
    groups = A_HEADS // A_KV_HEADS
    scale = A_HEAD_DIM ** -0.5

    def heads(t, nh):
        b_, n_, _ = t.shape
        return t.reshape(b_, n_, nh, A_HEAD_DIM).transpose(0, 2, 1, 3)

    q = apply_axial_rope(heads(q_l, A_HEADS), ang)
    k = apply_axial_rope(heads(k_l, A_KV_HEADS), ang)
    v = heads(v_l, A_KV_HEADS)
    kc, vc = heads(k_c, A_KV_HEADS), heads(v_c, A_KV_HEADS)
    b, _, n, d = q.shape
    nb = n // BLOCK
    n_ctx = kc.shape[2]
    qb = q.reshape(b, A_KV_HEADS, groups, nb, BLOCK, d)

    def band(t):
        tp = jnp.pad(t, ((0, 0), (0, 0), (BLOCK, BLOCK), (0, 0))).reshape(b, A_KV_HEADS, nb + 2, BLOCK, d)
        return jnp.concatenate([tp[:, :, :-2], tp[:, :, 1:-1], tp[:, :, 2:]], axis=3)

    kb, vb = band(k), band(v)
    qi = jnp.arange(BLOCK)[:, None]
    kj = jnp.arange(3 * BLOCK)[None, :]
    k_abs = jnp.arange(nb)[:, None, None] * BLOCK - BLOCK + kj[None]
    valid = (jnp.abs(kj - BLOCK - qi) <= WINDOW)[None] & (k_abs >= 0) & (k_abs < n)
    s_band = jnp.where(valid, jnp.einsum('bkgnqd,bknjd->bkgnqj', qb, kb).astype(jnp.float32) * scale, -jnp.inf)
    s_ctx = jnp.einsum('bkgnqd,bkcd->bkgnqc', qb, kc).astype(jnp.float32) * scale
    sink_col = jnp.broadcast_to(sink.astype(jnp.float32).reshape(A_KV_HEADS, groups, 1, 1, 1), s_ctx.shape[:-1] + (1,))
    p = jax.nn.softmax(jnp.concatenate([s_band, s_ctx, sink_col], axis=-1), axis=-1).astype(v.dtype)
    o = (jnp.einsum('bkgnqj,bknjd->bkgnqd', p[..., :3 * BLOCK], vb)
         + jnp.einsum('bkgnqc,bkcd->bkgnqd', p[..., 3 * BLOCK:3 * BLOCK + n_ctx], vc))
    out_l = merge_heads(o.reshape(b, A_HEADS, n, d))
    out_c = None
    if with_ctx_out:
        qc = heads(q_c, A_HEADS).reshape(b, A_KV_HEADS, groups, n_ctx, d)
        s = jnp.einsum('bkgqd,bkcd->bkgqc', qc, kc).astype(jnp.float32) * scale
        sink_c = jnp.broadcast_to(sink.astype(jnp.float32).reshape(A_KV_HEADS, groups, 1, 1), s.shape[:-1] + (1,))
        pc = jax.nn.softmax(jnp.concatenate([s, sink_c], axis=-1), axis=-1)[..., :n_ctx].astype(vc.dtype)
        out_c = merge_heads(jnp.einsum('bkgqc,bkcd->bkgqd', pc, vc).reshape(b, A_HEADS, n_ctx, d))
    return out_l, out_c


def hyena_filter_spectrum(L, w1, b1, w2, b2, freq, w3):
    f32 = jnp.float32
    t = jnp.arange(L, dtype=f32) / L
    kf = jnp.arange(1, HY_POS_FREQS + 1, dtype=f32)
    ang = 2.0 * math.pi * t[:, None] * kf
    feats = jnp.concatenate([t[:, None], jnp.sin(ang), jnp.cos(ang)], axis=-1)
    fr = freq.astype(f32)
    hdn = jnp.sin(fr * (feats @ w1.astype(f32) + b1.astype(f32)))
    hdn = jnp.sin(fr * (hdn @ w2.astype(f32) + b2.astype(f32)))
    h = (hdn @ w3.astype(f32)).reshape(L, HY_ORDER, 2, HY_WIDTH)
    deltas = jnp.abs(jnp.linspace(math.log(HY_DECAY_TARGET) / HY_SLOW_DECAY,
                                  math.log(HY_DECAY_TARGET) / HY_FAST_DECAY, HY_WIDTH, dtype=f32))
    h = h * jnp.exp(-t[:, None] * deltas)[:, None, None, :]
    g = jnp.concatenate([h[:, :, 0], jnp.zeros((1, HY_ORDER, HY_WIDTH), f32), h[:0:-1, :, 1]], axis=0)
    g = g / jnp.sum(jnp.abs(g), axis=0, keepdims=True)
    return jnp.fft.rfft(g, axis=0)


def fft_long_conv(z, spec):
    L = z.shape[1]
    zf = jnp.fft.rfft(z.astype(jnp.float32), n=2 * L, axis=1)
    return jnp.fft.irfft(zf * spec, n=2 * L, axis=1)[:, :L].astype(z.dtype)


def hyena_mixer(p_l, p_c, short_w, w1, b1, w2, b2, freq, w3, bias, with_ctx_out):
    def run(p):
        L = p.shape[1]
        pad = HY_SHORT // 2
        pp = jnp.pad(p, ((0, 0), (pad, pad), (0, 0)))
        u = sum(short_w[j] * pp[:, j:j + L] for j in range(HY_SHORT))
        v, *gates = jnp.split(u, HY_ORDER + 1, axis=-1)
        spec = hyena_filter_spectrum(L, w1, b1, w2, b2, freq, w3)
        z = v
        for o_i, xg in enumerate(gates):
            z = xg * (fft_long_conv(z, spec[:, o_i]) + bias[o_i] * z)
        return z
    return run(p_l), (run(p_c) if with_ctx_out else None)


def mlstm_chunkwise(q, k, v, li, lf, state):
    b, nh, L, d = q.shape
    nc = L // ML_CHUNK

    def chunks(t):
        return jnp.moveaxis(t.reshape(b, nh, nc, ML_CHUNK, *t.shape[3:]), 2, 0)

    lower = jnp.tril(jnp.ones((ML_CHUNK, ML_CHUNK), dtype=bool))

    def step(carry, inp):
        C, nvec, m = carry
        qc, kc, vc, lic, lfc = inp
        cum = jnp.cumsum(lfc, axis=-1)
        log_intra = jnp.where(lower, cum[..., :, None] - cum[..., None, :] + lic[..., None, :], -jnp.inf)
        log_inter = cum + m[..., None]
        m_t = jnp.maximum(log_inter, jnp.max(log_intra, axis=-1))
        w_inter = jnp.exp(log_inter - m_t)
        s = jnp.einsum('bhtd,bhsd->bhts', qc, kc) * jnp.exp(log_intra - m_t[..., None])
        num = w_inter[..., None] * jnp.einsum('bhvd,bhtd->bhtv', C, qc) + jnp.einsum('bhts,bhsv->bhtv', s, vc)
        den = w_inter * jnp.einsum('bhd,bhtd->bht', nvec, qc) + jnp.sum(s, axis=-1)
        h = num / jnp.maximum(jnp.abs(den), jnp.exp(-m_t))[..., None]
        total = cum[..., -1]
        log_w = total[..., None] - cum + lic
        m_new = jnp.maximum(total + m, jnp.max(log_w, axis=-1))
        carry_decay = jnp.exp(total + m - m_new)
        w = jnp.exp(log_w - m_new[..., None])
        C_new = carry_decay[..., None, None] * C + jnp.einsum('bhs,bhsv,bhsd->bhvd', w, vc, kc)
        n_new = carry_decay[..., None] * nvec + jnp.einsum('bhs,bhsd->bhd', w, kc)
        return (C_new, n_new, m_new), h

    state, h = lax.scan(step, state, (chunks(q), chunks(k), chunks(v), chunks(li), chunks(lf)))
    return jnp.moveaxis(h, 0, 2).reshape(b, nh, L, d), state


def mlstm_mixer(parts_l, parts_c, gate_bias, out_norm, with_ctx_out):
    f32 = jnp.float32

    def prep(q, k, v, o, g):
        b_, n_, _ = q.shape
        heads = lambda t: t.astype(f32).reshape(b_, n_, ML_HEADS, ML_HEAD_DIM).transpose(0, 2, 1, 3)
        gates = (g.astype(f32).reshape(b_, n_, 4, ML_HEADS) + gate_bias.astype(f32)).transpose(2, 0, 3, 1)
        return heads(q), heads(k) * ML_HEAD_DIM ** -0.5, heads(v), gates

    def both_directions(q, k, v, gates, state_f, state_b):
        flip = lambda t: jnp.flip(t, axis=2)
        h_f, end_f = mlstm_chunkwise(q, k, v, gates[0], jax.nn.log_sigmoid(gates[1]), state_f)
        h_b, end_b = mlstm_chunkwise(flip(q), flip(k), flip(v), flip(gates[2]),
                                     flip(jax.nn.log_sigmoid(gates[3])), state_b)
        return h_f + flip(h_b), end_f, end_b

    def finish(h, o):
        h = merge_heads(rmsnorm(h, out_norm.reshape(ML_HEADS, 1, ML_HEAD_DIM)))
        return (jax.nn.sigmoid(o.astype(f32)) * h).astype(o.dtype)

    qc, kc, vc, gc = prep(*parts_c)
    b = qc.shape[0]
    zero = (jnp.zeros((b, ML_HEADS, ML_HEAD_DIM, ML_HEAD_DIM), f32),
            jnp.zeros((b, ML_HEADS, ML_HEAD_DIM), f32), jnp.zeros((b, ML_HEADS), f32))
    h_c, end_f, end_b = both_directions(qc, kc, vc, gc, zero, zero)
    h_l, _, _ = both_directions(*prep(*parts_l), end_f, end_b)
    out_c = finish(h_c, parts_c[3]) if with_ctx_out else None
    return finish(h_l, parts_l[3]), out_c


def mla_mixer(parts_l, parts_c, q_norm, w_uq, kv_norm, w_ukv, ang, with_ctx_out):
    scale = (MLA_NOPE + MLA_ROPE) ** -0.5

    def queries(q_lat, rope):
        b_, n_, _ = q_lat.shape
        q = (rmsnorm(q_lat, q_norm) @ w_uq).reshape(b_, n_, MLA_HEADS, MLA_NOPE + MLA_ROPE).transpose(0, 2, 1, 3)
        q_rope = apply_axial_rope(q[..., MLA_NOPE:], ang) if rope else q[..., MLA_NOPE:]
        return jnp.concatenate([q[..., :MLA_NOPE], q_rope], axis=-1)

    def keys_values(kv_lat, k_rope, rope):
        b_, n_, _ = kv_lat.shape
        kv = (rmsnorm(kv_lat, kv_norm) @ w_ukv).reshape(b_, n_, MLA_HEADS, MLA_NOPE + MLA_V).transpose(0, 2, 1, 3)
        k_rope = k_rope[:, None]
        if rope:
            k_rope = apply_axial_rope(k_rope, ang)
        k = jnp.concatenate([kv[..., :MLA_NOPE], jnp.broadcast_to(k_rope, (b_, MLA_HEADS, n_, MLA_ROPE))], axis=-1)
        return k, kv[..., MLA_NOPE:]

    q_l = queries(parts_l[0], True)
    k_l, v_l = keys_values(parts_l[1], parts_l[2], True)
    k_c, v_c = keys_values(parts_c[1], parts_c[2], False)
    k_all = jnp.concatenate([k_l, k_c], axis=2)
    v_all = jnp.concatenate([v_l, v_c], axis=2)
    b, nh, n, dq = q_l.shape
    nb = n // BLOCK
    q_blocks = jnp.moveaxis(q_l.reshape(b, nh, nb, BLOCK, dq), 2, 0)
    o = lax.map(lambda qb: dense_attend(qb, k_all, v_all, scale), q_blocks)
    out_l = merge_heads(jnp.moveaxis(o, 0, 2).reshape(b, nh, n, MLA_V))
    out_c = merge_heads(dense_attend(queries(parts_c[0], False), k_c, v_c, scale)) if with_ctx_out else None
    return out_l, out_c


def moe_swiglu(h, router, w_gate, w_up, w_down):
    logits = (h @ router).astype(jnp.float32)
    top_val, top_idx = lax.top_k(logits, TOP_K)
    weights = jax.nn.softmax(top_val, axis=-1)
    combine = jnp.einsum('btk,btke->bte', weights,
                         jax.nn.one_hot(top_idx, N_EXPERTS, dtype=jnp.float32)).astype(h.dtype)
    return sum(combine[..., e:e + 1] * swiglu(h, w_gate[e], w_up[e], w_down[e]) for e in range(N_EXPERTS))


def setup_inputs(seed: int = 0) -> dict:
    key = jax.random.key(seed)
    keys = iter(jax.random.split(key, 40))

    def nrm(shape, scale):
        return scale * jax.random.normal(next(keys), shape, jnp.float32)

    def gain(shape):
        return 1.0 + nrm(shape, 0.05)

    D = D_MODEL
    HID = HY_FILTER_HIDDEN
    return {
        'x': nrm((BATCH, SEQ, D), 1.0),
        'c': nrm((BATCH, D), 1.0),
        'ctx': nrm((BATCH, CTX_LEN, D), 1.0),
        'c_ctx': nrm((D,), 1.0),
        'w_mod': nrm((DEPTH, D, N_MOD * D), 0.5 * D ** -0.5),
        'b_mod': nrm((DEPTH, N_MOD * D), 0.02),
        'g_mix_pre': gain((DEPTH, D)),
        'g_mix_post': gain((DEPTH, D)),
        'g_ffn_pre': gain((DEPTH, D)),
        'g_ffn_post': gain((DEPTH, D)),
        'w_in': nrm((DEPTH, D, IN_WIDTH), D ** -0.5),
        'w_out': nrm((DEPTH, MIX_WIDTH, D), MIX_WIDTH ** -0.5),
        'attn_sink': nrm((DEPTH, A_HEADS), 1.0),
        'hy_short': nrm((DEPTH, HY_SHORT, (HY_ORDER + 1) * HY_WIDTH), HY_SHORT ** -0.5),
        'hy_w1': nrm((DEPTH, HY_POS_DIM, HID), HY_POS_DIM ** -0.5),
        'hy_b1': nrm((DEPTH, HID), 0.1),
        'hy_w2': nrm((DEPTH, HID, HID), HID ** -0.5),
        'hy_b2': nrm((DEPTH, HID), 0.1),
        'hy_freq': gain((DEPTH, HID)),
        'hy_w3': nrm((DEPTH, HID, HY_ORDER * 2 * HY_WIDTH), HID ** -0.5),
        'hy_bias': nrm((DEPTH, HY_ORDER, HY_WIDTH), 1.0),
        'ml_gate_bias': nrm((DEPTH, 4, ML_HEADS), 0.3) + jnp.array([0.0, 3.0, 0.0, 3.0], jnp.float32)[None, :, None],
        'ml_norm': gain((DEPTH, GROUP_WIDTH)),
        'mla_q_norm': gain((DEPTH, MLA_Q_RANK)),
        'mla_w_uq': nrm((DEPTH, MLA_Q_RANK, MLA_HEADS * (MLA_NOPE + MLA_ROPE)), MLA_Q_RANK ** -0.5),
        'mla_kv_norm': gain((DEPTH, MLA_KV_RANK)),
        'mla_w_ukv': nrm((DEPTH, MLA_KV_RANK, MLA_HEADS * (MLA_NOPE + MLA_V)), MLA_KV_RANK ** -0.5),
        'ffn_w_gate': nrm((N_DENSE, D, FFN_DIM), D ** -0.5),
        'ffn_w_up': nrm((N_DENSE, D, FFN_DIM), D ** -0.5),
        'ffn_w_down': nrm((N_DENSE, FFN_DIM, D), FFN_DIM ** -0.5),
        'moe_router': nrm((N_MOE, D, N_EXPERTS), D ** -0.5),
        'moe_w_gate': nrm((N_MOE, N_EXPERTS, D, EXPERT_DIM), D ** -0.5),
        'moe_w_up': nrm((N_MOE, N_EXPERTS, D, EXPERT_DIM), D ** -0.5),
        'moe_w_down': nrm((N_MOE, N_EXPERTS, EXPERT_DIM, D), EXPERT_DIM ** -0.5),
    }


def reference(x, c, ctx, c_ctx, w_mod, b_mod, g_mix_pre, g_mix_post, g_ffn_pre, g_ffn_post,
              w_in, w_out, attn_sink, hy_short, hy_w1, hy_b1, hy_w2, hy_b2, hy_freq, hy_w3, hy_bias,
              ml_gate_bias, ml_norm, mla_q_norm, mla_w_uq, mla_kv_norm, mla_w_ukv,
              ffn_w_gate, ffn_w_up, ffn_w_down, moe_router, moe_w_gate, moe_w_up, moe_w_down):
    rows = x.shape[1] // GRID_W
    ang_a = axial_angles(rows, A_HEAD_DIM)
    ang_m = axial_angles(rows, MLA_ROPE)

    def channel_mix(h, layer):
        i = layer // 2
        if layer % 2 == 0:
            return swiglu(h, ffn_w_gate[i], ffn_w_up[i], ffn_w_down[i])
        return moe_swiglu(h, moe_router[i], moe_w_gate[i], moe_w_up[i], moe_w_down[i])

    x_lat, x_ctx = x, ctx
    for layer in range(DEPTH):
        ctx_out = layer < DEPTH - 1
        mod_l = jnp.split((jax.nn.silu(c) @ w_mod[layer] + b_mod[layer])[:, None, :], N_MOD, axis=-1)
        mod_c = jnp.split((jax.nn.silu(c_ctx) @ w_mod[layer] + b_mod[layer])[None, None, :], N_MOD, axis=-1)

        pl = split_cols(modulate(rmsnorm(x_lat, g_mix_pre[layer]), mod_l[0], mod_l[1]) @ w_in[layer])
        pc = split_cols(modulate(rmsnorm(x_ctx, g_mix_pre[layer]), mod_c[0], mod_c[1]) @ w_in[layer])
        a_l, a_c = window_gqa(pl[0], pl[1], pl[2], pc[0], pc[1], pc[2], attn_sink[layer], ang_a, ctx_out)
        b_l, b_c = hyena_mixer(pl[3], pc[3], hy_short[layer], hy_w1[layer], hy_b1[layer], hy_w2[layer],
                               hy_b2[layer], hy_freq[layer], hy_w3[layer], hy_bias[layer], ctx_out)
        m_l, m_c = mlstm_mixer(pl[4:9], pc[4:9], ml_gate_bias[layer], ml_norm[layer], ctx_out)
        d_l, d_c = mla_mixer(pl[9:12], pc[9:12], mla_q_norm[layer], mla_w_uq[layer], mla_kv_norm[layer],
                             mla_w_ukv[layer], ang_m, ctx_out)
        y_l = jnp.concatenate([a_l, b_l, m_l, d_l], axis=-1) @ w_out[layer]
        x_lat = x_lat + mod_l[2] * rmsnorm(y_l, g_mix_post[layer])

        h_l = modulate(rmsnorm(x_lat, g_ffn_pre[layer]), mod_l[3], mod_l[4])
        x_lat = x_lat + mod_l[5] * rmsnorm(channel_mix(h_l, layer), g_ffn_post[layer])

        if ctx_out:
            y_c = jnp.concatenate([a_c, b_c, m_c, d_c], axis=-1) @ w_out[layer]
            x_ctx = x_ctx + mod_c[2] * rmsnorm(y_c, g_mix_post[layer])
            h_c = modulate(rmsnorm(x_ctx, g_ffn_pre[layer]), mod_c[3], mod_c[4])
            x_ctx = x_ctx + mod_c[5] * rmsnorm(channel_mix(h_c, layer), g_ffn_post[layer])
    return x_lat
```

```python
import functools
import math

import numpy as np
import jax
import jax.numpy as jnp
from jax import lax
from jax.experimental import pallas as pl
from jax.experimental.pallas import tpu as pltpu

F32 = jnp.float32
BF16 = jnp.bfloat16
HI = lax.Precision.HIGHEST

EPS = 1e-6
ROPE_BASE = 10000.0
GRID_W = 64
BLOCK = 128
WINDOW = 128
N_MOD = 6
A_HEADS, A_KV_HEADS = 8, 2
HY_ORDER, HY_SHORT, HY_POS_FREQS = 2, 3, 8
HY_DECAY_TARGET, HY_FAST_DECAY, HY_SLOW_DECAY = 1e-2, 0.3, 1.5
ML_HEADS, ML_CHUNK = 4, 128
MLA_HEADS, MLA_NOPE, MLA_ROPE = 8, 64, 32
N_EXPERTS, TOP_K = 8, 2

LANE = 128
ROW_TILE = 256
NEG = -1e30
VMEM_LIMIT = 56 * 1024 * 1024


def _cp(sem, vmem=VMEM_LIMIT):
    return pltpu.CompilerParams(dimension_semantics=sem, vmem_limit_bytes=vmem)


def _gmm_kernel(gid_ref, a_ref, w_ref, *rest, cast, epilogue, has_mul):
    if has_mul:
        mul_ref, rest = rest[0], rest[1:]
    o_ref = rest[0]
    m = pl.program_id(1)
    if cast:
        wbf_ref = rest[1]
        prev = gid_ref[jnp.maximum(m - 1, 0)]

        @pl.when((m == 0) | (gid_ref[m] != prev))
        def _():
            wbf_ref[...] = w_ref[...].astype(BF16)

        w = wbf_ref[...]
    else:
        w = w_ref[...]
    acc = jnp.dot(a_ref[...], w, preferred_element_type=F32)
    if epilogue == "silu":
        acc = acc * jax.nn.sigmoid(acc)
    if has_mul:
        acc = acc * mul_ref[...].astype(F32)
    o_ref[...] = acc.astype(o_ref.dtype)


def gmm(a, w, gid, *, tm, tn, n_tiles, out_dtype, n_off=0, epilogue=None, mul=None, name="gmm"):
    M, K = a.shape
    mt = M // tm
    assert mt * tm == M and w.shape[1] == K and gid.shape == (mt,)
    cast = w.dtype != BF16
    in_specs = [pl.BlockSpec((tm, K), lambda n, m, g: (m, 0)),
                pl.BlockSpec((None, K, tn), lambda n, m, g: (g[m], 0, n + n_off))]
    args = [a, w]
    if mul is not None:
        in_specs.append(pl.BlockSpec((tm, tn), lambda n, m, g: (m, n)))
        args.append(mul)
    kern = functools.partial(_gmm_kernel, cast=cast, epilogue=epilogue, has_mul=mul is not None)
    return pl.pallas_call(
        kern,
        grid_spec=pltpu.PrefetchScalarGridSpec(
            num_scalar_prefetch=1, grid=(n_tiles, mt), in_specs=in_specs,
            out_specs=pl.BlockSpec((tm, tn), lambda n, m, g: (m, n)),
            scratch_shapes=[pltpu.VMEM((K, tn), BF16)] if cast else []),
        out_shape=jax.ShapeDtypeStruct((M, n_tiles * tn), out_dtype),
        compiler_params=_cp(("arbitrary", "arbitrary")),
        name=name,
    )(gid, *args)


def _rms(v, g):
    return v * lax.rsqrt(jnp.mean(v * v, axis=-1, keepdims=True) + EPS) * g


def _norm_mod_kernel(x_ref, g_ref, sh_ref, sc_ref, h_ref):
    h_ref[...] = (_rms(x_ref[...], g_ref[...]) * (1.0 + sc_ref[...]) + sh_ref[...]).astype(h_ref.dtype)


def _post_kernel(*refs, with_next, with_router):
    x_ref, y_ref, gp_ref, gate_ref = refs[:4]
    refs = refs[4:]
    xn = x_ref[...] + gate_ref[...] * _rms(y_ref[...], gp_ref[...])
    if not with_next:
        refs[0][...] = xn
        return
    gn_ref, sh_ref, sc_ref = refs[:3]
    refs = refs[3:]
    if with_router:
        r_ref, refs = refs[0], refs[1:]
    refs[0][...] = xn
    h = _rms(xn, gn_ref[...]) * (1.0 + sc_ref[...]) + sh_ref[...]
    refs[1][...] = h.astype(BF16)
    if with_router:
        hf_ref, rw_ref, ri_ref = refs[2:5]
        hf_ref[...] = h
        logits = jnp.dot(h, r_ref[...], preferred_element_type=F32, precision=HI)
        lane = lax.broadcasted_iota(jnp.int32, logits.shape, 1)
        logits = jnp.where(lane < N_EXPERTS, logits, -jnp.inf)
        v1 = jnp.max(logits, axis=1, keepdims=True)
        i1 = jnp.min(jnp.where(logits == v1, lane, LANE), axis=1, keepdims=True)
        l2 = jnp.where(lane == i1, -jnp.inf, logits)
        v2 = jnp.max(l2, axis=1, keepdims=True)
        i2 = jnp.min(jnp.where(l2 == v2, lane, LANE), axis=1, keepdims=True)
        e = jnp.exp(v2 - v1)
        w1 = 1.0 / (1.0 + e)
        w2 = e / (1.0 + e)
        rw_ref[...] = jnp.where(lane == 0, w1, jnp.where(lane == 1, w2, 0.0))
        ri_ref[...] = jnp.where(lane == 0, i1, jnp.where(lane == 1, i2, 0))


class _Rows:
    def __init__(self, B, S, NC, D):
        self.B, self.S, self.NC, self.D = B, S, NC, D
        self.T = B * S + B * NC
        assert S % ROW_TILE == 0 and NC % ROW_TILE == 0
        self.nt = self.T // ROW_TILE

    def mod_row(self, i):
        n_lat = self.B * self.S // ROW_TILE
        return jnp.where(i < n_lat, i // (self.S // ROW_TILE), self.B)


def _row_spec(D):
    return pl.BlockSpec((ROW_TILE, D), lambda i: (i, 0))


def _vec_spec(D):
    return pl.BlockSpec((1, D), lambda i: (0, 0))


def _mod_spec(rows, j):
    return pl.BlockSpec((None, None, 1, rows.D), lambda i: (rows.mod_row(i), j, 0, 0))


def norm_mod(rows, x, g, mods, j_shift, j_scale):
    D = rows.D
    return pl.pallas_call(
        _norm_mod_kernel, grid=(rows.nt,),
        in_specs=[_row_spec(D), _vec_spec(D), _mod_spec(rows, j_shift), _mod_spec(rows, j_scale)],
        out_specs=_row_spec(D),
        out_shape=jax.ShapeDtypeStruct((rows.T, D), BF16),
        compiler_params=_cp(("parallel",)), name="norm_mod",
    )(x, g, mods, mods)


def post(rows, x, y, g_post, mods, j_gate, nxt=None, router=None):
    D, T = rows.D, rows.T
    in_specs = [_row_spec(D), _row_spec(D), _vec_spec(D), _mod_spec(rows, j_gate)]
    args = [x, y, g_post, mods]
    out_specs = [_row_spec(D)]
    out_shape = [jax.ShapeDtypeStruct((T, D), F32)]
    if nxt is not None:
        g_next, mods_next, j_shift, j_scale = nxt
        in_specs += [_vec_spec(D), _mod_spec(rows, j_shift), _mod_spec(rows, j_scale)]
        args += [g_next, mods_next, mods_next]
        out_specs.append(_row_spec(D))
        out_shape.append(jax.ShapeDtypeStruct((T, D), BF16))
        if router is not None:
            in_specs.append(pl.BlockSpec((D, LANE), lambda i: (0, 0)))
            args.append(router)
            out_specs += [_row_spec(D), _row_spec(LANE), _row_spec(LANE)]
            out_shape += [jax.ShapeDtypeStruct((T, D), F32), jax.ShapeDtypeStruct((T, LANE), F32),
                          jax.ShapeDtypeStruct((T, LANE), jnp.int32)]
    kern = functools.partial(_post_kernel, with_next=nxt is not None, with_router=router is not None)
    return pl.pallas_call(
        kern, grid=(rows.nt,), in_specs=in_specs, out_specs=out_specs, out_shape=out_shape,
        compiler_params=_cp(("parallel",)), name="post",
    )(*args)


def _rot_pairs(x, q):
    lane = lax.broadcasted_iota(jnp.int32, x.shape, 1)
    n = x.shape[1]
    return jnp.where(lane % (2 * q) < q, -pltpu.roll(x, n - q, 1), pltpu.roll(x, q, 1))


def _axial_tables(S, rot_dim):
    rows = S // GRID_W
    r = jnp.repeat(jnp.arange(rows, dtype=F32), GRID_W)
    col = jnp.tile(jnp.arange(GRID_W, dtype=F32), rows)
    half = rot_dim // 2
    inv = ROPE_BASE ** (-jnp.arange(0, half, 2, dtype=F32) / half)
    ar, ac = r[:, None] * inv, col[:, None] * inv
    ang = jnp.concatenate([ar, ar, ac, ac], axis=-1)
    return jnp.cos(ang), jnp.sin(ang)


def _attn_rope_kernel(p_ref, cos_ref, sin_ref, o_ref, *, scale):
    cos, sin = cos_ref[...], sin_ref[...]
    nq = A_HEADS * 64 // LANE
    for s in range(nq + 1):
        xs = p_ref[:, s * LANE:(s + 1) * LANE]
        r = xs * cos + _rot_pairs(xs, 16) * sin
        if s < nq:
            r = r * scale
        o_ref[:, s * LANE:(s + 1) * LANE] = r.astype(o_ref.dtype)
    o_ref[:, (nq + 1) * LANE:] = p_ref[:, (nq + 1) * LANE:].astype(o_ref.dtype)


def attn_rope(rows, p1, cos, sin):
    W = 768
    return pl.pallas_call(
        functools.partial(_attn_rope_kernel, scale=64 ** -0.5), grid=(rows.nt,),
        in_specs=[pl.BlockSpec((ROW_TILE, W), lambda i: (i, 0)), _row_spec(LANE), _row_spec(LANE)],
        out_specs=pl.BlockSpec((ROW_TILE, W), lambda i: (i, 0)),
        out_shape=jax.ShapeDtypeStruct((rows.T, W), BF16),
        compiler_params=_cp(("parallel",)), name="attn_rope",
    )(p1, cos, sin)


def _nt(a, b):
    return lax.dot_general(a, b, (((1,), (1,)), ((), ())), preferred_element_type=F32)


def _softmax_av(scores, values, sink=None):
    m = scores[0].max(axis=1, keepdims=True)
    for s in scores[1:]:
        m = jnp.maximum(m, s.max(axis=1, keepdims=True))
    if sink is not None:
        m = jnp.maximum(m, sink)
    den = jnp.exp(sink - m) if sink is not None else 0.0
    acc = None
    for s, v in zip(scores, values):
        p = jnp.exp(s - m)
        den = den + p.sum(axis=1, keepdims=True)
        pv = jnp.dot(p.astype(BF16), v, preferred_element_type=F32)
        acc = pv if acc is None else acc + pv
    return acc / den


def _window_kernel(sink_ref, q_ref, kp_ref, kc_ref, kn_ref, vp_ref, vc_ref, vn_ref, kx_ref, vx_ref, o_ref, *, nb):
    n = pl.program_id(1)
    groups = A_HEADS // A_KV_HEADS
    d = 64
    kb = jnp.concatenate([kp_ref[...], kc_ref[...], kn_ref[...]], axis=0)
    vb = jnp.concatenate([vp_ref[...], vc_ref[...], vn_ref[...]], axis=0)
    Q = groups * BLOCK
    qi = lax.broadcasted_iota(jnp.int32, (Q, 3 * BLOCK), 0) % BLOCK
    kj = lax.broadcasted_iota(jnp.int32, (Q, 3 * BLOCK), 1)
    k_abs = (n - 1) * BLOCK + kj
    valid = (jnp.abs(kj - BLOCK - qi) <= WINDOW) & (k_abs >= 0) & (k_abs < nb * BLOCK)
    hrow = lax.broadcasted_iota(jnp.int32, (Q, 1), 0) // BLOCK
    outs = []
    for g in range(A_KV_HEADS):
        qg = jnp.concatenate([q_ref[:, (g * groups + h) * d:(g * groups + h + 1) * d] for h in range(groups)], axis=0)
        ksl = slice(g * d, (g + 1) * d)
        s_band = jnp.where(valid, _nt(qg, kb[:, ksl]), NEG)
        s_ctx = _nt(qg, kx_ref[:, ksl])
        sink = jnp.zeros((Q, 1), F32)
        for h in range(groups):
            sink = jnp.where(hrow == h, sink_ref[g * groups + h], sink)
        o = _softmax_av([s_band, s_ctx], [vb[:, ksl], vx_ref[:, ksl]], sink)
        outs += [o[h * BLOCK:(h + 1) * BLOCK] for h in range(groups)]
    o_ref[...] = jnp.concatenate(outs, axis=1).astype(o_ref.dtype)


def window_attn(rows, qkv, sink):
    B, S, NC = rows.B, rows.S, rows.NC
    nb = S // BLOCK
    cb = B * S // NC

    def kv_spec(col, off):
        return pl.BlockSpec((BLOCK, LANE), lambda b, n, s: (b * nb + jnp.clip(n + off, 0, nb - 1), col))

    in_specs = [pl.BlockSpec((BLOCK, 512), lambda b, n, s: (b * nb + n, 0)),
                kv_spec(4, -1), kv_spec(4, 0), kv_spec(4, 1), kv_spec(5, -1), kv_spec(5, 0), kv_spec(5, 1),
                pl.BlockSpec((NC, LANE), lambda b, n, s: (cb + b, 4)),
                pl.BlockSpec((NC, LANE), lambda b, n, s: (cb + b, 5))]
    return pl.pallas_call(
        functools.partial(_window_kernel, nb=nb),
        grid_spec=pltpu.PrefetchScalarGridSpec(
            num_scalar_prefetch=1, grid=(B, nb), in_specs=in_specs,
            out_specs=pl.BlockSpec((BLOCK, 512), lambda b, n, s: (b * nb + n, 0))),
        out_shape=jax.ShapeDtypeStruct((B * S, 512), BF16),
        compiler_params=_cp(("parallel", "parallel")), name="window_attn",
    )(sink, *([qkv] * 9))


def _ctx_attn_kernel(sink_ref, q_ref, k_ref, v_ref, o_ref, *, heads, kv_heads, dk, dv, use_sink):
    groups = heads // kv_heads
    outs = []
    for h in range(heads):
        g = h // groups
        s = _nt(q_ref[:, h * dk:(h + 1) * dk], k_ref[:, g * dk:(g + 1) * dk])
        sink = jnp.full((s.shape[0], 1), sink_ref[h], F32) if use_sink else None
        outs.append(_softmax_av([s], [v_ref[:, g * dv:(g + 1) * dv]], sink))
    o_ref[...] = jnp.concatenate(outs, axis=1).astype(o_ref.dtype)


def ctx_attn(rows, q, k, v, sink, *, q_col, k_col, v_col, heads, kv_heads, dk, dv, use_sink, name):
    B, S, NC = rows.B, rows.S, rows.NC
    cb = B * S // NC
    return pl.pallas_call(
        functools.partial(_ctx_attn_kernel, heads=heads, kv_heads=kv_heads, dk=dk, dv=dv, use_sink=use_sink),
        grid_spec=pltpu.PrefetchScalarGridSpec(
            num_scalar_prefetch=1, grid=(B,),
            in_specs=[pl.BlockSpec((NC, heads * dk), lambda b, s: (cb + b, q_col)),
                      pl.BlockSpec((NC, kv_heads * dk), lambda b, s: (cb + b, k_col)),
                      pl.BlockSpec((NC, kv_heads * dv), lambda b, s: (cb + b, v_col))],
            out_specs=pl.BlockSpec((NC, heads * dv), lambda b, s: (b, 0))),
        out_shape=jax.ShapeDtypeStruct((B * NC, heads * dv), BF16),
        compiler_params=_cp(("parallel",)), name=name,
    )(sink, q, k, v)


def _mla_prep_kernel(q_ref, k_ref, kr_ref, cos_ref, sin_ref, qo_ref, ko_ref, *, scale):
    cos, sin = cos_ref[...], sin_ref[...]
    kr = kr_ref[...]
    kr = kr * cos + _rot_pairs(kr, 8) * sin
    for h in range(MLA_HEADS):
        sl = slice(h * LANE, (h + 1) * LANE)
        qh = q_ref[:, sl]
        qo_ref[:, sl] = ((qh * cos + _rot_pairs(qh, 8) * sin) * scale).astype(qo_ref.dtype)
        ko_ref[:, sl] = (k_ref[:, sl] + kr).astype(ko_ref.dtype)


def mla_prep(rows, q, k, p2, cos, sin):
    W = MLA_HEADS * LANE
    wide = pl.BlockSpec((ROW_TILE, W), lambda i: (i, 0))
    return pl.pallas_call(
        functools.partial(_mla_prep_kernel, scale=(MLA_NOPE + MLA_ROPE) ** -0.5), grid=(rows.nt,),
        in_specs=[wide, wide, pl.BlockSpec((ROW_TILE, LANE), lambda i: (i, 6)), _row_spec(LANE), _row_spec(LANE)],
        out_specs=[wide, wide],
        out_shape=[jax.ShapeDtypeStruct((rows.T, W), BF16)] * 2,
        compiler_params=_cp(("parallel",)), name="mla_prep",
    )(q, k, p2, cos, sin)


def _mla_kernel(q_ref, kl_ref, kx_ref, vl_ref, vx_ref, o_ref):
    outs = []
    for j in range(2):
        sl, vs = slice(j * LANE, (j + 1) * LANE), slice(j * 64, (j + 1) * 64)
        q = q_ref[:, sl]
        outs.append(_softmax_av([_nt(q, kl_ref[:, sl]), _nt(q, kx_ref[:, sl])], [vl_ref[:, vs], vx_ref[:, vs]]))
    o_ref[...] = jnp.concatenate(outs, axis=1).astype(o_ref.dtype)


def mla_attn(rows, qm, km, vm, tq=256):
    B, S, NC = rows.B, rows.S, rows.NC
    cb = B * S // NC
    nq = S // tq
    return pl.pallas_call(
        _mla_kernel, grid=(B, MLA_HEADS // 2, nq),
        in_specs=[pl.BlockSpec((tq, 2 * LANE), lambda b, h, i: (b * nq + i, h)),
                  pl.BlockSpec((S, 2 * LANE), lambda b, h, i: (b, h)),
                  pl.BlockSpec((NC, 2 * LANE), lambda b, h, i: (cb + b, h)),
                  pl.BlockSpec((S, LANE), lambda b, h, i: (b, h)),
                  pl.BlockSpec((NC, LANE), lambda b, h, i: (cb + b, h))],
        out_specs=pl.BlockSpec((tq, LANE), lambda b, h, i: (b * nq + i, h)),
        out_shape=jax.ShapeDtypeStruct((B * S, MLA_HEADS * 64), BF16),
        compiler_params=_cp(("parallel", "parallel", "parallel")), name="mla_attn",
    )(qm, km, km, vm, vm)


def _rmsnorm_cols_kernel(x_ref, g_ref, o_ref):
    o_ref[...] = _rms(x_ref[...], g_ref[...]).astype(o_ref.dtype)


def rmsnorm_cols(rows, x, g, width, col):
    return pl.pallas_call(
        _rmsnorm_cols_kernel, grid=(rows.nt,),
        in_specs=[pl.BlockSpec((ROW_TILE, width), lambda i: (i, col)), _vec_spec(width)],
        out_specs=_row_spec(width),
        out_shape=jax.ShapeDtypeStruct((rows.T, width), BF16),
        compiler_params=_cp(("parallel",)), name="rmsnorm_cols",
    )(x, g)


def _log_sigmoid(x):
    return jnp.minimum(x, 0.0) - jnp.log(1.0 + jnp.exp(-jnp.abs(x)))


def _mlstm_kernel(*refs, backward, scale):
    nh, L = ML_HEADS, ML_CHUNK
    q_refs, k_refs, v_refs = refs[0:nh], refs[nh:2 * nh], refs[2 * nh:3 * nh]
    g_ref, gb_ref, o_ref, c_ref, n_ref, m_ref = refs[3 * nh:]
    step = pl.program_id(1)

    @pl.when(step == 0)
    def _():
        c_ref[...] = jnp.zeros_like(c_ref)
        n_ref[...] = jnp.zeros_like(n_ref)
        m_ref[...] = jnp.zeros_like(m_ref)

    t_i = lax.broadcasted_iota(jnp.int32, (L, L), 0)
    s_i = lax.broadcasted_iota(jnp.int32, (L, L), 1)
    mask = (s_i >= t_i) if backward else (s_i <= t_i)
    maskf = mask.astype(F32)
    gb = g_ref[...] + gb_ref[...]
    ls = _log_sigmoid(gb)
    gbt, lst = gb.T, ls.T
    cum_c = jnp.dot(maskf, ls, preferred_element_type=F32, precision=HI)
    cum_r = jnp.dot(lst, maskf.T, preferred_element_type=F32, precision=HI)
    last = 0 if backward else L - 1
    gi0, gf0 = (2 * nh, 3 * nh) if backward else (0, nh)
    for h in range(nh):
        gi, gf = gi0 + h, gf0 + h
        q = q_refs[h][...]
        kf = k_refs[h][...] * scale
        vf = v_refs[h][...]
        qb, kb = q.astype(BF16), kf.astype(BF16)
        cumc, cumr = cum_c[:, gf:gf + 1], cum_r[gf:gf + 1, :]
        li_r, li_c = gbt[gi:gi + 1, :], gb[:, gi:gi + 1]
        m11 = m_ref[h][0:1, 0:1]
        log_intra = jnp.where(mask, cumc - cumr + li_r, NEG)
        log_inter = cumc + m11
        m_t = jnp.maximum(log_inter, log_intra.max(axis=1, keepdims=True))
        w_inter = jnp.exp(log_inter - m_t)
        s = _nt(qb, kb) * jnp.exp(log_intra - m_t)
        cmat = c_ref[h]
        num = w_inter * _nt(qb, cmat.astype(BF16)) + jnp.dot(s.astype(BF16), vf.astype(BF16),
                                                             preferred_element_type=F32)
        den = w_inter * jnp.sum(q * n_ref[h][0:1, :], axis=1, keepdims=True) + s.sum(axis=1, keepdims=True)
        o_ref[:, h * L:(h + 1) * L] = num / jnp.maximum(jnp.abs(den), jnp.exp(-m_t))
        total = cumc[last:last + 1, :]
        log_w = total - cumc + li_c
        m_new = jnp.maximum(total + m11, log_w.max(axis=0, keepdims=True))
        decay = jnp.exp(total + m11 - m_new)
        w = jnp.exp(log_w - m_new)
        upd = lax.dot_general((vf * w).astype(BF16), kb, (((0,), (0,)), ((), ())), preferred_element_type=F32)
        c_ref[h] = decay * cmat + upd
        n_ref[h] = jnp.broadcast_to(decay * n_ref[h][0:1, :] + jnp.sum(w * kf, axis=0, keepdims=True), (8, L))
        m_ref[h] = jnp.broadcast_to(m_new, (8, LANE))


def mlstm_dir(rows, p1, gbias, backward):
    B, S, NC = rows.B, rows.S, rows.NC
    nh, L = ML_HEADS, ML_CHUNK
    ncx, ncl = NC // L, S // L
    first_ctx = B * S // L

    def row_block(b, j):
        cx = (ncx - 1 - j) if backward else j
        cl = (ncl - 1 - (j - ncx)) if backward else (j - ncx)
        return jnp.where(j < ncx, first_ctx + b * ncx + cx, b * ncl + cl)

    def col_spec(col):
        return pl.BlockSpec((L, LANE), lambda b, j: (row_block(b, j), col))

    q0 = 2304 // LANE
    in_specs = ([col_spec(q0 + h) for h in range(nh)] + [col_spec(q0 + 4 + h) for h in range(nh)]
                + [col_spec(q0 + 8 + h) for h in range(nh)]
                + [col_spec(4352 // LANE), pl.BlockSpec((1, LANE), lambda b, j: (0, 0))])
    return pl.pallas_call(
        functools.partial(_mlstm_kernel, backward=backward, scale=L ** -0.5),
        grid=(B, ncx + ncl), in_specs=in_specs,
        out_specs=pl.BlockSpec((L, nh * L), lambda b, j: (row_block(b, j), 0)),
        out_shape=jax.ShapeDtypeStruct((rows.T, nh * L), F32),
        scratch_shapes=[pltpu.VMEM((nh, L, L), F32), pltpu.VMEM((nh, 8, L), F32), pltpu.VMEM((nh, 8, LANE), F32)],
        compiler_params=_cp(("parallel", "arbitrary")), name="mlstm_bwd" if backward else "mlstm_fwd",
    )(*([p1] * (3 * nh + 1)), gbias)


def _mlstm_finish_kernel(hf_ref, hb_ref, o_ref, g_ref, out_ref):
    h = _rms(hf_ref[...] + hb_ref[...], g_ref[...])
    out_ref[...] = (jax.nn.sigmoid(o_ref[...]) * h).astype(out_ref.dtype)


def mlstm_finish(rows, hf, hb, p1, gnorm):
    blk = lambda c0: pl.BlockSpec((ROW_TILE, LANE), lambda i, h: (i, c0 + h))
    return pl.pallas_call(
        _mlstm_finish_kernel, grid=(rows.nt, ML_HEADS),
        in_specs=[blk(0), blk(0), blk(3840 // LANE), pl.BlockSpec((1, LANE), lambda i, h: (0, h))],
        out_specs=blk(0),
        out_shape=jax.ShapeDtypeStruct((rows.T, ML_HEADS * ML_CHUNK), BF16),
        compiler_params=_cp(("parallel", "parallel")), name="mlstm_finish",
    )(hf, hb, p1, gnorm)


def _short_conv_kernel(p_ref, w_ref, o_ref):
    x = p_ref[...]
    L = x.shape[0]
    row = lax.broadcasted_iota(jnp.int32, x.shape, 0)
    prev = jnp.where(row == 0, 0.0, pltpu.roll(x, 1, 0))
    nxt = jnp.where(row == L - 1, 0.0, pltpu.roll(x, L - 1, 0))
    o_ref[...] = w_ref[0:1, :] * prev + w_ref[1:2, :] * x + w_ref[2:3, :] * nxt


def short_conv(p1, short_w, L, row_block0, B):
    ncol = 1536 // LANE
    per = 512 // LANE
    return pl.pallas_call(
        _short_conv_kernel, grid=(B, ncol),
        in_specs=[pl.BlockSpec((L, LANE), lambda b, c: (row_block0 + b, 768 // LANE + c)),
                  pl.BlockSpec((HY_SHORT, LANE), lambda b, c: (0, c))],
        out_specs=pl.BlockSpec((None, L, LANE), lambda b, c: (c // per, 0, b * per + c % per)),
        out_shape=jax.ShapeDtypeStruct((3, L, B * 512), F32),
        compiler_params=_cp(("parallel", "parallel")), name="hy_short_conv",
    )(p1, short_w)


def _filter_kernel(feat_ref, w1_ref, b1_ref, w2_ref, b2_ref, fr_ref, w3f_ref, w3b_ref, dl_ref, o_ref, hid_ref):
    L = feat_ref.shape[0]

    @pl.when((pl.program_id(0) == 0) & (pl.program_id(1) == 0))
    def _():
        fr = fr_ref[...]
        h1 = jnp.sin(fr * (jnp.dot(feat_ref[...], w1_ref[...], preferred_element_type=F32, precision=HI)
                           + b1_ref[...]))
        hid_ref[...] = jnp.sin(fr * (jnp.dot(h1, w2_ref[...], preferred_element_type=F32, precision=HI)
                                     + b2_ref[...]))

    hid = hid_ref[...]
    t = lax.broadcasted_iota(jnp.int32, (L, 1), 0).astype(F32) / L
    dec = jnp.exp(-t * dl_ref[...])
    hf = jnp.dot(hid, w3f_ref[...], preferred_element_type=F32, precision=HI) * dec
    hb = jnp.dot(hid, w3b_ref[...], preferred_element_type=F32, precision=HI) * dec
    hb = jnp.where(lax.broadcasted_iota(jnp.int32, hb.shape, 0) == 0, 0.0, hb)
    inv = 1.0 / (jnp.sum(jnp.abs(hf), axis=0, keepdims=True) + jnp.sum(jnp.abs(hb), axis=0, keepdims=True))
    o_ref[0] = ((hf + hb) * inv).astype(o_ref.dtype)
    o_ref[1] = ((hf - hb) * inv).astype(o_ref.dtype)


def hyena_filters(feats, w1p, b1, w2, b2, fr, w3, deltas, L):
    hid = w2.shape[0]
    per = 512 // LANE
    full = lambda shape: pl.BlockSpec(shape, lambda o, c: (0, 0))
    return pl.pallas_call(
        _filter_kernel, grid=(HY_ORDER, per),
        in_specs=[full((L, LANE)), full((LANE, hid)), full((1, hid)), full((hid, hid)), full((1, hid)),
                  full((1, hid)),
                  pl.BlockSpec((hid, LANE), lambda o, c: (0, (o * 2) * per + c)),
                  pl.BlockSpec((hid, LANE), lambda o, c: (0, (o * 2 + 1) * per + c)),
                  pl.BlockSpec((1, LANE), lambda o, c: (0, c))],
        out_specs=pl.BlockSpec((2, L, LANE), lambda o, c: (0, 0, o * per + c)),
        out_shape=jax.ShapeDtypeStruct((2, L, HY_ORDER * 512), BF16),
        scratch_shapes=[pltpu.VMEM((L, hid), F32)],
        compiler_params=_cp(("arbitrary", "arbitrary")), name="hy_filters",
    )(feats, w1p, b1, w2, b2, fr, w3, w3, deltas)


def _spec_mul_kernel(z_ref, g_ref, y_ref, *, scale, B):
    gre, gim = g_ref[0], g_ref[1]
    W = gre.shape[1]
    for b in range(B):
        sl = slice(b * W, (b + 1) * W)
        zre, zim = z_ref[0, :, sl], z_ref[1, :, sl]
        y_ref[0, :, sl] = ((zre * gre - zim * gim) * scale).astype(y_ref.dtype)
        y_ref[1, :, sl] = ((zre * gim + zim * gre) * scale).astype(y_ref.dtype)


def spec_mul(z, g, order, L, B, tk):
    return pl.pallas_call(
        functools.partial(_spec_mul_kernel, scale=1.0 / L, B=B), grid=(L // tk,),
        in_specs=[pl.BlockSpec((2, tk, B * 512), lambda i: (0, i, 0)),
                  pl.BlockSpec((2, tk, 512), lambda i: (0, i, order))],
        out_specs=pl.BlockSpec((2, tk, B * 512), lambda i: (0, i, 0)),
        out_shape=jax.ShapeDtypeStruct((2, L, B * 512), BF16),
        compiler_params=_cp(("parallel",)), name="hy_spec_mul",
    )(z, g)


def _gate_kernel(x_ref, c_ref, z_ref, b_ref, o_ref):
    o_ref[...] = (x_ref[...] * (c_ref[...] + b_ref[...] * z_ref[...])).astype(o_ref.dtype)


def hyena_gate(u3, which_x, conv, z_arr, which_z, bias, order, L, B, tl, final):
    nl = L // tl
    in_specs = [pl.BlockSpec((None, tl, 512), lambda i, b: (which_x, i, b)),
                pl.BlockSpec((tl, 512), lambda i, b: (i, b)),
                pl.BlockSpec((None, tl, 512), lambda i, b: (which_z, i, b)),
                pl.BlockSpec((None, 1, 512), lambda i, b: (order, 0, 0))]
    if final:
        out_spec = pl.BlockSpec((tl, 512), lambda i, b: (b * nl + i, 0))
        out_shape = jax.ShapeDtypeStruct((B * L, 512), BF16)
    else:
        out_spec = pl.BlockSpec((None, tl, 512), lambda i, b: (0, i, b))
        out_shape = jax.ShapeDtypeStruct((1, L, B * 512), F32)
    return pl.pallas_call(
        _gate_kernel, grid=(nl, B), in_specs=in_specs, out_specs=out_spec, out_shape=out_shape,
        compiler_params=_cp(("parallel", "parallel")), name="hy_gate",
    )(u3, conv, z_arr, bias)


def _dft_matrices(L):
    k = jnp.arange(L, dtype=jnp.int32)[:, None]
    t = jnp.arange(L, dtype=jnp.int32)[None, :]
    phase = ((2 * k + 1) * t) % (4 * L)
    ang = phase.astype(F32) * (math.pi / (2 * L))
    fwd = jnp.concatenate([jnp.cos(ang), -jnp.sin(ang)], axis=0).astype(BF16)
    return fwd, fwd.T


def _filter_feats(L):
    t = jnp.arange(L, dtype=F32) / L
    kf = jnp.arange(1, HY_POS_FREQS + 1, dtype=F32)
    ang = 2.0 * math.pi * t[:, None] * kf
    feats = jnp.concatenate([t[:, None], jnp.sin(ang), jnp.cos(ang)], axis=-1)
    return jnp.pad(feats, ((0, 0), (0, LANE - feats.shape[1])))


def hyena_seq(p1, L, row_block0, B, consts, wts, tm, tl):
    fwd, inv, feats = consts
    short_w, w1p, b1, w2, b2, fr, w3, deltas, bias = wts
    u3 = short_conv(p1, short_w, L, row_block0, B)
    filt = hyena_filters(feats, w1p, b1, w2, b2, fr, w3, deltas, L)
    gid_f = jnp.repeat(jnp.arange(2, dtype=jnp.int32), L // tm)
    zeros_f = jnp.zeros((2 * L // tm,), jnp.int32)
    zeros_i = jnp.zeros((L // tm,), jnp.int32)
    NW = B * 512
    gspec = gmm(fwd, filt, gid_f, tm=tm, tn=512, n_tiles=HY_ORDER, out_dtype=F32, name="hy_dft_filter")
    gspec = gspec.reshape(2, L, HY_ORDER * 512)
    z_arr, which_z = u3, 0
    out = None
    for o in range(HY_ORDER):
        zs = gmm(fwd, z_arr, zeros_f + which_z, tm=tm, tn=512, n_tiles=NW // 512, out_dtype=F32, name="hy_dft_fwd")
        y = spec_mul(zs.reshape(2, L, NW), gspec, o, L, B, tl)
        conv = gmm(inv, y.reshape(1, 2 * L, NW), zeros_i, tm=tm, tn=512, n_tiles=NW // 512, out_dtype=F32,
                   name="hy_dft_inv")
        final = o == HY_ORDER - 1
        out = hyena_gate(u3, 1 + o, conv, z_arr, which_z, bias, o, L, B, tl, final)
        z_arr, which_z = out, 0
    return out


def _gather_kernel(idx_ref, src_ref, o_ref, buf_ref, sem, *, tm):
    base = pl.program_id(0) * tm

    def row_copy(r, src_row):
        return pltpu.make_async_copy(src_ref.at[pl.ds(src_row, 1)], buf_ref.at[pl.ds(r, 1)], sem)

    def issue(r, carry):
        row_copy(r, idx_ref[base + r]).start()
        return carry

    def wait(r, carry):
        row_copy(r, 0).wait()
        return carry

    lax.fori_loop(0, tm, issue, 0)
    lax.fori_loop(0, tm, wait, 0)
    o_ref[...] = buf_ref[...].astype(o_ref.dtype)


def gather_rows(src, idx, tm, out_dtype):
    M = idx.shape[0]
    D = src.shape[1]
    return pl.pallas_call(
        functools.partial(_gather_kernel, tm=tm),
        grid_spec=pltpu.PrefetchScalarGridSpec(
            num_scalar_prefetch=1, grid=(M // tm,),
            in_specs=[pl.BlockSpec(memory_space=pl.ANY)],
            out_specs=pl.BlockSpec((tm, D), lambda i, idx: (i, 0)),
            scratch_shapes=[pltpu.VMEM((tm, D), src.dtype), pltpu.SemaphoreType.DMA(())]),
        out_shape=jax.ShapeDtypeStruct((M, D), out_dtype),
        compiler_params=_cp(("arbitrary",)), name="gather_rows",
    )(idx, src)


def _combine_kernel(y0_ref, y1_ref, w_ref, o_ref):
    w = w_ref[...]
    o_ref[...] = w[:, 0:1] * y0_ref[...] + w[:, 1:2] * y1_ref[...]


def moe_combine(rows, yg, rw):
    D, T = rows.D, rows.T
    return pl.pallas_call(
        _combine_kernel, grid=(rows.nt,),
        in_specs=[pl.BlockSpec((None, ROW_TILE, D), lambda i: (0, i, 0)),
                  pl.BlockSpec((None, ROW_TILE, D), lambda i: (1, i, 0)), _row_spec(LANE)],
        out_specs=_row_spec(D), out_shape=jax.ShapeDtypeStruct((T, D), F32),
        compiler_params=_cp(("parallel",)), name="moe_combine",
    )(yg, yg, rw)


def moe_ffn(rows, h_f32, rw, ri, w_gate, w_up, w_down, layer_moe, tm=256):
    T, D = rows.T, rows.D
    E = w_gate.shape[2]
    e_flat = ri[:, :TOP_K].T.reshape(-1)
    onehot = (e_flat[:, None] == jnp.arange(N_EXPERTS, dtype=jnp.int32)[None, :]).astype(jnp.int32)
    csum = jnp.cumsum(onehot, axis=0)
    rank = jnp.sum(csum * onehot, axis=1) - 1
    counts = csum[-1]
    padded = ((counts + tm - 1) // tm) * tm
    ends = jnp.cumsum(padded)
    starts = ends - padded
    dest = starts[e_flat] + rank
    m_pad = TOP_K * T + N_EXPERTS * tm
    token = jnp.tile(jnp.arange(T, dtype=jnp.int32), TOP_K)
    src_tok = jnp.zeros((m_pad,), jnp.int32).at[dest].set(token)
    tile_row = jnp.arange(m_pad // tm, dtype=jnp.int32) * tm
    gid = jnp.minimum(jnp.searchsorted(ends, tile_row, side="right").astype(jnp.int32), N_EXPERTS - 1)
    gid = gid + layer_moe * N_EXPERTS

    xs = gather_rows(h_f32, src_tok, tm, BF16)
    g = gmm(xs, w_gate, gid, tm=tm, tn=E // 2, n_tiles=2, out_dtype=BF16, epilogue="silu", name="moe_gate")
    hh = gmm(xs, w_up, gid, tm=tm, tn=E // 2, n_tiles=2, out_dtype=BF16, mul=g, name="moe_up")
    y = gmm(hh, w_down, gid, tm=tm, tn=D // 2, n_tiles=2, out_dtype=F32, name="moe_down")
    yg = gather_rows(y, dest, ROW_TILE, F32).reshape(TOP_K, T, D)
    return moe_combine(rows, yg, rw)


def kernel(x, c, ctx, c_ctx, w_mod, b_mod, g_mix_pre, g_mix_post, g_ffn_pre, g_ffn_post, w_in, w_out, attn_sink,
           hy_short, hy_w1, hy_b1, hy_w2, hy_b2, hy_freq, hy_w3, hy_bias, ml_gate_bias, ml_norm, mla_q_norm,
           mla_w_uq, mla_kv_norm, mla_w_ukv, ffn_w_gate, ffn_w_up, ffn_w_down, moe_router, moe_w_gate, moe_w_up,
           moe_w_down):
    B, S, D = x.shape
    NC = ctx.shape[1]
    depth = w_mod.shape[0]
    rows = _Rows(B, S, NC, D)
    T = rows.T
    TM = 512 if T % 512 == 0 else ROW_TILE
    n_mt = T // TM
    ffn_dim = ffn_w_gate.shape[2]

    xs = jnp.concatenate([x.reshape(B * S, D), ctx.reshape(B * NC, D)], axis=0)

    cm = jnp.concatenate([c, c_ctx[None, :]], axis=0)
    cm = jnp.pad(jax.nn.silu(cm), ((0, 16 - (B + 1)), (0, 0))).astype(BF16)
    mod_all = gmm(jnp.tile(cm, (depth, 1)), w_mod, jnp.arange(depth, dtype=jnp.int32), tm=16, tn=1536,
                  n_tiles=N_MOD * D // 1536, out_dtype=F32, name="adaln")
    mods = (mod_all.reshape(depth, 16, N_MOD * D)[:, :B + 1] + b_mod[:, None, :]).reshape(depth, B + 1, N_MOD, 1, D)

    def rope_table(rot_dim, reps, lane0, width):
        cos, sin = _axial_tables(S, rot_dim)
        cos, sin = jnp.tile(cos, (1, reps)), jnp.tile(sin, (1, reps))
        padw = ((0, 0), (lane0, width - lane0 - cos.shape[1]))
        cos = jnp.pad(cos, padw, constant_values=1.0)
        sin = jnp.pad(sin, padw)
        cos = jnp.concatenate([jnp.tile(cos, (B, 1)), jnp.ones((B * NC, width), F32)], axis=0)
        sin = jnp.concatenate([jnp.tile(sin, (B, 1)), jnp.zeros((B * NC, width), F32)], axis=0)
        return cos, sin

    cos_a, sin_a = rope_table(64, 2, 0, LANE)
    cos_m, sin_m = rope_table(MLA_ROPE, 1, MLA_NOPE, LANE)
    hy_consts = {L: _dft_matrices(L) + (_filter_feats(L),) for L in (S, NC)}
    deltas = jnp.abs(jnp.linspace(math.log(HY_DECAY_TARGET) / HY_SLOW_DECAY,
                                  math.log(HY_DECAY_TARGET) / HY_FAST_DECAY, 512, dtype=F32))[None, :]

    tail = w_in[:, :, 4368:]
    w_tail = jnp.concatenate([tail[:, :, :768], jnp.zeros((depth, D, 64), F32), tail[:, :, 768:800],
                              jnp.zeros((depth, D, 32), F32)], axis=-1)
    wq = mla_w_uq.reshape(depth, -1, MLA_HEADS, MLA_NOPE + MLA_ROPE)
    wq = jnp.pad(wq, ((0, 0), (0, 0), (0, 0), (0, LANE - MLA_NOPE - MLA_ROPE))).reshape(depth, -1, MLA_HEADS * LANE)
    wkv = mla_w_ukv.reshape(depth, -1, MLA_HEADS, 2 * 64)
    wk = jnp.pad(wkv[..., :MLA_NOPE], ((0, 0), (0, 0), (0, 0), (0, LANE - MLA_NOPE)))
    wk = wk.reshape(depth, -1, MLA_HEADS * LANE)
    wv = wkv[..., MLA_NOPE:].reshape(depth, -1, MLA_HEADS * 64)
    w1p = jnp.pad(hy_w1, ((0, 0), (0, LANE - hy_w1.shape[1]), (0, 0)))
    gbias = jnp.pad(ml_gate_bias.reshape(depth, 1, 4 * ML_HEADS), ((0, 0), (0, 0), (0, LANE - 4 * ML_HEADS)))
    router = jnp.pad(moe_router, ((0, 0), (0, 0), (0, LANE - N_EXPERTS)))
    n_moe = moe_w_gate.shape[0]
    mw_gate = moe_w_gate.reshape(n_moe * N_EXPERTS, D, -1)
    mw_up = moe_w_up.reshape(n_moe * N_EXPERTS, D, -1)
    mw_down = moe_w_down.reshape(n_moe * N_EXPERTS, -1, D)
    vec = lambda a, l: a[l][None, :]

    h = norm_mod(rows, xs, vec(g_mix_pre, 0), mods[0], 0, 1)
    for layer in range(depth):
        gid = jnp.full((n_mt,), layer, jnp.int32)
        ml = mods[layer]
        p1 = gmm(h, w_in, gid, tm=TM, tn=768, n_tiles=6, out_dtype=F32, name="proj_in")
        p2 = gmm(h, w_tail, gid, tm=TM, tn=896, n_tiles=1, out_dtype=F32, name="proj_tail")

        qkv = attn_rope(rows, p1, cos_a, sin_a)
        a_l = window_attn(rows, qkv, attn_sink[layer])
        a_c = ctx_attn(rows, qkv, qkv, qkv, attn_sink[layer], q_col=0, k_col=4, v_col=5, heads=A_HEADS,
                       kv_heads=A_KV_HEADS, dk=64, dv=64, use_sink=True, name="window_ctx")

        hy_w = (hy_short[layer], w1p[layer], vec(hy_b1, layer), hy_w2[layer], vec(hy_b2, layer),
                vec(hy_freq, layer), hy_w3[layer], deltas, hy_bias[layer][:, None, :])
        b_l = hyena_seq(p1, S, 0, B, hy_consts[S], hy_w, tm=512, tl=512)
        b_c = hyena_seq(p1, NC, B * S // NC, B, hy_consts[NC], hy_w, tm=NC, tl=NC)

        hf = mlstm_dir(rows, p1, gbias[layer], backward=False)
        hb = mlstm_dir(rows, p1, gbias[layer], backward=True)
        m_all = mlstm_finish(rows, hf, hb, p1, vec(ml_norm, layer))

        qn = rmsnorm_cols(rows, p2, vec(mla_q_norm, layer), 512, 0)
        kvn = rmsnorm_cols(rows, p2, vec(mla_kv_norm, layer), 256, 2)
        q_up = gmm(qn, wq, gid, tm=TM, tn=MLA_HEADS * LANE, n_tiles=1, out_dtype=F32, name="mla_uq")
        k_up = gmm(kvn, wk, gid, tm=TM, tn=MLA_HEADS * LANE, n_tiles=1, out_dtype=F32, name="mla_uk")
        vm = gmm(kvn, wv, gid, tm=TM, tn=MLA_HEADS * 64, n_tiles=1, out_dtype=BF16, name="mla_uv")
        qm, km = mla_prep(rows, q_up, k_up, p2, cos_m, sin_m)
        d_l = mla_attn(rows, qm, km, vm)
        d_c = ctx_attn(rows, qm, km, vm, attn_sink[layer], q_col=0, k_col=0, v_col=0, heads=MLA_HEADS,
                       kv_heads=MLA_HEADS, dk=LANE, dv=64, use_sink=False, name="mla_ctx")

        ycat = jnp.concatenate([jnp.concatenate([a_l, a_c], axis=0), jnp.concatenate([b_l, b_c], axis=0),
                                m_all, jnp.concatenate([d_l, d_c], axis=0)], axis=1)
        y = gmm(ycat, w_out, gid, tm=TM, tn=1024, n_tiles=D // 1024, out_dtype=F32, name="proj_out")

        i = layer // 2
        nxt = (vec(g_ffn_pre, layer), ml, 3, 4)
        if layer % 2 == 0:
            xs, h2 = post(rows, xs, y, vec(g_mix_post, layer), ml, 2, nxt=nxt)
            gi = jnp.full((n_mt,), i, jnp.int32)
            g = gmm(h2, ffn_w_gate, gi, tm=TM, tn=ffn_dim // 4, n_tiles=4, out_dtype=BF16, epilogue="silu",
                    name="ffn_gate")
            hh = gmm(h2, ffn_w_up, gi, tm=TM, tn=ffn_dim // 4, n_tiles=4, out_dtype=BF16, mul=g, name="ffn_up")
            gi2 = jnp.full((T // ROW_TILE,), i, jnp.int32)
            y2 = gmm(hh, ffn_w_down, gi2, tm=ROW_TILE, tn=512, n_tiles=D // 512, out_dtype=F32, name="ffn_down")
        else:
            xs, h2, h2f, rw, ri = post(rows, xs, y, vec(g_mix_post, layer), ml, 2, nxt=nxt, router=router[i])
            y2 = moe_ffn(rows, h2f, rw, ri, mw_gate, mw_up, mw_down, i)
        if layer + 1 < depth:
            xs, h = post(rows, xs, y2, vec(g_ffn_post, layer), ml, 5,
                         nxt=(vec(g_mix_pre, layer + 1), mods[layer + 1], 0, 1))
        else:
            (xs,) = post(rows, xs, y2, vec(g_ffn_post, layer), ml, 5)
    return xs[:B * S].reshape(B, S, D)
```

```python
import functools
import math

import numpy as np
import jax
import jax.numpy as jnp
from jax import lax
from jax.experimental import pallas as pl
from jax.experimental.pallas import tpu as pltpu

F32 = jnp.float32
BF16 = jnp.bfloat16
HI = lax.Precision.HIGHEST

EPS = 1e-6
ROPE_BASE = 10000.0
GRID_W = 64
BLOCK = 128
WINDOW = 128
N_MOD = 6
A_HEADS, A_KV_HEADS = 8, 2
HY_ORDER, HY_SHORT, HY_POS_FREQS = 2, 3, 8
HY_DECAY_TARGET, HY_FAST_DECAY, HY_SLOW_DECAY = 1e-2, 0.3, 1.5
ML_HEADS, ML_CHUNK = 4, 128
MLA_HEADS, MLA_NOPE, MLA_ROPE = 8, 64, 32
N_EXPERTS, TOP_K = 8, 2

LANE = 128
ROW_TILE = 256
NEG = -1e30
VMEM_LIMIT = 56 * 1024 * 1024


def _cp(sem, vmem=VMEM_LIMIT):
    return pltpu.CompilerParams(dimension_semantics=sem, vmem_limit_bytes=vmem)


def _gmm_kernel(gid_ref, *refs, n_a, cast, epilogue, has_mul, w_cols, tn, n_off):
    a_refs, w_ref, rest = refs[:n_a], refs[n_a], refs[n_a + 1:]
    if has_mul:
        mul_ref, rest = rest[0], rest[1:]
    o_ref = rest[0]
    n, m = pl.program_id(0), pl.program_id(1)
    if cast:
        wbf_ref = rest[1]
        prev = gid_ref[jnp.maximum(m - 1, 0)]

        @pl.when((m == 0) | (gid_ref[m] != prev))
        def _():
            w = w_ref[...]
            if w_cols is not None:
                col = (n + n_off) * tn + lax.broadcasted_iota(jnp.int32, w.shape, 1)
                w = jnp.where(col < w_cols, w, 0.0)
            wbf_ref[...] = w.astype(BF16)

        w_at = lambda lo, hi: wbf_ref[lo:hi, :]
    else:
        w_at = lambda lo, hi: w_ref[lo:hi, :]
    acc, lo = None, 0
    for a_ref in a_refs:
        k = a_ref.shape[1]
        part = jnp.dot(a_ref[...], w_at(lo, lo + k), preferred_element_type=F32)
        acc = part if acc is None else acc + part
        lo += k
    if epilogue == "silu":
        acc = acc * jax.nn.sigmoid(acc)
    if has_mul:
        acc = acc * mul_ref[...].astype(F32)
    o_ref[...] = acc.astype(o_ref.dtype)


def gmm(a, w, gid, *, tm, tn, n_tiles, out_dtype, n_off=0, epilogue=None, mul=None, m_rows=None, name="gmm"):
    a_list = list(a) if isinstance(a, (list, tuple)) else [a]
    M = a_list[0].shape[0] if m_rows is None else m_rows
    K = sum(p.shape[1] for p in a_list)
    mt = M // tm
    assert mt * tm == M and w.shape[1] == K and gid.shape == (mt,)
    cast = w.dtype != BF16
    partial_cols = (n_off + n_tiles) * tn > w.shape[2]
    assert cast or not partial_cols
    in_specs = [pl.BlockSpec((tm, p.shape[1]), lambda n, m, g: (m, 0)) for p in a_list]
    in_specs.append(pl.BlockSpec((None, K, tn), lambda n, m, g: (g[m], 0, n + n_off)))
    args = a_list + [w]
    if mul is not None:
        in_specs.append(pl.BlockSpec((tm, tn), lambda n, m, g: (m, n)))
        args.append(mul)
    kern = functools.partial(_gmm_kernel, n_a=len(a_list), cast=cast, epilogue=epilogue, has_mul=mul is not None,
                             w_cols=w.shape[2] if partial_cols else None, tn=tn, n_off=n_off)
    return pl.pallas_call(
        kern,
        grid_spec=pltpu.PrefetchScalarGridSpec(
            num_scalar_prefetch=1, grid=(n_tiles, mt), in_specs=in_specs,
            out_specs=pl.BlockSpec((tm, tn), lambda n, m, g: (m, n)),
            scratch_shapes=[pltpu.VMEM((K, tn), BF16)] if cast else []),
        out_shape=jax.ShapeDtypeStruct((M, n_tiles * tn), out_dtype),
        compiler_params=_cp(("arbitrary", "arbitrary")),
        name=name,
    )(gid, *args)


def _rms(v, g):
    return v * lax.rsqrt(jnp.mean(v * v, axis=-1, keepdims=True) + EPS) * g


def _norm_mod_kernel(x_ref, g_ref, sh_ref, sc_ref, h_ref):
    h_ref[...] = (_rms(x_ref[...], g_ref[...]) * (1.0 + sc_ref[...]) + sh_ref[...]).astype(h_ref.dtype)


def _post_kernel(*refs, with_next, with_router):
    x_ref, y_ref, gp_ref, gate_ref = refs[:4]
    refs = refs[4:]
    xn = x_ref[...] + gate_ref[...] * _rms(y_ref[...], gp_ref[...])
    if not with_next:
        refs[0][...] = xn
        return
    gn_ref, sh_ref, sc_ref = refs[:3]
    refs = refs[3:]
    if with_router:
        r_ref, refs = refs[0], refs[1:]
    refs[0][...] = xn
    h = _rms(xn, gn_ref[...]) * (1.0 + sc_ref[...]) + sh_ref[...]
    refs[1][...] = h.astype(BF16)
    if with_router:
        hf_ref, rw_ref, ri_ref = refs[2:5]
        hf_ref[...] = h
        logits = jnp.dot(h, r_ref[...], preferred_element_type=F32, precision=HI)
        lane = lax.broadcasted_iota(jnp.int32, logits.shape, 1)
        logits = jnp.where(lane < N_EXPERTS, logits, -jnp.inf)
        v1 = jnp.max(logits, axis=1, keepdims=True)
        i1 = jnp.min(jnp.where(logits == v1, lane, LANE), axis=1, keepdims=True)
        l2 = jnp.where(lane == i1, -jnp.inf, logits)
        v2 = jnp.max(l2, axis=1, keepdims=True)
        i2 = jnp.min(jnp.where(l2 == v2, lane, LANE), axis=1, keepdims=True)
        e = jnp.exp(v2 - v1)
        w1 = 1.0 / (1.0 + e)
        w2 = e / (1.0 + e)
        rw_ref[...] = jnp.where(lane == 0, w1, jnp.where(lane == 1, w2, 0.0))
        ri_ref[...] = jnp.where(lane == 0, i1, jnp.where(lane == 1, i2, 0))


class _Rows:
    def __init__(self, B, S, NC, D):
        self.B, self.S, self.NC, self.D = B, S, NC, D
        self.T = B * S + B * NC
        assert S % ROW_TILE == 0 and NC % ROW_TILE == 0
        self.nt = self.T // ROW_TILE

    def mod_row(self, i):
        n_lat = self.B * self.S // ROW_TILE
        return jnp.where(i < n_lat, i // (self.S // ROW_TILE), self.B)


def _row_spec(D):
    return pl.BlockSpec((ROW_TILE, D), lambda i: (i, 0))


def _vec_spec(D):
    return pl.BlockSpec((1, D), lambda i: (0, 0))


def _mod_spec(rows, j):
    return pl.BlockSpec((None, None, 1, rows.D), lambda i: (rows.mod_row(i), j, 0, 0))


def norm_mod(rows, x, g, mods, j_shift, j_scale):
    D = rows.D
    return pl.pallas_call(
        _norm_mod_kernel, grid=(rows.nt,),
        in_specs=[_row_spec(D), _vec_spec(D), _mod_spec(rows, j_shift), _mod_spec(rows, j_scale)],
        out_specs=_row_spec(D),
        out_shape=jax.ShapeDtypeStruct((rows.T, D), BF16),
        compiler_params=_cp(("parallel",)), name="norm_mod",
    )(x, g, mods, mods)


def post(rows, x, y, g_post, mods, j_gate, nxt=None, router=None, latent_only=False):
    D = rows.D
    T = rows.B * rows.S if latent_only else rows.T
    in_specs = [_row_spec(D), _row_spec(D), _vec_spec(D), _mod_spec(rows, j_gate)]
    args = [x, y, g_post, mods]
    out_specs = [_row_spec(D)]
    out_shape = [jax.ShapeDtypeStruct((T, D), F32)]
    if nxt is not None:
        g_next, mods_next, j_shift, j_scale = nxt
        in_specs += [_vec_spec(D), _mod_spec(rows, j_shift), _mod_spec(rows, j_scale)]
        args += [g_next, mods_next, mods_next]
        out_specs.append(_row_spec(D))
        out_shape.append(jax.ShapeDtypeStruct((T, D), BF16))
        if router is not None:
            in_specs.append(pl.BlockSpec((D, LANE), lambda i: (0, 0)))
            args.append(router)
            out_specs += [_row_spec(D), _row_spec(LANE), _row_spec(LANE)]
            out_shape += [jax.ShapeDtypeStruct((T, D), F32), jax.ShapeDtypeStruct((T, LANE), F32),
                          jax.ShapeDtypeStruct((T, LANE), jnp.int32)]
    kern = functools.partial(_post_kernel, with_next=nxt is not None, with_router=router is not None)
    return pl.pallas_call(
        kern, grid=(T // ROW_TILE,), in_specs=in_specs, out_specs=out_specs, out_shape=out_shape,
        compiler_params=_cp(("parallel",)), name="post",
    )(*args)


def _rot_pairs(x, q):
    lane = lax.broadcasted_iota(jnp.int32, x.shape, 1)
    n = x.shape[1]
    return jnp.where(lane % (2 * q) < q, -pltpu.roll(x, n - q, 1), pltpu.roll(x, q, 1))


def _axial_tables(S, rot_dim):
    rows = S // GRID_W
    r = jnp.repeat(jnp.arange(rows, dtype=F32), GRID_W)
    col = jnp.tile(jnp.arange(GRID_W, dtype=F32), rows)
    half = rot_dim // 2
    inv = ROPE_BASE ** (-jnp.arange(0, half, 2, dtype=F32) / half)
    ar, ac = r[:, None] * inv, col[:, None] * inv
    ang = jnp.concatenate([ar, ar, ac, ac], axis=-1)
    return jnp.cos(ang), jnp.sin(ang)


def _attn_rope_kernel(p_ref, cos_ref, sin_ref, o_ref, *, scale):
    cos, sin = cos_ref[...], sin_ref[...]
    nq = A_HEADS * 64 // LANE
    for s in range(nq + 1):
        xs = p_ref[:, s * LANE:(s + 1) * LANE]
        r = xs * cos + _rot_pairs(xs, 16) * sin
        if s < nq:
            r = r * scale
        o_ref[:, s * LANE:(s + 1) * LANE] = r.astype(o_ref.dtype)
    o_ref[:, (nq + 1) * LANE:] = p_ref[:, (nq + 1) * LANE:].astype(o_ref.dtype)


def attn_rope(rows, p1, cos, sin):
    W = 768
    return pl.pallas_call(
        functools.partial(_attn_rope_kernel, scale=64 ** -0.5), grid=(rows.nt,),
        in_specs=[pl.BlockSpec((ROW_TILE, W), lambda i: (i, 0)), _row_spec(LANE), _row_spec(LANE)],
        out_specs=pl.BlockSpec((ROW_TILE, W), lambda i: (i, 0)),
        out_shape=jax.ShapeDtypeStruct((rows.T, W), BF16),
        compiler_params=_cp(("parallel",)), name="attn_rope",
    )(p1, cos, sin)


def _nt(a, b):
    return lax.dot_general(a, b, (((1,), (1,)), ((), ())), preferred_element_type=F32)


def _softmax_av(scores, values, sink=None):
    m = scores[0].max(axis=1, keepdims=True)
    for s in scores[1:]:
        m = jnp.maximum(m, s.max(axis=1, keepdims=True))
    if sink is not None:
        m = jnp.maximum(m, sink)
    den = jnp.exp(sink - m) if sink is not None else 0.0
    acc = None
    for s, v in zip(scores, values):
        p = jnp.exp(s - m)
        den = den + p.sum(axis=1, keepdims=True)
        pv = jnp.dot(p.astype(BF16), v, preferred_element_type=F32)
        acc = pv if acc is None else acc + pv
    return acc / den


def _window_kernel(sink_ref, q_ref, kp_ref, kc_ref, kn_ref, vp_ref, vc_ref, vn_ref, kx_ref, vx_ref, o_ref, *, nb):
    n = pl.program_id(1)
    groups = A_HEADS // A_KV_HEADS
    d = 64
    kb = jnp.concatenate([kp_ref[...], kc_ref[...], kn_ref[...]], axis=0)
    vb = jnp.concatenate([vp_ref[...], vc_ref[...], vn_ref[...]], axis=0)
    Q = groups * BLOCK
    qi = lax.broadcasted_iota(jnp.int32, (Q, 3 * BLOCK), 0) % BLOCK
    kj = lax.broadcasted_iota(jnp.int32, (Q, 3 * BLOCK), 1)
    k_abs = (n - 1) * BLOCK + kj
    valid = (jnp.abs(kj - BLOCK - qi) <= WINDOW) & (k_abs >= 0) & (k_abs < nb * BLOCK) & (n < nb)
    hrow = lax.broadcasted_iota(jnp.int32, (Q, 1), 0) // BLOCK
    outs = []
    for g in range(A_KV_HEADS):
        qg = jnp.concatenate([q_ref[:, (g * groups + h) * d:(g * groups + h + 1) * d] for h in range(groups)], axis=0)
        ksl = slice(g * d, (g + 1) * d)
        s_band = jnp.where(valid, _nt(qg, kb[:, ksl]), NEG)
        s_ctx = _nt(qg, kx_ref[:, ksl])
        sink = jnp.zeros((Q, 1), F32)
        for h in range(groups):
            sink = jnp.where(hrow == h, sink_ref[g * groups + h], sink)
        o = _softmax_av([s_band, s_ctx], [vb[:, ksl], vx_ref[:, ksl]], sink)
        outs += [o[h * BLOCK:(h + 1) * BLOCK] for h in range(groups)]
    o_ref[...] = jnp.concatenate(outs, axis=1).astype(o_ref.dtype)


def window_attn(rows, qkv, sink):
    B, S, NC = rows.B, rows.S, rows.NC
    nb, ncb = S // BLOCK, NC // BLOCK
    cb = B * S // NC

    def q_block(b, n):
        return jnp.where(n < nb, b * nb + n, B * nb + b * ncb + (n - nb))

    def kv_spec(col, off):
        return pl.BlockSpec((BLOCK, LANE), lambda b, n, s: (b * nb + jnp.clip(n + off, 0, nb - 1), col))

    in_specs = [pl.BlockSpec((BLOCK, 512), lambda b, n, s: (q_block(b, n), 0)),
                kv_spec(4, -1), kv_spec(4, 0), kv_spec(4, 1), kv_spec(5, -1), kv_spec(5, 0), kv_spec(5, 1),
                pl.BlockSpec((NC, LANE), lambda b, n, s: (cb + b, 4)),
                pl.BlockSpec((NC, LANE), lambda b, n, s: (cb + b, 5))]
    return pl.pallas_call(
        functools.partial(_window_kernel, nb=nb),
        grid_spec=pltpu.PrefetchScalarGridSpec(
            num_scalar_prefetch=1, grid=(B, nb + ncb), in_specs=in_specs,
            out_specs=pl.BlockSpec((BLOCK, 512), lambda b, n, s: (q_block(b, n), 0))),
        out_shape=jax.ShapeDtypeStruct((rows.T, 512), BF16),
        compiler_params=_cp(("parallel", "parallel")), name="window_attn",
    )(sink, *([qkv] * 9))


def _mla_prep_kernel(q_ref, k_ref, kr_ref, cos_ref, sin_ref, qo_ref, ko_ref, *, scale):
    cos, sin = cos_ref[...], sin_ref[...]
    kr = kr_ref[...]
    kr = kr * cos + _rot_pairs(kr, 8) * sin
    for h in range(MLA_HEADS):
        sl = slice(h * LANE, (h + 1) * LANE)
        qh = q_ref[:, sl]
        qo_ref[:, sl] = ((qh * cos + _rot_pairs(qh, 8) * sin) * scale).astype(qo_ref.dtype)
        ko_ref[:, sl] = (k_ref[:, sl] + kr).astype(ko_ref.dtype)


def mla_prep(rows, q, k, p2, cos, sin):
    W = MLA_HEADS * LANE
    wide = pl.BlockSpec((ROW_TILE, W), lambda i: (i, 0))
    return pl.pallas_call(
        functools.partial(_mla_prep_kernel, scale=(MLA_NOPE + MLA_ROPE) ** -0.5), grid=(rows.nt,),
        in_specs=[wide, wide, _row_spec(LANE), _row_spec(LANE), _row_spec(LANE)],
        out_specs=[wide, wide],
        out_shape=[jax.ShapeDtypeStruct((rows.T, W), BF16)] * 2,
        compiler_params=_cp(("parallel",)), name="mla_prep",
    )(q, k, p2, cos, sin)


def _mla_kernel(q_ref, kl_ref, kx_ref, vl_ref, vx_ref, o_ref, *, nq):
    def run(with_latent_keys):
        outs = []
        for j in range(2):
            sl, vs = slice(j * LANE, (j + 1) * LANE), slice(j * 64, (j + 1) * 64)
            q = q_ref[:, sl]
            scores, values = [_nt(q, kx_ref[:, sl])], [vx_ref[:, vs]]
            if with_latent_keys:
                scores.insert(0, _nt(q, kl_ref[:, sl]))
                values.insert(0, vl_ref[:, vs])
            outs.append(_softmax_av(scores, values))
        o_ref[...] = jnp.concatenate(outs, axis=1).astype(o_ref.dtype)

    i = pl.program_id(2)
    pl.when(i < nq)(lambda: run(True))
    pl.when(i >= nq)(lambda: run(False))


def mla_attn(rows, qm, km, vm, tq=256):
    B, S, NC = rows.B, rows.S, rows.NC
    cb = B * S // NC
    nq, nqc = S // tq, NC // tq

    def q_block(b, i):
        return jnp.where(i < nq, b * nq + i, B * nq + b * nqc + (i - nq))

    return pl.pallas_call(
        functools.partial(_mla_kernel, nq=nq), grid=(B, MLA_HEADS // 2, nq + nqc),
        in_specs=[pl.BlockSpec((tq, 2 * LANE), lambda b, h, i: (q_block(b, i), h)),
                  pl.BlockSpec((S, 2 * LANE), lambda b, h, i: (b, h)),
                  pl.BlockSpec((NC, 2 * LANE), lambda b, h, i: (cb + b, h)),
                  pl.BlockSpec((S, LANE), lambda b, h, i: (b, h)),
                  pl.BlockSpec((NC, LANE), lambda b, h, i: (cb + b, h))],
        out_specs=pl.BlockSpec((tq, LANE), lambda b, h, i: (q_block(b, i), h)),
        out_shape=jax.ShapeDtypeStruct((rows.T, MLA_HEADS * 64), BF16),
        compiler_params=_cp(("parallel", "parallel", "parallel")), name="mla_attn",
    )(qm, km, km, vm, vm)


def _mla_norms_kernel(x_ref, gq_ref, gkv_ref, q_ref, kv_ref, kr_ref):
    x = x_ref[...]
    q_ref[...] = _rms(x[:, 16:528], gq_ref[...]).astype(q_ref.dtype)
    kv_ref[...] = _rms(x[:, 528:784], gkv_ref[...]).astype(kv_ref.dtype)
    z = lambda w: jnp.zeros((x.shape[0], w), F32)
    kr_ref[...] = jnp.concatenate([z(MLA_NOPE), x[:, 784:816], z(LANE - MLA_NOPE - MLA_ROPE)], axis=1)


def mla_norms(rows, p2, gq, gkv):
    T = rows.T
    return pl.pallas_call(
        _mla_norms_kernel, grid=(rows.nt,),
        in_specs=[_row_spec(p2.shape[1]), _vec_spec(512), _vec_spec(256)],
        out_specs=[_row_spec(512), _row_spec(256), _row_spec(LANE)],
        out_shape=[jax.ShapeDtypeStruct((T, 512), BF16), jax.ShapeDtypeStruct((T, 256), BF16),
                   jax.ShapeDtypeStruct((T, LANE), F32)],
        compiler_params=_cp(("parallel",)), name="mla_norms",
    )(p2, gq, gkv)


def _log_sigmoid(x):
    return jnp.minimum(x, 0.0) - jnp.log(1.0 + jnp.exp(-jnp.abs(x)))


def _mlstm_kernel(*refs, backward, scale):
    nh, L = ML_HEADS, ML_CHUNK
    q_refs, k_refs, v_refs = refs[0:nh], refs[nh:2 * nh], refs[2 * nh:3 * nh]
    g_ref, gb_ref, o_ref, c_ref, n_ref, m_ref = refs[3 * nh:]
    step = pl.program_id(1)

    @pl.when(step == 0)
    def _():
        c_ref[...] = jnp.zeros_like(c_ref)
        n_ref[...] = jnp.zeros_like(n_ref)
        m_ref[...] = jnp.zeros_like(m_ref)

    t_i = lax.broadcasted_iota(jnp.int32, (L, L), 0)
    s_i = lax.broadcasted_iota(jnp.int32, (L, L), 1)
    mask = (s_i >= t_i) if backward else (s_i <= t_i)
    maskf = mask.astype(F32)
    gb = g_ref[...] + gb_ref[...]
    ls = _log_sigmoid(gb)
    gbt, lst = gb.T, ls.T
    cum_c = jnp.dot(maskf, ls, preferred_element_type=F32, precision=HI)
    cum_r = jnp.dot(lst, maskf.T, preferred_element_type=F32, precision=HI)
    last = 0 if backward else L - 1
    gi0, gf0 = (2 * nh, 3 * nh) if backward else (0, nh)
    for h in range(nh):
        gi, gf = gi0 + h, gf0 + h
        q = q_refs[h][...]
        kf = k_refs[h][...] * scale
        vf = v_refs[h][...]
        qb, kb = q.astype(BF16), kf.astype(BF16)
        cumc, cumr = cum_c[:, gf:gf + 1], cum_r[gf:gf + 1, :]
        li_r, li_c = gbt[gi:gi + 1, :], gb[:, gi:gi + 1]
        m11 = m_ref[h][0:1, 0:1]
        log_intra = jnp.where(mask, cumc - cumr + li_r, NEG)
        log_inter = cumc + m11
        m_t = jnp.maximum(log_inter, log_intra.max(axis=1, keepdims=True))
        w_inter = jnp.exp(log_inter - m_t)
        s = _nt(qb, kb) * jnp.exp(log_intra - m_t)
        cmat = c_ref[h]
        num = w_inter * _nt(qb, cmat.astype(BF16)) + jnp.dot(s.astype(BF16), vf.astype(BF16),
                                                             preferred_element_type=F32)
        den = w_inter * jnp.sum(q * n_ref[h][0:1, :], axis=1, keepdims=True) + s.sum(axis=1, keepdims=True)
        o_ref[:, h * L:(h + 1) * L] = num / jnp.maximum(jnp.abs(den), jnp.exp(-m_t))
        total = cumc[last:last + 1, :]
        log_w = total - cumc + li_c
        m_new = jnp.maximum(total + m11, log_w.max(axis=0, keepdims=True))
        decay = jnp.exp(total + m11 - m_new)
        w = jnp.exp(log_w - m_new)
        upd = lax.dot_general((vf * w).astype(BF16), kb, (((0,), (0,)), ((), ())), preferred_element_type=F32)
        c_ref[h] = decay * cmat + upd
        n_ref[h] = jnp.broadcast_to(decay * n_ref[h][0:1, :] + jnp.sum(w * kf, axis=0, keepdims=True), (8, L))
        m_ref[h] = jnp.broadcast_to(m_new, (8, LANE))


def mlstm_dir(rows, p1, gbias, backward):
    B, S, NC = rows.B, rows.S, rows.NC
    nh, L = ML_HEADS, ML_CHUNK
    ncx, ncl = NC // L, S // L
    first_ctx = B * S // L

    def row_block(b, j):
        cx = (ncx - 1 - j) if backward else j
        cl = (ncl - 1 - (j - ncx)) if backward else (j - ncx)
        return jnp.where(j < ncx, first_ctx + b * ncx + cx, b * ncl + cl)

    def col_spec(col):
        return pl.BlockSpec((L, LANE), lambda b, j: (row_block(b, j), col))

    q0 = 2304 // LANE
    in_specs = ([col_spec(q0 + h) for h in range(nh)] + [col_spec(q0 + 4 + h) for h in range(nh)]
                + [col_spec(q0 + 8 + h) for h in range(nh)]
                + [col_spec(4352 // LANE), pl.BlockSpec((1, LANE), lambda b, j: (0, 0))])
    return pl.pallas_call(
        functools.partial(_mlstm_kernel, backward=backward, scale=L ** -0.5),
        grid=(B, ncx + ncl), in_specs=in_specs,
        out_specs=pl.BlockSpec((L, nh * L), lambda b, j: (row_block(b, j), 0)),
        out_shape=jax.ShapeDtypeStruct((rows.T, nh * L), F32),
        scratch_shapes=[pltpu.VMEM((nh, L, L), F32), pltpu.VMEM((nh, 8, L), F32), pltpu.VMEM((nh, 8, LANE), F32)],
        compiler_params=_cp(("parallel", "arbitrary")), name="mlstm_bwd" if backward else "mlstm_fwd",
    )(*([p1] * (3 * nh + 1)), gbias)


def _mlstm_finish_kernel(hf_ref, hb_ref, *refs):
    o_refs, g_ref, out_ref = refs[:ML_HEADS], refs[ML_HEADS], refs[ML_HEADS + 1]
    for h in range(ML_HEADS):
        sl = slice(h * ML_CHUNK, (h + 1) * ML_CHUNK)
        hn = _rms(hf_ref[:, sl] + hb_ref[:, sl], g_ref[:, sl])
        out_ref[:, sl] = (jax.nn.sigmoid(o_refs[h][...]) * hn).astype(out_ref.dtype)


def mlstm_finish(rows, hf, hb, p1, gnorm):
    W = ML_HEADS * ML_CHUNK
    o_specs = [pl.BlockSpec((ROW_TILE, LANE), functools.partial(lambda h, i: (i, 3840 // LANE + h), h))
               for h in range(ML_HEADS)]
    return pl.pallas_call(
        _mlstm_finish_kernel, grid=(rows.nt,),
        in_specs=[_row_spec(W), _row_spec(W)] + o_specs + [_vec_spec(W)],
        out_specs=_row_spec(W),
        out_shape=jax.ShapeDtypeStruct((rows.T, W), BF16),
        compiler_params=_cp(("parallel",)), name="mlstm_finish",
    )(hf, hb, *([p1] * ML_HEADS), gnorm)


def _short_conv_kernel(p_ref, w_ref, o_ref):
    x = p_ref[...]
    L = x.shape[0]
    row = lax.broadcasted_iota(jnp.int32, x.shape, 0)
    prev = jnp.where(row == 0, 0.0, pltpu.roll(x, 1, 0))
    nxt = jnp.where(row == L - 1, 0.0, pltpu.roll(x, L - 1, 0))
    o_ref[...] = w_ref[0:1, :] * prev + w_ref[1:2, :] * x + w_ref[2:3, :] * nxt


def short_conv(p1, short_w, L, row_block0, B):
    ncol = 1536 // LANE
    per = 512 // LANE
    return pl.pallas_call(
        _short_conv_kernel, grid=(B, ncol),
        in_specs=[pl.BlockSpec((L, LANE), lambda b, c: (row_block0 + b, 768 // LANE + c)),
                  pl.BlockSpec((HY_SHORT, LANE), lambda b, c: (0, c))],
        out_specs=pl.BlockSpec((None, L, LANE), lambda b, c: (c // per, 0, b * per + c % per)),
        out_shape=jax.ShapeDtypeStruct((3, L, B * 512), F32),
        compiler_params=_cp(("parallel", "parallel")), name="hy_short_conv",
    )(p1, short_w)


def _filter_kernel(feat_ref, w1_ref, b1_ref, w2_ref, b2_ref, fr_ref, w3f_ref, w3b_ref, dl_ref, o_ref, hid_ref):
    L = feat_ref.shape[0]

    @pl.when((pl.program_id(0) == 0) & (pl.program_id(1) == 0))
    def _():
        fr = fr_ref[...]
        h1 = jnp.sin(fr * (jnp.dot(feat_ref[...], w1_ref[...], preferred_element_type=F32, precision=HI)
                           + b1_ref[...]))
        hid_ref[...] = jnp.sin(fr * (jnp.dot(h1, w2_ref[...], preferred_element_type=F32, precision=HI)
                                     + b2_ref[...]))

    hid = hid_ref[...]
    t = lax.broadcasted_iota(jnp.int32, (L, 1), 0).astype(F32) / L
    dec = jnp.exp(-t * dl_ref[...])
    hf = jnp.dot(hid, w3f_ref[...], preferred_element_type=F32, precision=HI) * dec
    hb = jnp.dot(hid, w3b_ref[...], preferred_element_type=F32, precision=HI) * dec
    hb = jnp.where(lax.broadcasted_iota(jnp.int32, hb.shape, 0) == 0, 0.0, hb)
    inv = 1.0 / (jnp.sum(jnp.abs(hf), axis=0, keepdims=True) + jnp.sum(jnp.abs(hb), axis=0, keepdims=True))
    o_ref[0] = ((hf + hb) * inv).astype(o_ref.dtype)
    o_ref[1] = ((hf - hb) * inv).astype(o_ref.dtype)


def hyena_filters(feats, w1p, b1, w2, b2, fr, w3, deltas, L):
    hid = w2.shape[0]
    per = 512 // LANE
    full = lambda shape: pl.BlockSpec(shape, lambda o, c: (0, 0))
    return pl.pallas_call(
        _filter_kernel, grid=(HY_ORDER, per),
        in_specs=[full((L, LANE)), full((LANE, hid)), full((1, hid)), full((hid, hid)), full((1, hid)),
                  full((1, hid)),
                  pl.BlockSpec((hid, LANE), lambda o, c: (0, (o * 2) * per + c)),
                  pl.BlockSpec((hid, LANE), lambda o, c: (0, (o * 2 + 1) * per + c)),
                  pl.BlockSpec((1, LANE), lambda o, c: (0, c))],
        out_specs=pl.BlockSpec((2, L, LANE), lambda o, c: (0, 0, o * per + c)),
        out_shape=jax.ShapeDtypeStruct((2, L, HY_ORDER * 512), BF16),
        scratch_shapes=[pltpu.VMEM((L, hid), F32)],
        compiler_params=_cp(("arbitrary", "arbitrary")), name="hy_filters",
    )(feats, w1p, b1, w2, b2, fr, w3, w3, deltas)


def _spec_mul_kernel(z_ref, g_ref, y_ref, *, scale, B):
    gre, gim = g_ref[0], g_ref[1]
    W = gre.shape[1]
    for b in range(B):
        sl = slice(b * W, (b + 1) * W)
        zre, zim = z_ref[0, :, sl], z_ref[1, :, sl]
        y_ref[0, :, sl] = ((zre * gre - zim * gim) * scale).astype(y_ref.dtype)
        y_ref[1, :, sl] = ((zre * gim + zim * gre) * scale).astype(y_ref.dtype)


def spec_mul(z, g, order, L, B, tk):
    return pl.pallas_call(
        functools.partial(_spec_mul_kernel, scale=1.0 / L, B=B), grid=(L // tk,),
        in_specs=[pl.BlockSpec((2, tk, B * 512), lambda i: (0, i, 0)),
                  pl.BlockSpec((2, tk, 512), lambda i: (0, i, order))],
        out_specs=pl.BlockSpec((2, tk, B * 512), lambda i: (0, i, 0)),
        out_shape=jax.ShapeDtypeStruct((2, L, B * 512), BF16),
        compiler_params=_cp(("parallel",)), name="hy_spec_mul",
    )(z, g)


def _gate_kernel(x_ref, c_ref, z_ref, b_ref, o_ref):
    o_ref[...] = (x_ref[...] * (c_ref[...] + b_ref[...] * z_ref[...])).astype(o_ref.dtype)


def hyena_gate(u3, which_x, conv, z_arr, which_z, bias, order, L, B, tl, final=None):
    nl = L // tl
    n_fill = 0
    ci = lambda i: jnp.minimum(i, nl - 1)
    in_specs = [pl.BlockSpec((None, tl, 512), lambda i, b: (which_x, ci(i), b)),
                pl.BlockSpec((tl, 512), lambda i, b: (ci(i), b)),
                pl.BlockSpec((None, tl, 512), lambda i, b: (which_z, ci(i), b)),
                pl.BlockSpec((None, 1, 512), lambda i, b: (order, 0, 0))]
    args = [u3, conv, z_arr, bias]
    aliases = {}
    kern = _gate_kernel
    if final is not None:
        T, row0, into = final
        out_shape = jax.ShapeDtypeStruct((T, 512), BF16)
        if into is not None:
            out_spec = pl.BlockSpec((tl, 512), lambda i, b: (row0 // tl + b * nl + i, 0))
            in_specs.append(pl.BlockSpec(memory_space=pl.ANY))
            args.append(into)
            aliases = {4: 0}
            kern = lambda x, c, z, b, into_ref, o: _gate_kernel(x, c, z, b, o)
        else:
            assert row0 == 0
            n_fill = (T - B * L) // tl
            out_spec = pl.BlockSpec((tl, 512), lambda i, b: (jnp.where(i < nl, b * nl + i, B * nl + (i - nl)), 0))

            def kern(x, c, z, b, o):
                i = pl.program_id(0)
                pl.when(i < nl)(lambda: _gate_kernel(x, c, z, b, o))

                @pl.when(i >= nl)
                def _():
                    o[...] = jnp.zeros_like(o)
    else:
        out_spec = pl.BlockSpec((None, tl, 512), lambda i, b: (0, i, b))
        out_shape = jax.ShapeDtypeStruct((1, L, B * 512), F32)
    return pl.pallas_call(
        kern, grid=(nl + n_fill, B), in_specs=in_specs, out_specs=out_spec, out_shape=out_shape,
        input_output_aliases=aliases,
        compiler_params=_cp(("arbitrary", "arbitrary")), name="hy_gate",
    )(*args)


def _dft_gen_kernel(ca_ref, sa_ref, cb_ref, sb_ref, ca2_ref, sa2_ref, cb2_ref, sb2_ref, fwd_ref, inv_ref):
    L = cb_ref.shape[1]
    ca, sa, cb, sb = ca_ref[...], sa_ref[...], cb_ref[...], sb_ref[...]
    fwd_ref[0] = (ca * cb - sa * sb).astype(fwd_ref.dtype)
    fwd_ref[1] = (-(sa * cb + ca * sb)).astype(fwd_ref.dtype)
    ca, sa, cb, sb = ca2_ref[...], sa2_ref[...], cb2_ref[...], sb2_ref[...]
    inv_ref[:, :L] = (ca * cb - sa * sb).astype(inv_ref.dtype)
    inv_ref[:, L:] = (-(sa * cb + ca * sb)).astype(inv_ref.dtype)


def _dft_matrices(L, tb):
    nb = L // tb
    col = jnp.arange(L, dtype=jnp.int32)[None, :]
    r0 = (jnp.arange(nb, dtype=jnp.int32) * tb)[:, None]
    i = jnp.arange(tb, dtype=jnp.int32)[:, None]
    unit = math.pi / (2 * L)
    trig = lambda ph: (jnp.cos((ph % (4 * L)).astype(F32) * unit), jnp.sin((ph % (4 * L)).astype(F32) * unit))
    ca, sa = trig((2 * r0 + 1) * col)
    cb, sb = trig(2 * i * col)
    ca2, sa2 = trig((2 * col + 1) * r0)
    cb2, sb2 = trig((2 * col + 1) * i)
    blk = lambda: pl.BlockSpec((None, 1, L), lambda j: (j, 0, 0))
    shared = lambda: pl.BlockSpec((tb, L), lambda j: (0, 0))
    a3 = lambda a: a[:, None, :]
    fwd, inv = pl.pallas_call(
        _dft_gen_kernel, grid=(nb,),
        in_specs=[blk(), blk(), shared(), shared(), blk(), blk(), shared(), shared()],
        out_specs=[pl.BlockSpec((2, tb, L), lambda j: (0, j, 0)), pl.BlockSpec((tb, 2 * L), lambda j: (j, 0))],
        out_shape=[jax.ShapeDtypeStruct((2, L, L), BF16), jax.ShapeDtypeStruct((L, 2 * L), BF16)],
        compiler_params=_cp(("parallel",)), name="hy_dft_gen",
    )(a3(ca), a3(sa), cb, sb, a3(ca2), a3(sa2), cb2, sb2)
    return fwd.reshape(2 * L, L), inv


def _filter_feats(L):
    t = jnp.arange(L, dtype=F32) / L
    kf = jnp.arange(1, HY_POS_FREQS + 1, dtype=F32)
    ang = 2.0 * math.pi * t[:, None] * kf
    feats = jnp.concatenate([t[:, None], jnp.sin(ang), jnp.cos(ang)], axis=-1)
    return jnp.pad(feats, ((0, 0), (0, LANE - feats.shape[1])))


def hyena_seq(p1, L, row_block0, B, consts, wts, tm, tl, T, into=None):
    fwd, inv, feats = consts
    short_w, w1p, b1, w2, b2, fr, w3, deltas, bias = wts
    u3 = short_conv(p1, short_w, L, row_block0, B)
    filt = hyena_filters(feats, w1p, b1, w2, b2, fr, w3, deltas, L)
    gid_f = jnp.repeat(jnp.arange(2, dtype=jnp.int32), L // tm)
    zeros_f = jnp.zeros((2 * L // tm,), jnp.int32)
    zeros_i = jnp.zeros((L // tm,), jnp.int32)
    NW = B * 512
    gspec = gmm(fwd, filt, gid_f, tm=tm, tn=512, n_tiles=HY_ORDER, out_dtype=F32, name="hy_dft_filter")
    gspec = gspec.reshape(2, L, HY_ORDER * 512)
    z_arr, which_z = u3, 0
    out = None
    for o in range(HY_ORDER):
        zs = gmm(fwd, z_arr, zeros_f + which_z, tm=tm, tn=512, n_tiles=NW // 512, out_dtype=F32, name="hy_dft_fwd")
        y = spec_mul(zs.reshape(2, L, NW), gspec, o, L, B, tl)
        conv = gmm(inv, y.reshape(1, 2 * L, NW), zeros_i, tm=tm, tn=512, n_tiles=NW // 512, out_dtype=F32,
                   name="hy_dft_inv")
        final = (T, row_block0 * L, into) if o == HY_ORDER - 1 else None
        out = hyena_gate(u3, 1 + o, conv, z_arr, which_z, bias, o, L, B, tl, final)
        z_arr, which_z = out, 0
    return out


def _gather_kernel(idx_ref, src_ref, o_ref, buf_ref, sem, *, tm):
    i, n = pl.program_id(0), pl.num_programs(0)

    def issue_tile(tile, slot):
        def issue(r, carry):
            pltpu.make_async_copy(src_ref.at[pl.ds(idx_ref[tile * tm + r], 1)], buf_ref.at[slot, pl.ds(r, 1)],
                                  sem.at[slot]).start()
            return carry

        lax.fori_loop(0, tm, issue, 0, unroll=8)

    @pl.when(i == 0)
    def _():
        issue_tile(0, 0)

    @pl.when(i + 1 < n)
    def _():
        issue_tile(i + 1, (i + 1) % 2)

    slot = i % 2
    pltpu.make_async_copy(src_ref.at[pl.ds(0, tm)], buf_ref.at[slot], sem.at[slot]).wait()
    o_ref[...] = buf_ref[slot].astype(o_ref.dtype)


def gather_rows(src, idx, tm, out_dtype):
    M = idx.shape[0]
    D = src.shape[1]
    return pl.pallas_call(
        functools.partial(_gather_kernel, tm=tm),
        grid_spec=pltpu.PrefetchScalarGridSpec(
            num_scalar_prefetch=1, grid=(M // tm,),
            in_specs=[pl.BlockSpec(memory_space=pl.ANY)],
            out_specs=pl.BlockSpec((tm, D), lambda i, idx: (i, 0)),
            scratch_shapes=[pltpu.VMEM((2, tm, D), src.dtype), pltpu.SemaphoreType.DMA((2,))]),
        out_shape=jax.ShapeDtypeStruct((M, D), out_dtype),
        compiler_params=_cp(("arbitrary",)), name="gather_rows",
    )(idx, src)


def _combine_kernel(y0_ref, y1_ref, w_ref, o_ref):
    w = w_ref[...]
    o_ref[...] = w[:, 0:1] * y0_ref[...] + w[:, 1:2] * y1_ref[...]


def moe_combine(rows, yg, rw):
    D, T = rows.D, rows.T
    return pl.pallas_call(
        _combine_kernel, grid=(rows.nt,),
        in_specs=[pl.BlockSpec((None, ROW_TILE, D), lambda i: (0, i, 0)),
                  pl.BlockSpec((None, ROW_TILE, D), lambda i: (1, i, 0)), _row_spec(LANE)],
        out_specs=_row_spec(D), out_shape=jax.ShapeDtypeStruct((T, D), F32),
        compiler_params=_cp(("parallel",)), name="moe_combine",
    )(yg, yg, rw)


def moe_ffn(rows, h_f32, rw, ri, w_gate, w_up, w_down, layer_moe, tm=256):
    T, D = rows.T, rows.D
    E = w_gate.shape[2]
    e_flat = ri[:, :TOP_K].T.reshape(-1)
    onehot = (e_flat[:, None] == jnp.arange(N_EXPERTS, dtype=jnp.int32)[None, :]).astype(jnp.int32)
    csum = jnp.cumsum(onehot, axis=0)
    rank = jnp.sum(csum * onehot, axis=1) - 1
    counts = csum[-1]
    padded = ((counts + tm - 1) // tm) * tm
    ends = jnp.cumsum(padded)
    starts = ends - padded
    dest = starts[e_flat] + rank
    m_pad = TOP_K * T + N_EXPERTS * tm
    token = jnp.tile(jnp.arange(T, dtype=jnp.int32), TOP_K)
    src_tok = jnp.zeros((m_pad,), jnp.int32).at[dest].set(token)
    tile_row = jnp.arange(m_pad // tm, dtype=jnp.int32) * tm
    gid = jnp.minimum(jnp.sum((tile_row[:, None] >= ends[None, :]).astype(jnp.int32), axis=1), N_EXPERTS - 1)
    gid = gid + layer_moe * N_EXPERTS

    xs = gather_rows(h_f32, src_tok, tm, BF16)
    g = gmm(xs, w_gate, gid, tm=tm, tn=E // 2, n_tiles=2, out_dtype=BF16, epilogue="silu", name="moe_gate")
    hh = gmm(xs, w_up, gid, tm=tm, tn=E // 2, n_tiles=2, out_dtype=BF16, mul=g, name="moe_up")
    y = gmm(hh, w_down, gid, tm=tm, tn=D // 2, n_tiles=2, out_dtype=F32, name="moe_down")
    yg = gather_rows(y, dest, ROW_TILE, F32).reshape(TOP_K, T, D)
    return moe_combine(rows, yg, rw)


def kernel(x, c, ctx, c_ctx, w_mod, b_mod, g_mix_pre, g_mix_post, g_ffn_pre, g_ffn_post, w_in, w_out, attn_sink,
           hy_short, hy_w1, hy_b1, hy_w2, hy_b2, hy_freq, hy_w3, hy_bias, ml_gate_bias, ml_norm, mla_q_norm,
           mla_w_uq, mla_kv_norm, mla_w_ukv, ffn_w_gate, ffn_w_up, ffn_w_down, moe_router, moe_w_gate, moe_w_up,
           moe_w_down):
    B, S, D = x.shape
    NC = ctx.shape[1]
    depth = w_mod.shape[0]
    rows = _Rows(B, S, NC, D)
    T = rows.T
    TM = 512 if T % 512 == 0 else ROW_TILE
    n_mt = T // TM
    ffn_dim = ffn_w_gate.shape[2]

    xs = jnp.concatenate([x.reshape(B * S, D), ctx.reshape(B * NC, D)], axis=0)

    cm = jnp.concatenate([c, c_ctx[None, :]], axis=0)
    cm = jnp.pad(jax.nn.silu(cm), ((0, 16 - (B + 1)), (0, 0))).astype(BF16)
    mod_all = gmm(jnp.tile(cm, (depth, 1)), w_mod, jnp.arange(depth, dtype=jnp.int32), tm=16, tn=1536,
                  n_tiles=N_MOD * D // 1536, out_dtype=F32, name="adaln")
    mods = (mod_all.reshape(depth, 16, N_MOD * D)[:, :B + 1] + b_mod[:, None, :]).reshape(depth, B + 1, N_MOD, 1, D)

    def rope_table(rot_dim, reps, lane0, width):
        cos, sin = _axial_tables(S, rot_dim)
        cos, sin = jnp.tile(cos, (1, reps)), jnp.tile(sin, (1, reps))
        padw = ((0, 0), (lane0, width - lane0 - cos.shape[1]))
        cos = jnp.pad(cos, padw, constant_values=1.0)
        sin = jnp.pad(sin, padw)
        cos = jnp.concatenate([jnp.tile(cos, (B, 1)), jnp.ones((B * NC, width), F32)], axis=0)
        sin = jnp.concatenate([jnp.tile(sin, (B, 1)), jnp.zeros((B * NC, width), F32)], axis=0)
        return cos, sin

    cos_a, sin_a = rope_table(64, 2, 0, LANE)
    cos_m, sin_m = rope_table(MLA_ROPE, 1, MLA_NOPE, LANE)
    hy_consts = {L: _dft_matrices(L, min(L, ROW_TILE)) + (_filter_feats(L),) for L in (S, NC)}
    deltas = jnp.abs(jnp.linspace(math.log(HY_DECAY_TARGET) / HY_SLOW_DECAY,
                                  math.log(HY_DECAY_TARGET) / HY_FAST_DECAY, 512, dtype=F32))[None, :]

    wq = mla_w_uq.reshape(depth, -1, MLA_HEADS, MLA_NOPE + MLA_ROPE)
    wq = jnp.pad(wq, ((0, 0), (0, 0), (0, 0), (0, LANE - MLA_NOPE - MLA_ROPE))).reshape(depth, -1, MLA_HEADS * LANE)
    wkv = mla_w_ukv.reshape(depth, -1, MLA_HEADS, 2 * 64)
    wk = jnp.pad(wkv[..., :MLA_NOPE], ((0, 0), (0, 0), (0, 0), (0, LANE - MLA_NOPE)))
    wk = wk.reshape(depth, -1, MLA_HEADS * LANE)
    wv = wkv[..., MLA_NOPE:].reshape(depth, -1, MLA_HEADS * 64)
    w1p = jnp.pad(hy_w1, ((0, 0), (0, LANE - hy_w1.shape[1]), (0, 0)))
    gbias = jnp.pad(ml_gate_bias.reshape(depth, 1, 4 * ML_HEADS), ((0, 0), (0, 0), (0, LANE - 4 * ML_HEADS)))
    router = jnp.pad(moe_router, ((0, 0), (0, 0), (0, LANE - N_EXPERTS)))
    n_moe = moe_w_gate.shape[0]
    mw_gate = moe_w_gate.reshape(n_moe * N_EXPERTS, D, -1)
    mw_up = moe_w_up.reshape(n_moe * N_EXPERTS, D, -1)
    mw_down = moe_w_down.reshape(n_moe * N_EXPERTS, -1, D)
    vec = lambda a, l: a[l][None, :]

    h = norm_mod(rows, xs, vec(g_mix_pre, 0), mods[0], 0, 1)
    for layer in range(depth):
        gid = jnp.full((n_mt,), layer, jnp.int32)
        ml = mods[layer]
        p1 = gmm(h, w_in, gid, tm=TM, tn=768, n_tiles=6, out_dtype=F32, name="proj_in")
        p2 = gmm(h, w_in, gid, tm=TM, tn=256, n_tiles=4, n_off=4352 // 256, out_dtype=F32, name="proj_tail")

        qkv = attn_rope(rows, p1, cos_a, sin_a)
        a_all = window_attn(rows, qkv, attn_sink[layer])

        hy_w = (hy_short[layer], w1p[layer], vec(hy_b1, layer), hy_w2[layer], vec(hy_b2, layer),
                vec(hy_freq, layer), hy_w3[layer], deltas, hy_bias[layer][:, None, :])
        b_l = hyena_seq(p1, S, 0, B, hy_consts[S], hy_w, tm=512, tl=512, T=T)
        b_all = hyena_seq(p1, NC, B * S // NC, B, hy_consts[NC], hy_w, tm=NC, tl=NC, T=T, into=b_l)

        hf = mlstm_dir(rows, p1, gbias[layer], backward=False)
        hb = mlstm_dir(rows, p1, gbias[layer], backward=True)
        m_all = mlstm_finish(rows, hf, hb, p1, vec(ml_norm, layer))

        qn, kvn, kr = mla_norms(rows, p2, vec(mla_q_norm, layer), vec(mla_kv_norm, layer))
        q_up = gmm(qn, wq, gid, tm=TM, tn=MLA_HEADS * LANE, n_tiles=1, out_dtype=F32, name="mla_uq")
        k_up = gmm(kvn, wk, gid, tm=TM, tn=MLA_HEADS * LANE, n_tiles=1, out_dtype=F32, name="mla_uk")
        vm = gmm(kvn, wv, gid, tm=TM, tn=MLA_HEADS * 64, n_tiles=1, out_dtype=BF16, name="mla_uv")
        qm, km = mla_prep(rows, q_up, k_up, kr, cos_m, sin_m)
        d_all = mla_attn(rows, qm, km, vm)

        y = gmm([a_all, b_all, m_all, d_all], w_out, gid, tm=TM, tn=1024, n_tiles=D // 1024, out_dtype=F32,
                name="proj_out")

        i = layer // 2
        nxt = (vec(g_ffn_pre, layer), ml, 3, 4)
        if layer % 2 == 0:
            xs, h2 = post(rows, xs, y, vec(g_mix_post, layer), ml, 2, nxt=nxt)
            gi = jnp.full((n_mt,), i, jnp.int32)
            g = gmm(h2, ffn_w_gate, gi, tm=TM, tn=ffn_dim // 4, n_tiles=4, out_dtype=BF16, epilogue="silu",
                    name="ffn_gate")
            hh = gmm(h2, ffn_w_up, gi, tm=TM, tn=ffn_dim // 4, n_tiles=4, out_dtype=BF16, mul=g, name="ffn_up")
            gi2 = jnp.full((T // ROW_TILE,), i, jnp.int32)
            y2 = gmm(hh, ffn_w_down, gi2, tm=ROW_TILE, tn=512, n_tiles=D // 512, out_dtype=F32, name="ffn_down")
        else:
            xs, h2, h2f, rw, ri = post(rows, xs, y, vec(g_mix_post, layer), ml, 2, nxt=nxt, router=router[i])
            y2 = moe_ffn(rows, h2f, rw, ri, mw_gate, mw_up, mw_down, i)
        if layer + 1 < depth:
            xs, h = post(rows, xs, y2, vec(g_ffn_post, layer), ml, 5,
                         nxt=(vec(g_mix_pre, layer + 1), mods[layer + 1], 0, 1))
        else:
            (xs,) = post(rows, xs, y2, vec(g_ffn_post, layer), ml, 5, latent_only=True)
    return xs.reshape(B, S, D)
```

```python
import functools
import math

import numpy as np
import jax
import jax.numpy as jnp
from jax import lax
from jax.experimental import pallas as pl
from jax.experimental.pallas import tpu as pltpu

F32 = jnp.float32
BF16 = jnp.bfloat16
HI = lax.Precision.HIGHEST

EPS = 1e-6
ROPE_BASE = 10000.0
GRID_W = 64
BLOCK = 128
WINDOW = 128
N_MOD = 6
A_HEADS, A_KV_HEADS = 8, 2
HY_ORDER, HY_SHORT, HY_POS_FREQS = 2, 3, 8
HY_DECAY_TARGET, HY_FAST_DECAY, HY_SLOW_DECAY = 1e-2, 0.3, 1.5
ML_HEADS, ML_CHUNK = 4, 128
MLA_HEADS, MLA_NOPE, MLA_ROPE = 8, 64, 32
N_EXPERTS, TOP_K = 8, 2

LANE = 128
ROW_TILE = 256
NEG = -1e30
LOG2E = math.log2(math.e)
VMEM_LIMIT = 56 * 1024 * 1024


def _cp(sem, vmem=VMEM_LIMIT):
    return pltpu.CompilerParams(dimension_semantics=sem, vmem_limit_bytes=vmem)


def _gmm_kernel(gid_ref, used_ref, *refs, n_a, cast, w_t, epilogue, has_mul, w_n, tn, n_off):
    a_refs, w_ref, rest = refs[:n_a], refs[n_a], refs[n_a + 1:]
    if has_mul:
        mul_ref, rest = rest[0], rest[1:]
    o_ref = rest[0]
    n, m = pl.program_id(0), pl.program_id(1)
    n_axis = 0 if w_t else 1
    if cast:
        wbf_ref = rest[1]
        prev = gid_ref[jnp.maximum(m - 1, 0)]

        @pl.when((m == 0) | (gid_ref[m] != prev))
        def _():
            w = w_ref[...]
            if w_n is not None:
                col = (n + n_off) * tn + lax.broadcasted_iota(jnp.int32, w.shape, n_axis)
                w = jnp.where(col < w_n, w, 0.0)
            wbf_ref[...] = w.astype(BF16)

        src = wbf_ref
    else:
        src = w_ref

    @pl.when(m < used_ref[0])
    def _():
        acc, lo = None, 0
        for a_ref in a_refs:
            k = a_ref.shape[1]
            if w_t:
                part = lax.dot_general(a_ref[...], src[:, lo:lo + k], (((1,), (1,)), ((), ())),
                                       preferred_element_type=F32)
            else:
                part = jnp.dot(a_ref[...], src[lo:lo + k, :], preferred_element_type=F32)
            acc = part if acc is None else acc + part
            lo += k
        if epilogue == "silu":
            acc = acc * jax.nn.sigmoid(acc)
        if has_mul:
            acc = acc * mul_ref[...].astype(F32)
        o_ref[...] = acc.astype(o_ref.dtype)

    @pl.when(m >= used_ref[0])
    def _():
        o_ref[...] = jnp.zeros_like(o_ref)


def gmm(a, w, gid, *, tm, tn, n_tiles, out_dtype, n_off=0, epilogue=None, mul=None, w_t=False, used=None,
        name="gmm"):
    a_list = list(a) if isinstance(a, (list, tuple)) else [a]
    M = a_list[0].shape[0]
    K = sum(p.shape[1] for p in a_list)
    mt = M // tm
    k_axis, n_axis = (2, 1) if w_t else (1, 2)
    assert mt * tm == M and w.shape[k_axis] == K and gid.shape == (mt,)
    cast = w.dtype != BF16
    partial_n = (n_off + n_tiles) * tn > w.shape[n_axis]
    assert cast or not partial_n
    if used is None:
        used = jnp.full((1,), mt, jnp.int32)
    in_specs = [pl.BlockSpec((tm, p.shape[1]), lambda n, m, g, u: (m, 0)) for p in a_list]
    if w_t:
        in_specs.append(pl.BlockSpec((None, tn, K), lambda n, m, g, u: (g[m], n + n_off, 0)))
    else:
        in_specs.append(pl.BlockSpec((None, K, tn), lambda n, m, g, u: (g[m], 0, n + n_off)))
    args = a_list + [w]
    if mul is not None:
        in_specs.append(pl.BlockSpec((tm, tn), lambda n, m, g, u: (m, n)))
        args.append(mul)
    kern = functools.partial(_gmm_kernel, n_a=len(a_list), cast=cast, w_t=w_t, epilogue=epilogue,
                             has_mul=mul is not None, w_n=w.shape[n_axis] if partial_n else None, tn=tn, n_off=n_off)
    return pl.pallas_call(
        kern,
        grid_spec=pltpu.PrefetchScalarGridSpec(
            num_scalar_prefetch=2, grid=(n_tiles, mt), in_specs=in_specs,
            out_specs=pl.BlockSpec((tm, tn), lambda n, m, g, u: (m, n)),
            scratch_shapes=[pltpu.VMEM((tn, K) if w_t else (K, tn), BF16)] if cast else []),
        out_shape=jax.ShapeDtypeStruct((M, n_tiles * tn), out_dtype),
        compiler_params=_cp(("arbitrary", "arbitrary")),
        name=name,
    )(gid, used, *args)


def _rms(v, g):
    return v * lax.rsqrt(jnp.mean(v * v, axis=-1, keepdims=True) + EPS) * g


def _norm_mod_kernel(x_ref, g_ref, sh_ref, sc_ref, h_ref):
    h_ref[...] = (_rms(x_ref[...], g_ref[...]) * (1.0 + sc_ref[...]) + sh_ref[...]).astype(h_ref.dtype)


def _post_kernel(*refs, with_next, with_router):
    x_ref, y_ref, gp_ref, gate_ref = refs[:4]
    refs = refs[4:]
    xn = x_ref[...] + gate_ref[...] * _rms(y_ref[...].astype(F32), gp_ref[...])
    if not with_next:
        refs[0][...] = xn
        return
    gn_ref, sh_ref, sc_ref = refs[:3]
    refs = refs[3:]
    if with_router:
        r_ref, refs = refs[0], refs[1:]
    refs[0][...] = xn
    h = _rms(xn, gn_ref[...]) * (1.0 + sc_ref[...]) + sh_ref[...]
    refs[1][...] = h.astype(BF16)
    if with_router:
        hf_ref, rw_ref, ri_ref = refs[2:5]
        hf_ref[...] = h
        logits = jnp.dot(h, r_ref[...], preferred_element_type=F32, precision=HI)
        lane = lax.broadcasted_iota(jnp.int32, logits.shape, 1)
        logits = jnp.where(lane < N_EXPERTS, logits, -jnp.inf)
        v1 = jnp.max(logits, axis=1, keepdims=True)
        i1 = jnp.min(jnp.where(logits == v1, lane, LANE), axis=1, keepdims=True)
        l2 = jnp.where(lane == i1, -jnp.inf, logits)
        v2 = jnp.max(l2, axis=1, keepdims=True)
        i2 = jnp.min(jnp.where(l2 == v2, lane, LANE), axis=1, keepdims=True)
        e = jnp.exp(v2 - v1)
        w1 = 1.0 / (1.0 + e)
        w2 = e / (1.0 + e)
        rw_ref[...] = jnp.where(lane == 0, w1, jnp.where(lane == 1, w2, 0.0))
        ri_ref[...] = jnp.where(lane == 0, i1, jnp.where(lane == 1, i2, 0))


class _Rows:
    def __init__(self, B, S, NC, D):
        self.B, self.S, self.NC, self.D = B, S, NC, D
        self.T = B * S + B * NC
        assert S % ROW_TILE == 0 and NC % ROW_TILE == 0
        self.nt = self.T // ROW_TILE

    def mod_row(self, i):
        n_lat = self.B * self.S // ROW_TILE
        return jnp.where(i < n_lat, i // (self.S // ROW_TILE), self.B)


def _row_spec(D):
    return pl.BlockSpec((ROW_TILE, D), lambda i: (i, 0))


def _vec_spec(D):
    return pl.BlockSpec((1, D), lambda i: (0, 0))


def _mod_spec(rows, j):
    return pl.BlockSpec((None, None, 1, rows.D), lambda i: (rows.mod_row(i), j, 0, 0))


def norm_mod(rows, x, g, mods, j_shift, j_scale):
    D = rows.D
    return pl.pallas_call(
        _norm_mod_kernel, grid=(rows.nt,),
        in_specs=[_row_spec(D), _vec_spec(D), _mod_spec(rows, j_shift), _mod_spec(rows, j_scale)],
        out_specs=_row_spec(D),
        out_shape=jax.ShapeDtypeStruct((rows.T, D), BF16),
        compiler_params=_cp(("parallel",)), name="norm_mod",
    )(x, g, mods, mods)


def post(rows, x, y, g_post, mods, j_gate, nxt=None, router=None, latent_only=False):
    D = rows.D
    T = rows.B * rows.S if latent_only else rows.T
    in_specs = [_row_spec(D), _row_spec(D), _vec_spec(D), _mod_spec(rows, j_gate)]
    args = [x, y, g_post, mods]
    out_specs = [_row_spec(D)]
    out_shape = [jax.ShapeDtypeStruct((T, D), F32)]
    if nxt is not None:
        g_next, mods_next, j_shift, j_scale = nxt
        in_specs += [_vec_spec(D), _mod_spec(rows, j_shift), _mod_spec(rows, j_scale)]
        args += [g_next, mods_next, mods_next]
        out_specs.append(_row_spec(D))
        out_shape.append(jax.ShapeDtypeStruct((T, D), BF16))
        if router is not None:
            in_specs.append(pl.BlockSpec((D, LANE), lambda i: (0, 0)))
            args.append(router)
            out_specs += [_row_spec(D), _row_spec(LANE), _row_spec(LANE)]
            out_shape += [jax.ShapeDtypeStruct((T, D), F32), jax.ShapeDtypeStruct((T, LANE), F32),
                          jax.ShapeDtypeStruct((T, LANE), jnp.int32)]
    kern = functools.partial(_post_kernel, with_next=nxt is not None, with_router=router is not None)
    return pl.pallas_call(
        kern, grid=(T // ROW_TILE,), in_specs=in_specs, out_specs=out_specs, out_shape=out_shape,
        compiler_params=_cp(("parallel",)), name="post",
    )(*args)


def _rot_pairs(x, q):
    lane = lax.broadcasted_iota(jnp.int32, x.shape, 1)
    n = x.shape[1]
    return jnp.where(lane % (2 * q) < q, -pltpu.roll(x, n - q, 1), pltpu.roll(x, q, 1))


def _axial_tables(S, rot_dim):
    rows = S // GRID_W
    r = jnp.repeat(jnp.arange(rows, dtype=F32), GRID_W)
    col = jnp.tile(jnp.arange(GRID_W, dtype=F32), rows)
    half = rot_dim // 2
    inv = ROPE_BASE ** (-jnp.arange(0, half, 2, dtype=F32) / half)
    ar, ac = r[:, None] * inv, col[:, None] * inv
    ang = jnp.concatenate([ar, ar, ac, ac], axis=-1)
    return jnp.cos(ang), jnp.sin(ang)


def _attn_rope_kernel(p_ref, cos_ref, sin_ref, o_ref, *, scale):
    cos, sin = cos_ref[...], sin_ref[...]
    nq = A_HEADS * 64 // LANE
    for s in range(nq + 1):
        xs = p_ref[:, s * LANE:(s + 1) * LANE]
        r = xs * cos + _rot_pairs(xs, 16) * sin
        if s < nq:
            r = r * scale
        o_ref[:, s * LANE:(s + 1) * LANE] = r.astype(o_ref.dtype)
    o_ref[:, (nq + 1) * LANE:] = p_ref[:, (nq + 1) * LANE:].astype(o_ref.dtype)


def attn_rope(rows, p1, cos, sin):
    W = 768
    return pl.pallas_call(
        functools.partial(_attn_rope_kernel, scale=64 ** -0.5 * LOG2E), grid=(rows.nt,),
        in_specs=[pl.BlockSpec((ROW_TILE, W), lambda i: (i, 0)), _row_spec(LANE), _row_spec(LANE)],
        out_specs=pl.BlockSpec((ROW_TILE, W), lambda i: (i, 0)),
        out_shape=jax.ShapeDtypeStruct((rows.T, W), BF16),
        compiler_params=_cp(("parallel",)), name="attn_rope",
    )(p1, cos, sin)


def _nt(a, b):
    return lax.dot_general(a, b, (((1,), (1,)), ((), ())), preferred_element_type=F32)


def _softmax_av(scores, values, sink=None):
    m = scores[0].max(axis=1, keepdims=True)
    for s in scores[1:]:
        m = jnp.maximum(m, s.max(axis=1, keepdims=True))
    if sink is not None:
        m = jnp.maximum(m, sink)
    den = jnp.exp2(sink - m) if sink is not None else 0.0
    acc = None
    for s, v in zip(scores, values):
        p = jnp.exp2(s - m)
        den = den + p.sum(axis=1, keepdims=True)
        pv = jnp.dot(p.astype(BF16), v, preferred_element_type=F32)
        acc = pv if acc is None else acc + pv
    return acc / den


def _window_kernel(sink_ref, q_ref, kp_ref, kc_ref, kn_ref, vp_ref, vc_ref, vn_ref, kx_ref, vx_ref, o_ref, *, nb):
    n = pl.program_id(1)
    groups = A_HEADS // A_KV_HEADS
    d = 64
    kb = jnp.concatenate([kp_ref[...], kc_ref[...], kn_ref[...]], axis=0)
    vb = jnp.concatenate([vp_ref[...], vc_ref[...], vn_ref[...]], axis=0)
    Q = groups * BLOCK
    qi = lax.broadcasted_iota(jnp.int32, (Q, 3 * BLOCK), 0) % BLOCK
    kj = lax.broadcasted_iota(jnp.int32, (Q, 3 * BLOCK), 1)
    k_abs = (n - 1) * BLOCK + kj
    valid = (jnp.abs(kj - BLOCK - qi) <= WINDOW) & (k_abs >= 0) & (k_abs < nb * BLOCK) & (n < nb)
    hrow = lax.broadcasted_iota(jnp.int32, (Q, 1), 0) // BLOCK
    outs = []
    for g in range(A_KV_HEADS):
        qg = jnp.concatenate([q_ref[:, (g * groups + h) * d:(g * groups + h + 1) * d] for h in range(groups)], axis=0)
        ksl = slice(g * d, (g + 1) * d)
        s_band = jnp.where(valid, _nt(qg, kb[:, ksl]), NEG)
        s_ctx = _nt(qg, kx_ref[:, ksl])
        sink = jnp.zeros((Q, 1), F32)
        for h in range(groups):
            sink = jnp.where(hrow == h, sink_ref[g * groups + h] * LOG2E, sink)
        o = _softmax_av([s_band, s_ctx], [vb[:, ksl], vx_ref[:, ksl]], sink)
        outs += [o[h * BLOCK:(h + 1) * BLOCK] for h in range(groups)]
    o_ref[...] = jnp.concatenate(outs, axis=1).astype(o_ref.dtype)


def window_attn(rows, qkv, sink):
    B, S, NC = rows.B, rows.S, rows.NC
    nb, ncb = S // BLOCK, NC // BLOCK
    cb = B * S // NC

    def q_block(b, n):
        return jnp.where(n < nb, b * nb + n, B * nb + b * ncb + (n - nb))

    def kv_spec(col, off):
        return pl.BlockSpec((BLOCK, LANE), lambda b, n, s: (b * nb + jnp.clip(n + off, 0, nb - 1), col))

    in_specs = [pl.BlockSpec((BLOCK, 512), lambda b, n, s: (q_block(b, n), 0)),
                kv_spec(4, -1), kv_spec(4, 0), kv_spec(4, 1), kv_spec(5, -1), kv_spec(5, 0), kv_spec(5, 1),
                pl.BlockSpec((NC, LANE), lambda b, n, s: (cb + b, 4)),
                pl.BlockSpec((NC, LANE), lambda b, n, s: (cb + b, 5))]
    return pl.pallas_call(
        functools.partial(_window_kernel, nb=nb),
        grid_spec=pltpu.PrefetchScalarGridSpec(
            num_scalar_prefetch=1, grid=(B, nb + ncb), in_specs=in_specs,
            out_specs=pl.BlockSpec((BLOCK, 512), lambda b, n, s: (q_block(b, n), 0))),
        out_shape=jax.ShapeDtypeStruct((rows.T, 512), BF16),
        compiler_params=_cp(("parallel", "parallel")), name="window_attn",
    )(sink, *([qkv] * 9))


def _mla_prep_kernel(q_ref, k_ref, kr_ref, cos_ref, sin_ref, qo_ref, ko_ref, *, scale):
    cos, sin = cos_ref[...], sin_ref[...]
    kr = kr_ref[...]
    kr = kr * cos + _rot_pairs(kr, 8) * sin
    for h in range(MLA_HEADS):
        sl = slice(h * LANE, (h + 1) * LANE)
        qh = q_ref[:, sl]
        qo_ref[:, sl] = ((qh * cos + _rot_pairs(qh, 8) * sin) * scale).astype(qo_ref.dtype)
        ko_ref[:, sl] = (k_ref[:, sl] + kr).astype(ko_ref.dtype)


def mla_prep(rows, q, k, p2, cos, sin):
    W = MLA_HEADS * LANE
    wide = pl.BlockSpec((ROW_TILE, W), lambda i: (i, 0))
    return pl.pallas_call(
        functools.partial(_mla_prep_kernel, scale=(MLA_NOPE + MLA_ROPE) ** -0.5 * LOG2E), grid=(rows.nt,),
        in_specs=[wide, wide, _row_spec(LANE), _row_spec(LANE), _row_spec(LANE)],
        out_specs=[wide, wide],
        out_shape=[jax.ShapeDtypeStruct((rows.T, W), BF16)] * 2,
        compiler_params=_cp(("parallel",)), name="mla_prep",
    )(q, k, p2, cos, sin)


def _mla_kernel(q_ref, kl_ref, kx_ref, vl_ref, vx_ref, o_ref, *, nq):
    def run(with_latent_keys):
        outs = []
        for j in range(2):
            sl, vs = slice(j * LANE, (j + 1) * LANE), slice(j * 64, (j + 1) * 64)
            q = q_ref[:, sl]
            scores, values = [_nt(q, kx_ref[:, sl])], [vx_ref[:, vs]]
            if with_latent_keys:
                scores.insert(0, _nt(q, kl_ref[:, sl]))
                values.insert(0, vl_ref[:, vs])
            outs.append(_softmax_av(scores, values))
        o_ref[...] = jnp.concatenate(outs, axis=1).astype(o_ref.dtype)

    i = pl.program_id(2)
    pl.when(i < nq)(lambda: run(True))
    pl.when(i >= nq)(lambda: run(False))


def mla_attn(rows, qm, km, vm, tq=256):
    B, S, NC = rows.B, rows.S, rows.NC
    cb = B * S // NC
    nq, nqc = S // tq, NC // tq

    def q_block(b, i):
        return jnp.where(i < nq, b * nq + i, B * nq + b * nqc + (i - nq))

    return pl.pallas_call(
        functools.partial(_mla_kernel, nq=nq), grid=(B, MLA_HEADS // 2, nq + nqc),
        in_specs=[pl.BlockSpec((tq, 2 * LANE), lambda b, h, i: (q_block(b, i), h)),
                  pl.BlockSpec((S, 2 * LANE), lambda b, h, i: (b, h)),
                  pl.BlockSpec((NC, 2 * LANE), lambda b, h, i: (cb + b, h)),
                  pl.BlockSpec((S, LANE), lambda b, h, i: (b, h)),
                  pl.BlockSpec((NC, LANE), lambda b, h, i: (cb + b, h))],
        out_specs=pl.BlockSpec((tq, LANE), lambda b, h, i: (q_block(b, i), h)),
        out_shape=jax.ShapeDtypeStruct((rows.T, MLA_HEADS * 64), BF16),
        compiler_params=_cp(("parallel", "parallel", "parallel")), name="mla_attn",
    )(qm, km, km, vm, vm)


def _mla_norms_kernel(x0_ref, x1_ref, x2_ref, x3_ref, gq_ref, gkv_ref, q_ref, kv_ref, kr_ref):
    x = jnp.concatenate([x0_ref[...], x1_ref[...], x2_ref[...], x3_ref[...]], axis=1)
    q_ref[...] = _rms(x[:, 16:528], gq_ref[...]).astype(q_ref.dtype)
    kv_ref[...] = _rms(x[:, 528:784], gkv_ref[...]).astype(kv_ref.dtype)
    z = lambda w: jnp.zeros((x.shape[0], w), F32)
    kr_ref[...] = jnp.concatenate([z(MLA_NOPE), x[:, 784:816], z(LANE - MLA_NOPE - MLA_ROPE)], axis=1)


def mla_norms(rows, p2, gq, gkv):
    T = rows.T
    return pl.pallas_call(
        _mla_norms_kernel, grid=(rows.nt,),
        in_specs=[pl.BlockSpec((ROW_TILE, 256), functools.partial(lambda j, i: (i, 4352 // 256 + j), j))
                  for j in range(4)] + [_vec_spec(512), _vec_spec(256)],
        out_specs=[_row_spec(512), _row_spec(256), _row_spec(LANE)],
        out_shape=[jax.ShapeDtypeStruct((T, 512), BF16), jax.ShapeDtypeStruct((T, 256), BF16),
                   jax.ShapeDtypeStruct((T, LANE), F32)],
        compiler_params=_cp(("parallel",)), name="mla_norms",
    )(p2, p2, p2, p2, gq, gkv)


def _log_sigmoid(x):
    return jnp.minimum(x, 0.0) - jnp.log(1.0 + jnp.exp(-jnp.abs(x)))


def _mlstm_kernel(*refs, scale):
    nh = ML_HEADS
    per = 3 * nh + 1
    gb_ref, of_ref, ob_ref, c_ref, n_ref, m_ref = refs[2 * per:]

    @pl.when(pl.program_id(1) == 0)
    def _():
        c_ref[...] = jnp.zeros_like(c_ref)
        n_ref[...] = jnp.zeros_like(n_ref)
        m_ref[...] = jnp.zeros_like(m_ref)

    for d, o_ref in enumerate((of_ref, ob_ref)):
        r = refs[d * per:(d + 1) * per]
        _mlstm_chunk(r[0:nh], r[nh:2 * nh], r[2 * nh:3 * nh], r[3 * nh], gb_ref, o_ref,
                     c_ref.at[d], n_ref.at[d], m_ref.at[d], backward=d == 1, scale=scale)


def _mlstm_chunk(q_refs, k_refs, v_refs, g_ref, gb_ref, o_ref, c_ref, n_ref, m_ref, *, backward, scale):
    nh, L = ML_HEADS, ML_CHUNK
    t_i = lax.broadcasted_iota(jnp.int32, (L, L), 0)
    s_i = lax.broadcasted_iota(jnp.int32, (L, L), 1)
    mask = (s_i >= t_i) if backward else (s_i <= t_i)
    maskf = mask.astype(F32)
    gb = g_ref[...] + gb_ref[...]
    ls = _log_sigmoid(gb)
    gbt, lst = gb.T, ls.T
    cum_c = jnp.dot(maskf, ls, preferred_element_type=F32, precision=HI)
    cum_r = jnp.dot(lst, maskf.T, preferred_element_type=F32, precision=HI)
    last = 0 if backward else L - 1
    gi0, gf0 = (2 * nh, 3 * nh) if backward else (0, nh)
    for h in range(nh):
        gi, gf = gi0 + h, gf0 + h
        q = q_refs[h][...]
        kf = k_refs[h][...] * scale
        vf = v_refs[h][...]
        qb, kb = q.astype(BF16), kf.astype(BF16)
        cumc, cumr = cum_c[:, gf:gf + 1], cum_r[gf:gf + 1, :]
        li_r, li_c = gbt[gi:gi + 1, :], gb[:, gi:gi + 1]
        m11 = m_ref[h][0:1, 0:1]
        log_intra = jnp.where(mask, cumc - cumr + li_r, NEG)
        log_inter = cumc + m11
        m_t = jnp.maximum(log_inter, log_intra.max(axis=1, keepdims=True))
        w_inter = jnp.exp(log_inter - m_t)
        s = _nt(qb, kb) * jnp.exp(log_intra - m_t)
        cmat = c_ref[h]
        num = w_inter * _nt(qb, cmat.astype(BF16)) + jnp.dot(s.astype(BF16), vf.astype(BF16),
                                                             preferred_element_type=F32)
        den = w_inter * jnp.sum(q * n_ref[h][0:1, :], axis=1, keepdims=True) + s.sum(axis=1, keepdims=True)
        o_ref[:, h * L:(h + 1) * L] = num / jnp.maximum(jnp.abs(den), jnp.exp(-m_t))
        total = cumc[last:last + 1, :]
        log_w = total - cumc + li_c
        m_new = jnp.maximum(total + m11, log_w.max(axis=0, keepdims=True))
        decay = jnp.exp(total + m11 - m_new)
        w = jnp.exp(log_w - m_new)
        upd = lax.dot_general((vf * w).astype(BF16), kb, (((0,), (0,)), ((), ())), preferred_element_type=F32)
        c_ref[h] = decay * cmat + upd
        n_ref[h] = jnp.broadcast_to(decay * n_ref[h][0:1, :] + jnp.sum(w * kf, axis=0, keepdims=True), (8, L))
        m_ref[h] = jnp.broadcast_to(m_new, (8, LANE))


def mlstm_scan(rows, p1, gbias):
    B, S, NC = rows.B, rows.S, rows.NC
    nh, L = ML_HEADS, ML_CHUNK
    ncx, ncl = NC // L, S // L
    first_ctx = B * S // L

    def row_block(backward, b, j):
        cx = (ncx - 1 - j) if backward else j
        cl = (ncl - 1 - (j - ncx)) if backward else (j - ncx)
        return jnp.where(j < ncx, first_ctx + b * ncx + cx, b * ncl + cl)

    def col_spec(backward, col):
        return pl.BlockSpec((L, LANE), lambda b, j: (row_block(backward, b, j), col))

    q0 = 2304 // LANE
    in_specs = []
    for backward in (False, True):
        in_specs += [col_spec(backward, q0 + c) for c in range(3 * nh)] + [col_spec(backward, 4352 // LANE)]
    in_specs.append(pl.BlockSpec((1, LANE), lambda b, j: (0, 0)))
    out = jax.ShapeDtypeStruct((rows.T, nh * L), F32)
    return pl.pallas_call(
        functools.partial(_mlstm_kernel, scale=L ** -0.5),
        grid=(B, ncx + ncl), in_specs=in_specs,
        out_specs=[pl.BlockSpec((L, nh * L), functools.partial(lambda bw, b, j: (row_block(bw, b, j), 0), bw))
                   for bw in (False, True)],
        out_shape=[out, out],
        scratch_shapes=[pltpu.VMEM((2, nh, L, L), F32), pltpu.VMEM((2, nh, 8, L), F32),
                        pltpu.VMEM((2, nh, 8, LANE), F32)],
        compiler_params=_cp(("parallel", "arbitrary")), name="mlstm_scan",
    )(*([p1] * (2 * (3 * nh + 1))), gbias)


def _mlstm_finish_kernel(hf_ref, hb_ref, *refs):
    o_refs, g_ref, out_ref = refs[:ML_HEADS], refs[ML_HEADS], refs[ML_HEADS + 1]
    for h in range(ML_HEADS):
        sl = slice(h * ML_CHUNK, (h + 1) * ML_CHUNK)
        hn = _rms(hf_ref[:, sl] + hb_ref[:, sl], g_ref[:, sl])
        out_ref[:, sl] = (jax.nn.sigmoid(o_refs[h][...]) * hn).astype(out_ref.dtype)


def mlstm_finish(rows, hf, hb, p1, gnorm):
    W = ML_HEADS * ML_CHUNK
    o_specs = [pl.BlockSpec((ROW_TILE, LANE), functools.partial(lambda h, i: (i, 3840 // LANE + h), h))
               for h in range(ML_HEADS)]
    return pl.pallas_call(
        _mlstm_finish_kernel, grid=(rows.nt,),
        in_specs=[_row_spec(W), _row_spec(W)] + o_specs + [_vec_spec(W)],
        out_specs=_row_spec(W),
        out_shape=jax.ShapeDtypeStruct((rows.T, W), BF16),
        compiler_params=_cp(("parallel",)), name="mlstm_finish",
    )(hf, hb, *([p1] * ML_HEADS), gnorm)


def _short_conv_kernel(p_ref, w_ref, o_ref):
    x = p_ref[...]
    L = x.shape[0]
    row = lax.broadcasted_iota(jnp.int32, x.shape, 0)
    prev = jnp.where(row == 0, 0.0, pltpu.roll(x, 1, 0))
    nxt = jnp.where(row == L - 1, 0.0, pltpu.roll(x, L - 1, 0))
    o_ref[...] = w_ref[0:1, :] * prev + w_ref[1:2, :] * x + w_ref[2:3, :] * nxt


def short_conv(p1, short_w, L, row_block0, B):
    ncol = 1536 // LANE
    per = 512 // LANE
    return pl.pallas_call(
        _short_conv_kernel, grid=(B, ncol),
        in_specs=[pl.BlockSpec((L, LANE), lambda b, c: (row_block0 + b, 768 // LANE + c)),
                  pl.BlockSpec((HY_SHORT, LANE), lambda b, c: (0, c))],
        out_specs=pl.BlockSpec((None, L, LANE), lambda b, c: (c // per, 0, b * per + c % per)),
        out_shape=jax.ShapeDtypeStruct((3, L, B * 512), F32),
        compiler_params=_cp(("parallel", "parallel")), name="hy_short_conv",
    )(p1, short_w)


def _filter_kernel(feat_ref, w1_ref, b1_ref, w2_ref, b2_ref, fr_ref, w3f_ref, w3b_ref, dl_ref, o_ref, hid_ref):
    L = feat_ref.shape[0]

    @pl.when((pl.program_id(0) == 0) & (pl.program_id(1) == 0))
    def _():
        fr = fr_ref[...]
        h1 = jnp.sin(fr * (jnp.dot(feat_ref[...], w1_ref[...], preferred_element_type=F32, precision=HI)
                           + b1_ref[...]))
        hid_ref[...] = jnp.sin(fr * (jnp.dot(h1, w2_ref[...], preferred_element_type=F32, precision=HI)
                                     + b2_ref[...]))

    hid = hid_ref[...]
    t = lax.broadcasted_iota(jnp.int32, (L, 1), 0).astype(F32) / L
    dec = jnp.exp(-t * dl_ref[...])
    hf = jnp.dot(hid, w3f_ref[...], preferred_element_type=F32, precision=HI) * dec
    hb = jnp.dot(hid, w3b_ref[...], preferred_element_type=F32, precision=HI) * dec
    hb = jnp.where(lax.broadcasted_iota(jnp.int32, hb.shape, 0) == 0, 0.0, hb)
    inv = 1.0 / (jnp.sum(jnp.abs(hf), axis=0, keepdims=True) + jnp.sum(jnp.abs(hb), axis=0, keepdims=True))
    o_ref[0] = ((hf + hb) * inv).astype(o_ref.dtype)
    o_ref[1] = ((hf - hb) * inv).astype(o_ref.dtype)


def hyena_filters(feats, w1p, b1, w2, b2, fr, w3, deltas, L):
    hid = w2.shape[0]
    per = 512 // LANE
    full = lambda shape: pl.BlockSpec(shape, lambda o, c: (0, 0))
    return pl.pallas_call(
        _filter_kernel, grid=(HY_ORDER, per),
        in_specs=[full((L, LANE)), full((LANE, hid)), full((1, hid)), full((hid, hid)), full((1, hid)),
                  full((1, hid)),
                  pl.BlockSpec((hid, LANE), lambda o, c: (0, (o * 2) * per + c)),
                  pl.BlockSpec((hid, LANE), lambda o, c: (0, (o * 2 + 1) * per + c)),
                  pl.BlockSpec((1, LANE), lambda o, c: (0, c))],
        out_specs=pl.BlockSpec((2, L, LANE), lambda o, c: (0, 0, o * per + c)),
        out_shape=jax.ShapeDtypeStruct((2, L, HY_ORDER * 512), BF16),
        scratch_shapes=[pltpu.VMEM((L, hid), F32)],
        compiler_params=_cp(("arbitrary", "arbitrary")), name="hy_filters",
    )(feats, w1p, b1, w2, b2, fr, w3, w3, deltas)


def _spec_mul_kernel(z_ref, g_ref, y_ref, *, scale, B):
    gre, gim = g_ref[0], g_ref[1]
    W = gre.shape[1]
    for b in range(B):
        sl = slice(b * W, (b + 1) * W)
        zre, zim = z_ref[0, :, sl], z_ref[1, :, sl]
        y_ref[0, :, sl] = ((zre * gre - zim * gim) * scale).astype(y_ref.dtype)
        y_ref[1, :, sl] = ((zre * gim + zim * gre) * scale).astype(y_ref.dtype)


def spec_mul(z, g, order, L, B, tk):
    return pl.pallas_call(
        functools.partial(_spec_mul_kernel, scale=1.0 / L, B=B), grid=(L // tk,),
        in_specs=[pl.BlockSpec((2, tk, B * 512), lambda i: (0, i, 0)),
                  pl.BlockSpec((2, tk, 512), lambda i: (0, i, order))],
        out_specs=pl.BlockSpec((2, tk, B * 512), lambda i: (0, i, 0)),
        out_shape=jax.ShapeDtypeStruct((2, L, B * 512), BF16),
        compiler_params=_cp(("parallel",)), name="hy_spec_mul",
    )(z, g)


def _gate_kernel(x_ref, c_ref, z_ref, b_ref, o_ref):
    o_ref[...] = (x_ref[...] * (c_ref[...] + b_ref[...] * z_ref[...])).astype(o_ref.dtype)


def hyena_gate(u3, which_x, conv, z_arr, which_z, bias, order, L, B, tl, final=None):
    nl = L // tl
    n_fill = 0
    ci = lambda i: jnp.minimum(i, nl - 1)
    in_specs = [pl.BlockSpec((None, tl, 512), lambda i, b: (which_x, ci(i), b)),
                pl.BlockSpec((tl, 512), lambda i, b: (ci(i), b)),
                pl.BlockSpec((None, tl, 512), lambda i, b: (which_z, ci(i), b)),
                pl.BlockSpec((None, 1, 512), lambda i, b: (order, 0, 0))]
    args = [u3, conv, z_arr, bias]
    aliases = {}
    kern = _gate_kernel
    if final is not None:
        T, row0, into = final
        out_shape = jax.ShapeDtypeStruct((T, 512), BF16)
        if into is not None:
            out_spec = pl.BlockSpec((tl, 512), lambda i, b: (row0 // tl + b * nl + i, 0))
            in_specs.append(pl.BlockSpec(memory_space=pl.ANY))
            args.append(into)
            aliases = {4: 0}
            kern = lambda x, c, z, b, into_ref, o: _gate_kernel(x, c, z, b, o)
        else:
            assert row0 == 0
            n_fill = (T - B * L) // tl
            out_spec = pl.BlockSpec((tl, 512), lambda i, b: (jnp.where(i < nl, b * nl + i, B * nl + (i - nl)), 0))

            def kern(x, c, z, b, o):
                i = pl.program_id(0)
                pl.when(i < nl)(lambda: _gate_kernel(x, c, z, b, o))

                @pl.when(i >= nl)
                def _():
                    o[...] = jnp.zeros_like(o)
    else:
        out_spec = pl.BlockSpec((None, tl, 512), lambda i, b: (0, i, b))
        out_shape = jax.ShapeDtypeStruct((1, L, B * 512), F32)
    return pl.pallas_call(
        kern, grid=(nl + n_fill, B), in_specs=in_specs, out_specs=out_spec, out_shape=out_shape,
        input_output_aliases=aliases,
        compiler_params=_cp(("arbitrary", "arbitrary")), name="hy_gate",
    )(*args)


def _dft_gen_kernel(ca_ref, sa_ref, cb_ref, sb_ref, ca2_ref, sa2_ref, cb2_ref, sb2_ref, fwd_ref, inv_ref):
    L = cb_ref.shape[1]
    ca, sa, cb, sb = ca_ref[...], sa_ref[...], cb_ref[...], sb_ref[...]
    fwd_ref[0] = (ca * cb - sa * sb).astype(fwd_ref.dtype)
    fwd_ref[1] = (-(sa * cb + ca * sb)).astype(fwd_ref.dtype)
    ca, sa, cb, sb = ca2_ref[...], sa2_ref[...], cb2_ref[...], sb2_ref[...]
    inv_ref[:, :L] = (ca * cb - sa * sb).astype(inv_ref.dtype)
    inv_ref[:, L:] = (-(sa * cb + ca * sb)).astype(inv_ref.dtype)


def _dft_matrices(L, tb):
    nb = L // tb
    col = jnp.arange(L, dtype=jnp.int32)[None, :]
    r0 = (jnp.arange(nb, dtype=jnp.int32) * tb)[:, None]
    i = jnp.arange(tb, dtype=jnp.int32)[:, None]
    unit = math.pi / (2 * L)
    trig = lambda ph: (jnp.cos((ph % (4 * L)).astype(F32) * unit), jnp.sin((ph % (4 * L)).astype(F32) * unit))
    ca, sa = trig((2 * r0 + 1) * col)
    cb, sb = trig(2 * i * col)
    ca2, sa2 = trig((2 * col + 1) * r0)
    cb2, sb2 = trig((2 * col + 1) * i)
    blk = lambda: pl.BlockSpec((None, 1, L), lambda j: (j, 0, 0))
    shared = lambda: pl.BlockSpec((tb, L), lambda j: (0, 0))
    a3 = lambda a: a[:, None, :]
    fwd, inv = pl.pallas_call(
        _dft_gen_kernel, grid=(nb,),
        in_specs=[blk(), blk(), shared(), shared(), blk(), blk(), shared(), shared()],
        out_specs=[pl.BlockSpec((2, tb, L), lambda j: (0, j, 0)), pl.BlockSpec((tb, 2 * L), lambda j: (j, 0))],
        out_shape=[jax.ShapeDtypeStruct((2, L, L), BF16), jax.ShapeDtypeStruct((L, 2 * L), BF16)],
        compiler_params=_cp(("parallel",)), name="hy_dft_gen",
    )(a3(ca), a3(sa), cb, sb, a3(ca2), a3(sa2), cb2, sb2)
    return fwd.reshape(2 * L, L), inv


def _filter_feats(L):
    t = jnp.arange(L, dtype=F32) / L
    kf = jnp.arange(1, HY_POS_FREQS + 1, dtype=F32)
    ang = 2.0 * math.pi * t[:, None] * kf
    feats = jnp.concatenate([t[:, None], jnp.sin(ang), jnp.cos(ang)], axis=-1)
    return jnp.pad(feats, ((0, 0), (0, LANE - feats.shape[1])))


def hyena_seq(p1, L, row_block0, B, consts, wts, tm, tl, T, into=None):
    fwd, inv, feats = consts
    short_w, w1p, b1, w2, b2, fr, w3, deltas, bias = wts
    u3 = short_conv(p1, short_w, L, row_block0, B)
    filt = hyena_filters(feats, w1p, b1, w2, b2, fr, w3, deltas, L)
    gid_f = jnp.repeat(jnp.arange(2, dtype=jnp.int32), L // tm)
    zeros_f = jnp.zeros((2 * L // tm,), jnp.int32)
    zeros_i = jnp.zeros((L // tm,), jnp.int32)
    NW = B * 512
    gspec = gmm(fwd, filt, gid_f, tm=tm, tn=512, n_tiles=HY_ORDER, out_dtype=F32, name="hy_dft_filter")
    gspec = gspec.reshape(2, L, HY_ORDER * 512)
    z_arr, which_z = u3, 0
    out = None
    for o in range(HY_ORDER):
        zs = gmm(fwd, z_arr, zeros_f + which_z, tm=tm, tn=512, n_tiles=NW // 512, out_dtype=F32, name="hy_dft_fwd")
        y = spec_mul(zs.reshape(2, L, NW), gspec, o, L, B, tl)
        conv = gmm(inv, y.reshape(1, 2 * L, NW), zeros_i, tm=tm, tn=512, n_tiles=NW // 512, out_dtype=F32,
                   name="hy_dft_inv")
        final = (T, row_block0 * L, into) if o == HY_ORDER - 1 else None
        out = hyena_gate(u3, 1 + o, conv, z_arr, which_z, bias, o, L, B, tl, final)
        z_arr, which_z = out, 0
    return out


def _gather_kernel(idx_ref, src_ref, o_ref, buf_ref, sem, *, tm):
    i, n = pl.program_id(0), pl.num_programs(0)

    def issue_tile(tile, slot):
        def issue(r, carry):
            pltpu.make_async_copy(src_ref.at[pl.ds(idx_ref[tile * tm + r], 1)], buf_ref.at[slot, pl.ds(r, 1)],
                                  sem.at[slot]).start()
            return carry

        lax.fori_loop(0, tm, issue, 0, unroll=8)

    @pl.when(i == 0)
    def _():
        issue_tile(0, 0)

    @pl.when(i + 1 < n)
    def _():
        issue_tile(i + 1, (i + 1) % 2)

    slot = i % 2
    pltpu.make_async_copy(src_ref.at[pl.ds(0, tm)], buf_ref.at[slot], sem.at[slot]).wait()
    o_ref[...] = buf_ref[slot].astype(o_ref.dtype)


def gather_rows(src, idx, tm, out_dtype):
    M = idx.shape[0]
    D = src.shape[1]
    return pl.pallas_call(
        functools.partial(_gather_kernel, tm=tm),
        grid_spec=pltpu.PrefetchScalarGridSpec(
            num_scalar_prefetch=1, grid=(M // tm,),
            in_specs=[pl.BlockSpec(memory_space=pl.ANY)],
            out_specs=pl.BlockSpec((tm, D), lambda i, idx: (i, 0)),
            scratch_shapes=[pltpu.VMEM((2, tm, D), src.dtype), pltpu.SemaphoreType.DMA((2,))]),
        out_shape=jax.ShapeDtypeStruct((M, D), out_dtype),
        compiler_params=_cp(("arbitrary",)), name="gather_rows",
    )(idx, src)


def _combine_kernel(y0_ref, y1_ref, w_ref, o_ref):
    w = w_ref[...]
    o_ref[...] = w[:, 0:1] * y0_ref[...] + w[:, 1:2] * y1_ref[...]


def moe_combine(rows, yg, rw):
    D, T = rows.D, rows.T
    return pl.pallas_call(
        _combine_kernel, grid=(rows.nt,),
        in_specs=[pl.BlockSpec((None, ROW_TILE, D), lambda i: (0, i, 0)),
                  pl.BlockSpec((None, ROW_TILE, D), lambda i: (1, i, 0)), _row_spec(LANE)],
        out_specs=_row_spec(D), out_shape=jax.ShapeDtypeStruct((T, D), F32),
        compiler_params=_cp(("parallel",)), name="moe_combine",
    )(yg, yg, rw)


def moe_ffn(rows, h_f32, rw, ri, w_gate, w_up, w_down, layer_moe, tm=512):
    T, D = rows.T, rows.D
    E = w_gate.shape[2]
    e_flat = ri[:, :TOP_K].T.reshape(-1)
    onehot = (e_flat[:, None] == jnp.arange(N_EXPERTS, dtype=jnp.int32)[None, :]).astype(jnp.int32)
    csum = jnp.cumsum(onehot, axis=0)
    rank = jnp.sum(csum * onehot, axis=1) - 1
    counts = csum[-1]
    padded = ((counts + tm - 1) // tm) * tm
    ends = jnp.cumsum(padded)
    starts = ends - padded
    dest = starts[e_flat] + rank
    m_pad = TOP_K * T + N_EXPERTS * tm
    token = jnp.tile(jnp.arange(T, dtype=jnp.int32), TOP_K)
    src_tok = jnp.zeros((m_pad,), jnp.int32).at[dest].set(token)
    tile_row = jnp.arange(m_pad // tm, dtype=jnp.int32) * tm
    gid = jnp.minimum(jnp.sum((tile_row[:, None] >= ends[None, :]).astype(jnp.int32), axis=1), N_EXPERTS - 1)
    gid = gid + layer_moe * N_EXPERTS
    used = (ends[-1:] // tm).astype(jnp.int32)

    xs = gather_rows(h_f32, src_tok, tm, BF16)
    g = gmm(xs, w_gate, gid, tm=tm, tn=E // 2, n_tiles=2, out_dtype=BF16, epilogue="silu", used=used,
            name="moe_gate")
    hh = gmm(xs, w_up, gid, tm=tm, tn=E // 2, n_tiles=2, out_dtype=BF16, mul=g, used=used, name="moe_up")
    y = gmm(hh, w_down, gid, tm=tm, tn=D // 2, n_tiles=2, out_dtype=F32, used=used, name="moe_down")
    yg = gather_rows(y, dest, ROW_TILE, F32).reshape(TOP_K, T, D)
    return moe_combine(rows, yg, rw)


def kernel(x, c, ctx, c_ctx, w_mod, b_mod, g_mix_pre, g_mix_post, g_ffn_pre, g_ffn_post, w_in, w_out, attn_sink,
           hy_short, hy_w1, hy_b1, hy_w2, hy_b2, hy_freq, hy_w3, hy_bias, ml_gate_bias, ml_norm, mla_q_norm,
           mla_w_uq, mla_kv_norm, mla_w_ukv, ffn_w_gate, ffn_w_up, ffn_w_down, moe_router, moe_w_gate, moe_w_up,
           moe_w_down):
    B, S, D = x.shape
    NC = ctx.shape[1]
    depth = w_mod.shape[0]
    rows = _Rows(B, S, NC, D)
    T = rows.T
    TM = 512 if T % 512 == 0 else ROW_TILE
    n_mt = T // TM
    ffn_dim = ffn_w_gate.shape[2]

    xs = jnp.concatenate([x.reshape(B * S, D), ctx.reshape(B * NC, D)], axis=0)

    cm = jnp.concatenate([c, c_ctx[None, :]], axis=0)
    cm = jnp.pad(jax.nn.silu(cm), ((0, 16 - (B + 1)), (0, 0))).astype(BF16)
    mod_all = gmm(jnp.tile(cm, (depth, 1)), w_mod, jnp.arange(depth, dtype=jnp.int32), tm=16, tn=1536,
                  n_tiles=N_MOD * D // 1536, out_dtype=F32, name="adaln")
    mods = (mod_all.reshape(depth, 16, N_MOD * D)[:, :B + 1] + b_mod[:, None, :]).reshape(depth, B + 1, N_MOD, 1, D)

    def rope_table(rot_dim, reps, lane0, width):
        cos, sin = _axial_tables(S, rot_dim)
        cos, sin = jnp.tile(cos, (1, reps)), jnp.tile(sin, (1, reps))
        padw = ((0, 0), (lane0, width - lane0 - cos.shape[1]))
        cos = jnp.pad(cos, padw, constant_values=1.0)
        sin = jnp.pad(sin, padw)
        cos = jnp.concatenate([jnp.tile(cos, (B, 1)), jnp.ones((B * NC, width), F32)], axis=0)
        sin = jnp.concatenate([jnp.tile(sin, (B, 1)), jnp.zeros((B * NC, width), F32)], axis=0)
        return cos, sin

    cos_a, sin_a = rope_table(64, 2, 0, LANE)
    cos_m, sin_m = rope_table(MLA_ROPE, 1, MLA_NOPE, LANE)
    hy_consts = {L: _dft_matrices(L, min(L, ROW_TILE)) + (_filter_feats(L),) for L in (S, NC)}
    deltas = jnp.abs(jnp.linspace(math.log(HY_DECAY_TARGET) / HY_SLOW_DECAY,
                                  math.log(HY_DECAY_TARGET) / HY_FAST_DECAY, 512, dtype=F32))[None, :]

    w_in_t = jnp.swapaxes(w_in, 1, 2)
    wq = mla_w_uq.reshape(depth, -1, MLA_HEADS, MLA_NOPE + MLA_ROPE)
    wq = jnp.pad(wq, ((0, 0), (0, 0), (0, 0), (0, LANE - MLA_NOPE - MLA_ROPE))).reshape(depth, -1, MLA_HEADS * LANE)
    wkv = mla_w_ukv.reshape(depth, -1, MLA_HEADS, 2 * 64)
    wk = jnp.pad(wkv[..., :MLA_NOPE], ((0, 0), (0, 0), (0, 0), (0, LANE - MLA_NOPE)))
    wk = wk.reshape(depth, -1, MLA_HEADS * LANE)
    wv = wkv[..., MLA_NOPE:].reshape(depth, -1, MLA_HEADS * 64)
    w1p = jnp.pad(hy_w1, ((0, 0), (0, LANE - hy_w1.shape[1]), (0, 0)))
    gbias = jnp.pad(ml_gate_bias.reshape(depth, 1, 4 * ML_HEADS), ((0, 0), (0, 0), (0, LANE - 4 * ML_HEADS)))
    router = jnp.pad(moe_router, ((0, 0), (0, 0), (0, LANE - N_EXPERTS)))
    n_moe = moe_w_gate.shape[0]
    mw_gate = moe_w_gate.reshape(n_moe * N_EXPERTS, D, -1)
    mw_up = moe_w_up.reshape(n_moe * N_EXPERTS, D, -1)
    mw_down = moe_w_down.reshape(n_moe * N_EXPERTS, -1, D)
    vec = lambda a, l: a[l][None, :]

    h = norm_mod(rows, xs, vec(g_mix_pre, 0), mods[0], 0, 1)
    for layer in range(depth):
        gid = jnp.full((n_mt,), layer, jnp.int32)
        ml = mods[layer]
        p1 = gmm(h, w_in_t, gid, tm=TM, tn=768, n_tiles=7, out_dtype=F32, w_t=True, name="proj_in")
        p2 = p1

        qkv = attn_rope(rows, p1, cos_a, sin_a)
        a_all = window_attn(rows, qkv, attn_sink[layer])

        hy_w = (hy_short[layer], w1p[layer], vec(hy_b1, layer), hy_w2[layer], vec(hy_b2, layer),
                vec(hy_freq, layer), hy_w3[layer], deltas, hy_bias[layer][:, None, :])
        b_l = hyena_seq(p1, S, 0, B, hy_consts[S], hy_w, tm=512, tl=512, T=T)
        b_all = hyena_seq(p1, NC, B * S // NC, B, hy_consts[NC], hy_w, tm=NC, tl=NC, T=T, into=b_l)

        hf, hb = mlstm_scan(rows, p1, gbias[layer])
        m_all = mlstm_finish(rows, hf, hb, p1, vec(ml_norm, layer))

        qn, kvn, kr = mla_norms(rows, p2, vec(mla_q_norm, layer), vec(mla_kv_norm, layer))
        q_up = gmm(qn, wq, gid, tm=TM, tn=MLA_HEADS * LANE, n_tiles=1, out_dtype=F32, name="mla_uq")
        k_up = gmm(kvn, wk, gid, tm=TM, tn=MLA_HEADS * LANE, n_tiles=1, out_dtype=F32, name="mla_uk")
        vm = gmm(kvn, wv, gid, tm=TM, tn=MLA_HEADS * 64, n_tiles=1, out_dtype=BF16, name="mla_uv")
        qm, km = mla_prep(rows, q_up, k_up, kr, cos_m, sin_m)
        d_all = mla_attn(rows, qm, km, vm)

        y = gmm([a_all, b_all, m_all, d_all], w_out, gid, tm=TM, tn=1024, n_tiles=D // 1024, out_dtype=BF16,
                name="proj_out")

        i = layer // 2
        nxt = (vec(g_ffn_pre, layer), ml, 3, 4)
        if layer % 2 == 0:
            xs, h2 = post(rows, xs, y, vec(g_mix_post, layer), ml, 2, nxt=nxt)
            gi = jnp.full((n_mt,), i, jnp.int32)
            g = gmm(h2, ffn_w_gate, gi, tm=TM, tn=ffn_dim // 4, n_tiles=4, out_dtype=BF16, epilogue="silu",
                    name="ffn_gate")
            hh = gmm(h2, ffn_w_up, gi, tm=TM, tn=ffn_dim // 4, n_tiles=4, out_dtype=BF16, mul=g, name="ffn_up")
            y2 = gmm(hh, ffn_w_down, gi, tm=TM, tn=512, n_tiles=D // 512, out_dtype=BF16, name="ffn_down")
        else:
            xs, h2, h2f, rw, ri = post(rows, xs, y, vec(g_mix_post, layer), ml, 2, nxt=nxt, router=router[i])
            y2 = moe_ffn(rows, h2f, rw, ri, mw_gate, mw_up, mw_down, i)
        if layer + 1 < depth:
            xs, h = post(rows, xs, y2, vec(g_ffn_post, layer), ml, 5,
                         nxt=(vec(g_mix_pre, layer + 1), mods[layer + 1], 0, 1))
        else:
            (xs,) = post(rows, xs, y2, vec(g_ffn_post, layer), ml, 5, latent_only=True)
    return xs.reshape(B, S, D)
```

```python
import functools
import math

import numpy as np
import jax
import jax.numpy as jnp
from jax import lax
from jax.experimental import pallas as pl
from jax.experimental.pallas import tpu as pltpu

F32 = jnp.float32
BF16 = jnp.bfloat16
HI = lax.Precision.HIGHEST

EPS = 1e-6
ROPE_BASE = 10000.0
GRID_W = 64
BLOCK = 128
WINDOW = 128
N_MOD = 6
A_HEADS, A_KV_HEADS = 8, 2
HY_ORDER, HY_SHORT, HY_POS_FREQS = 2, 3, 8
HY_DECAY_TARGET, HY_FAST_DECAY, HY_SLOW_DECAY = 1e-2, 0.3, 1.5
ML_HEADS, ML_CHUNK = 4, 128
MLA_HEADS, MLA_NOPE, MLA_ROPE = 8, 64, 32
N_EXPERTS, TOP_K = 8, 2

PA_COLS = 3840
PB_GATES = 4352 - PA_COLS
MLA_HP = 8
LANE = 128
ROW_TILE = 256
NEG = -1e30
LOG2E = math.log2(math.e)
VMEM_LIMIT = 56 * 1024 * 1024


def _cp(sem, vmem=VMEM_LIMIT):
    return pltpu.CompilerParams(dimension_semantics=sem, vmem_limit_bytes=vmem)


def _gmm_kernel(gid_ref, used_ref, *refs, n_a, cast, w_t, epilogue, n_extra, w_n, tn, n_off):
    a_refs, w_ref, rest = refs[:n_a], refs[n_a], refs[n_a + 1:]
    extra_refs, rest = rest[:n_extra], rest[n_extra:]
    o_ref = rest[0]
    n, m = pl.program_id(0), pl.program_id(1)
    n_axis = 0 if w_t else 1
    if cast:
        wbf_ref = rest[1]
        prev = gid_ref[jnp.maximum(m - 1, 0)]

        @pl.when((m == 0) | (gid_ref[m] != prev))
        def _():
            w = w_ref[...]
            if w_n is not None:
                col = (n + n_off) * tn + lax.broadcasted_iota(jnp.int32, w.shape, n_axis)
                w = jnp.where(col < w_n, w, 0.0)
            wbf_ref[...] = w.astype(BF16)

        src = wbf_ref
    else:
        src = w_ref

    @pl.when(m < used_ref[0])
    def _():
        acc, lo = None, 0
        for a_ref in a_refs:
            k = a_ref.shape[1]
            if w_t:
                part = lax.dot_general(a_ref[...], src[:, lo:lo + k], (((1,), (1,)), ((), ())),
                                       preferred_element_type=F32)
            else:
                part = jnp.dot(a_ref[...], src[lo:lo + k, :], preferred_element_type=F32)
            acc = part if acc is None else acc + part
            lo += k
        if epilogue is not None:
            acc = epilogue(acc, *extra_refs)
        o_ref[...] = acc.astype(o_ref.dtype)

    @pl.when(m >= used_ref[0])
    def _():
        o_ref[...] = jnp.zeros_like(o_ref)


def _silu(acc):
    return acc * jax.nn.sigmoid(acc)


def _times(acc, g_ref):
    return acc * g_ref[...].astype(F32)


def gmm(a, w, gid, *, tm, tn, n_tiles, out_dtype, n_off=0, epilogue=None, extra=(), w_t=False, used=None,
        name="gmm"):
    a_list = list(a) if isinstance(a, (list, tuple)) else [a]
    M = a_list[0].shape[0]
    K = sum(p.shape[1] for p in a_list)
    mt = M // tm
    k_axis, n_axis = (2, 1) if w_t else (1, 2)
    assert mt * tm == M and w.shape[k_axis] == K and gid.shape == (mt,)
    cast = w.dtype != BF16
    partial_n = (n_off + n_tiles) * tn > w.shape[n_axis]
    assert cast or not partial_n
    if used is None:
        used = jnp.full((1,), mt, jnp.int32)
    in_specs = [pl.BlockSpec((tm, p.shape[1]), lambda n, m, g, u: (m, 0)) for p in a_list]
    if w_t:
        in_specs.append(pl.BlockSpec((None, tn, K), lambda n, m, g, u: (g[m], n + n_off, 0)))
    else:
        in_specs.append(pl.BlockSpec((None, K, tn), lambda n, m, g, u: (g[m], 0, n + n_off)))
    args = a_list + [w]
    for arr, block, imap in extra:
        in_specs.append(pl.BlockSpec(block, functools.partial(lambda f, n, m, g, u: f(n, m), imap)))
        args.append(arr)
    kern = functools.partial(_gmm_kernel, n_a=len(a_list), cast=cast, w_t=w_t, epilogue=epilogue,
                             n_extra=len(extra), w_n=w.shape[n_axis] if partial_n else None, tn=tn, n_off=n_off)
    return pl.pallas_call(
        kern,
        grid_spec=pltpu.PrefetchScalarGridSpec(
            num_scalar_prefetch=2, grid=(n_tiles, mt), in_specs=in_specs,
            out_specs=pl.BlockSpec((tm, tn), lambda n, m, g, u: (m, n)),
            scratch_shapes=[pltpu.VMEM((tn, K) if w_t else (K, tn), BF16)] if cast else []),
        out_shape=jax.ShapeDtypeStruct((M, n_tiles * tn), out_dtype),
        compiler_params=_cp(("arbitrary", "arbitrary")),
        name=name,
    )(gid, used, *args)


def _rms(v, g):
    return v * lax.rsqrt(jnp.mean(v * v, axis=-1, keepdims=True) + EPS) * g


def _norm_mod_kernel(x_ref, g_ref, sh_ref, sc_ref, h_ref):
    h_ref[...] = (_rms(x_ref[...], g_ref[...]) * (1.0 + sc_ref[...]) + sh_ref[...]).astype(h_ref.dtype)


def _post_kernel(*refs, with_next, with_router):
    x_ref, y_ref, gp_ref, gate_ref = refs[:4]
    refs = refs[4:]
    xn = x_ref[...] + gate_ref[...] * _rms(y_ref[...].astype(F32), gp_ref[...])
    if not with_next:
        refs[0][...] = xn
        return
    gn_ref, sh_ref, sc_ref = refs[:3]
    refs = refs[3:]
    if with_router:
        r_ref, refs = refs[0], refs[1:]
    refs[0][...] = xn
    h = _rms(xn, gn_ref[...]) * (1.0 + sc_ref[...]) + sh_ref[...]
    refs[1][...] = h.astype(BF16)
    if with_router:
        hf_ref, rw_ref, ri_ref = refs[2:5]
        hf_ref[...] = h
        logits = jnp.dot(h, r_ref[...], preferred_element_type=F32, precision=HI)
        lane = lax.broadcasted_iota(jnp.int32, logits.shape, 1)
        logits = jnp.where(lane < N_EXPERTS, logits, -jnp.inf)
        v1 = jnp.max(logits, axis=1, keepdims=True)
        i1 = jnp.min(jnp.where(logits == v1, lane, LANE), axis=1, keepdims=True)
        l2 = jnp.where(lane == i1, -jnp.inf, logits)
        v2 = jnp.max(l2, axis=1, keepdims=True)
        i2 = jnp.min(jnp.where(l2 == v2, lane, LANE), axis=1, keepdims=True)
        e = jnp.exp(v2 - v1)
        w1 = 1.0 / (1.0 + e)
        w2 = e / (1.0 + e)
        rw_ref[...] = jnp.where(lane == 0, w1, jnp.where(lane == 1, w2, 0.0))
        ri_ref[...] = jnp.where(lane == 0, i1, jnp.where(lane == 1, i2, 0))


class _Rows:
    def __init__(self, B, S, NC, D):
        self.B, self.S, self.NC, self.D = B, S, NC, D
        self.T = B * S + B * NC
        assert S % ROW_TILE == 0 and NC % ROW_TILE == 0
        self.nt = self.T // ROW_TILE

    def mod_row(self, i):
        n_lat = self.B * self.S // ROW_TILE
        return jnp.where(i < n_lat, i // (self.S // ROW_TILE), self.B)


def _row_spec(D):
    return pl.BlockSpec((ROW_TILE, D), lambda i: (i, 0))


def _vec_spec(D):
    return pl.BlockSpec((1, D), lambda i: (0, 0))


def _mod_spec(rows, j):
    return pl.BlockSpec((None, None, 1, rows.D), lambda i: (rows.mod_row(i), j, 0, 0))


def norm_mod(rows, x, g, mods, j_shift, j_scale):
    D = rows.D
    return pl.pallas_call(
        _norm_mod_kernel, grid=(rows.nt,),
        in_specs=[_row_spec(D), _vec_spec(D), _mod_spec(rows, j_shift), _mod_spec(rows, j_scale)],
        out_specs=_row_spec(D),
        out_shape=jax.ShapeDtypeStruct((rows.T, D), BF16),
        compiler_params=_cp(("parallel",)), name="norm_mod",
    )(x, g, mods, mods)


def post(rows, x, y, g_post, mods, j_gate, nxt=None, router=None, latent_only=False):
    D = rows.D
    T = rows.B * rows.S if latent_only else rows.T
    in_specs = [_row_spec(D), _row_spec(D), _vec_spec(D), _mod_spec(rows, j_gate)]
    args = [x, y, g_post, mods]
    out_specs = [_row_spec(D)]
    out_shape = [jax.ShapeDtypeStruct((T, D), F32)]
    if nxt is not None:
        g_next, mods_next, j_shift, j_scale = nxt
        in_specs += [_vec_spec(D), _mod_spec(rows, j_shift), _mod_spec(rows, j_scale)]
        args += [g_next, mods_next, mods_next]
        out_specs.append(_row_spec(D))
        out_shape.append(jax.ShapeDtypeStruct((T, D), BF16))
        if router is not None:
            in_specs.append(pl.BlockSpec((D, LANE), lambda i: (0, 0)))
            args.append(router)
            out_specs += [_row_spec(D), _row_spec(LANE), _row_spec(LANE)]
            out_shape += [jax.ShapeDtypeStruct((T, D), F32), jax.ShapeDtypeStruct((T, LANE), F32),
                          jax.ShapeDtypeStruct((T, LANE), jnp.int32)]
    kern = functools.partial(_post_kernel, with_next=nxt is not None, with_router=router is not None)
    return pl.pallas_call(
        kern, grid=(T // ROW_TILE,), in_specs=in_specs, out_specs=out_specs, out_shape=out_shape,
        compiler_params=_cp(("parallel",)), name="post",
    )(*args)


def _rot_pairs(x, q):
    lane = lax.broadcasted_iota(jnp.int32, x.shape, 1)
    n = x.shape[1]
    return jnp.where(lane % (2 * q) < q, -pltpu.roll(x, n - q, 1), pltpu.roll(x, q, 1))


def _axial_tables(S, rot_dim):
    rows = S // GRID_W
    r = jnp.repeat(jnp.arange(rows, dtype=F32), GRID_W)
    col = jnp.tile(jnp.arange(GRID_W, dtype=F32), rows)
    half = rot_dim // 2
    inv = ROPE_BASE ** (-jnp.arange(0, half, 2, dtype=F32) / half)
    ar, ac = r[:, None] * inv, col[:, None] * inv
    ang = jnp.concatenate([ar, ar, ac, ac], axis=-1)
    return jnp.cos(ang), jnp.sin(ang)


def _attn_rope_kernel(p_ref, cos_ref, sin_ref, o_ref, *, scale):
    cos, sin = cos_ref[...], sin_ref[...]
    nq = A_HEADS * 64 // LANE
    for s in range(nq + 1):
        xs = p_ref[:, s * LANE:(s + 1) * LANE].astype(F32)
        r = xs * cos + _rot_pairs(xs, 16) * sin
        if s < nq:
            r = r * scale
        o_ref[:, s * LANE:(s + 1) * LANE] = r.astype(o_ref.dtype)
    o_ref[:, (nq + 1) * LANE:] = p_ref[:, (nq + 1) * LANE:].astype(o_ref.dtype)


def attn_rope(rows, p1, cos, sin):
    W = 768
    return pl.pallas_call(
        functools.partial(_attn_rope_kernel, scale=64 ** -0.5 * LOG2E), grid=(rows.nt,),
        in_specs=[pl.BlockSpec((ROW_TILE, W), lambda i: (i, 0)), _row_spec(LANE), _row_spec(LANE)],
        out_specs=pl.BlockSpec((ROW_TILE, W), lambda i: (i, 0)),
        out_shape=jax.ShapeDtypeStruct((rows.T, W), BF16),
        compiler_params=_cp(("parallel",)), name="attn_rope",
    )(p1, cos, sin)


def _nt(a, b):
    return lax.dot_general(a, b, (((1,), (1,)), ((), ())), preferred_element_type=F32)


def _softmax_av(scores, values, sink=None):
    m = scores[0].max(axis=1, keepdims=True)
    for s in scores[1:]:
        m = jnp.maximum(m, s.max(axis=1, keepdims=True))
    if sink is not None:
        m = jnp.maximum(m, sink)
    den = jnp.exp2(sink - m) if sink is not None else 0.0
    acc = None
    for s, v in zip(scores, values):
        p = jnp.exp2(s - m)
        den = den + p.sum(axis=1, keepdims=True)
        pv = jnp.dot(p.astype(BF16), v, preferred_element_type=F32)
        acc = pv if acc is None else acc + pv
    return acc / den


def _window_kernel(sink_ref, q_ref, kp_ref, kc_ref, kn_ref, vp_ref, vc_ref, vn_ref, kx_ref, vx_ref, o_ref, *, nb):
    n = pl.program_id(1)
    groups = A_HEADS // A_KV_HEADS
    d = 64
    kb = jnp.concatenate([kp_ref[...], kc_ref[...], kn_ref[...]], axis=0)
    vb = jnp.concatenate([vp_ref[...], vc_ref[...], vn_ref[...]], axis=0)
    Q = groups * BLOCK
    qi = lax.broadcasted_iota(jnp.int32, (Q, 3 * BLOCK), 0) % BLOCK
    kj = lax.broadcasted_iota(jnp.int32, (Q, 3 * BLOCK), 1)
    k_abs = (n - 1) * BLOCK + kj
    valid = (jnp.abs(kj - BLOCK - qi) <= WINDOW) & (k_abs >= 0) & (k_abs < nb * BLOCK) & (n < nb)
    hrow = lax.broadcasted_iota(jnp.int32, (Q, 1), 0) // BLOCK
    outs = []
    for g in range(A_KV_HEADS):
        qg = jnp.concatenate([q_ref[:, (g * groups + h) * d:(g * groups + h + 1) * d] for h in range(groups)], axis=0)
        ksl = slice(g * d, (g + 1) * d)
        s_band = jnp.where(valid, _nt(qg, kb[:, ksl]), NEG)
        s_ctx = _nt(qg, kx_ref[:, ksl])
        sink = jnp.zeros((Q, 1), F32)
        for h in range(groups):
            sink = jnp.where(hrow == h, sink_ref[g * groups + h] * LOG2E, sink)
        o = _softmax_av([s_band, s_ctx], [vb[:, ksl], vx_ref[:, ksl]], sink)
        outs += [o[h * BLOCK:(h + 1) * BLOCK] for h in range(groups)]
    o_ref[...] = jnp.concatenate(outs, axis=1).astype(o_ref.dtype)


def window_attn(rows, qkv, sink):
    B, S, NC = rows.B, rows.S, rows.NC
    nb, ncb = S // BLOCK, NC // BLOCK
    cb = B * S // NC

    def q_block(b, n):
        return jnp.where(n < nb, b * nb + n, B * nb + b * ncb + (n - nb))

    def kv_spec(col, off):
        return pl.BlockSpec((BLOCK, LANE), lambda b, n, s: (b * nb + jnp.clip(n + off, 0, nb - 1), col))

    in_specs = [pl.BlockSpec((BLOCK, 512), lambda b, n, s: (q_block(b, n), 0)),
                kv_spec(4, -1), kv_spec(4, 0), kv_spec(4, 1), kv_spec(5, -1), kv_spec(5, 0), kv_spec(5, 1),
                pl.BlockSpec((NC, LANE), lambda b, n, s: (cb + b, 4)),
                pl.BlockSpec((NC, LANE), lambda b, n, s: (cb + b, 5))]
    return pl.pallas_call(
        functools.partial(_window_kernel, nb=nb),
        grid_spec=pltpu.PrefetchScalarGridSpec(
            num_scalar_prefetch=1, grid=(B, nb + ncb), in_specs=in_specs,
            out_specs=pl.BlockSpec((BLOCK, 512), lambda b, n, s: (q_block(b, n), 0))),
        out_shape=jax.ShapeDtypeStruct((rows.T, 512), BF16),
        compiler_params=_cp(("parallel", "parallel")), name="window_attn",
    )(sink, *([qkv] * 9))


def _mla_prep_kernel(q_ref, k_ref, kr_ref, cos_ref, sin_ref, qo_ref, ko_ref, *, scale):
    cos, sin = cos_ref[...], sin_ref[...]
    kr = kr_ref[...]
    kr = kr * cos + _rot_pairs(kr, 8) * sin
    for h in range(MLA_HEADS):
        sl = slice(h * LANE, (h + 1) * LANE)
        qh = q_ref[:, sl]
        qo_ref[:, sl] = ((qh * cos + _rot_pairs(qh, 8) * sin) * scale).astype(qo_ref.dtype)
        ko_ref[:, sl] = (k_ref[:, sl] + kr).astype(ko_ref.dtype)


def mla_prep(rows, q, k, p2, cos, sin):
    W = MLA_HEADS * LANE
    wide = pl.BlockSpec((ROW_TILE, W), lambda i: (i, 0))
    return pl.pallas_call(
        functools.partial(_mla_prep_kernel, scale=(MLA_NOPE + MLA_ROPE) ** -0.5 * LOG2E), grid=(rows.nt,),
        in_specs=[wide, wide, _row_spec(LANE), _row_spec(LANE), _row_spec(LANE)],
        out_specs=[wide, wide],
        out_shape=[jax.ShapeDtypeStruct((rows.T, W), BF16)] * 2,
        compiler_params=_cp(("parallel",)), name="mla_prep",
    )(q, k, p2, cos, sin)


def _mla_kernel(q_ref, kl_ref, kx_ref, vl_ref, vx_ref, o_ref, *, nq):
    def run(with_latent_keys):
        outs = []
        for j in range(MLA_HP):
            sl, vs = slice(j * LANE, (j + 1) * LANE), slice(j * 64, (j + 1) * 64)
            q = q_ref[:, sl]
            scores, values = [_nt(q, kx_ref[:, sl])], [vx_ref[:, vs]]
            if with_latent_keys:
                scores.insert(0, _nt(q, kl_ref[:, sl]))
                values.insert(0, vl_ref[:, vs])
            outs.append(_softmax_av(scores, values))
        o_ref[...] = jnp.concatenate(outs, axis=1).astype(o_ref.dtype)

    i = pl.program_id(2)
    pl.when(i < nq)(lambda: run(True))
    pl.when(i >= nq)(lambda: run(False))


def mla_attn(rows, qm, km, vm, tq=256):
    B, S, NC = rows.B, rows.S, rows.NC
    cb = B * S // NC
    nq, nqc = S // tq, NC // tq
    kw, vw = MLA_HP * LANE, MLA_HP * 64

    def q_block(b, i):
        return jnp.where(i < nq, b * nq + i, B * nq + b * nqc + (i - nq))

    return pl.pallas_call(
        functools.partial(_mla_kernel, nq=nq), grid=(B, MLA_HEADS // MLA_HP, nq + nqc),
        in_specs=[pl.BlockSpec((tq, kw), lambda b, h, i: (q_block(b, i), h)),
                  pl.BlockSpec((S, kw), lambda b, h, i: (b, h)),
                  pl.BlockSpec((NC, kw), lambda b, h, i: (cb + b, h)),
                  pl.BlockSpec((S, vw), lambda b, h, i: (b, h)),
                  pl.BlockSpec((NC, vw), lambda b, h, i: (cb + b, h))],
        out_specs=pl.BlockSpec((tq, vw), lambda b, h, i: (q_block(b, i), h)),
        out_shape=jax.ShapeDtypeStruct((rows.T, MLA_HEADS * 64), BF16),
        compiler_params=_cp(("parallel", "parallel", "parallel")), name="mla_attn",
    )(qm, km, km, vm, vm)


def _mla_norms_kernel(x0_ref, x1_ref, x2_ref, x3_ref, gq_ref, gkv_ref, q_ref, kv_ref, kr_ref):
    x = jnp.concatenate([x0_ref[...], x1_ref[...], x2_ref[...], x3_ref[...]], axis=1)
    q_ref[...] = _rms(x[:, 16:528], gq_ref[...]).astype(q_ref.dtype)
    kv_ref[...] = _rms(x[:, 528:784], gkv_ref[...]).astype(kv_ref.dtype)
    z = lambda w: jnp.zeros((x.shape[0], w), F32)
    kr_ref[...] = jnp.concatenate([z(MLA_NOPE), x[:, 784:816], z(LANE - MLA_NOPE - MLA_ROPE)], axis=1)


def mla_norms(rows, p2, gq, gkv):
    T = rows.T
    return pl.pallas_call(
        _mla_norms_kernel, grid=(rows.nt,),
        in_specs=[pl.BlockSpec((ROW_TILE, 256), functools.partial(lambda j, i: (i, PB_GATES // 256 + j), j))
                  for j in range(4)] + [_vec_spec(512), _vec_spec(256)],
        out_specs=[_row_spec(512), _row_spec(256), _row_spec(LANE)],
        out_shape=[jax.ShapeDtypeStruct((T, 512), BF16), jax.ShapeDtypeStruct((T, 256), BF16),
                   jax.ShapeDtypeStruct((T, LANE), F32)],
        compiler_params=_cp(("parallel",)), name="mla_norms",
    )(p2, p2, p2, p2, gq, gkv)


def _log_sigmoid(x):
    return jnp.minimum(x, 0.0) - jnp.log(1.0 + jnp.exp(-jnp.abs(x)))


def _mlstm_kernel(*refs, scale):
    nh = ML_HEADS
    per = 3 * nh + 1
    gb_ref, of_ref, ob_ref, c_ref, n_ref, m_ref = refs[2 * per:]

    @pl.when(pl.program_id(1) == 0)
    def _():
        c_ref[...] = jnp.zeros_like(c_ref)
        n_ref[...] = jnp.zeros_like(n_ref)
        m_ref[...] = jnp.zeros_like(m_ref)

    for d, o_ref in enumerate((of_ref, ob_ref)):
        r = refs[d * per:(d + 1) * per]
        _mlstm_chunk(r[0:nh], r[nh:2 * nh], r[2 * nh:3 * nh], r[3 * nh], gb_ref, o_ref,
                     c_ref.at[d], n_ref.at[d], m_ref.at[d], backward=d == 1, scale=scale)


def _mlstm_chunk(q_refs, k_refs, v_refs, g_ref, gb_ref, o_ref, c_ref, n_ref, m_ref, *, backward, scale):
    nh, L = ML_HEADS, ML_CHUNK
    t_i = lax.broadcasted_iota(jnp.int32, (L, L), 0)
    s_i = lax.broadcasted_iota(jnp.int32, (L, L), 1)
    mask = (s_i >= t_i) if backward else (s_i <= t_i)
    maskf = mask.astype(F32)
    gb = g_ref[...] + gb_ref[...]
    ls = _log_sigmoid(gb)
    gbt, lst = gb.T, ls.T
    cum_c = jnp.dot(maskf, ls, preferred_element_type=F32, precision=HI)
    cum_r = jnp.dot(lst, maskf.T, preferred_element_type=F32, precision=HI)
    last = 0 if backward else L - 1
    gi0, gf0 = (2 * nh, 3 * nh) if backward else (0, nh)
    for h in range(nh):
        gi, gf = gi0 + h, gf0 + h
        q = q_refs[h][...].astype(F32)
        kf = k_refs[h][...].astype(F32) * scale
        vf = v_refs[h][...].astype(F32)
        qb, kb = q.astype(BF16), kf.astype(BF16)
        cumc, cumr = cum_c[:, gf:gf + 1], cum_r[gf:gf + 1, :]
        li_r, li_c = gbt[gi:gi + 1, :], gb[:, gi:gi + 1]
        m11 = m_ref[h][0:1, 0:1]
        log_intra = jnp.where(mask, cumc - cumr + li_r, NEG)
        log_inter = cumc + m11
        m_t = jnp.maximum(log_inter, log_intra.max(axis=1, keepdims=True))
        w_inter = jnp.exp(log_inter - m_t)
        s = _nt(qb, kb) * jnp.exp(log_intra - m_t)
        cmat = c_ref[h]
        num = w_inter * _nt(qb, cmat.astype(BF16)) + jnp.dot(s.astype(BF16), vf.astype(BF16),
                                                             preferred_element_type=F32)
        den = w_inter * jnp.sum(q * n_ref[h][0:1, :], axis=1, keepdims=True) + s.sum(axis=1, keepdims=True)
        o_ref[:, h * L:(h + 1) * L] = num / jnp.maximum(jnp.abs(den), jnp.exp(-m_t))
        total = cumc[last:last + 1, :]
        log_w = total - cumc + li_c
        m_new = jnp.maximum(total + m11, log_w.max(axis=0, keepdims=True))
        decay = jnp.exp(total + m11 - m_new)
        w = jnp.exp(log_w - m_new)
        upd = lax.dot_general((vf * w).astype(BF16), kb, (((0,), (0,)), ((), ())), preferred_element_type=F32)
        c_ref[h] = decay * cmat + upd
        n_ref[h] = jnp.broadcast_to(decay * n_ref[h][0:1, :] + jnp.sum(w * kf, axis=0, keepdims=True), (8, L))
        m_ref[h] = jnp.broadcast_to(m_new, (8, LANE))


def mlstm_scan(rows, pa, pb, gbias):
    B, S, NC = rows.B, rows.S, rows.NC
    nh, L = ML_HEADS, ML_CHUNK
    ncx, ncl = NC // L, S // L
    first_ctx = B * S // L

    def row_block(backward, b, j):
        cx = (ncx - 1 - j) if backward else j
        cl = (ncl - 1 - (j - ncx)) if backward else (j - ncx)
        return jnp.where(j < ncx, first_ctx + b * ncx + cx, b * ncl + cl)

    def col_spec(backward, col):
        return pl.BlockSpec((L, LANE), lambda b, j: (row_block(backward, b, j), col))

    q0 = 2304 // LANE
    in_specs = []
    for backward in (False, True):
        in_specs += [col_spec(backward, q0 + c) for c in range(3 * nh)] + [col_spec(backward, PB_GATES // LANE)]
    in_specs.append(pl.BlockSpec((1, LANE), lambda b, j: (0, 0)))
    out = jax.ShapeDtypeStruct((rows.T, nh * L), F32)
    return pl.pallas_call(
        functools.partial(_mlstm_kernel, scale=L ** -0.5),
        grid=(B, ncx + ncl), in_specs=in_specs,
        out_specs=[pl.BlockSpec((L, nh * L), functools.partial(lambda bw, b, j: (row_block(bw, b, j), 0), bw))
                   for bw in (False, True)],
        out_shape=[out, out],
        scratch_shapes=[pltpu.VMEM((2, nh, L, L), F32), pltpu.VMEM((2, nh, 8, L), F32),
                        pltpu.VMEM((2, nh, 8, LANE), F32)],
        compiler_params=_cp(("parallel", "arbitrary")), name="mlstm_scan",
    )(*(([pa] * (3 * nh) + [pb]) * 2), gbias)


def _mlstm_finish_kernel(hf_ref, hb_ref, *refs):
    o_refs, g_ref, out_ref = refs[:ML_HEADS], refs[ML_HEADS], refs[ML_HEADS + 1]
    for h in range(ML_HEADS):
        sl = slice(h * ML_CHUNK, (h + 1) * ML_CHUNK)
        hn = _rms(hf_ref[:, sl] + hb_ref[:, sl], g_ref[:, sl])
        out_ref[:, sl] = (jax.nn.sigmoid(o_refs[h][...]) * hn).astype(out_ref.dtype)


def mlstm_finish(rows, hf, hb, p1, gnorm):
    W = ML_HEADS * ML_CHUNK
    o_specs = [pl.BlockSpec((ROW_TILE, LANE), functools.partial(lambda h, i: (i, h), h))
               for h in range(ML_HEADS)]
    return pl.pallas_call(
        _mlstm_finish_kernel, grid=(rows.nt,),
        in_specs=[_row_spec(W), _row_spec(W)] + o_specs + [_vec_spec(W)],
        out_specs=_row_spec(W),
        out_shape=jax.ShapeDtypeStruct((rows.T, W), BF16),
        compiler_params=_cp(("parallel",)), name="mlstm_finish",
    )(hf, hb, *([p1] * ML_HEADS), gnorm)


def _short_conv_kernel(p_ref, w_ref, o_ref):
    x = p_ref[...].astype(F32)
    L = x.shape[0]
    row = lax.broadcasted_iota(jnp.int32, x.shape, 0)
    prev = jnp.where(row == 0, 0.0, pltpu.roll(x, 1, 0))
    nxt = jnp.where(row == L - 1, 0.0, pltpu.roll(x, L - 1, 0))
    o_ref[...] = w_ref[0:1, :] * prev + w_ref[1:2, :] * x + w_ref[2:3, :] * nxt


def short_conv(p1, short_w, L, row_block0, B):
    ncol = 1536 // LANE
    per = 512 // LANE
    return pl.pallas_call(
        _short_conv_kernel, grid=(B, ncol),
        in_specs=[pl.BlockSpec((L, LANE), lambda b, c: (row_block0 + b, 768 // LANE + c)),
                  pl.BlockSpec((HY_SHORT, LANE), lambda b, c: (0, c))],
        out_specs=pl.BlockSpec((None, L, LANE), lambda b, c: (c // per, 0, b * per + c % per)),
        out_shape=jax.ShapeDtypeStruct((3, L, B * 512), F32),
        compiler_params=_cp(("parallel", "parallel")), name="hy_short_conv",
    )(p1, short_w)


def _filter_kernel(feat_ref, w1_ref, b1_ref, w2_ref, b2_ref, fr_ref, w3f_ref, w3b_ref, dl_ref, o_ref, hid_ref):
    L = feat_ref.shape[0]

    @pl.when((pl.program_id(0) == 0) & (pl.program_id(1) == 0))
    def _():
        fr = fr_ref[...]
        h1 = jnp.sin(fr * (jnp.dot(feat_ref[...], w1_ref[...], preferred_element_type=F32, precision=HI)
                           + b1_ref[...]))
        hid_ref[...] = jnp.sin(fr * (jnp.dot(h1, w2_ref[...], preferred_element_type=F32, precision=HI)
                                     + b2_ref[...]))

    hid = hid_ref[...]
    t = lax.broadcasted_iota(jnp.int32, (L, 1), 0).astype(F32) / L
    dec = jnp.exp(-t * dl_ref[...])
    hf = jnp.dot(hid, w3f_ref[...], preferred_element_type=F32, precision=HI) * dec
    hb = jnp.dot(hid, w3b_ref[...], preferred_element_type=F32, precision=HI) * dec
    hb = jnp.where(lax.broadcasted_iota(jnp.int32, hb.shape, 0) == 0, 0.0, hb)
    inv = 1.0 / (jnp.sum(jnp.abs(hf), axis=0, keepdims=True) + jnp.sum(jnp.abs(hb), axis=0, keepdims=True))
    o_ref[0] = ((hf + hb) * inv).astype(o_ref.dtype)
    o_ref[1] = ((hf - hb) * inv).astype(o_ref.dtype)


def hyena_filters(feats, w1p, b1, w2, b2, fr, w3, deltas, L):
    hid = w2.shape[0]
    per = 512 // LANE
    full = lambda shape: pl.BlockSpec(shape, lambda o, c: (0, 0))
    return pl.pallas_call(
        _filter_kernel, grid=(HY_ORDER, per),
        in_specs=[full((L, LANE)), full((LANE, hid)), full((1, hid)), full((hid, hid)), full((1, hid)),
                  full((1, hid)),
                  pl.BlockSpec((hid, LANE), lambda o, c: (0, (o * 2) * per + c)),
                  pl.BlockSpec((hid, LANE), lambda o, c: (0, (o * 2 + 1) * per + c)),
                  pl.BlockSpec((1, LANE), lambda o, c: (0, c))],
        out_specs=pl.BlockSpec((2, L, LANE), lambda o, c: (0, 0, o * per + c)),
        out_shape=jax.ShapeDtypeStruct((2, L, HY_ORDER * 512), BF16),
        scratch_shapes=[pltpu.VMEM((L, hid), F32)],
        compiler_params=_cp(("arbitrary", "arbitrary")), name="hy_filters",
    )(feats, w1p, b1, w2, b2, fr, w3, w3, deltas)


def _gate_kernel(x_ref, c_ref, z_ref, b_ref, o_ref):
    o_ref[...] = (x_ref[...] * (c_ref[...] + b_ref[...] * z_ref[...])).astype(o_ref.dtype)


def hyena_gate(u3, which_x, conv, z_arr, which_z, bias, order, L, B, tl, final=None):
    nl = L // tl
    n_fill = 0
    ci = lambda i: jnp.minimum(i, nl - 1)
    in_specs = [pl.BlockSpec((None, tl, 512), lambda i, b: (which_x, ci(i), b)),
                pl.BlockSpec((tl, 512), lambda i, b: (ci(i), b)),
                pl.BlockSpec((None, tl, 512), lambda i, b: (which_z, ci(i), b)),
                pl.BlockSpec((None, 1, 512), lambda i, b: (order, 0, 0))]
    args = [u3, conv, z_arr, bias]
    aliases = {}
    kern = _gate_kernel
    if final is not None:
        T, row0, into = final
        out_shape = jax.ShapeDtypeStruct((T, 512), BF16)
        if into is not None:
            out_spec = pl.BlockSpec((tl, 512), lambda i, b: (row0 // tl + b * nl + i, 0))
            in_specs.append(pl.BlockSpec(memory_space=pl.ANY))
            args.append(into)
            aliases = {4: 0}
            kern = lambda x, c, z, b, into_ref, o: _gate_kernel(x, c, z, b, o)
        else:
            assert row0 == 0
            n_fill = (T - B * L) // tl
            out_spec = pl.BlockSpec((tl, 512), lambda i, b: (jnp.where(i < nl, b * nl + i, B * nl + (i - nl)), 0))

            def kern(x, c, z, b, o):
                i = pl.program_id(0)
                pl.when(i < nl)(lambda: _gate_kernel(x, c, z, b, o))

                @pl.when(i >= nl)
                def _():
                    o[...] = jnp.zeros_like(o)
    else:
        out_spec = pl.BlockSpec((None, tl, 512), lambda i, b: (0, i, b))
        out_shape = jax.ShapeDtypeStruct((1, L, B * 512), F32)
    return pl.pallas_call(
        kern, grid=(nl + n_fill, B), in_specs=in_specs, out_specs=out_spec, out_shape=out_shape,
        input_output_aliases=aliases,
        compiler_params=_cp(("arbitrary", "arbitrary")), name="hy_gate",
    )(*args)


def _dft_gen_kernel(ca_ref, sa_ref, cb_ref, sb_ref, ca2_ref, sa2_ref, cb2_ref, sb2_ref, fwd_ref, fwdp_ref, inv_ref):
    tb, L = cb_ref.shape
    ca, sa, cb, sb = ca_ref[...], sa_ref[...], cb_ref[...], sb_ref[...]
    c, s = (ca * cb - sa * sb).astype(fwd_ref.dtype), (-(sa * cb + ca * sb)).astype(fwd_ref.dtype)
    fwd_ref[0], fwd_ref[1] = c, s
    fwdp_ref[0], fwdp_ref[1] = c, s
    ca, sa, cb, sb = ca2_ref[...], sa2_ref[...], cb2_ref[...], sb2_ref[...]
    c, s = (ca * cb - sa * sb).astype(inv_ref.dtype), (-(sa * cb + ca * sb)).astype(inv_ref.dtype)
    for j in range(L // tb):
        inv_ref[:, 2 * j * tb:(2 * j + 1) * tb] = c[:, j * tb:(j + 1) * tb]
        inv_ref[:, (2 * j + 1) * tb:(2 * j + 2) * tb] = s[:, j * tb:(j + 1) * tb]


def _dft_matrices(L, tb):
    nb = L // tb
    col = jnp.arange(L, dtype=jnp.int32)[None, :]
    r0 = (jnp.arange(nb, dtype=jnp.int32) * tb)[:, None]
    i = jnp.arange(tb, dtype=jnp.int32)[:, None]
    unit = math.pi / (2 * L)
    trig = lambda ph: (jnp.cos((ph % (4 * L)).astype(F32) * unit), jnp.sin((ph % (4 * L)).astype(F32) * unit))
    ca, sa = trig((2 * r0 + 1) * col)
    cb, sb = trig(2 * i * col)
    ca2, sa2 = trig((2 * col + 1) * r0)
    cb2, sb2 = trig((2 * col + 1) * i)
    blk = lambda: pl.BlockSpec((None, 1, L), lambda j: (j, 0, 0))
    shared = lambda: pl.BlockSpec((tb, L), lambda j: (0, 0))
    a3 = lambda a: a[:, None, :]
    fwd, fwd_plain, inv = pl.pallas_call(
        _dft_gen_kernel, grid=(nb,),
        in_specs=[blk(), blk(), shared(), shared(), blk(), blk(), shared(), shared()],
        out_specs=[pl.BlockSpec((None, 2, tb, L), lambda j: (j, 0, 0, 0)),
                   pl.BlockSpec((2, tb, L), lambda j: (0, j, 0)), pl.BlockSpec((tb, 2 * L), lambda j: (j, 0))],
        out_shape=[jax.ShapeDtypeStruct((nb, 2, tb, L), BF16), jax.ShapeDtypeStruct((2, L, L), BF16),
                   jax.ShapeDtypeStruct((L, 2 * L), BF16)],
        compiler_params=_cp(("parallel",)), name="hy_dft_gen",
    )(a3(ca), a3(sa), cb, sb, a3(ca2), a3(sa2), cb2, sb2)
    return fwd.reshape(2 * L, L), fwd_plain.reshape(2 * L, L), inv


def _filter_feats(L):
    t = jnp.arange(L, dtype=F32) / L
    kf = jnp.arange(1, HY_POS_FREQS + 1, dtype=F32)
    ang = 2.0 * math.pi * t[:, None] * kf
    feats = jnp.concatenate([t[:, None], jnp.sin(ang), jnp.cos(ang)], axis=-1)
    return jnp.pad(feats, ((0, 0), (0, LANE - feats.shape[1])))


def hyena_seq(p1, L, row_block0, B, consts, wts, tm, tl, T, into=None):
    fwd, fwd_plain, inv, feats = consts
    short_w, w1p, b1, w2, b2, fr, w3, deltas, bias = wts
    tb = min(L, ROW_TILE)
    nkb = L // tb
    u3 = short_conv(p1, short_w, L, row_block0, B)
    filt = hyena_filters(feats, w1p, b1, w2, b2, fr, w3, deltas, L)
    gid_f = jnp.repeat(jnp.arange(2, dtype=jnp.int32), L // tm)
    zeros_f = jnp.zeros((nkb,), jnp.int32)
    zeros_i = jnp.zeros((L // tm,), jnp.int32)
    NW = B * 512
    gspec = gmm(fwd_plain, filt, gid_f, tm=tm, tn=512, n_tiles=HY_ORDER, out_dtype=F32, name="hy_dft_filter")

    def cmul(acc, gre_ref, gim_ref):
        zre, zim, gre, gim = acc[:tb], acc[tb:], gre_ref[...], gim_ref[...]
        return jnp.concatenate([zre * gre - zim * gim, zre * gim + zim * gre], axis=0) * (1.0 / L)

    def gated(acc, x_ref, z_ref, b_ref):
        return x_ref[...] * (acc + b_ref[...] * z_ref[...])

    z_arr, which_z = u3, 0
    out = None
    for o in range(HY_ORDER):
        y = gmm(fwd, z_arr, zeros_f + which_z, tm=2 * tb, tn=512, n_tiles=B, out_dtype=BF16, epilogue=cmul,
                extra=[(gspec, (tb, 512), functools.partial(lambda o, n, m: (m, o), o)),
                       (gspec, (tb, 512), functools.partial(lambda o, n, m: (nkb + m, o), o))], name="hy_dft_fwd")
        y = y.reshape(1, 2 * L, NW)
        if o < HY_ORDER - 1:
            out = gmm(inv, y, zeros_i, tm=tm, tn=512, n_tiles=B, out_dtype=F32, epilogue=gated,
                      extra=[(u3, (None, tm, 512), functools.partial(lambda o, n, m: (1 + o, m, n), o)),
                             (z_arr, (None, tm, 512), functools.partial(lambda w, n, m: (w, m, n), which_z)),
                             (bias, (None, 1, 512), functools.partial(lambda o, n, m: (o, 0, 0), o))],
                      name="hy_dft_inv_gate").reshape(1, L, NW)
        else:
            conv = gmm(inv, y, zeros_i, tm=tm, tn=512, n_tiles=B, out_dtype=F32, name="hy_dft_inv")
            out = hyena_gate(u3, 1 + o, conv, z_arr, which_z, bias, o, L, B, tl, (T, row_block0 * L, into))
        z_arr, which_z = out, 0
    return out


def _gather_kernel(idx_ref, src_ref, *refs, tm, k):
    if k > 1:
        w_ref, o_ref, buf_ref, sem = refs
    else:
        o_ref, buf_ref, sem = refs
    i, n = pl.program_id(0), pl.num_programs(0)
    M = n * tm

    def issue_tile(tile, slot):
        for j in range(k):
            def issue(r, carry):
                pltpu.make_async_copy(src_ref.at[pl.ds(idx_ref[j * M + tile * tm + r], 1)],
                                      buf_ref.at[slot, j, pl.ds(r, 1)], sem.at[slot]).start()
                return carry

            lax.fori_loop(0, tm, issue, 0, unroll=8)

    @pl.when(i == 0)
    def _():
        issue_tile(0, 0)

    @pl.when(i + 1 < n)
    def _():
        issue_tile(i + 1, (i + 1) % 2)

    slot = i % 2
    for j in range(k):
        pltpu.make_async_copy(src_ref.at[pl.ds(0, tm)], buf_ref.at[slot, j], sem.at[slot]).wait()
    if k > 1:
        w = w_ref[...]
        acc = w[:, 0:1] * buf_ref[slot, 0]
        for j in range(1, k):
            acc = acc + w[:, j:j + 1] * buf_ref[slot, j]
        o_ref[...] = acc.astype(o_ref.dtype)
    else:
        o_ref[...] = buf_ref[slot, 0].astype(o_ref.dtype)


def gather_rows(src, idx, tm, out_dtype, weights=None, k=1):
    M = idx.shape[0] // k
    D = src.shape[1]
    in_specs = [pl.BlockSpec(memory_space=pl.ANY)]
    args = [src]
    if k > 1:
        in_specs.append(pl.BlockSpec((tm, LANE), lambda i, idx: (i, 0)))
        args.append(weights)
    return pl.pallas_call(
        functools.partial(_gather_kernel, tm=tm, k=k),
        grid_spec=pltpu.PrefetchScalarGridSpec(
            num_scalar_prefetch=1, grid=(M // tm,), in_specs=in_specs,
            out_specs=pl.BlockSpec((tm, D), lambda i, idx: (i, 0)),
            scratch_shapes=[pltpu.VMEM((2, k, tm, D), src.dtype), pltpu.SemaphoreType.DMA((2,))]),
        out_shape=jax.ShapeDtypeStruct((M, D), out_dtype),
        compiler_params=_cp(("arbitrary",)), name="gather_rows" if k == 1 else "gather_combine",
    )(idx, *args)


def moe_ffn(rows, h_f32, rw, ri, w_gate, w_up, w_down, layer_moe, tm=512):
    T, D = rows.T, rows.D
    E = w_gate.shape[2]
    e_flat = ri[:, :TOP_K].T.reshape(-1)
    onehot = (e_flat[:, None] == jnp.arange(N_EXPERTS, dtype=jnp.int32)[None, :]).astype(jnp.int32)
    csum = jnp.cumsum(onehot, axis=0)
    rank = jnp.sum(csum * onehot, axis=1) - 1
    counts = csum[-1]
    padded = ((counts + tm - 1) // tm) * tm
    ends = jnp.cumsum(padded)
    starts = ends - padded
    dest = starts[e_flat] + rank
    m_pad = TOP_K * T + N_EXPERTS * tm
    token = jnp.tile(jnp.arange(T, dtype=jnp.int32), TOP_K)
    src_tok = jnp.zeros((m_pad,), jnp.int32).at[dest].set(token)
    tile_row = jnp.arange(m_pad // tm, dtype=jnp.int32) * tm
    gid = jnp.minimum(jnp.sum((tile_row[:, None] >= ends[None, :]).astype(jnp.int32), axis=1), N_EXPERTS - 1)
    gid = gid + layer_moe * N_EXPERTS
    used = (ends[-1:] // tm).astype(jnp.int32)

    xs = gather_rows(h_f32, src_tok, tm, BF16)
    tile = lambda n, m: (m, n)
    g = gmm(xs, w_gate, gid, tm=tm, tn=E // 2, n_tiles=2, out_dtype=BF16, epilogue=_silu, used=used,
            name="moe_gate")
    hh = gmm(xs, w_up, gid, tm=tm, tn=E // 2, n_tiles=2, out_dtype=BF16, epilogue=_times,
             extra=[(g, (tm, E // 2), tile)], used=used, name="moe_up")
    y = gmm(hh, w_down, gid, tm=tm, tn=D // 2, n_tiles=2, out_dtype=F32, used=used, name="moe_down")
    return gather_rows(y, dest, ROW_TILE, BF16, weights=rw, k=TOP_K)


def kernel(x, c, ctx, c_ctx, w_mod, b_mod, g_mix_pre, g_mix_post, g_ffn_pre, g_ffn_post, w_in, w_out, attn_sink,
           hy_short, hy_w1, hy_b1, hy_w2, hy_b2, hy_freq, hy_w3, hy_bias, ml_gate_bias, ml_norm, mla_q_norm,
           mla_w_uq, mla_kv_norm, mla_w_ukv, ffn_w_gate, ffn_w_up, ffn_w_down, moe_router, moe_w_gate, moe_w_up,
           moe_w_down):
    B, S, D = x.shape
    NC = ctx.shape[1]
    depth = w_mod.shape[0]
    rows = _Rows(B, S, NC, D)
    T = rows.T
    TM = 512 if T % 512 == 0 else ROW_TILE
    n_mt = T // TM
    ffn_dim = ffn_w_gate.shape[2]

    xs = jnp.concatenate([x.reshape(B * S, D), ctx.reshape(B * NC, D)], axis=0)

    cm = jnp.concatenate([c, c_ctx[None, :]], axis=0)
    cm = jnp.pad(jax.nn.silu(cm), ((0, 16 - (B + 1)), (0, 0))).astype(BF16)
    mod_all = gmm(jnp.tile(cm, (depth, 1)), w_mod, jnp.arange(depth, dtype=jnp.int32), tm=16, tn=1536,
                  n_tiles=N_MOD * D // 1536, out_dtype=F32, name="adaln")
    mods = (mod_all.reshape(depth, 16, N_MOD * D)[:, :B + 1] + b_mod[:, None, :]).reshape(depth, B + 1, N_MOD, 1, D)

    def rope_table(rot_dim, reps, lane0, width):
        cos, sin = _axial_tables(S, rot_dim)
        cos, sin = jnp.tile(cos, (1, reps)), jnp.tile(sin, (1, reps))
        padw = ((0, 0), (lane0, width - lane0 - cos.shape[1]))
        cos = jnp.pad(cos, padw, constant_values=1.0)
        sin = jnp.pad(sin, padw)
        cos = jnp.concatenate([jnp.tile(cos, (B, 1)), jnp.ones((B * NC, width), F32)], axis=0)
        sin = jnp.concatenate([jnp.tile(sin, (B, 1)), jnp.zeros((B * NC, width), F32)], axis=0)
        return cos, sin

    cos_a, sin_a = rope_table(64, 2, 0, LANE)
    cos_m, sin_m = rope_table(MLA_ROPE, 1, MLA_NOPE, LANE)
    hy_consts = {L: _dft_matrices(L, min(L, ROW_TILE)) + (_filter_feats(L),) for L in (S, NC)}
    deltas = jnp.abs(jnp.linspace(math.log(HY_DECAY_TARGET) / HY_SLOW_DECAY,
                                  math.log(HY_DECAY_TARGET) / HY_FAST_DECAY, 512, dtype=F32))[None, :]

    w_in_t = jnp.swapaxes(w_in, 1, 2)
    wq = mla_w_uq.reshape(depth, -1, MLA_HEADS, MLA_NOPE + MLA_ROPE)
    wq = jnp.pad(wq, ((0, 0), (0, 0), (0, 0), (0, LANE - MLA_NOPE - MLA_ROPE))).reshape(depth, -1, MLA_HEADS * LANE)
    wkv = mla_w_ukv.reshape(depth, -1, MLA_HEADS, 2 * 64)
    wk = jnp.pad(wkv[..., :MLA_NOPE], ((0, 0), (0, 0), (0, 0), (0, LANE - MLA_NOPE)))
    wk = wk.reshape(depth, -1, MLA_HEADS * LANE)
    wv = wkv[..., MLA_NOPE:].reshape(depth, -1, MLA_HEADS * 64)
    w1p = jnp.pad(hy_w1, ((0, 0), (0, LANE - hy_w1.shape[1]), (0, 0)))
    gbias = jnp.pad(ml_gate_bias.reshape(depth, 1, 4 * ML_HEADS), ((0, 0), (0, 0), (0, LANE - 4 * ML_HEADS)))
    router = jnp.pad(moe_router, ((0, 0), (0, 0), (0, LANE - N_EXPERTS)))
    n_moe = moe_w_gate.shape[0]
    mw_gate = moe_w_gate.reshape(n_moe * N_EXPERTS, D, -1)
    mw_up = moe_w_up.reshape(n_moe * N_EXPERTS, D, -1)
    mw_down = moe_w_down.reshape(n_moe * N_EXPERTS, -1, D)
    vec = lambda a, l: a[l][None, :]

    h = norm_mod(rows, xs, vec(g_mix_pre, 0), mods[0], 0, 1)
    for layer in range(depth):
        gid = jnp.full((n_mt,), layer, jnp.int32)
        ml = mods[layer]
        p1 = gmm(h, w_in_t, gid, tm=TM, tn=768, n_tiles=PA_COLS // 768, out_dtype=BF16, w_t=True, name="proj_in")
        p2 = gmm(h, w_in_t, gid, tm=TM, tn=768, n_tiles=2, n_off=PA_COLS // 768, out_dtype=F32, w_t=True,
                 name="proj_in_f32")

        qkv = attn_rope(rows, p1, cos_a, sin_a)
        a_all = window_attn(rows, qkv, attn_sink[layer])

        hy_w = (hy_short[layer], w1p[layer], vec(hy_b1, layer), hy_w2[layer], vec(hy_b2, layer),
                vec(hy_freq, layer), hy_w3[layer], deltas, hy_bias[layer][:, None, :])
        b_l = hyena_seq(p1, S, 0, B, hy_consts[S], hy_w, tm=512, tl=512, T=T)
        b_all = hyena_seq(p1, NC, B * S // NC, B, hy_consts[NC], hy_w, tm=NC, tl=NC, T=T, into=b_l)

        hf, hb = mlstm_scan(rows, p1, p2, gbias[layer])
        m_all = mlstm_finish(rows, hf, hb, p2, vec(ml_norm, layer))

        qn, kvn, kr = mla_norms(rows, p2, vec(mla_q_norm, layer), vec(mla_kv_norm, layer))
        q_up = gmm(qn, wq, gid, tm=TM, tn=MLA_HEADS * LANE, n_tiles=1, out_dtype=F32, name="mla_uq")
        k_up = gmm(kvn, wk, gid, tm=TM, tn=MLA_HEADS * LANE, n_tiles=1, out_dtype=F32, name="mla_uk")
        vm = gmm(kvn, wv, gid, tm=TM, tn=MLA_HEADS * 64, n_tiles=1, out_dtype=BF16, name="mla_uv")
        qm, km = mla_prep(rows, q_up, k_up, kr, cos_m, sin_m)
        d_all = mla_attn(rows, qm, km, vm)

        y = gmm([a_all, b_all, m_all, d_all], w_out, gid, tm=TM, tn=1024, n_tiles=D // 1024, out_dtype=BF16,
                name="proj_out")

        i = layer // 2
        nxt = (vec(g_ffn_pre, layer), ml, 3, 4)
        if layer % 2 == 0:
            xs, h2 = post(rows, xs, y, vec(g_mix_post, layer), ml, 2, nxt=nxt)
            gi = jnp.full((n_mt,), i, jnp.int32)
            g = gmm(h2, ffn_w_gate, gi, tm=TM, tn=ffn_dim // 4, n_tiles=4, out_dtype=BF16, epilogue=_silu,
                    name="ffn_gate")
            hh = gmm(h2, ffn_w_up, gi, tm=TM, tn=ffn_dim // 4, n_tiles=4, out_dtype=BF16, epilogue=_times,
                     extra=[(g, (TM, ffn_dim // 4), lambda n, m: (m, n))], name="ffn_up")
            y2 = gmm(hh, ffn_w_down, gi, tm=TM, tn=512, n_tiles=D // 512, out_dtype=BF16, name="ffn_down")
        else:
            xs, h2, h2f, rw, ri = post(rows, xs, y, vec(g_mix_post, layer), ml, 2, nxt=nxt, router=router[i])
            y2 = moe_ffn(rows, h2f, rw, ri, mw_gate, mw_up, mw_down, i)
        if layer + 1 < depth:
            xs, h = post(rows, xs, y2, vec(g_ffn_post, layer), ml, 5,
                         nxt=(vec(g_mix_pre, layer + 1), mods[layer + 1], 0, 1))
        else:
            (xs,) = post(rows, xs, y2, vec(g_ffn_post, layer), ml, 5, latent_only=True)
    return xs.reshape(B, S, D)
```

```python
import functools
import math

import numpy as np
import jax
import jax.numpy as jnp
from jax import lax
from jax.experimental import pallas as pl
from jax.experimental.pallas import tpu as pltpu

F32 = jnp.float32
BF16 = jnp.bfloat16
HI = lax.Precision.HIGHEST

EPS = 1e-6
ROPE_BASE = 10000.0
GRID_W = 64
BLOCK = 128
WINDOW = 128
N_MOD = 6
A_HEADS, A_KV_HEADS = 8, 2
HY_ORDER, HY_SHORT, HY_POS_FREQS = 2, 3, 8
HY_DECAY_TARGET, HY_FAST_DECAY, HY_SLOW_DECAY = 1e-2, 0.3, 1.5
ML_HEADS, ML_CHUNK = 4, 128
MLA_HEADS, MLA_NOPE, MLA_ROPE = 8, 64, 32
N_EXPERTS, TOP_K = 8, 2

PA_COLS = 3840
PB_GATES = 4352 - PA_COLS
MLA_HP = 8
LANE = 128
ROW_TILE = 256
NEG = -1e30
LOG2E = math.log2(math.e)
VMEM_LIMIT = 56 * 1024 * 1024


def _cp(sem, vmem=VMEM_LIMIT):
    return pltpu.CompilerParams(dimension_semantics=sem, vmem_limit_bytes=vmem)


def _gmm_kernel(gid_ref, used_ref, *refs, n_a, cast, w_t, epilogue, n_extra, w_n, tn, n_off):
    a_refs, w_ref, rest = refs[:n_a], refs[n_a], refs[n_a + 1:]
    extra_refs, rest = rest[:n_extra], rest[n_extra:]
    o_ref = rest[0]
    n, m = pl.program_id(0), pl.program_id(1)
    n_axis = 0 if w_t else 1
    if cast:
        wbf_ref = rest[1]
        prev = gid_ref[jnp.maximum(m - 1, 0)]

        @pl.when((m == 0) | (gid_ref[m] != prev))
        def _():
            w = w_ref[...]
            if w_n is not None:
                col = (n + n_off) * tn + lax.broadcasted_iota(jnp.int32, w.shape, n_axis)
                w = jnp.where(col < w_n, w, 0.0)
            wbf_ref[...] = w.astype(BF16)

        src = wbf_ref
    else:
        src = w_ref

    @pl.when(m < used_ref[0])
    def _():
        acc, lo = None, 0
        for a_ref in a_refs:
            k = a_ref.shape[1]
            if w_t:
                part = lax.dot_general(a_ref[...], src[:, lo:lo + k], (((1,), (1,)), ((), ())),
                                       preferred_element_type=F32)
            else:
                part = jnp.dot(a_ref[...], src[lo:lo + k, :], preferred_element_type=F32)
            acc = part if acc is None else acc + part
            lo += k
        if epilogue is not None:
            acc = epilogue(acc, *extra_refs)
        o_ref[...] = acc.astype(o_ref.dtype)

    @pl.when(m >= used_ref[0])
    def _():
        o_ref[...] = jnp.zeros_like(o_ref)


def _silu(acc):
    return acc * jax.nn.sigmoid(acc)


def _times(acc, g_ref):
    return acc * g_ref[...].astype(F32)


def gmm(a, w, gid, *, tm, tn, n_tiles, out_dtype, n_off=0, epilogue=None, extra=(), w_t=False, used=None,
        w_single_buffer=False, name="gmm"):
    a_list = list(a) if isinstance(a, (list, tuple)) else [a]
    M = a_list[0].shape[0]
    K = sum(p.shape[1] for p in a_list)
    mt = M // tm
    k_axis, n_axis = (2, 1) if w_t else (1, 2)
    assert mt * tm == M and w.shape[k_axis] == K and gid.shape == (mt,)
    cast = w.dtype != BF16
    partial_n = (n_off + n_tiles) * tn > w.shape[n_axis]
    assert cast or not partial_n
    if used is None:
        used = jnp.full((1,), mt, jnp.int32)
    in_specs = [pl.BlockSpec((tm, p.shape[1]), lambda n, m, g, u: (m, 0)) for p in a_list]
    w_mode = dict(pipeline_mode=pl.Buffered(1)) if w_single_buffer else {}
    if w_t:
        in_specs.append(pl.BlockSpec((None, tn, K), lambda n, m, g, u: (g[m], n + n_off, 0), **w_mode))
    else:
        in_specs.append(pl.BlockSpec((None, K, tn), lambda n, m, g, u: (g[m], 0, n + n_off), **w_mode))
    args = a_list + [w]
    for arr, block, imap in extra:
        in_specs.append(pl.BlockSpec(block, functools.partial(lambda f, n, m, g, u: f(n, m), imap)))
        args.append(arr)
    kern = functools.partial(_gmm_kernel, n_a=len(a_list), cast=cast, w_t=w_t, epilogue=epilogue,
                             n_extra=len(extra), w_n=w.shape[n_axis] if partial_n else None, tn=tn, n_off=n_off)
    return pl.pallas_call(
        kern,
        grid_spec=pltpu.PrefetchScalarGridSpec(
            num_scalar_prefetch=2, grid=(n_tiles, mt), in_specs=in_specs,
            out_specs=pl.BlockSpec((tm, tn), lambda n, m, g, u: (m, n)),
            scratch_shapes=[pltpu.VMEM((tn, K) if w_t else (K, tn), BF16)] if cast else []),
        out_shape=jax.ShapeDtypeStruct((M, n_tiles * tn), out_dtype),
        compiler_params=_cp(("arbitrary", "arbitrary")),
        name=name,
    )(gid, used, *args)


def _rms(v, g):
    return v * lax.rsqrt(jnp.mean(v * v, axis=-1, keepdims=True) + EPS) * g


def _norm_mod_kernel(x_ref, g_ref, sh_ref, sc_ref, h_ref):
    h_ref[...] = (_rms(x_ref[...], g_ref[...]) * (1.0 + sc_ref[...]) + sh_ref[...]).astype(h_ref.dtype)


def _post_kernel(*refs, with_next, with_router):
    x_ref, y_ref, gp_ref, gate_ref = refs[:4]
    refs = refs[4:]
    xn = x_ref[...] + gate_ref[...] * _rms(y_ref[...].astype(F32), gp_ref[...])
    if not with_next:
        refs[0][...] = xn
        return
    gn_ref, sh_ref, sc_ref = refs[:3]
    refs = refs[3:]
    if with_router:
        r_ref, refs = refs[0], refs[1:]
    refs[0][...] = xn
    h = _rms(xn, gn_ref[...]) * (1.0 + sc_ref[...]) + sh_ref[...]
    refs[1][...] = h.astype(BF16)
    if with_router:
        hf_ref, rw_ref, ri_ref = refs[2:5]
        hf_ref[...] = h
        logits = jnp.dot(h, r_ref[...], preferred_element_type=F32, precision=HI)
        lane = lax.broadcasted_iota(jnp.int32, logits.shape, 1)
        logits = jnp.where(lane < N_EXPERTS, logits, -jnp.inf)
        v1 = jnp.max(logits, axis=1, keepdims=True)
        i1 = jnp.min(jnp.where(logits == v1, lane, LANE), axis=1, keepdims=True)
        l2 = jnp.where(lane == i1, -jnp.inf, logits)
        v2 = jnp.max(l2, axis=1, keepdims=True)
        i2 = jnp.min(jnp.where(l2 == v2, lane, LANE), axis=1, keepdims=True)
        e = jnp.exp(v2 - v1)
        w1 = 1.0 / (1.0 + e)
        w2 = e / (1.0 + e)
        rw_ref[...] = jnp.where(lane == 0, w1, jnp.where(lane == 1, w2, 0.0))
        ri_ref[...] = jnp.where(lane == 0, i1, jnp.where(lane == 1, i2, 0))


class _Rows:
    def __init__(self, B, S, NC, D):
        self.B, self.S, self.NC, self.D = B, S, NC, D
        self.T = B * S + B * NC
        assert S % ROW_TILE == 0 and NC % ROW_TILE == 0
        self.nt = self.T // ROW_TILE

    def mod_row(self, i):
        n_lat = self.B * self.S // ROW_TILE
        return jnp.where(i < n_lat, i // (self.S // ROW_TILE), self.B)


def _row_spec(D):
    return pl.BlockSpec((ROW_TILE, D), lambda i: (i, 0))


def _vec_spec(D):
    return pl.BlockSpec((1, D), lambda i: (0, 0))


def _mod_spec(rows, j):
    return pl.BlockSpec((None, None, 1, rows.D), lambda i: (rows.mod_row(i), j, 0, 0))


def norm_mod(rows, x, g, mods, j_shift, j_scale):
    D = rows.D
    return pl.pallas_call(
        _norm_mod_kernel, grid=(rows.nt,),
        in_specs=[_row_spec(D), _vec_spec(D), _mod_spec(rows, j_shift), _mod_spec(rows, j_scale)],
        out_specs=_row_spec(D),
        out_shape=jax.ShapeDtypeStruct((rows.T, D), BF16),
        compiler_params=_cp(("parallel",)), name="norm_mod",
    )(x, g, mods, mods)


def post(rows, x, y, g_post, mods, j_gate, nxt=None, router=None, latent_only=False):
    D = rows.D
    T = rows.B * rows.S if latent_only else rows.T
    in_specs = [_row_spec(D), _row_spec(D), _vec_spec(D), _mod_spec(rows, j_gate)]
    args = [x, y, g_post, mods]
    out_specs = [_row_spec(D)]
    out_shape = [jax.ShapeDtypeStruct((T, D), F32)]
    if nxt is not None:
        g_next, mods_next, j_shift, j_scale = nxt
        in_specs += [_vec_spec(D), _mod_spec(rows, j_shift), _mod_spec(rows, j_scale)]
        args += [g_next, mods_next, mods_next]
        out_specs.append(_row_spec(D))
        out_shape.append(jax.ShapeDtypeStruct((T, D), BF16))
        if router is not None:
            in_specs.append(pl.BlockSpec((D, LANE), lambda i: (0, 0)))
            args.append(router)
            out_specs += [_row_spec(D), _row_spec(LANE), _row_spec(LANE)]
            out_shape += [jax.ShapeDtypeStruct((T, D), F32), jax.ShapeDtypeStruct((T, LANE), F32),
                          jax.ShapeDtypeStruct((T, LANE), jnp.int32)]
    kern = functools.partial(_post_kernel, with_next=nxt is not None, with_router=router is not None)
    return pl.pallas_call(
        kern, grid=(T // ROW_TILE,), in_specs=in_specs, out_specs=out_specs, out_shape=out_shape,
        compiler_params=_cp(("parallel",)), name="post",
    )(*args)


def _rot_pairs(x, q):
    lane = lax.broadcasted_iota(jnp.int32, x.shape, 1)
    n = x.shape[1]
    return jnp.where(lane % (2 * q) < q, -pltpu.roll(x, n - q, 1), pltpu.roll(x, q, 1))


def _axial_tables(S, rot_dim):
    rows = S // GRID_W
    r = jnp.repeat(jnp.arange(rows, dtype=F32), GRID_W)
    col = jnp.tile(jnp.arange(GRID_W, dtype=F32), rows)
    half = rot_dim // 2
    inv = ROPE_BASE ** (-jnp.arange(0, half, 2, dtype=F32) / half)
    ar, ac = r[:, None] * inv, col[:, None] * inv
    ang = jnp.concatenate([ar, ar, ac, ac], axis=-1)
    return jnp.cos(ang), jnp.sin(ang)


def _attn_rope_kernel(p_ref, cos_ref, sin_ref, o_ref, *, scale):
    cos, sin = cos_ref[...], sin_ref[...]
    nq = A_HEADS * 64 // LANE
    for s in range(nq + 1):
        xs = p_ref[:, s * LANE:(s + 1) * LANE].astype(F32)
        r = xs * cos + _rot_pairs(xs, 16) * sin
        if s < nq:
            r = r * scale
        o_ref[:, s * LANE:(s + 1) * LANE] = r.astype(o_ref.dtype)
    o_ref[:, (nq + 1) * LANE:] = p_ref[:, (nq + 1) * LANE:].astype(o_ref.dtype)


def attn_rope(rows, p1, cos, sin):
    W = 768
    return pl.pallas_call(
        functools.partial(_attn_rope_kernel, scale=64 ** -0.5 * LOG2E), grid=(rows.nt,),
        in_specs=[pl.BlockSpec((ROW_TILE, W), lambda i: (i, 0)), _row_spec(LANE), _row_spec(LANE)],
        out_specs=pl.BlockSpec((ROW_TILE, W), lambda i: (i, 0)),
        out_shape=jax.ShapeDtypeStruct((rows.T, W), BF16),
        compiler_params=_cp(("parallel",)), name="attn_rope",
    )(p1, cos, sin)


def _nt(a, b):
    return lax.dot_general(a, b, (((1,), (1,)), ((), ())), preferred_element_type=F32)


def _softmax_av(scores, values, sink=None):
    m = scores[0].max(axis=1, keepdims=True)
    for s in scores[1:]:
        m = jnp.maximum(m, s.max(axis=1, keepdims=True))
    if sink is not None:
        m = jnp.maximum(m, sink)
    den = jnp.exp2(sink - m) if sink is not None else 0.0
    acc = None
    for s, v in zip(scores, values):
        p = jnp.exp2(s - m)
        den = den + p.sum(axis=1, keepdims=True)
        pv = jnp.dot(p.astype(BF16), v, preferred_element_type=F32)
        acc = pv if acc is None else acc + pv
    return acc / den


def _window_kernel(sink_ref, q_ref, kp_ref, kc_ref, kn_ref, vp_ref, vc_ref, vn_ref, kx_ref, vx_ref, o_ref, *, nb):
    n = pl.program_id(1)
    groups = A_HEADS // A_KV_HEADS
    d = 64
    kb = jnp.concatenate([kp_ref[...], kc_ref[...], kn_ref[...]], axis=0)
    vb = jnp.concatenate([vp_ref[...], vc_ref[...], vn_ref[...]], axis=0)
    Q = groups * BLOCK
    qi = lax.broadcasted_iota(jnp.int32, (Q, 3 * BLOCK), 0) % BLOCK
    kj = lax.broadcasted_iota(jnp.int32, (Q, 3 * BLOCK), 1)
    k_abs = (n - 1) * BLOCK + kj
    valid = (jnp.abs(kj - BLOCK - qi) <= WINDOW) & (k_abs >= 0) & (k_abs < nb * BLOCK) & (n < nb)
    hrow = lax.broadcasted_iota(jnp.int32, (Q, 1), 0) // BLOCK
    outs = []
    for g in range(A_KV_HEADS):
        qg = jnp.concatenate([q_ref[:, (g * groups + h) * d:(g * groups + h + 1) * d] for h in range(groups)], axis=0)
        ksl = slice(g * d, (g + 1) * d)
        s_band = jnp.where(valid, _nt(qg, kb[:, ksl]), NEG)
        s_ctx = _nt(qg, kx_ref[:, ksl])
        sink = jnp.zeros((Q, 1), F32)
        for h in range(groups):
            sink = jnp.where(hrow == h, sink_ref[g * groups + h] * LOG2E, sink)
        o = _softmax_av([s_band, s_ctx], [vb[:, ksl], vx_ref[:, ksl]], sink)
        outs += [o[h * BLOCK:(h + 1) * BLOCK] for h in range(groups)]
    o_ref[...] = jnp.concatenate(outs, axis=1).astype(o_ref.dtype)


def window_attn(rows, qkv, sink):
    B, S, NC = rows.B, rows.S, rows.NC
    nb, ncb = S // BLOCK, NC // BLOCK
    cb = B * S // NC

    def q_block(b, n):
        return jnp.where(n < nb, b * nb + n, B * nb + b * ncb + (n - nb))

    def kv_spec(col, off):
        return pl.BlockSpec((BLOCK, LANE), lambda b, n, s: (b * nb + jnp.clip(n + off, 0, nb - 1), col))

    in_specs = [pl.BlockSpec((BLOCK, 512), lambda b, n, s: (q_block(b, n), 0)),
                kv_spec(4, -1), kv_spec(4, 0), kv_spec(4, 1), kv_spec(5, -1), kv_spec(5, 0), kv_spec(5, 1),
                pl.BlockSpec((NC, LANE), lambda b, n, s: (cb + b, 4)),
                pl.BlockSpec((NC, LANE), lambda b, n, s: (cb + b, 5))]
    return pl.pallas_call(
        functools.partial(_window_kernel, nb=nb),
        grid_spec=pltpu.PrefetchScalarGridSpec(
            num_scalar_prefetch=1, grid=(B, nb + ncb), in_specs=in_specs,
            out_specs=pl.BlockSpec((BLOCK, 512), lambda b, n, s: (q_block(b, n), 0))),
        out_shape=jax.ShapeDtypeStruct((rows.T, 512), BF16),
        compiler_params=_cp(("parallel", "parallel")), name="window_attn",
    )(sink, *([qkv] * 9))


MLA_SCALE = (MLA_NOPE + MLA_ROPE) ** -0.5 * LOG2E


def _mla_q_epilogue(acc, cos_ref, sin_ref):
    cos, sin = cos_ref[...], sin_ref[...]
    slabs = [acc[:, h * LANE:(h + 1) * LANE] for h in range(MLA_HEADS)]
    return jnp.concatenate([(s * cos + _rot_pairs(s, 8) * sin) * MLA_SCALE for s in slabs], axis=1)


def _mla_k_epilogue(acc, kr_ref, cos_ref, sin_ref):
    kr = kr_ref[...]
    kr = kr * cos_ref[...] + _rot_pairs(kr, 8) * sin_ref[...]
    return jnp.concatenate([acc[:, h * LANE:(h + 1) * LANE] + kr for h in range(MLA_HEADS)], axis=1)


def _mla_kernel(q_ref, kl_ref, kx_ref, vl_ref, vx_ref, o_ref, *, nq):
    def run(with_latent_keys):
        outs = []
        for j in range(MLA_HP):
            sl, vs = slice(j * LANE, (j + 1) * LANE), slice(j * 64, (j + 1) * 64)
            q = q_ref[:, sl]
            scores, values = [_nt(q, kx_ref[:, sl])], [vx_ref[:, vs]]
            if with_latent_keys:
                scores.insert(0, _nt(q, kl_ref[:, sl]))
                values.insert(0, vl_ref[:, vs])
            outs.append(_softmax_av(scores, values))
        o_ref[...] = jnp.concatenate(outs, axis=1).astype(o_ref.dtype)

    i = pl.program_id(2)
    pl.when(i < nq)(lambda: run(True))
    pl.when(i >= nq)(lambda: run(False))


def mla_attn(rows, qm, km, vm, tq=256):
    B, S, NC = rows.B, rows.S, rows.NC
    cb = B * S // NC
    nq, nqc = S // tq, NC // tq
    kw, vw = MLA_HP * LANE, MLA_HP * 64

    def q_block(b, i):
        return jnp.where(i < nq, b * nq + i, B * nq + b * nqc + (i - nq))

    return pl.pallas_call(
        functools.partial(_mla_kernel, nq=nq), grid=(B, MLA_HEADS // MLA_HP, nq + nqc),
        in_specs=[pl.BlockSpec((tq, kw), lambda b, h, i: (q_block(b, i), h)),
                  pl.BlockSpec((S, kw), lambda b, h, i: (b, h)),
                  pl.BlockSpec((NC, kw), lambda b, h, i: (cb + b, h)),
                  pl.BlockSpec((S, vw), lambda b, h, i: (b, h)),
                  pl.BlockSpec((NC, vw), lambda b, h, i: (cb + b, h))],
        out_specs=pl.BlockSpec((tq, vw), lambda b, h, i: (q_block(b, i), h)),
        out_shape=jax.ShapeDtypeStruct((rows.T, MLA_HEADS * 64), BF16),
        compiler_params=_cp(("parallel", "parallel", "parallel")), name="mla_attn",
    )(qm, km, km, vm, vm)


def _mla_norms_kernel(x0_ref, x1_ref, x2_ref, x3_ref, gq_ref, gkv_ref, q_ref, kv_ref, kr_ref):
    x = jnp.concatenate([x0_ref[...], x1_ref[...], x2_ref[...], x3_ref[...]], axis=1)
    q_ref[...] = _rms(x[:, 16:528], gq_ref[...]).astype(q_ref.dtype)
    kv_ref[...] = _rms(x[:, 528:784], gkv_ref[...]).astype(kv_ref.dtype)
    z = lambda w: jnp.zeros((x.shape[0], w), F32)
    kr_ref[...] = jnp.concatenate([z(MLA_NOPE), x[:, 784:816], z(LANE - MLA_NOPE - MLA_ROPE)], axis=1)


def mla_norms(rows, p2, gq, gkv):
    T = rows.T
    return pl.pallas_call(
        _mla_norms_kernel, grid=(rows.nt,),
        in_specs=[pl.BlockSpec((ROW_TILE, 256), functools.partial(lambda j, i: (i, PB_GATES // 256 + j), j))
                  for j in range(4)] + [_vec_spec(512), _vec_spec(256)],
        out_specs=[_row_spec(512), _row_spec(256), _row_spec(LANE)],
        out_shape=[jax.ShapeDtypeStruct((T, 512), BF16), jax.ShapeDtypeStruct((T, 256), BF16),
                   jax.ShapeDtypeStruct((T, LANE), F32)],
        compiler_params=_cp(("parallel",)), name="mla_norms",
    )(p2, p2, p2, p2, gq, gkv)


def _log_sigmoid(x):
    return jnp.minimum(x, 0.0) - jnp.log(1.0 + jnp.exp(-jnp.abs(x)))


def _mlstm_kernel(*refs, scale):
    nh = ML_HEADS
    per = 3 * nh + 1
    gb_ref, of_ref, ob_ref, c_ref, n_ref, m_ref = refs[2 * per:]

    @pl.when(pl.program_id(1) == 0)
    def _():
        c_ref[...] = jnp.zeros_like(c_ref)
        n_ref[...] = jnp.zeros_like(n_ref)
        m_ref[...] = jnp.zeros_like(m_ref)

    for d, o_ref in enumerate((of_ref, ob_ref)):
        r = refs[d * per:(d + 1) * per]
        _mlstm_chunk(r[0:nh], r[nh:2 * nh], r[2 * nh:3 * nh], r[3 * nh], gb_ref, o_ref,
                     c_ref.at[d], n_ref.at[d], m_ref.at[d], backward=d == 1, scale=scale)


def _mlstm_chunk(q_refs, k_refs, v_refs, g_ref, gb_ref, o_ref, c_ref, n_ref, m_ref, *, backward, scale):
    nh, L = ML_HEADS, ML_CHUNK
    t_i = lax.broadcasted_iota(jnp.int32, (L, L), 0)
    s_i = lax.broadcasted_iota(jnp.int32, (L, L), 1)
    mask = (s_i >= t_i) if backward else (s_i <= t_i)
    maskf = mask.astype(F32)
    gb = g_ref[...] + gb_ref[...]
    ls = _log_sigmoid(gb)
    gbt, lst = gb.T, ls.T
    cum_c = jnp.dot(maskf, ls, preferred_element_type=F32, precision=HI)
    cum_r = jnp.dot(lst, maskf.T, preferred_element_type=F32, precision=HI)
    last = 0 if backward else L - 1
    gi0, gf0 = (2 * nh, 3 * nh) if backward else (0, nh)
    for h in range(nh):
        gi, gf = gi0 + h, gf0 + h
        q = q_refs[h][...].astype(F32)
        kf = k_refs[h][...].astype(F32) * scale
        vf = v_refs[h][...].astype(F32)
        qb, kb = q.astype(BF16), kf.astype(BF16)
        cumc, cumr = cum_c[:, gf:gf + 1], cum_r[gf:gf + 1, :]
        li_r, li_c = gbt[gi:gi + 1, :], gb[:, gi:gi + 1]
        m11 = m_ref[h][0:1, 0:1]
        log_intra = jnp.where(mask, cumc - cumr + li_r, NEG)
        log_inter = cumc + m11
        m_t = jnp.maximum(log_inter, log_intra.max(axis=1, keepdims=True))
        w_inter = jnp.exp(log_inter - m_t)
        s = _nt(qb, kb) * jnp.exp(log_intra - m_t)
        cmat = c_ref[h]
        num = w_inter * _nt(qb, cmat.astype(BF16)) + jnp.dot(s.astype(BF16), vf.astype(BF16),
                                                             preferred_element_type=F32)
        den = w_inter * jnp.sum(q * n_ref[h][0:1, :], axis=1, keepdims=True) + s.sum(axis=1, keepdims=True)
        o_ref[:, h * L:(h + 1) * L] = num / jnp.maximum(jnp.abs(den), jnp.exp(-m_t))
        total = cumc[last:last + 1, :]
        log_w = total - cumc + li_c
        m_new = jnp.maximum(total + m11, log_w.max(axis=0, keepdims=True))
        decay = jnp.exp(total + m11 - m_new)
        w = jnp.exp(log_w - m_new)
        upd = lax.dot_general((vf * w).astype(BF16), kb, (((0,), (0,)), ((), ())), preferred_element_type=F32)
        c_ref[h] = decay * cmat + upd
        n_ref[h] = jnp.broadcast_to(decay * n_ref[h][0:1, :] + jnp.sum(w * kf, axis=0, keepdims=True), (8, L))
        m_ref[h] = jnp.broadcast_to(m_new, (8, LANE))


def mlstm_scan(rows, pa, pb, gbias):
    B, S, NC = rows.B, rows.S, rows.NC
    nh, L = ML_HEADS, ML_CHUNK
    ncx, ncl = NC // L, S // L
    first_ctx = B * S // L

    def row_block(backward, b, j):
        cx = (ncx - 1 - j) if backward else j
        cl = (ncl - 1 - (j - ncx)) if backward else (j - ncx)
        return jnp.where(j < ncx, first_ctx + b * ncx + cx, b * ncl + cl)

    def col_spec(backward, col):
        return pl.BlockSpec((L, LANE), lambda b, j: (row_block(backward, b, j), col))

    q0 = 2304 // LANE
    in_specs = []
    for backward in (False, True):
        in_specs += [col_spec(backward, q0 + c) for c in range(3 * nh)] + [col_spec(backward, PB_GATES // LANE)]
    in_specs.append(pl.BlockSpec((1, LANE), lambda b, j: (0, 0)))
    out = jax.ShapeDtypeStruct((rows.T, nh * L), F32)
    return pl.pallas_call(
        functools.partial(_mlstm_kernel, scale=L ** -0.5),
        grid=(B, ncx + ncl), in_specs=in_specs,
        out_specs=[pl.BlockSpec((L, nh * L), functools.partial(lambda bw, b, j: (row_block(bw, b, j), 0), bw))
                   for bw in (False, True)],
        out_shape=[out, out],
        scratch_shapes=[pltpu.VMEM((2, nh, L, L), F32), pltpu.VMEM((2, nh, 8, L), F32),
                        pltpu.VMEM((2, nh, 8, LANE), F32)],
        compiler_params=_cp(("parallel", "arbitrary")), name="mlstm_scan",
    )(*(([pa] * (3 * nh) + [pb]) * 2), gbias)


def _mlstm_finish_kernel(hf_ref, hb_ref, *refs):
    o_refs, g_ref, out_ref = refs[:ML_HEADS], refs[ML_HEADS], refs[ML_HEADS + 1]
    for h in range(ML_HEADS):
        sl = slice(h * ML_CHUNK, (h + 1) * ML_CHUNK)
        hn = _rms(hf_ref[:, sl] + hb_ref[:, sl], g_ref[:, sl])
        out_ref[:, sl] = (jax.nn.sigmoid(o_refs[h][...]) * hn).astype(out_ref.dtype)


def mlstm_finish(rows, hf, hb, p1, gnorm):
    W = ML_HEADS * ML_CHUNK
    o_specs = [pl.BlockSpec((ROW_TILE, LANE), functools.partial(lambda h, i: (i, h), h))
               for h in range(ML_HEADS)]
    return pl.pallas_call(
        _mlstm_finish_kernel, grid=(rows.nt,),
        in_specs=[_row_spec(W), _row_spec(W)] + o_specs + [_vec_spec(W)],
        out_specs=_row_spec(W),
        out_shape=jax.ShapeDtypeStruct((rows.T, W), BF16),
        compiler_params=_cp(("parallel",)), name="mlstm_finish",
    )(hf, hb, *([p1] * ML_HEADS), gnorm)


def _short_conv_kernel(p_ref, w_ref, o_ref):
    x = p_ref[...].astype(F32)
    L = x.shape[0]
    row = lax.broadcasted_iota(jnp.int32, x.shape, 0)
    prev = jnp.where(row == 0, 0.0, pltpu.roll(x, 1, 0))
    nxt = jnp.where(row == L - 1, 0.0, pltpu.roll(x, L - 1, 0))
    o_ref[...] = w_ref[0:1, :] * prev + w_ref[1:2, :] * x + w_ref[2:3, :] * nxt


def short_conv(p1, short_w, L, row_block0, B):
    ncol = 1536 // LANE
    per = 512 // LANE
    return pl.pallas_call(
        _short_conv_kernel, grid=(B, ncol),
        in_specs=[pl.BlockSpec((L, LANE), lambda b, c: (row_block0 + b, 768 // LANE + c)),
                  pl.BlockSpec((HY_SHORT, LANE), lambda b, c: (0, c))],
        out_specs=pl.BlockSpec((None, L, LANE), lambda b, c: (c // per, 0, b * per + c % per)),
        out_shape=jax.ShapeDtypeStruct((3, L, B * 512), F32),
        compiler_params=_cp(("parallel", "parallel")), name="hy_short_conv",
    )(p1, short_w)


def _filter_kernel(feat_ref, w1_ref, b1_ref, w2_ref, b2_ref, fr_ref, w3f_ref, w3b_ref, dl_ref, o_ref, hid_ref):
    L = feat_ref.shape[0]

    @pl.when((pl.program_id(0) == 0) & (pl.program_id(1) == 0))
    def _():
        fr = fr_ref[...]
        h1 = jnp.sin(fr * (jnp.dot(feat_ref[...], w1_ref[...], preferred_element_type=F32, precision=HI)
                           + b1_ref[...]))
        hid_ref[...] = jnp.sin(fr * (jnp.dot(h1, w2_ref[...], preferred_element_type=F32, precision=HI)
                                     + b2_ref[...]))

    hid = hid_ref[...]
    t = lax.broadcasted_iota(jnp.int32, (L, 1), 0).astype(F32) / L
    dec = jnp.exp(-t * dl_ref[...])
    hf = jnp.dot(hid, w3f_ref[...], preferred_element_type=F32, precision=HI) * dec
    hb = jnp.dot(hid, w3b_ref[...], preferred_element_type=F32, precision=HI) * dec
    hb = jnp.where(lax.broadcasted_iota(jnp.int32, hb.shape, 0) == 0, 0.0, hb)
    inv = 1.0 / (jnp.sum(jnp.abs(hf), axis=0, keepdims=True) + jnp.sum(jnp.abs(hb), axis=0, keepdims=True))
    o_ref[0] = ((hf + hb) * inv).astype(o_ref.dtype)
    o_ref[1] = ((hf - hb) * inv).astype(o_ref.dtype)


def hyena_filters(feats, w1p, b1, w2, b2, fr, w3, deltas, L):
    hid = w2.shape[0]
    per = 512 // LANE
    full = lambda shape: pl.BlockSpec(shape, lambda o, c: (0, 0))
    return pl.pallas_call(
        _filter_kernel, grid=(HY_ORDER, per),
        in_specs=[full((L, LANE)), full((LANE, hid)), full((1, hid)), full((hid, hid)), full((1, hid)),
                  full((1, hid)),
                  pl.BlockSpec((hid, LANE), lambda o, c: (0, (o * 2) * per + c)),
                  pl.BlockSpec((hid, LANE), lambda o, c: (0, (o * 2 + 1) * per + c)),
                  pl.BlockSpec((1, LANE), lambda o, c: (0, c))],
        out_specs=pl.BlockSpec((2, L, LANE), lambda o, c: (0, 0, o * per + c)),
        out_shape=jax.ShapeDtypeStruct((2, L, HY_ORDER * 512), BF16),
        scratch_shapes=[pltpu.VMEM((L, hid), F32)],
        compiler_params=_cp(("arbitrary", "arbitrary")), name="hy_filters",
    )(feats, w1p, b1, w2, b2, fr, w3, w3, deltas)


def _gate_kernel(x_ref, c_ref, z_ref, b_ref, o_ref):
    o_ref[...] = (x_ref[...] * (c_ref[...] + b_ref[...] * z_ref[...])).astype(o_ref.dtype)


def hyena_gate(u3, which_x, conv, z_arr, which_z, bias, order, L, B, tl, final=None):
    nl = L // tl
    n_fill = 0
    ci = lambda i: jnp.minimum(i, nl - 1)
    in_specs = [pl.BlockSpec((None, tl, 512), lambda i, b: (which_x, ci(i), b)),
                pl.BlockSpec((tl, 512), lambda i, b: (ci(i), b)),
                pl.BlockSpec((None, tl, 512), lambda i, b: (which_z, ci(i), b)),
                pl.BlockSpec((None, 1, 512), lambda i, b: (order, 0, 0))]
    args = [u3, conv, z_arr, bias]
    aliases = {}
    kern = _gate_kernel
    if final is not None:
        T, row0, into = final
        out_shape = jax.ShapeDtypeStruct((T, 512), BF16)
        if into is not None:
            out_spec = pl.BlockSpec((tl, 512), lambda i, b: (row0 // tl + b * nl + i, 0))
            in_specs.append(pl.BlockSpec(memory_space=pl.ANY))
            args.append(into)
            aliases = {4: 0}
            kern = lambda x, c, z, b, into_ref, o: _gate_kernel(x, c, z, b, o)
        else:
            assert row0 == 0
            n_fill = (T - B * L) // tl
            out_spec = pl.BlockSpec((tl, 512), lambda i, b: (jnp.where(i < nl, b * nl + i, B * nl + (i - nl)), 0))

            def kern(x, c, z, b, o):
                i = pl.program_id(0)
                pl.when(i < nl)(lambda: _gate_kernel(x, c, z, b, o))

                @pl.when(i >= nl)
                def _():
                    o[...] = jnp.zeros_like(o)
    else:
        out_spec = pl.BlockSpec((None, tl, 512), lambda i, b: (0, i, b))
        out_shape = jax.ShapeDtypeStruct((1, L, B * 512), F32)
    return pl.pallas_call(
        kern, grid=(nl + n_fill, B), in_specs=in_specs, out_specs=out_spec, out_shape=out_shape,
        input_output_aliases=aliases,
        compiler_params=_cp(("arbitrary", "arbitrary")), name="hy_gate",
    )(*args)


def _dft_gen_kernel(ca_ref, sa_ref, cb_ref, sb_ref, ca2_ref, sa2_ref, cb2_ref, sb2_ref, fwd_ref, fwdp_ref, inv_ref):
    tb, L = cb_ref.shape
    ca, sa, cb, sb = ca_ref[...], sa_ref[...], cb_ref[...], sb_ref[...]
    c, s = (ca * cb - sa * sb).astype(fwd_ref.dtype), (-(sa * cb + ca * sb)).astype(fwd_ref.dtype)
    fwd_ref[0], fwd_ref[1] = c, s
    fwdp_ref[0], fwdp_ref[1] = c, s
    ca, sa, cb, sb = ca2_ref[...], sa2_ref[...], cb2_ref[...], sb2_ref[...]
    c, s = (ca * cb - sa * sb).astype(inv_ref.dtype), (-(sa * cb + ca * sb)).astype(inv_ref.dtype)
    for j in range(L // tb):
        inv_ref[:, 2 * j * tb:(2 * j + 1) * tb] = c[:, j * tb:(j + 1) * tb]
        inv_ref[:, (2 * j + 1) * tb:(2 * j + 2) * tb] = s[:, j * tb:(j + 1) * tb]


def _dft_matrices(L, tb):
    nb = L // tb
    col = jnp.arange(L, dtype=jnp.int32)[None, :]
    r0 = (jnp.arange(nb, dtype=jnp.int32) * tb)[:, None]
    i = jnp.arange(tb, dtype=jnp.int32)[:, None]
    unit = math.pi / (2 * L)
    trig = lambda ph: (jnp.cos((ph % (4 * L)).astype(F32) * unit), jnp.sin((ph % (4 * L)).astype(F32) * unit))
    ca, sa = trig((2 * r0 + 1) * col)
    cb, sb = trig(2 * i * col)
    ca2, sa2 = trig((2 * col + 1) * r0)
    cb2, sb2 = trig((2 * col + 1) * i)
    blk = lambda: pl.BlockSpec((None, 1, L), lambda j: (j, 0, 0))
    shared = lambda: pl.BlockSpec((tb, L), lambda j: (0, 0))
    a3 = lambda a: a[:, None, :]
    fwd, fwd_plain, inv = pl.pallas_call(
        _dft_gen_kernel, grid=(nb,),
        in_specs=[blk(), blk(), shared(), shared(), blk(), blk(), shared(), shared()],
        out_specs=[pl.BlockSpec((None, 2, tb, L), lambda j: (j, 0, 0, 0)),
                   pl.BlockSpec((2, tb, L), lambda j: (0, j, 0)), pl.BlockSpec((tb, 2 * L), lambda j: (j, 0))],
        out_shape=[jax.ShapeDtypeStruct((nb, 2, tb, L), BF16), jax.ShapeDtypeStruct((2, L, L), BF16),
                   jax.ShapeDtypeStruct((L, 2 * L), BF16)],
        compiler_params=_cp(("parallel",)), name="hy_dft_gen",
    )(a3(ca), a3(sa), cb, sb, a3(ca2), a3(sa2), cb2, sb2)
    return fwd.reshape(2 * L, L), fwd_plain.reshape(2 * L, L), inv


def _filter_feats(L):
    t = jnp.arange(L, dtype=F32) / L
    kf = jnp.arange(1, HY_POS_FREQS + 1, dtype=F32)
    ang = 2.0 * math.pi * t[:, None] * kf
    feats = jnp.concatenate([t[:, None], jnp.sin(ang), jnp.cos(ang)], axis=-1)
    return jnp.pad(feats, ((0, 0), (0, LANE - feats.shape[1])))


def hyena_seq(p1, L, row_block0, B, consts, wts, tm, tl, T, into=None):
    fwd, fwd_plain, inv, feats = consts
    short_w, w1p, b1, w2, b2, fr, w3, deltas, bias = wts
    tb = min(L, ROW_TILE)
    nkb = L // tb
    u3 = short_conv(p1, short_w, L, row_block0, B)
    filt = hyena_filters(feats, w1p, b1, w2, b2, fr, w3, deltas, L)
    gid_f = jnp.repeat(jnp.arange(2, dtype=jnp.int32), L // tm)
    zeros_f = jnp.zeros((nkb,), jnp.int32)
    zeros_i = jnp.zeros((L // tm,), jnp.int32)
    NW = B * 512
    gspec = gmm(fwd_plain, filt, gid_f, tm=tm, tn=512, n_tiles=HY_ORDER, out_dtype=F32, name="hy_dft_filter")

    def cmul(acc, gre_ref, gim_ref):
        zre, zim, gre, gim = acc[:tb], acc[tb:], gre_ref[...], gim_ref[...]
        return jnp.concatenate([zre * gre - zim * gim, zre * gim + zim * gre], axis=0) * (1.0 / L)

    def gated(acc, x_ref, z_ref, b_ref):
        return x_ref[...] * (acc + b_ref[...] * z_ref[...])

    z_arr, which_z = u3, 0
    out = None
    for o in range(HY_ORDER):
        y = gmm(fwd, z_arr, zeros_f + which_z, tm=2 * tb, tn=512, n_tiles=B, out_dtype=BF16, epilogue=cmul,
                extra=[(gspec, (tb, 512), functools.partial(lambda o, n, m: (m, o), o)),
                       (gspec, (tb, 512), functools.partial(lambda o, n, m: (nkb + m, o), o))], name="hy_dft_fwd")
        y = y.reshape(1, 2 * L, NW)
        if o < HY_ORDER - 1:
            out = gmm(inv, y, zeros_i, tm=tm, tn=512, n_tiles=B, out_dtype=F32, epilogue=gated,
                      extra=[(u3, (None, tm, 512), functools.partial(lambda o, n, m: (1 + o, m, n), o)),
                             (z_arr, (None, tm, 512), functools.partial(lambda w, n, m: (w, m, n), which_z)),
                             (bias, (None, 1, 512), functools.partial(lambda o, n, m: (o, 0, 0), o))],
                      name="hy_dft_inv_gate").reshape(1, L, NW)
        else:
            conv = gmm(inv, y, zeros_i, tm=tm, tn=512, n_tiles=B, out_dtype=F32, name="hy_dft_inv")
            out = hyena_gate(u3, 1 + o, conv, z_arr, which_z, bias, o, L, B, tl, (T, row_block0 * L, into))
        z_arr, which_z = out, 0
    return out


def _gather_kernel(idx_ref, src_ref, *refs, tm, k):
    if k > 1:
        w_ref, o_ref, buf_ref, sem = refs
    else:
        o_ref, buf_ref, sem = refs
    i, n = pl.program_id(0), pl.num_programs(0)
    M = n * tm

    def issue_tile(tile, slot):
        for j in range(k):
            def issue(r, carry):
                pltpu.make_async_copy(src_ref.at[pl.ds(idx_ref[j * M + tile * tm + r], 1)],
                                      buf_ref.at[slot, j, pl.ds(r, 1)], sem.at[slot]).start()
                return carry

            lax.fori_loop(0, tm, issue, 0, unroll=8)

    @pl.when(i == 0)
    def _():
        issue_tile(0, 0)

    @pl.when(i + 1 < n)
    def _():
        issue_tile(i + 1, (i + 1) % 2)

    slot = i % 2
    for j in range(k):
        pltpu.make_async_copy(src_ref.at[pl.ds(0, tm)], buf_ref.at[slot, j], sem.at[slot]).wait()
    if k > 1:
        w = w_ref[...]
        acc = w[:, 0:1] * buf_ref[slot, 0]
        for j in range(1, k):
            acc = acc + w[:, j:j + 1] * buf_ref[slot, j]
        o_ref[...] = acc.astype(o_ref.dtype)
    else:
        o_ref[...] = buf_ref[slot, 0].astype(o_ref.dtype)


def gather_rows(src, idx, tm, out_dtype, weights=None, k=1):
    M = idx.shape[0] // k
    D = src.shape[1]
    in_specs = [pl.BlockSpec(memory_space=pl.ANY)]
    args = [src]
    if k > 1:
        in_specs.append(pl.BlockSpec((tm, LANE), lambda i, idx: (i, 0)))
        args.append(weights)
    return pl.pallas_call(
        functools.partial(_gather_kernel, tm=tm, k=k),
        grid_spec=pltpu.PrefetchScalarGridSpec(
            num_scalar_prefetch=1, grid=(M // tm,), in_specs=in_specs,
            out_specs=pl.BlockSpec((tm, D), lambda i, idx: (i, 0)),
            scratch_shapes=[pltpu.VMEM((2, k, tm, D), src.dtype), pltpu.SemaphoreType.DMA((2,))]),
        out_shape=jax.ShapeDtypeStruct((M, D), out_dtype),
        compiler_params=_cp(("arbitrary",)), name="gather_rows" if k == 1 else "gather_combine",
    )(idx, *args)


def moe_ffn(rows, h_f32, rw, ri, w_gate, w_up, w_down, layer_moe, tm=512):
    T, D = rows.T, rows.D
    E = w_gate.shape[2]
    e_flat = ri[:, :TOP_K].T.reshape(-1)
    onehot = (e_flat[:, None] == jnp.arange(N_EXPERTS, dtype=jnp.int32)[None, :]).astype(jnp.int32)
    csum = jnp.cumsum(onehot, axis=0)
    rank = jnp.sum(csum * onehot, axis=1) - 1
    counts = csum[-1]
    padded = ((counts + tm - 1) // tm) * tm
    ends = jnp.cumsum(padded)
    starts = ends - padded
    dest = starts[e_flat] + rank
    m_pad = TOP_K * T + N_EXPERTS * tm
    token = jnp.tile(jnp.arange(T, dtype=jnp.int32), TOP_K)
    src_tok = jnp.zeros((m_pad,), jnp.int32).at[dest].set(token)
    tile_row = jnp.arange(m_pad // tm, dtype=jnp.int32) * tm
    gid = jnp.minimum(jnp.sum((tile_row[:, None] >= ends[None, :]).astype(jnp.int32), axis=1), N_EXPERTS - 1)
    gid = gid + layer_moe * N_EXPERTS
    used = (ends[-1:] // tm).astype(jnp.int32)

    xs = gather_rows(h_f32, src_tok, tm, BF16)
    tile = lambda n, m: (m, n)
    g = gmm(xs, w_gate, gid, tm=tm, tn=E // 2, n_tiles=2, out_dtype=BF16, epilogue=_silu, used=used,
            name="moe_gate")
    hh = gmm(xs, w_up, gid, tm=tm, tn=E // 2, n_tiles=2, out_dtype=BF16, epilogue=_times,
             extra=[(g, (tm, E // 2), tile)], used=used, name="moe_up")
    y = gmm(hh, w_down, gid, tm=tm, tn=D // 2, n_tiles=2, out_dtype=F32, used=used, name="moe_down")
    return gather_rows(y, dest, ROW_TILE, BF16, weights=rw, k=TOP_K)


def kernel(x, c, ctx, c_ctx, w_mod, b_mod, g_mix_pre, g_mix_post, g_ffn_pre, g_ffn_post, w_in, w_out, attn_sink,
           hy_short, hy_w1, hy_b1, hy_w2, hy_b2, hy_freq, hy_w3, hy_bias, ml_gate_bias, ml_norm, mla_q_norm,
           mla_w_uq, mla_kv_norm, mla_w_ukv, ffn_w_gate, ffn_w_up, ffn_w_down, moe_router, moe_w_gate, moe_w_up,
           moe_w_down):
    B, S, D = x.shape
    NC = ctx.shape[1]
    depth = w_mod.shape[0]
    rows = _Rows(B, S, NC, D)
    T = rows.T
    TM = 512 if T % 512 == 0 else ROW_TILE
    n_mt = T // TM
    ffn_dim = ffn_w_gate.shape[2]

    xs = jnp.concatenate([x.reshape(B * S, D), ctx.reshape(B * NC, D)], axis=0)

    cm = jnp.concatenate([c, c_ctx[None, :]], axis=0)
    cm = jnp.pad(jax.nn.silu(cm), ((0, 16 - (B + 1)), (0, 0))).astype(BF16)
    mod_all = gmm(jnp.tile(cm, (depth, 1)), w_mod, jnp.arange(depth, dtype=jnp.int32), tm=16, tn=1536,
                  n_tiles=N_MOD * D // 1536, out_dtype=F32, name="adaln")
    mods = (mod_all.reshape(depth, 16, N_MOD * D)[:, :B + 1] + b_mod[:, None, :]).reshape(depth, B + 1, N_MOD, 1, D)

    def rope_table(rot_dim, reps, lane0, width):
        cos, sin = _axial_tables(S, rot_dim)
        cos, sin = jnp.tile(cos, (1, reps)), jnp.tile(sin, (1, reps))
        padw = ((0, 0), (lane0, width - lane0 - cos.shape[1]))
        cos = jnp.pad(cos, padw, constant_values=1.0)
        sin = jnp.pad(sin, padw)
        cos = jnp.concatenate([jnp.tile(cos, (B, 1)), jnp.ones((B * NC, width), F32)], axis=0)
        sin = jnp.concatenate([jnp.tile(sin, (B, 1)), jnp.zeros((B * NC, width), F32)], axis=0)
        return cos, sin

    cos_a, sin_a = rope_table(64, 2, 0, LANE)
    cos_m, sin_m = rope_table(MLA_ROPE, 1, MLA_NOPE, LANE)
    hy_consts = {L: _dft_matrices(L, min(L, ROW_TILE)) + (_filter_feats(L),) for L in (S, NC)}
    deltas = jnp.abs(jnp.linspace(math.log(HY_DECAY_TARGET) / HY_SLOW_DECAY,
                                  math.log(HY_DECAY_TARGET) / HY_FAST_DECAY, 512, dtype=F32))[None, :]

    w_in_t = jnp.swapaxes(w_in, 1, 2)
    wq = mla_w_uq.reshape(depth, -1, MLA_HEADS, MLA_NOPE + MLA_ROPE)
    wq = jnp.pad(wq, ((0, 0), (0, 0), (0, 0), (0, LANE - MLA_NOPE - MLA_ROPE))).reshape(depth, -1, MLA_HEADS * LANE)
    wkv = mla_w_ukv.reshape(depth, -1, MLA_HEADS, 2 * 64)
    wk = jnp.pad(wkv[..., :MLA_NOPE], ((0, 0), (0, 0), (0, 0), (0, LANE - MLA_NOPE)))
    wk = wk.reshape(depth, -1, MLA_HEADS * LANE)
    wv = wkv[..., MLA_NOPE:].reshape(depth, -1, MLA_HEADS * 64)
    w1p = jnp.pad(hy_w1, ((0, 0), (0, LANE - hy_w1.shape[1]), (0, 0)))
    gbias = jnp.pad(ml_gate_bias.reshape(depth, 1, 4 * ML_HEADS), ((0, 0), (0, 0), (0, LANE - 4 * ML_HEADS)))
    router = jnp.pad(moe_router, ((0, 0), (0, 0), (0, LANE - N_EXPERTS)))
    n_moe = moe_w_gate.shape[0]
    mw_gate = moe_w_gate.reshape(n_moe * N_EXPERTS, D, -1)
    mw_up = moe_w_up.reshape(n_moe * N_EXPERTS, D, -1)
    mw_down = moe_w_down.reshape(n_moe * N_EXPERTS, -1, D)
    vec = lambda a, l: a[l][None, :]

    h = norm_mod(rows, xs, vec(g_mix_pre, 0), mods[0], 0, 1)
    for layer in range(depth):
        gid = jnp.full((n_mt,), layer, jnp.int32)
        ml = mods[layer]
        p1 = gmm(h, w_in_t, gid, tm=TM, tn=1280, n_tiles=PA_COLS // 1280, out_dtype=BF16, w_t=True, name="proj_in")
        p2 = gmm(h, w_in_t, gid, tm=TM, tn=768, n_tiles=2, n_off=PA_COLS // 768, out_dtype=F32, w_t=True,
                 name="proj_in_f32")

        qkv = attn_rope(rows, p1, cos_a, sin_a)
        a_all = window_attn(rows, qkv, attn_sink[layer])

        hy_w = (hy_short[layer], w1p[layer], vec(hy_b1, layer), hy_w2[layer], vec(hy_b2, layer),
                vec(hy_freq, layer), hy_w3[layer], deltas, hy_bias[layer][:, None, :])
        b_l = hyena_seq(p1, S, 0, B, hy_consts[S], hy_w, tm=512, tl=512, T=T)
        b_all = hyena_seq(p1, NC, B * S // NC, B, hy_consts[NC], hy_w, tm=NC, tl=NC, T=T, into=b_l)

        hf, hb = mlstm_scan(rows, p1, p2, gbias[layer])
        m_all = mlstm_finish(rows, hf, hb, p2, vec(ml_norm, layer))

        qn, kvn, kr = mla_norms(rows, p2, vec(mla_q_norm, layer), vec(mla_kv_norm, layer))
        slab = lambda a: (a, (TM, LANE), lambda n, m: (m, 0))
        qm = gmm(qn, wq, gid, tm=TM, tn=MLA_HEADS * LANE, n_tiles=1, out_dtype=BF16, epilogue=_mla_q_epilogue,
                 extra=[slab(cos_m), slab(sin_m)], name="mla_uq")
        km = gmm(kvn, wk, gid, tm=TM, tn=MLA_HEADS * LANE, n_tiles=1, out_dtype=BF16, epilogue=_mla_k_epilogue,
                 extra=[slab(kr), slab(cos_m), slab(sin_m)], name="mla_uk")
        vm = gmm(kvn, wv, gid, tm=TM, tn=MLA_HEADS * 64, n_tiles=1, out_dtype=BF16, name="mla_uv")
        d_all = mla_attn(rows, qm, km, vm)

        y = gmm([a_all, b_all, m_all, d_all], w_out, gid, tm=TM, tn=1024, n_tiles=D // 1024, out_dtype=BF16,
                name="proj_out")

        i = layer // 2
        nxt = (vec(g_ffn_pre, layer), ml, 3, 4)
        if layer % 2 == 0:
            xs, h2 = post(rows, xs, y, vec(g_mix_post, layer), ml, 2, nxt=nxt)
            gi = jnp.full((n_mt,), i, jnp.int32)
            g = gmm(h2, ffn_w_gate, gi, tm=TM, tn=ffn_dim // 2, n_tiles=2, out_dtype=BF16, epilogue=_silu,
                    w_single_buffer=True, name="ffn_gate")
            hh = gmm(h2, ffn_w_up, gi, tm=TM, tn=ffn_dim // 2, n_tiles=2, out_dtype=BF16, epilogue=_times,
                     extra=[(g, (TM, ffn_dim // 2), lambda n, m: (m, n))], w_single_buffer=True, name="ffn_up")
            y2 = gmm(hh, ffn_w_down, gi, tm=TM, tn=512, n_tiles=D // 512, out_dtype=BF16, name="ffn_down")
        else:
            xs, h2, h2f, rw, ri = post(rows, xs, y, vec(g_mix_post, layer), ml, 2, nxt=nxt, router=router[i])
            y2 = moe_ffn(rows, h2f, rw, ri, mw_gate, mw_up, mw_down, i)
        if layer + 1 < depth:
            xs, h = post(rows, xs, y2, vec(g_ffn_post, layer), ml, 5,
                         nxt=(vec(g_mix_pre, layer + 1), mods[layer + 1], 0, 1))
        else:
            (xs,) = post(rows, xs, y2, vec(g_ffn_post, layer), ml, 5, latent_only=True)
    return xs.reshape(B, S, D)
```

```python
import functools
import math

import numpy as np
import jax
import jax.numpy as jnp
from jax import lax
from jax.experimental import pallas as pl
from jax.experimental.pallas import tpu as pltpu

F32 = jnp.float32
BF16 = jnp.bfloat16
HI = lax.Precision.HIGHEST

EPS = 1e-6
ROPE_BASE = 10000.0
GRID_W = 64
BLOCK = 128
WINDOW = 128
N_MOD = 6
A_HEADS, A_KV_HEADS = 8, 2
HY_ORDER, HY_SHORT, HY_POS_FREQS = 2, 3, 8
HY_DECAY_TARGET, HY_FAST_DECAY, HY_SLOW_DECAY = 1e-2, 0.3, 1.5
ML_HEADS, ML_CHUNK = 4, 128
MLA_HEADS, MLA_NOPE, MLA_ROPE = 8, 64, 32
N_EXPERTS, TOP_K = 8, 2

PA_COLS = 3840
PB_GATES = 4352 - PA_COLS
MLA_HP = 8
LANE = 128
ROW_TILE = 256
NEG = -1e30
LOG2E = math.log2(math.e)
VMEM_LIMIT = 56 * 1024 * 1024


def _cp(sem, vmem=VMEM_LIMIT):
    return pltpu.CompilerParams(dimension_semantics=sem, vmem_limit_bytes=vmem)


def _gmm_kernel(gid_ref, used_ref, *refs, n_a, cast, w_t, epilogue, n_extra, w_n, tn, n_off):
    a_refs, w_ref, rest = refs[:n_a], refs[n_a], refs[n_a + 1:]
    extra_refs, rest = rest[:n_extra], rest[n_extra:]
    o_ref = rest[0]
    n, m = pl.program_id(0), pl.program_id(1)
    n_axis = 0 if w_t else 1
    if cast:
        wbf_ref = rest[1]
        prev = gid_ref[jnp.maximum(m - 1, 0)]

        @pl.when((m == 0) | (gid_ref[m] != prev))
        def _():
            w = w_ref[...]
            if w_n is not None:
                col = (n + n_off) * tn + lax.broadcasted_iota(jnp.int32, w.shape, n_axis)
                w = jnp.where(col < w_n, w, 0.0)
            wbf_ref[...] = w.astype(BF16)

        src = wbf_ref
    else:
        src = w_ref

    @pl.when(m < used_ref[0])
    def _():
        acc, lo = None, 0
        for a_ref in a_refs:
            k = a_ref.shape[1]
            if w_t:
                part = lax.dot_general(a_ref[...], src[:, lo:lo + k], (((1,), (1,)), ((), ())),
                                       preferred_element_type=F32)
            else:
                part = jnp.dot(a_ref[...], src[lo:lo + k, :], preferred_element_type=F32)
            acc = part if acc is None else acc + part
            lo += k
        if epilogue is not None:
            acc = epilogue(acc, *extra_refs)
        o_ref[...] = acc.astype(o_ref.dtype)

    @pl.when(m >= used_ref[0])
    def _():
        o_ref[...] = jnp.zeros_like(o_ref)


def _silu(acc):
    return acc * jax.nn.sigmoid(acc)


def _times(acc, g_ref):
    return acc * g_ref[...].astype(F32)


def gmm(a, w, gid, *, tm, tn, n_tiles, out_dtype, n_off=0, epilogue=None, extra=(), w_t=False, used=None,
        w_single_buffer=False, name="gmm"):
    a_list = list(a) if isinstance(a, (list, tuple)) else [a]
    M = a_list[0].shape[0]
    K = sum(p.shape[1] for p in a_list)
    mt = M // tm
    k_axis, n_axis = (2, 1) if w_t else (1, 2)
    assert mt * tm == M and w.shape[k_axis] == K and gid.shape == (mt,)
    cast = w.dtype != BF16
    partial_n = (n_off + n_tiles) * tn > w.shape[n_axis]
    assert cast or not partial_n
    if used is None:
        used = jnp.full((1,), mt, jnp.int32)
    in_specs = [pl.BlockSpec((tm, p.shape[1]), lambda n, m, g, u: (m, 0)) for p in a_list]
    w_mode = dict(pipeline_mode=pl.Buffered(1)) if w_single_buffer else {}
    if w_t:
        in_specs.append(pl.BlockSpec((None, tn, K), lambda n, m, g, u: (g[m], n + n_off, 0), **w_mode))
    else:
        in_specs.append(pl.BlockSpec((None, K, tn), lambda n, m, g, u: (g[m], 0, n + n_off), **w_mode))
    args = a_list + [w]
    for arr, block, imap in extra:
        in_specs.append(pl.BlockSpec(block, functools.partial(lambda f, n, m, g, u: f(n, m), imap)))
        args.append(arr)
    kern = functools.partial(_gmm_kernel, n_a=len(a_list), cast=cast, w_t=w_t, epilogue=epilogue,
                             n_extra=len(extra), w_n=w.shape[n_axis] if partial_n else None, tn=tn, n_off=n_off)
    return pl.pallas_call(
        kern,
        grid_spec=pltpu.PrefetchScalarGridSpec(
            num_scalar_prefetch=2, grid=(n_tiles, mt), in_specs=in_specs,
            out_specs=pl.BlockSpec((tm, tn), lambda n, m, g, u: (m, n)),
            scratch_shapes=[pltpu.VMEM((tn, K) if w_t else (K, tn), BF16)] if cast else []),
        out_shape=jax.ShapeDtypeStruct((M, n_tiles * tn), out_dtype),
        compiler_params=_cp(("arbitrary", "arbitrary")),
        name=name,
    )(gid, used, *args)


def _rms(v, g):
    return v * lax.rsqrt(jnp.mean(v * v, axis=-1, keepdims=True) + EPS) * g


def _norm_mod_kernel(x_ref, g_ref, sh_ref, sc_ref, h_ref):
    h_ref[...] = (_rms(x_ref[...], g_ref[...]) * (1.0 + sc_ref[...]) + sh_ref[...]).astype(h_ref.dtype)


def _post_kernel(*refs, with_next, with_router):
    x_ref, y_ref, gp_ref, gate_ref = refs[:4]
    refs = refs[4:]
    xn = x_ref[...] + gate_ref[...] * _rms(y_ref[...].astype(F32), gp_ref[...])
    if not with_next:
        refs[0][...] = xn
        return
    gn_ref, sh_ref, sc_ref = refs[:3]
    refs = refs[3:]
    if with_router:
        r_ref, refs = refs[0], refs[1:]
    refs[0][...] = xn
    h = _rms(xn, gn_ref[...]) * (1.0 + sc_ref[...]) + sh_ref[...]
    refs[1][...] = h.astype(BF16)
    if with_router:
        hf_ref, rw_ref, ri_ref = refs[2:5]
        hf_ref[...] = h
        logits = jnp.dot(h, r_ref[...], preferred_element_type=F32, precision=HI)
        lane = lax.broadcasted_iota(jnp.int32, logits.shape, 1)
        logits = jnp.where(lane < N_EXPERTS, logits, -jnp.inf)
        v1 = jnp.max(logits, axis=1, keepdims=True)
        i1 = jnp.min(jnp.where(logits == v1, lane, LANE), axis=1, keepdims=True)
        l2 = jnp.where(lane == i1, -jnp.inf, logits)
        v2 = jnp.max(l2, axis=1, keepdims=True)
        i2 = jnp.min(jnp.where(l2 == v2, lane, LANE), axis=1, keepdims=True)
        e = jnp.exp(v2 - v1)
        w1 = 1.0 / (1.0 + e)
        w2 = e / (1.0 + e)
        rw_ref[...] = jnp.where(lane == 0, w1, jnp.where(lane == 1, w2, 0.0))
        ri_ref[...] = jnp.where(lane == 0, i1, jnp.where(lane == 1, i2, 0))


class _Rows:
    def __init__(self, B, S, NC, D):
        self.B, self.S, self.NC, self.D = B, S, NC, D
        self.T = B * S + B * NC
        assert S % ROW_TILE == 0 and NC % ROW_TILE == 0
        self.nt = self.T // ROW_TILE

    def mod_row(self, i):
        n_lat = self.B * self.S // ROW_TILE
        return jnp.where(i < n_lat, i // (self.S // ROW_TILE), self.B)


def _row_spec(D):
    return pl.BlockSpec((ROW_TILE, D), lambda i: (i, 0))


def _vec_spec(D):
    return pl.BlockSpec((1, D), lambda i: (0, 0))


def _mod_spec(rows, j):
    return pl.BlockSpec((None, None, 1, rows.D), lambda i: (rows.mod_row(i), j, 0, 0))


def norm_mod(rows, x, g, mods, j_shift, j_scale):
    D = rows.D
    return pl.pallas_call(
        _norm_mod_kernel, grid=(rows.nt,),
        in_specs=[_row_spec(D), _vec_spec(D), _mod_spec(rows, j_shift), _mod_spec(rows, j_scale)],
        out_specs=_row_spec(D),
        out_shape=jax.ShapeDtypeStruct((rows.T, D), BF16),
        compiler_params=_cp(("parallel",)), name="norm_mod",
    )(x, g, mods, mods)


def post(rows, x, y, g_post, mods, j_gate, nxt=None, router=None, latent_only=False):
    D = rows.D
    T = rows.B * rows.S if latent_only else rows.T
    in_specs = [_row_spec(D), _row_spec(D), _vec_spec(D), _mod_spec(rows, j_gate)]
    args = [x, y, g_post, mods]
    out_specs = [_row_spec(D)]
    out_shape = [jax.ShapeDtypeStruct((T, D), F32)]
    if nxt is not None:
        g_next, mods_next, j_shift, j_scale = nxt
        in_specs += [_vec_spec(D), _mod_spec(rows, j_shift), _mod_spec(rows, j_scale)]
        args += [g_next, mods_next, mods_next]
        out_specs.append(_row_spec(D))
        out_shape.append(jax.ShapeDtypeStruct((T, D), BF16))
        if router is not None:
            in_specs.append(pl.BlockSpec((D, LANE), lambda i: (0, 0)))
            args.append(router)
            out_specs += [_row_spec(D), _row_spec(LANE), _row_spec(LANE)]
            out_shape += [jax.ShapeDtypeStruct((T, D), F32), jax.ShapeDtypeStruct((T, LANE), F32),
                          jax.ShapeDtypeStruct((T, LANE), jnp.int32)]
    kern = functools.partial(_post_kernel, with_next=nxt is not None, with_router=router is not None)
    return pl.pallas_call(
        kern, grid=(T // ROW_TILE,), in_specs=in_specs, out_specs=out_specs, out_shape=out_shape,
        compiler_params=_cp(("parallel",)), name="post",
    )(*args)


def _rot_pairs(x, q):
    lane = lax.broadcasted_iota(jnp.int32, x.shape, 1)
    n = x.shape[1]
    return jnp.where(lane % (2 * q) < q, -pltpu.roll(x, n - q, 1), pltpu.roll(x, q, 1))


def _axial_tables(S, rot_dim):
    rows = S // GRID_W
    r = jnp.repeat(jnp.arange(rows, dtype=F32), GRID_W)
    col = jnp.tile(jnp.arange(GRID_W, dtype=F32), rows)
    half = rot_dim // 2
    inv = ROPE_BASE ** (-jnp.arange(0, half, 2, dtype=F32) / half)
    ar, ac = r[:, None] * inv, col[:, None] * inv
    ang = jnp.concatenate([ar, ar, ac, ac], axis=-1)
    return jnp.cos(ang), jnp.sin(ang)


def _attn_rope_kernel(p_ref, cos_ref, sin_ref, o_ref, *, scale):
    cos, sin = cos_ref[...], sin_ref[...]
    nq = A_HEADS * 64 // LANE
    for s in range(nq + 1):
        xs = p_ref[:, s * LANE:(s + 1) * LANE].astype(F32)
        r = xs * cos + _rot_pairs(xs, 16) * sin
        if s < nq:
            r = r * scale
        o_ref[:, s * LANE:(s + 1) * LANE] = r.astype(o_ref.dtype)
    o_ref[:, (nq + 1) * LANE:] = p_ref[:, (nq + 1) * LANE:].astype(o_ref.dtype)


def attn_rope(rows, p1, cos, sin):
    W = 768
    return pl.pallas_call(
        functools.partial(_attn_rope_kernel, scale=64 ** -0.5 * LOG2E), grid=(rows.nt,),
        in_specs=[pl.BlockSpec((ROW_TILE, W), lambda i: (i, 0)), _row_spec(LANE), _row_spec(LANE)],
        out_specs=pl.BlockSpec((ROW_TILE, W), lambda i: (i, 0)),
        out_shape=jax.ShapeDtypeStruct((rows.T, W), BF16),
        compiler_params=_cp(("parallel",)), name="attn_rope",
    )(p1, cos, sin)


def _nt(a, b):
    return lax.dot_general(a, b, (((1,), (1,)), ((), ())), preferred_element_type=F32)


def _softmax_av(scores, values, sink=None):
    m = scores[0].max(axis=1, keepdims=True)
    for s in scores[1:]:
        m = jnp.maximum(m, s.max(axis=1, keepdims=True))
    if sink is not None:
        m = jnp.maximum(m, sink)
    den = jnp.exp2(sink - m) if sink is not None else 0.0
    acc = None
    for s, v in zip(scores, values):
        p = jnp.exp2(s - m)
        den = den + p.sum(axis=1, keepdims=True)
        pv = jnp.dot(p.astype(BF16), v, preferred_element_type=F32)
        acc = pv if acc is None else acc + pv
    return acc / den


def _window_kernel(sink_ref, q_ref, kp_ref, kc_ref, kn_ref, vp_ref, vc_ref, vn_ref, kx_ref, vx_ref, o_ref, *, nb):
    n = pl.program_id(1)
    groups = A_HEADS // A_KV_HEADS
    d = 64
    kb = jnp.concatenate([kp_ref[...], kc_ref[...], kn_ref[...]], axis=0)
    vb = jnp.concatenate([vp_ref[...], vc_ref[...], vn_ref[...]], axis=0)
    Q = groups * BLOCK
    qi = lax.broadcasted_iota(jnp.int32, (Q, 3 * BLOCK), 0) % BLOCK
    kj = lax.broadcasted_iota(jnp.int32, (Q, 3 * BLOCK), 1)
    k_abs = (n - 1) * BLOCK + kj
    valid = (jnp.abs(kj - BLOCK - qi) <= WINDOW) & (k_abs >= 0) & (k_abs < nb * BLOCK) & (n < nb)
    hrow = lax.broadcasted_iota(jnp.int32, (Q, 1), 0) // BLOCK
    outs = []
    for g in range(A_KV_HEADS):
        qg = jnp.concatenate([q_ref[:, (g * groups + h) * d:(g * groups + h + 1) * d] for h in range(groups)], axis=0)
        ksl = slice(g * d, (g + 1) * d)
        s_band = jnp.where(valid, _nt(qg, kb[:, ksl]), NEG)
        s_ctx = _nt(qg, kx_ref[:, ksl])
        sink = jnp.zeros((Q, 1), F32)
        for h in range(groups):
            sink = jnp.where(hrow == h, sink_ref[g * groups + h] * LOG2E, sink)
        o = _softmax_av([s_band, s_ctx], [vb[:, ksl], vx_ref[:, ksl]], sink)
        outs += [o[h * BLOCK:(h + 1) * BLOCK] for h in range(groups)]
    o_ref[...] = jnp.concatenate(outs, axis=1).astype(o_ref.dtype)


def window_attn(rows, qkv, sink):
    B, S, NC = rows.B, rows.S, rows.NC
    nb, ncb = S // BLOCK, NC // BLOCK
    cb = B * S // NC

    def q_block(b, n):
        return jnp.where(n < nb, b * nb + n, B * nb + b * ncb + (n - nb))

    def kv_spec(col, off):
        return pl.BlockSpec((BLOCK, LANE), lambda b, n, s: (b * nb + jnp.clip(n + off, 0, nb - 1), col))

    in_specs = [pl.BlockSpec((BLOCK, 512), lambda b, n, s: (q_block(b, n), 0)),
                kv_spec(4, -1), kv_spec(4, 0), kv_spec(4, 1), kv_spec(5, -1), kv_spec(5, 0), kv_spec(5, 1),
                pl.BlockSpec((NC, LANE), lambda b, n, s: (cb + b, 4)),
                pl.BlockSpec((NC, LANE), lambda b, n, s: (cb + b, 5))]
    return pl.pallas_call(
        functools.partial(_window_kernel, nb=nb),
        grid_spec=pltpu.PrefetchScalarGridSpec(
            num_scalar_prefetch=1, grid=(B, nb + ncb), in_specs=in_specs,
            out_specs=pl.BlockSpec((BLOCK, 512), lambda b, n, s: (q_block(b, n), 0))),
        out_shape=jax.ShapeDtypeStruct((rows.T, 512), BF16),
        compiler_params=_cp(("parallel", "parallel")), name="window_attn",
    )(sink, *([qkv] * 9))


MLA_SCALE = (MLA_NOPE + MLA_ROPE) ** -0.5 * LOG2E


def _mla_q_epilogue(acc, cos_ref, sin_ref):
    cos, sin = cos_ref[...], sin_ref[...]
    slabs = [acc[:, h * LANE:(h + 1) * LANE] for h in range(MLA_HEADS)]
    return jnp.concatenate([(s * cos + _rot_pairs(s, 8) * sin) * MLA_SCALE for s in slabs], axis=1)


def _mla_k_epilogue(acc, kr_ref, cos_ref, sin_ref):
    kr = kr_ref[...]
    kr = kr * cos_ref[...] + _rot_pairs(kr, 8) * sin_ref[...]
    return jnp.concatenate([acc[:, h * LANE:(h + 1) * LANE] + kr for h in range(MLA_HEADS)], axis=1)


def _mla_kernel(q_ref, kl_ref, kx_ref, vl_ref, vx_ref, o_ref, *, nq):
    def run(with_latent_keys):
        outs = []
        for j in range(MLA_HP):
            sl, vs = slice(j * LANE, (j + 1) * LANE), slice(j * 64, (j + 1) * 64)
            q = q_ref[:, sl]
            scores, values = [_nt(q, kx_ref[:, sl])], [vx_ref[:, vs]]
            if with_latent_keys:
                scores.insert(0, _nt(q, kl_ref[:, sl]))
                values.insert(0, vl_ref[:, vs])
            outs.append(_softmax_av(scores, values))
        o_ref[...] = jnp.concatenate(outs, axis=1).astype(o_ref.dtype)

    i = pl.program_id(2)
    pl.when(i < nq)(lambda: run(True))
    pl.when(i >= nq)(lambda: run(False))


def mla_attn(rows, qm, km, vm, tq=256):
    B, S, NC = rows.B, rows.S, rows.NC
    cb = B * S // NC
    nq, nqc = S // tq, NC // tq
    kw, vw = MLA_HP * LANE, MLA_HP * 64

    def q_block(b, i):
        return jnp.where(i < nq, b * nq + i, B * nq + b * nqc + (i - nq))

    return pl.pallas_call(
        functools.partial(_mla_kernel, nq=nq), grid=(B, MLA_HEADS // MLA_HP, nq + nqc),
        in_specs=[pl.BlockSpec((tq, kw), lambda b, h, i: (q_block(b, i), h)),
                  pl.BlockSpec((S, kw), lambda b, h, i: (b, h)),
                  pl.BlockSpec((NC, kw), lambda b, h, i: (cb + b, h)),
                  pl.BlockSpec((S, vw), lambda b, h, i: (b, h)),
                  pl.BlockSpec((NC, vw), lambda b, h, i: (cb + b, h))],
        out_specs=pl.BlockSpec((tq, vw), lambda b, h, i: (q_block(b, i), h)),
        out_shape=jax.ShapeDtypeStruct((rows.T, MLA_HEADS * 64), BF16),
        compiler_params=_cp(("parallel", "parallel", "parallel")), name="mla_attn",
    )(qm, km, km, vm, vm)


def _mla_norms_kernel(x0_ref, x1_ref, x2_ref, x3_ref, gq_ref, gkv_ref, q_ref, kv_ref, kr_ref):
    x = jnp.concatenate([x0_ref[...], x1_ref[...], x2_ref[...], x3_ref[...]], axis=1)
    q_ref[...] = _rms(x[:, 16:528], gq_ref[...]).astype(q_ref.dtype)
    kv_ref[...] = _rms(x[:, 528:784], gkv_ref[...]).astype(kv_ref.dtype)
    z = lambda w: jnp.zeros((x.shape[0], w), F32)
    kr_ref[...] = jnp.concatenate([z(MLA_NOPE), x[:, 784:816], z(LANE - MLA_NOPE - MLA_ROPE)], axis=1)


def mla_norms(rows, p2, gq, gkv):
    T = rows.T
    return pl.pallas_call(
        _mla_norms_kernel, grid=(rows.nt,),
        in_specs=[pl.BlockSpec((ROW_TILE, 256), functools.partial(lambda j, i: (i, PB_GATES // 256 + j), j))
                  for j in range(4)] + [_vec_spec(512), _vec_spec(256)],
        out_specs=[_row_spec(512), _row_spec(256), _row_spec(LANE)],
        out_shape=[jax.ShapeDtypeStruct((T, 512), BF16), jax.ShapeDtypeStruct((T, 256), BF16),
                   jax.ShapeDtypeStruct((T, LANE), F32)],
        compiler_params=_cp(("parallel",)), name="mla_norms",
    )(p2, p2, p2, p2, gq, gkv)


def _log_sigmoid(x):
    return jnp.minimum(x, 0.0) - jnp.log(1.0 + jnp.exp(-jnp.abs(x)))


def _mlstm_kernel(*refs, scale):
    nh = ML_HEADS
    per = 3 * nh + 1
    gb_ref, of_ref, ob_ref, c_ref, n_ref, m_ref = refs[2 * per:]

    @pl.when(pl.program_id(1) == 0)
    def _():
        c_ref[...] = jnp.zeros_like(c_ref)
        n_ref[...] = jnp.zeros_like(n_ref)
        m_ref[...] = jnp.zeros_like(m_ref)

    for d, o_ref in enumerate((of_ref, ob_ref)):
        r = refs[d * per:(d + 1) * per]
        _mlstm_chunk(r[0:nh], r[nh:2 * nh], r[2 * nh:3 * nh], r[3 * nh], gb_ref, o_ref,
                     c_ref.at[d], n_ref.at[d], m_ref.at[d], backward=d == 1, scale=scale)


def _mlstm_chunk(q_refs, k_refs, v_refs, g_ref, gb_ref, o_ref, c_ref, n_ref, m_ref, *, backward, scale):
    nh, L = ML_HEADS, ML_CHUNK
    t_i = lax.broadcasted_iota(jnp.int32, (L, L), 0)
    s_i = lax.broadcasted_iota(jnp.int32, (L, L), 1)
    mask = (s_i >= t_i) if backward else (s_i <= t_i)
    maskf = mask.astype(F32)
    gb = g_ref[...] + gb_ref[...]
    ls = _log_sigmoid(gb)
    gbt, lst = gb.T, ls.T
    cum_c = jnp.dot(maskf, ls, preferred_element_type=F32, precision=HI)
    cum_r = jnp.dot(lst, maskf.T, preferred_element_type=F32, precision=HI)
    last = 0 if backward else L - 1
    gi0, gf0 = (2 * nh, 3 * nh) if backward else (0, nh)
    for h in range(nh):
        gi, gf = gi0 + h, gf0 + h
        q = q_refs[h][...].astype(F32)
        kf = k_refs[h][...].astype(F32) * scale
        vf = v_refs[h][...].astype(F32)
        qb, kb = q.astype(BF16), kf.astype(BF16)
        cumc, cumr = cum_c[:, gf:gf + 1], cum_r[gf:gf + 1, :]
        li_r, li_c = gbt[gi:gi + 1, :], gb[:, gi:gi + 1]
        m11 = m_ref[h][0:1, 0:1]
        log_intra = jnp.where(mask, cumc - cumr + li_r, NEG)
        log_inter = cumc + m11
        m_t = jnp.maximum(log_inter, log_intra.max(axis=1, keepdims=True))
        w_inter = jnp.exp(log_inter - m_t)
        s = _nt(qb, kb) * jnp.exp(log_intra - m_t)
        cmat = c_ref[h]
        num = w_inter * _nt(qb, cmat.astype(BF16)) + jnp.dot(s.astype(BF16), vf.astype(BF16),
                                                             preferred_element_type=F32)
        den = w_inter * jnp.sum(q * n_ref[h][0:1, :], axis=1, keepdims=True) + s.sum(axis=1, keepdims=True)
        o_ref[:, h * L:(h + 1) * L] = num / jnp.maximum(jnp.abs(den), jnp.exp(-m_t))
        total = cumc[last:last + 1, :]
        log_w = total - cumc + li_c
        m_new = jnp.maximum(total + m11, log_w.max(axis=0, keepdims=True))
        decay = jnp.exp(total + m11 - m_new)
        w = jnp.exp(log_w - m_new)
        upd = lax.dot_general((vf * w).astype(BF16), kb, (((0,), (0,)), ((), ())), preferred_element_type=F32)
        c_ref[h] = decay * cmat + upd
        n_ref[h] = jnp.broadcast_to(decay * n_ref[h][0:1, :] + jnp.sum(w * kf, axis=0, keepdims=True), (8, L))
        m_ref[h] = jnp.broadcast_to(m_new, (8, LANE))


def mlstm_scan(rows, pa, pb, gbias):
    B, S, NC = rows.B, rows.S, rows.NC
    nh, L = ML_HEADS, ML_CHUNK
    ncx, ncl = NC // L, S // L
    first_ctx = B * S // L

    def row_block(backward, b, j):
        cx = (ncx - 1 - j) if backward else j
        cl = (ncl - 1 - (j - ncx)) if backward else (j - ncx)
        return jnp.where(j < ncx, first_ctx + b * ncx + cx, b * ncl + cl)

    def col_spec(backward, col):
        return pl.BlockSpec((L, LANE), lambda b, j: (row_block(backward, b, j), col))

    q0 = 2304 // LANE
    in_specs = []
    for backward in (False, True):
        in_specs += [col_spec(backward, q0 + c) for c in range(3 * nh)] + [col_spec(backward, PB_GATES // LANE)]
    in_specs.append(pl.BlockSpec((1, LANE), lambda b, j: (0, 0)))
    out = jax.ShapeDtypeStruct((rows.T, nh * L), F32)
    return pl.pallas_call(
        functools.partial(_mlstm_kernel, scale=L ** -0.5),
        grid=(B, ncx + ncl), in_specs=in_specs,
        out_specs=[pl.BlockSpec((L, nh * L), functools.partial(lambda bw, b, j: (row_block(bw, b, j), 0), bw))
                   for bw in (False, True)],
        out_shape=[out, out],
        scratch_shapes=[pltpu.VMEM((2, nh, L, L), F32), pltpu.VMEM((2, nh, 8, L), F32),
                        pltpu.VMEM((2, nh, 8, LANE), F32)],
        compiler_params=_cp(("parallel", "arbitrary")), name="mlstm_scan",
    )(*(([pa] * (3 * nh) + [pb]) * 2), gbias)


def _mlstm_finish_kernel(hf_ref, hb_ref, *refs):
    o_refs, g_ref, out_ref = refs[:ML_HEADS], refs[ML_HEADS], refs[ML_HEADS + 1]
    for h in range(ML_HEADS):
        sl = slice(h * ML_CHUNK, (h + 1) * ML_CHUNK)
        hn = _rms(hf_ref[:, sl] + hb_ref[:, sl], g_ref[:, sl])
        out_ref[:, sl] = (jax.nn.sigmoid(o_refs[h][...]) * hn).astype(out_ref.dtype)


def mlstm_finish(rows, hf, hb, p1, gnorm):
    W = ML_HEADS * ML_CHUNK
    o_specs = [pl.BlockSpec((ROW_TILE, LANE), functools.partial(lambda h, i: (i, h), h))
               for h in range(ML_HEADS)]
    return pl.pallas_call(
        _mlstm_finish_kernel, grid=(rows.nt,),
        in_specs=[_row_spec(W), _row_spec(W)] + o_specs + [_vec_spec(W)],
        out_specs=_row_spec(W),
        out_shape=jax.ShapeDtypeStruct((rows.T, W), BF16),
        compiler_params=_cp(("parallel",)), name="mlstm_finish",
    )(hf, hb, *([p1] * ML_HEADS), gnorm)


def _short_conv_kernel(p_ref, w_ref, o_ref):
    x = p_ref[...].astype(F32)
    L = x.shape[0]
    row = lax.broadcasted_iota(jnp.int32, x.shape, 0)
    prev = jnp.where(row == 0, 0.0, pltpu.roll(x, 1, 0))
    nxt = jnp.where(row == L - 1, 0.0, pltpu.roll(x, L - 1, 0))
    o_ref[...] = w_ref[0:1, :] * prev + w_ref[1:2, :] * x + w_ref[2:3, :] * nxt


def short_conv(p1, short_w, L, row_block0, B):
    ncol = 1536 // LANE
    per = 512 // LANE
    return pl.pallas_call(
        _short_conv_kernel, grid=(B, ncol),
        in_specs=[pl.BlockSpec((L, LANE), lambda b, c: (row_block0 + b, 768 // LANE + c)),
                  pl.BlockSpec((HY_SHORT, LANE), lambda b, c: (0, c))],
        out_specs=pl.BlockSpec((None, L, LANE), lambda b, c: (c // per, 0, b * per + c % per)),
        out_shape=jax.ShapeDtypeStruct((3, L, B * 512), F32),
        compiler_params=_cp(("parallel", "parallel")), name="hy_short_conv",
    )(p1, short_w)


def _filter_kernel(feat_ref, w1_ref, b1_ref, w2_ref, b2_ref, fr_ref, w3f_ref, w3b_ref, dl_ref, o_ref, hid_ref):
    L = feat_ref.shape[0]

    @pl.when((pl.program_id(0) == 0) & (pl.program_id(1) == 0))
    def _():
        fr = fr_ref[...]
        h1 = jnp.sin(fr * (jnp.dot(feat_ref[...], w1_ref[...], preferred_element_type=F32, precision=HI)
                           + b1_ref[...]))
        hid_ref[...] = jnp.sin(fr * (jnp.dot(h1, w2_ref[...], preferred_element_type=F32, precision=HI)
                                     + b2_ref[...]))

    hid = hid_ref[...]
    t = lax.broadcasted_iota(jnp.int32, (L, 1), 0).astype(F32) / L
    dec = jnp.exp(-t * dl_ref[...])
    hf = jnp.dot(hid, w3f_ref[...], preferred_element_type=F32, precision=HI) * dec
    hb = jnp.dot(hid, w3b_ref[...], preferred_element_type=F32, precision=HI) * dec
    hb = jnp.where(lax.broadcasted_iota(jnp.int32, hb.shape, 0) == 0, 0.0, hb)
    inv = 1.0 / (jnp.sum(jnp.abs(hf), axis=0, keepdims=True) + jnp.sum(jnp.abs(hb), axis=0, keepdims=True))
    o_ref[0] = ((hf + hb) * inv).astype(o_ref.dtype)
    o_ref[1] = ((hf - hb) * inv).astype(o_ref.dtype)


def hyena_filters(feats, w1p, b1, w2, b2, fr, w3, deltas, L):
    hid = w2.shape[0]
    per = 512 // LANE
    full = lambda shape: pl.BlockSpec(shape, lambda o, c: (0, 0))
    return pl.pallas_call(
        _filter_kernel, grid=(HY_ORDER, per),
        in_specs=[full((L, LANE)), full((LANE, hid)), full((1, hid)), full((hid, hid)), full((1, hid)),
                  full((1, hid)),
                  pl.BlockSpec((hid, LANE), lambda o, c: (0, (o * 2) * per + c)),
                  pl.BlockSpec((hid, LANE), lambda o, c: (0, (o * 2 + 1) * per + c)),
                  pl.BlockSpec((1, LANE), lambda o, c: (0, c))],
        out_specs=pl.BlockSpec((2, L, LANE), lambda o, c: (0, 0, o * per + c)),
        out_shape=jax.ShapeDtypeStruct((2, L, HY_ORDER * 512), BF16),
        scratch_shapes=[pltpu.VMEM((L, hid), F32)],
        compiler_params=_cp(("arbitrary", "arbitrary")), name="hy_filters",
    )(feats, w1p, b1, w2, b2, fr, w3, w3, deltas)


def _gate_kernel(x_ref, c_ref, z_ref, b_ref, o_ref):
    o_ref[...] = (x_ref[...] * (c_ref[...] + b_ref[...] * z_ref[...])).astype(o_ref.dtype)


def hyena_gate(u3, which_x, conv, z_arr, which_z, bias, order, L, B, tl, final=None):
    nl = L // tl
    n_fill = 0
    ci = lambda i: jnp.minimum(i, nl - 1)
    in_specs = [pl.BlockSpec((None, tl, 512), lambda i, b: (which_x, ci(i), b)),
                pl.BlockSpec((tl, 512), lambda i, b: (ci(i), b)),
                pl.BlockSpec((None, tl, 512), lambda i, b: (which_z, ci(i), b)),
                pl.BlockSpec((None, 1, 512), lambda i, b: (order, 0, 0))]
    args = [u3, conv, z_arr, bias]
    aliases = {}
    kern = _gate_kernel
    if final is not None:
        T, row0, into = final
        out_shape = jax.ShapeDtypeStruct((T, 512), BF16)
        if into is not None:
            out_spec = pl.BlockSpec((tl, 512), lambda i, b: (row0 // tl + b * nl + i, 0))
            in_specs.append(pl.BlockSpec(memory_space=pl.ANY))
            args.append(into)
            aliases = {4: 0}
            kern = lambda x, c, z, b, into_ref, o: _gate_kernel(x, c, z, b, o)
        else:
            assert row0 == 0
            n_fill = (T - B * L) // tl
            out_spec = pl.BlockSpec((tl, 512), lambda i, b: (jnp.where(i < nl, b * nl + i, B * nl + (i - nl)), 0))

            def kern(x, c, z, b, o):
                i = pl.program_id(0)
                pl.when(i < nl)(lambda: _gate_kernel(x, c, z, b, o))

                @pl.when(i >= nl)
                def _():
                    o[...] = jnp.zeros_like(o)
    else:
        out_spec = pl.BlockSpec((None, tl, 512), lambda i, b: (0, i, b))
        out_shape = jax.ShapeDtypeStruct((1, L, B * 512), F32)
    return pl.pallas_call(
        kern, grid=(nl + n_fill, B), in_specs=in_specs, out_specs=out_spec, out_shape=out_shape,
        input_output_aliases=aliases,
        compiler_params=_cp(("arbitrary", "arbitrary")), name="hy_gate",
    )(*args)


def _dft_gen_kernel(ca_ref, sa_ref, cb_ref, sb_ref, ca2_ref, sa2_ref, cb2_ref, sb2_ref, fwd_ref, fwdp_ref, inv_ref):
    tb, L = cb_ref.shape
    ca, sa, cb, sb = ca_ref[...], sa_ref[...], cb_ref[...], sb_ref[...]
    c, s = (ca * cb - sa * sb).astype(fwd_ref.dtype), (-(sa * cb + ca * sb)).astype(fwd_ref.dtype)
    fwd_ref[0], fwd_ref[1] = c, s
    fwdp_ref[0], fwdp_ref[1] = c, s
    ca, sa, cb, sb = ca2_ref[...], sa2_ref[...], cb2_ref[...], sb2_ref[...]
    c, s = (ca * cb - sa * sb).astype(inv_ref.dtype), (-(sa * cb + ca * sb)).astype(inv_ref.dtype)
    for j in range(L // tb):
        inv_ref[:, 2 * j * tb:(2 * j + 1) * tb] = c[:, j * tb:(j + 1) * tb]
        inv_ref[:, (2 * j + 1) * tb:(2 * j + 2) * tb] = s[:, j * tb:(j + 1) * tb]


def _dft_matrices(L, tb):
    nb = L // tb
    col = jnp.arange(L, dtype=jnp.int32)[None, :]
    r0 = (jnp.arange(nb, dtype=jnp.int32) * tb)[:, None]
    i = jnp.arange(tb, dtype=jnp.int32)[:, None]
    unit = math.pi / (2 * L)
    trig = lambda ph: (jnp.cos((ph % (4 * L)).astype(F32) * unit), jnp.sin((ph % (4 * L)).astype(F32) * unit))
    ca, sa = trig((2 * r0 + 1) * col)
    cb, sb = trig(2 * i * col)
    ca2, sa2 = trig((2 * col + 1) * r0)
    cb2, sb2 = trig((2 * col + 1) * i)
    blk = lambda: pl.BlockSpec((None, 1, L), lambda j: (j, 0, 0))
    shared = lambda: pl.BlockSpec((tb, L), lambda j: (0, 0))
    a3 = lambda a: a[:, None, :]
    fwd, fwd_plain, inv = pl.pallas_call(
        _dft_gen_kernel, grid=(nb,),
        in_specs=[blk(), blk(), shared(), shared(), blk(), blk(), shared(), shared()],
        out_specs=[pl.BlockSpec((None, 2, tb, L), lambda j: (j, 0, 0, 0)),
                   pl.BlockSpec((2, tb, L), lambda j: (0, j, 0)), pl.BlockSpec((tb, 2 * L), lambda j: (j, 0))],
        out_shape=[jax.ShapeDtypeStruct((nb, 2, tb, L), BF16), jax.ShapeDtypeStruct((2, L, L), BF16),
                   jax.ShapeDtypeStruct((L, 2 * L), BF16)],
        compiler_params=_cp(("parallel",)), name="hy_dft_gen",
    )(a3(ca), a3(sa), cb, sb, a3(ca2), a3(sa2), cb2, sb2)
    return fwd.reshape(2 * L, L), fwd_plain.reshape(2 * L, L), inv


def _filter_feats(L):
    t = jnp.arange(L, dtype=F32) / L
    kf = jnp.arange(1, HY_POS_FREQS + 1, dtype=F32)
    ang = 2.0 * math.pi * t[:, None] * kf
    feats = jnp.concatenate([t[:, None], jnp.sin(ang), jnp.cos(ang)], axis=-1)
    return jnp.pad(feats, ((0, 0), (0, LANE - feats.shape[1])))


def hyena_seq(p1, L, row_block0, B, consts, wts, tm, tl, T, into=None):
    fwd, fwd_plain, inv, feats = consts
    short_w, w1p, b1, w2, b2, fr, w3, deltas, bias = wts
    tb = min(L, ROW_TILE)
    nkb = L // tb
    u3 = short_conv(p1, short_w, L, row_block0, B)
    filt = hyena_filters(feats, w1p, b1, w2, b2, fr, w3, deltas, L)
    gid_f = jnp.repeat(jnp.arange(2, dtype=jnp.int32), L // tm)
    zeros_f = jnp.zeros((nkb,), jnp.int32)
    zeros_i = jnp.zeros((L // tm,), jnp.int32)
    NW = B * 512
    gspec = gmm(fwd_plain, filt, gid_f, tm=tm, tn=512, n_tiles=HY_ORDER, out_dtype=F32, name="hy_dft_filter")

    def cmul(acc, gre_ref, gim_ref):
        zre, zim, gre, gim = acc[:tb], acc[tb:], gre_ref[...], gim_ref[...]
        return jnp.concatenate([zre * gre - zim * gim, zre * gim + zim * gre], axis=0) * (1.0 / L)

    def gated(acc, x_ref, z_ref, b_ref):
        return x_ref[...] * (acc + b_ref[...] * z_ref[...])

    z_arr, which_z = u3, 0
    out = None
    for o in range(HY_ORDER):
        y = gmm(fwd, z_arr, zeros_f + which_z, tm=2 * tb, tn=512, n_tiles=B, out_dtype=BF16, epilogue=cmul,
                extra=[(gspec, (tb, 512), functools.partial(lambda o, n, m: (m, o), o)),
                       (gspec, (tb, 512), functools.partial(lambda o, n, m: (nkb + m, o), o))], name="hy_dft_fwd")
        y = y.reshape(1, 2 * L, NW)
        if o < HY_ORDER - 1:
            out = gmm(inv, y, zeros_i, tm=tm, tn=512, n_tiles=B, out_dtype=F32, epilogue=gated,
                      extra=[(u3, (None, tm, 512), functools.partial(lambda o, n, m: (1 + o, m, n), o)),
                             (z_arr, (None, tm, 512), functools.partial(lambda w, n, m: (w, m, n), which_z)),
                             (bias, (None, 1, 512), functools.partial(lambda o, n, m: (o, 0, 0), o))],
                      name="hy_dft_inv_gate").reshape(1, L, NW)
        else:
            conv = gmm(inv, y, zeros_i, tm=tm, tn=512, n_tiles=B, out_dtype=F32, name="hy_dft_inv")
            out = hyena_gate(u3, 1 + o, conv, z_arr, which_z, bias, o, L, B, tl, (T, row_block0 * L, into))
        z_arr, which_z = out, 0
    return out


CG = 256


def _short_conv_eo_kernel(p_ref, w_ref, o_ref, scr_ref):
    x = p_ref[...].astype(F32)
    L = x.shape[0]
    row = lax.broadcasted_iota(jnp.int32, x.shape, 0)
    prev = jnp.where(row == 0, 0.0, pltpu.roll(x, 1, 0))
    nxt = jnp.where(row == L - 1, 0.0, pltpu.roll(x, L - 1, 0))
    u = w_ref[0:1, :] * prev + w_ref[1:2, :] * x + w_ref[2:3, :] * nxt
    for j in range(CG // LANE):
        scr_ref[j] = u[:, j * LANE:(j + 1) * LANE]
        for par in range(2):
            o_ref[:, par * CG + j * LANE:par * CG + (j + 1) * LANE] = scr_ref[j, pl.ds(par, L // 2, stride=2), :]


def short_conv_eo(pa, short_w, L, row_block0, B):
    ng = 512 // CG
    return pl.pallas_call(
        _short_conv_eo_kernel, grid=(B, 3, ng),
        in_specs=[pl.BlockSpec((L, CG), lambda b, w, g: (row_block0 + b, 768 // CG + w * ng + g)),
                  pl.BlockSpec((HY_SHORT, CG), lambda b, w, g: (0, w * ng + g))],
        out_specs=pl.BlockSpec((None, L // 2, 2 * CG), lambda b, w, g: (w, 0, b * ng + g)),
        out_shape=jax.ShapeDtypeStruct((3, L // 2, B * 1024), F32),
        scratch_shapes=[pltpu.VMEM((CG // LANE, L, LANE), F32)],
        compiler_params=_cp(("parallel", "parallel", "parallel")), name="hy_short_conv",
    )(pa, short_w)


def _filter_eo_kernel(feat_ref, w1_ref, b1_ref, w2_ref, b2_ref, fr_ref, w3f_ref, w3b_ref, dl_ref, o_ref, hid_ref,
                      sf_ref, sb_ref):
    L = feat_ref.shape[0]
    H = L // 2

    @pl.when((pl.program_id(0) == 0) & (pl.program_id(1) == 0))
    def _():
        fr = fr_ref[...]
        h1 = jnp.sin(fr * (jnp.dot(feat_ref[...], w1_ref[...], preferred_element_type=F32, precision=HI)
                           + b1_ref[...]))
        hid_ref[...] = jnp.sin(fr * (jnp.dot(h1, w2_ref[...], preferred_element_type=F32, precision=HI)
                                     + b2_ref[...]))

    hid = hid_ref[...]
    t = lax.broadcasted_iota(jnp.int32, (L, 1), 0).astype(F32) / L
    dec = jnp.exp(-t * dl_ref[...])
    hf = jnp.dot(hid, w3f_ref[...], preferred_element_type=F32, precision=HI) * dec
    hb = jnp.dot(hid, w3b_ref[...], preferred_element_type=F32, precision=HI) * dec
    hb = jnp.where(lax.broadcasted_iota(jnp.int32, hb.shape, 0) == 0, 0.0, hb)
    inv = 1.0 / (jnp.sum(jnp.abs(hf), axis=0, keepdims=True) + jnp.sum(jnp.abs(hb), axis=0, keepdims=True))
    for part, (s_ref, hv) in enumerate(((sf_ref, hf * inv), (sb_ref, hb * inv))):
        for j in range(CG // LANE):
            s_ref[j] = hv[:, j * LANE:(j + 1) * LANE]
            for par in range(2):
                c0 = (part * 2 + par) * CG + j * LANE
                o_ref[:, c0:c0 + LANE] = s_ref[j, pl.ds(par, H, stride=2), :].astype(o_ref.dtype)


def hyena_filters_eo(feats, w1p, b1, w2, b2, fr, w3, deltas, L):
    hid = w2.shape[0]
    ng = 512 // CG
    full = lambda shape: pl.BlockSpec(shape, lambda o, g: (0, 0))
    return pl.pallas_call(
        _filter_eo_kernel, grid=(HY_ORDER, ng),
        in_specs=[full((L, LANE)), full((LANE, hid)), full((1, hid)), full((hid, hid)), full((1, hid)),
                  full((1, hid)),
                  pl.BlockSpec((hid, CG), lambda o, g: (0, (o * 2) * ng + g)),
                  pl.BlockSpec((hid, CG), lambda o, g: (0, (o * 2 + 1) * ng + g)),
                  pl.BlockSpec((1, CG), lambda o, g: (0, g))],
        out_specs=pl.BlockSpec((None, L // 2, 4 * CG), lambda o, g: (0, 0, o * ng + g)),
        out_shape=jax.ShapeDtypeStruct((1, L // 2, HY_ORDER * ng * 4 * CG), BF16),
        scratch_shapes=[pltpu.VMEM((L, hid), F32), pltpu.VMEM((CG // LANE, L, LANE), F32),
                        pltpu.VMEM((CG // LANE, L, LANE), F32)],
        compiler_params=_cp(("arbitrary", "arbitrary")), name="hy_filters",
    )(feats, w1p, b1, w2, b2, fr, w3, w3, deltas)


def _split_eo(acc, c_ref, s_ref):
    tb = acc.shape[0] // 2
    c, s = c_ref[...], s_ref[...]
    ere, eim, ore, oim = acc[:tb, :CG], acc[tb:, :CG], acc[:tb, CG:], acc[tb:, CG:]
    tre, tim = c * ore + s * oim, c * oim - s * ore
    return (ere + tre, eim + tim), (ere - tre, tim - eim)


def _filter_spec_epilogue(acc, c_ref, s_ref):
    (lre, lim), (hre, him) = _split_eo(acc, c_ref, s_ref)
    return jnp.concatenate([jnp.concatenate([lre, hre], axis=1), jnp.concatenate([lim, him], axis=1)], axis=0)


def _cmul(a, b):
    return a[0] * b[0] - a[1] * b[1], a[0] * b[1] + a[1] * b[0]


def _signal_spec_epilogue(acc, c_ref, s_ref, fre_ref, fim_ref, bre_ref, bim_ref, *, scale):
    c, s = c_ref[...], s_ref[...]
    x_lo, x_hi = _split_eo(acc, c_ref, s_ref)
    fre, fim, bre, bim = fre_ref[...], fim_ref[...], bre_ref[...], bim_ref[...]
    g_lo = (fre[:, :CG] + bre[:, :CG], fim[:, :CG] - bim[:, :CG])
    g_hi = (fre[:, CG:] + bre[:, CG:], fim[:, CG:] - bim[:, CG:])
    p_lo, p_hi = _cmul(x_lo, g_lo), _cmul(x_hi, g_hi)
    are, aim = p_lo[0] + p_hi[0], p_lo[1] - p_hi[1]
    dre, dim = p_lo[0] - p_hi[0], p_lo[1] + p_hi[1]
    bt_re, bt_im = dre * c - dim * s, dre * s + dim * c
    return jnp.concatenate([jnp.concatenate([are, bt_re], axis=1),
                            jnp.concatenate([aim, bt_im], axis=1)], axis=0) * scale


def _gated(acc, x_ref, z_ref, b_ref):
    return x_ref[...] * (acc + b_ref[...] * z_ref[...])


def _gate_eo_kernel(x_ref, c_ref, z_ref, b_ref, *refs, nl):
    o_ref, scr_ref = refs[-2:]
    i = pl.program_id(0)

    @pl.when(i < nl)
    def _():
        r = x_ref[...] * (c_ref[...] + b_ref[...] * z_ref[...])
        half = r.shape[0]
        for g in range(512 // CG):
            for j in range(CG // LANE):
                slab = g * (CG // LANE) + j
                for par in range(2):
                    c0 = (g * 2 + par) * CG + j * LANE
                    scr_ref[slab, pl.ds(par, half, stride=2), :] = r[:, c0:c0 + LANE]
                o_ref[:, slab * LANE:(slab + 1) * LANE] = scr_ref[slab].astype(o_ref.dtype)

    @pl.when(i >= nl)
    def _():
        o_ref[...] = jnp.zeros_like(o_ref)


def hyena_gate_eo(u3, conv, z1, bias_u, L, B, tl, T, row0, into):
    nl = L // tl
    th = tl // 2
    ci = lambda i: jnp.minimum(i, nl - 1)
    in_specs = [pl.BlockSpec((None, th, 1024), lambda i, b: (2, ci(i), b)),
                pl.BlockSpec((th, 1024), lambda i, b: (ci(i), b)),
                pl.BlockSpec((None, th, 1024), lambda i, b: (0, ci(i), b)),
                pl.BlockSpec((None, 1, 1024), lambda i, b: (HY_ORDER - 1, 0, 0))]
    args = [u3, conv, z1, bias_u]
    if into is not None:
        n_fill, aliases = 0, {4: 0}
        in_specs.append(pl.BlockSpec(memory_space=pl.ANY))
        args.append(into)
        out_spec = pl.BlockSpec((tl, 512), lambda i, b: (row0 // tl + b * nl + i, 0))
    else:
        assert row0 == 0
        n_fill, aliases = (T - B * L) // tl, {}
        out_spec = pl.BlockSpec((tl, 512), lambda i, b: (jnp.where(i < nl, b * nl + i, B * nl + (i - nl)), 0))
    return pl.pallas_call(
        functools.partial(_gate_eo_kernel, nl=nl), grid=(nl + n_fill, B), in_specs=in_specs, out_specs=out_spec,
        out_shape=jax.ShapeDtypeStruct((T, 512), BF16), input_output_aliases=aliases,
        scratch_shapes=[pltpu.VMEM((512 // LANE, tl, LANE), F32)],
        compiler_params=_cp(("arbitrary", "arbitrary")), name="hy_gate",
    )(*args)


def _hyena_consts(L):
    H = L // 2
    tb = min(H, ROW_TILE)
    fwd, _, inv = _dft_matrices(H, tb)
    w = (2 * jnp.arange(H, dtype=F32)[:, None] + 1.0) * (math.pi / (2 * L))
    twc, tws = jnp.broadcast_to(jnp.cos(w), (H, CG)), jnp.broadcast_to(jnp.sin(w), (H, CG))
    return fwd, inv, twc, tws, _filter_feats(L)


def hyena_seq_eo(pa, L, row_block0, B, consts, wts, T, into=None):
    fwd, inv, twc, tws, feats = consts
    short_w, w1p, b1, w2, b2, fr, w3, deltas, bias_u = wts
    H = L // 2
    tb = min(H, ROW_TILE)
    nkb = H // tb
    tm_i = min(H, 512)
    ng = 512 // CG
    NW = B * 1024
    u3 = short_conv_eo(pa, short_w, L, row_block0, B)
    filt = hyena_filters_eo(feats, w1p, b1, w2, b2, fr, w3, deltas, L)
    zeros_f = jnp.zeros((nkb,), jnp.int32)
    zeros_i = jnp.zeros((H // tm_i,), jnp.int32)
    tw = [(twc, (tb, CG), lambda n, m: (m, 0)), (tws, (tb, CG), lambda n, m: (m, 0))]
    gs = gmm(fwd, filt, zeros_f, tm=2 * tb, tn=2 * CG, n_tiles=HY_ORDER * ng * 2, out_dtype=F32,
             epilogue=_filter_spec_epilogue, extra=tw, name="hy_dft_filter")
    z_arr, which_z = u3, 0
    out = None
    for o in range(HY_ORDER):
        def g_blk(comp, part, o=o):
            return (gs, (tb, 2 * CG), lambda n, m: (2 * m + comp, ((o * ng) + n % ng) * 2 + part))

        y = gmm(fwd, z_arr, zeros_f + which_z, tm=2 * tb, tn=2 * CG, n_tiles=B * ng, out_dtype=BF16,
                epilogue=functools.partial(_signal_spec_epilogue, scale=1.0 / L),
                extra=tw + [g_blk(0, 0), g_blk(1, 0), g_blk(0, 1), g_blk(1, 1)], name="hy_dft_fwd")
        y = y.reshape(1, 2 * H, NW)
        if o < HY_ORDER - 1:
            out = gmm(inv, y, zeros_i, tm=tm_i, tn=2 * CG, n_tiles=B * ng, out_dtype=F32, epilogue=_gated,
                      extra=[(u3, (None, tm_i, 2 * CG), functools.partial(lambda o, n, m: (1 + o, m, n), o)),
                             (z_arr, (None, tm_i, 2 * CG), functools.partial(lambda w, n, m: (w, m, n), which_z)),
                             (bias_u, (None, 1, 2 * CG), functools.partial(lambda o, n, m: (o, 0, n % ng), o))],
                      name="hy_dft_inv_gate").reshape(1, H, NW)
        else:
            conv = gmm(inv, y, zeros_i, tm=tm_i, tn=2 * CG, n_tiles=B * ng, out_dtype=F32, name="hy_dft_inv")
            out = hyena_gate_eo(u3, conv, z_arr, bias_u, L, B, min(L, 512), T, row_block0 * L, into)
        z_arr, which_z = out, 0
    return out


def _gather_kernel(idx_ref, src_ref, *refs, tm, k):
    if k > 1:
        w_ref, o_ref, buf_ref, sem = refs
    else:
        o_ref, buf_ref, sem = refs
    i, n = pl.program_id(0), pl.num_programs(0)
    M = n * tm

    def issue_tile(tile, slot):
        for j in range(k):
            def issue(r, carry):
                pltpu.make_async_copy(src_ref.at[pl.ds(idx_ref[j * M + tile * tm + r], 1)],
                                      buf_ref.at[slot, j, pl.ds(r, 1)], sem.at[slot]).start()
                return carry

            lax.fori_loop(0, tm, issue, 0, unroll=8)

    @pl.when(i == 0)
    def _():
        issue_tile(0, 0)

    @pl.when(i + 1 < n)
    def _():
        issue_tile(i + 1, (i + 1) % 2)

    slot = i % 2
    for j in range(k):
        pltpu.make_async_copy(src_ref.at[pl.ds(0, tm)], buf_ref.at[slot, j], sem.at[slot]).wait()
    if k > 1:
        w = w_ref[...]
        acc = w[:, 0:1] * buf_ref[slot, 0]
        for j in range(1, k):
            acc = acc + w[:, j:j + 1] * buf_ref[slot, j]
        o_ref[...] = acc.astype(o_ref.dtype)
    else:
        o_ref[...] = buf_ref[slot, 0].astype(o_ref.dtype)


def gather_rows(src, idx, tm, out_dtype, weights=None, k=1):
    M = idx.shape[0] // k
    D = src.shape[1]
    in_specs = [pl.BlockSpec(memory_space=pl.ANY)]
    args = [src]
    if k > 1:
        in_specs.append(pl.BlockSpec((tm, LANE), lambda i, idx: (i, 0)))
        args.append(weights)
    return pl.pallas_call(
        functools.partial(_gather_kernel, tm=tm, k=k),
        grid_spec=pltpu.PrefetchScalarGridSpec(
            num_scalar_prefetch=1, grid=(M // tm,), in_specs=in_specs,
            out_specs=pl.BlockSpec((tm, D), lambda i, idx: (i, 0)),
            scratch_shapes=[pltpu.VMEM((2, k, tm, D), src.dtype), pltpu.SemaphoreType.DMA((2,))]),
        out_shape=jax.ShapeDtypeStruct((M, D), out_dtype),
        compiler_params=_cp(("arbitrary",)), name="gather_rows" if k == 1 else "gather_combine",
    )(idx, *args)


def moe_ffn(rows, h_f32, rw, ri, w_gate, w_up, w_down, layer_moe, tm=512):
    T, D = rows.T, rows.D
    E = w_gate.shape[2]
    e_flat = ri[:, :TOP_K].T.reshape(-1)
    onehot = (e_flat[:, None] == jnp.arange(N_EXPERTS, dtype=jnp.int32)[None, :]).astype(jnp.int32)
    csum = jnp.cumsum(onehot, axis=0)
    rank = jnp.sum(csum * onehot, axis=1) - 1
    counts = csum[-1]
    padded = ((counts + tm - 1) // tm) * tm
    ends = jnp.cumsum(padded)
    starts = ends - padded
    dest = starts[e_flat] + rank
    m_pad = TOP_K * T + N_EXPERTS * tm
    token = jnp.tile(jnp.arange(T, dtype=jnp.int32), TOP_K)
    src_tok = jnp.zeros((m_pad,), jnp.int32).at[dest].set(token)
    tile_row = jnp.arange(m_pad // tm, dtype=jnp.int32) * tm
    gid = jnp.minimum(jnp.sum((tile_row[:, None] >= ends[None, :]).astype(jnp.int32), axis=1), N_EXPERTS - 1)
    gid = gid + layer_moe * N_EXPERTS
    used = (ends[-1:] // tm).astype(jnp.int32)

    xs = gather_rows(h_f32, src_tok, tm, BF16)
    tile = lambda n, m: (m, n)
    g = gmm(xs, w_gate, gid, tm=tm, tn=E // 2, n_tiles=2, out_dtype=BF16, epilogue=_silu, used=used,
            name="moe_gate")
    hh = gmm(xs, w_up, gid, tm=tm, tn=E // 2, n_tiles=2, out_dtype=BF16, epilogue=_times,
             extra=[(g, (tm, E // 2), tile)], used=used, name="moe_up")
    y = gmm(hh, w_down, gid, tm=tm, tn=D // 2, n_tiles=2, out_dtype=F32, used=used, name="moe_down")
    return gather_rows(y, dest, ROW_TILE, BF16, weights=rw, k=TOP_K)


def kernel(x, c, ctx, c_ctx, w_mod, b_mod, g_mix_pre, g_mix_post, g_ffn_pre, g_ffn_post, w_in, w_out, attn_sink,
           hy_short, hy_w1, hy_b1, hy_w2, hy_b2, hy_freq, hy_w3, hy_bias, ml_gate_bias, ml_norm, mla_q_norm,
           mla_w_uq, mla_kv_norm, mla_w_ukv, ffn_w_gate, ffn_w_up, ffn_w_down, moe_router, moe_w_gate, moe_w_up,
           moe_w_down):
    B, S, D = x.shape
    NC = ctx.shape[1]
    depth = w_mod.shape[0]
    rows = _Rows(B, S, NC, D)
    T = rows.T
    TM = 512 if T % 512 == 0 else ROW_TILE
    n_mt = T // TM
    ffn_dim = ffn_w_gate.shape[2]

    xs = jnp.concatenate([x.reshape(B * S, D), ctx.reshape(B * NC, D)], axis=0)

    cm = jnp.concatenate([c, c_ctx[None, :]], axis=0)
    cm = jnp.pad(jax.nn.silu(cm), ((0, 16 - (B + 1)), (0, 0))).astype(BF16)
    mod_all = gmm(jnp.tile(cm, (depth, 1)), w_mod, jnp.arange(depth, dtype=jnp.int32), tm=16, tn=1536,
                  n_tiles=N_MOD * D // 1536, out_dtype=F32, name="adaln")
    mods = (mod_all.reshape(depth, 16, N_MOD * D)[:, :B + 1] + b_mod[:, None, :]).reshape(depth, B + 1, N_MOD, 1, D)

    def rope_table(rot_dim, reps, lane0, width):
        cos, sin = _axial_tables(S, rot_dim)
        cos, sin = jnp.tile(cos, (1, reps)), jnp.tile(sin, (1, reps))
        padw = ((0, 0), (lane0, width - lane0 - cos.shape[1]))
        cos = jnp.pad(cos, padw, constant_values=1.0)
        sin = jnp.pad(sin, padw)
        cos = jnp.concatenate([jnp.tile(cos, (B, 1)), jnp.ones((B * NC, width), F32)], axis=0)
        sin = jnp.concatenate([jnp.tile(sin, (B, 1)), jnp.zeros((B * NC, width), F32)], axis=0)
        return cos, sin

    cos_a, sin_a = rope_table(64, 2, 0, LANE)
    cos_m, sin_m = rope_table(MLA_ROPE, 1, MLA_NOPE, LANE)
    hy_consts = {L: _hyena_consts(L) for L in (S, NC)}
    deltas = jnp.abs(jnp.linspace(math.log(HY_DECAY_TARGET) / HY_SLOW_DECAY,
                                  math.log(HY_DECAY_TARGET) / HY_FAST_DECAY, 512, dtype=F32))[None, :]

    w_in_t = jnp.swapaxes(w_in, 1, 2)
    wq = mla_w_uq.reshape(depth, -1, MLA_HEADS, MLA_NOPE + MLA_ROPE)
    wq = jnp.pad(wq, ((0, 0), (0, 0), (0, 0), (0, LANE - MLA_NOPE - MLA_ROPE))).reshape(depth, -1, MLA_HEADS * LANE)
    wkv = mla_w_ukv.reshape(depth, -1, MLA_HEADS, 2 * 64)
    wk = jnp.pad(wkv[..., :MLA_NOPE], ((0, 0), (0, 0), (0, 0), (0, LANE - MLA_NOPE)))
    wk = wk.reshape(depth, -1, MLA_HEADS * LANE)
    wv = wkv[..., MLA_NOPE:].reshape(depth, -1, MLA_HEADS * 64)
    w1p = jnp.pad(hy_w1, ((0, 0), (0, LANE - hy_w1.shape[1]), (0, 0)))
    bias_u = jnp.broadcast_to(hy_bias.reshape(depth, HY_ORDER, 512 // CG, 1, CG),
                              (depth, HY_ORDER, 512 // CG, 2, CG)).reshape(depth, HY_ORDER, 1, 1024)
    gbias = jnp.pad(ml_gate_bias.reshape(depth, 1, 4 * ML_HEADS), ((0, 0), (0, 0), (0, LANE - 4 * ML_HEADS)))
    router = jnp.pad(moe_router, ((0, 0), (0, 0), (0, LANE - N_EXPERTS)))
    n_moe = moe_w_gate.shape[0]
    mw_gate = moe_w_gate.reshape(n_moe * N_EXPERTS, D, -1)
    mw_up = moe_w_up.reshape(n_moe * N_EXPERTS, D, -1)
    mw_down = moe_w_down.reshape(n_moe * N_EXPERTS, -1, D)
    vec = lambda a, l: a[l][None, :]

    h = norm_mod(rows, xs, vec(g_mix_pre, 0), mods[0], 0, 1)
    for layer in range(depth):
        gid = jnp.full((n_mt,), layer, jnp.int32)
        ml = mods[layer]
        p1 = gmm(h, w_in_t, gid, tm=TM, tn=1280, n_tiles=PA_COLS // 1280, out_dtype=BF16, w_t=True, name="proj_in")
        p2 = gmm(h, w_in_t, gid, tm=TM, tn=768, n_tiles=2, n_off=PA_COLS // 768, out_dtype=F32, w_t=True,
                 name="proj_in_f32")

        qkv = attn_rope(rows, p1, cos_a, sin_a)
        a_all = window_attn(rows, qkv, attn_sink[layer])

        hy_w = (hy_short[layer], w1p[layer], vec(hy_b1, layer), hy_w2[layer], vec(hy_b2, layer),
                vec(hy_freq, layer), hy_w3[layer], deltas, bias_u[layer])
        b_l = hyena_seq_eo(p1, S, 0, B, hy_consts[S], hy_w, T)
        b_all = hyena_seq_eo(p1, NC, B * S // NC, B, hy_consts[NC], hy_w, T, into=b_l)

        hf, hb = mlstm_scan(rows, p1, p2, gbias[layer])
        m_all = mlstm_finish(rows, hf, hb, p2, vec(ml_norm, layer))

        qn, kvn, kr = mla_norms(rows, p2, vec(mla_q_norm, layer), vec(mla_kv_norm, layer))
        slab = lambda a: (a, (TM, LANE), lambda n, m: (m, 0))
        qm = gmm(qn, wq, gid, tm=TM, tn=MLA_HEADS * LANE, n_tiles=1, out_dtype=BF16, epilogue=_mla_q_epilogue,
                 extra=[slab(cos_m), slab(sin_m)], name="mla_uq")
        km = gmm(kvn, wk, gid, tm=TM, tn=MLA_HEADS * LANE, n_tiles=1, out_dtype=BF16, epilogue=_mla_k_epilogue,
                 extra=[slab(kr), slab(cos_m), slab(sin_m)], name="mla_uk")
        vm = gmm(kvn, wv, gid, tm=TM, tn=MLA_HEADS * 64, n_tiles=1, out_dtype=BF16, name="mla_uv")
        d_all = mla_attn(rows, qm, km, vm)

        y = gmm([a_all, b_all, m_all, d_all], w_out, gid, tm=TM, tn=1024, n_tiles=D // 1024, out_dtype=BF16,
                name="proj_out")

        i = layer // 2
        nxt = (vec(g_ffn_pre, layer), ml, 3, 4)
        if layer % 2 == 0:
            xs, h2 = post(rows, xs, y, vec(g_mix_post, layer), ml, 2, nxt=nxt)
            gi = jnp.full((n_mt,), i, jnp.int32)
            g = gmm(h2, ffn_w_gate, gi, tm=TM, tn=ffn_dim // 2, n_tiles=2, out_dtype=BF16, epilogue=_silu,
                    w_single_buffer=True, name="ffn_gate")
            hh = gmm(h2, ffn_w_up, gi, tm=TM, tn=ffn_dim // 2, n_tiles=2, out_dtype=BF16, epilogue=_times,
                     extra=[(g, (TM, ffn_dim // 2), lambda n, m: (m, n))], w_single_buffer=True, name="ffn_up")
            y2 = gmm(hh, ffn_w_down, gi, tm=TM, tn=512, n_tiles=D // 512, out_dtype=BF16, name="ffn_down")
        else:
            xs, h2, h2f, rw, ri = post(rows, xs, y, vec(g_mix_post, layer), ml, 2, nxt=nxt, router=router[i])
            y2 = moe_ffn(rows, h2f, rw, ri, mw_gate, mw_up, mw_down, i)
        if layer + 1 < depth:
            xs, h = post(rows, xs, y2, vec(g_ffn_post, layer), ml, 5,
                         nxt=(vec(g_mix_pre, layer + 1), mods[layer + 1], 0, 1))
        else:
            (xs,) = post(rows, xs, y2, vec(g_ffn_post, layer), ml, 5, latent_only=True)
    return xs.reshape(B, S, D)
```

```python
import functools
import math

import numpy as np
import jax
import jax.numpy as jnp
from jax import lax
from jax.experimental import pallas as pl
from jax.experimental.pallas import tpu as pltpu

F32 = jnp.float32
BF16 = jnp.bfloat16
HI = lax.Precision.HIGHEST

EPS = 1e-6
ROPE_BASE = 10000.0
GRID_W = 64
BLOCK = 128
WINDOW = 128
N_MOD = 6
A_HEADS, A_KV_HEADS = 8, 2
HY_ORDER, HY_SHORT, HY_POS_FREQS = 2, 3, 8
HY_DECAY_TARGET, HY_FAST_DECAY, HY_SLOW_DECAY = 1e-2, 0.3, 1.5
ML_HEADS, ML_CHUNK = 4, 128
MLA_HEADS, MLA_NOPE, MLA_ROPE = 8, 64, 32
N_EXPERTS, TOP_K = 8, 2

PA_COLS = 3840
PB_GATES = 4352 - PA_COLS
MLA_HP = 8
LANE = 128
ROW_TILE = 256
NEG = -1e30
LOG2E = math.log2(math.e)
VMEM_LIMIT = 56 * 1024 * 1024


def _cp(sem, vmem=VMEM_LIMIT):
    return pltpu.CompilerParams(dimension_semantics=sem, vmem_limit_bytes=vmem)


def _gmm_kernel(gid_ref, used_ref, *refs, n_a, cast, w_t, epilogue, n_extra, w_n, tn, n_off):
    a_refs, w_ref, rest = refs[:n_a], refs[n_a], refs[n_a + 1:]
    extra_refs, rest = rest[:n_extra], rest[n_extra:]
    o_ref = rest[0]
    n, m = pl.program_id(0), pl.program_id(1)
    n_axis = 0 if w_t else 1
    if cast:
        wbf_ref = rest[1]
        prev = gid_ref[jnp.maximum(m - 1, 0)]

        @pl.when((m == 0) | (gid_ref[m] != prev))
        def _():
            w = w_ref[...]
            if w_n is not None:
                col = (n + n_off) * tn + lax.broadcasted_iota(jnp.int32, w.shape, n_axis)
                w = jnp.where(col < w_n, w, 0.0)
            wbf_ref[...] = w.astype(BF16)

        src = wbf_ref
    else:
        src = w_ref

    @pl.when(m < used_ref[0])
    def _():
        acc, lo = None, 0
        for a_ref in a_refs:
            k = a_ref.shape[1]
            if w_t:
                part = lax.dot_general(a_ref[...], src[:, lo:lo + k], (((1,), (1,)), ((), ())),
                                       preferred_element_type=F32)
            else:
                part = jnp.dot(a_ref[...], src[lo:lo + k, :], preferred_element_type=F32)
            acc = part if acc is None else acc + part
            lo += k
        if epilogue is not None:
            acc = epilogue(acc, *extra_refs)
        o_ref[...] = acc.astype(o_ref.dtype)

    @pl.when(m >= used_ref[0])
    def _():
        o_ref[...] = jnp.zeros_like(o_ref)


def _silu(acc):
    return acc * jax.nn.sigmoid(acc)


def _times(acc, g_ref):
    return acc * g_ref[...].astype(F32)


def gmm(a, w, gid, *, tm, tn, n_tiles, out_dtype, n_off=0, epilogue=None, extra=(), w_t=False, used=None,
        w_single_buffer=False, name="gmm"):
    a_list = list(a) if isinstance(a, (list, tuple)) else [a]
    M = a_list[0].shape[0]
    K = sum(p.shape[1] for p in a_list)
    mt = M // tm
    k_axis, n_axis = (2, 1) if w_t else (1, 2)
    assert mt * tm == M and w.shape[k_axis] == K and gid.shape == (mt,)
    cast = w.dtype != BF16
    partial_n = (n_off + n_tiles) * tn > w.shape[n_axis]
    assert cast or not partial_n
    if used is None:
        used = jnp.full((1,), mt, jnp.int32)
    in_specs = [pl.BlockSpec((tm, p.shape[1]), lambda n, m, g, u: (m, 0)) for p in a_list]
    w_mode = dict(pipeline_mode=pl.Buffered(1)) if w_single_buffer else {}
    if w_t:
        in_specs.append(pl.BlockSpec((None, tn, K), lambda n, m, g, u: (g[m], n + n_off, 0), **w_mode))
    else:
        in_specs.append(pl.BlockSpec((None, K, tn), lambda n, m, g, u: (g[m], 0, n + n_off), **w_mode))
    args = a_list + [w]
    for arr, block, imap in extra:
        in_specs.append(pl.BlockSpec(block, functools.partial(lambda f, n, m, g, u: f(n, m), imap)))
        args.append(arr)
    kern = functools.partial(_gmm_kernel, n_a=len(a_list), cast=cast, w_t=w_t, epilogue=epilogue,
                             n_extra=len(extra), w_n=w.shape[n_axis] if partial_n else None, tn=tn, n_off=n_off)
    return pl.pallas_call(
        kern,
        grid_spec=pltpu.PrefetchScalarGridSpec(
            num_scalar_prefetch=2, grid=(n_tiles, mt), in_specs=in_specs,
            out_specs=pl.BlockSpec((tm, tn), lambda n, m, g, u: (m, n)),
            scratch_shapes=[pltpu.VMEM((tn, K) if w_t else (K, tn), BF16)] if cast else []),
        out_shape=jax.ShapeDtypeStruct((M, n_tiles * tn), out_dtype),
        compiler_params=_cp(("arbitrary", "arbitrary")),
        name=name,
    )(gid, used, *args)


def _rms(v, g):
    return v * lax.rsqrt(jnp.mean(v * v, axis=-1, keepdims=True) + EPS) * g


def _norm_mod_kernel(x_ref, g_ref, sh_ref, sc_ref, h_ref):
    h_ref[...] = (_rms(x_ref[...], g_ref[...]) * (1.0 + sc_ref[...]) + sh_ref[...]).astype(h_ref.dtype)


def _post_kernel(*refs, with_next, with_router):
    x_ref, y_ref, gp_ref, gate_ref = refs[:4]
    refs = refs[4:]
    xn = x_ref[...] + gate_ref[...] * _rms(y_ref[...].astype(F32), gp_ref[...])
    if not with_next:
        refs[0][...] = xn
        return
    gn_ref, sh_ref, sc_ref = refs[:3]
    refs = refs[3:]
    if with_router:
        r_ref, refs = refs[0], refs[1:]
    refs[0][...] = xn
    h = _rms(xn, gn_ref[...]) * (1.0 + sc_ref[...]) + sh_ref[...]
    refs[1][...] = h.astype(BF16)
    if with_router:
        hf_ref, rw_ref, ri_ref = refs[2:5]
        hf_ref[...] = h
        logits = jnp.dot(h, r_ref[...], preferred_element_type=F32, precision=HI)
        lane = lax.broadcasted_iota(jnp.int32, logits.shape, 1)
        logits = jnp.where(lane < N_EXPERTS, logits, -jnp.inf)
        v1 = jnp.max(logits, axis=1, keepdims=True)
        i1 = jnp.min(jnp.where(logits == v1, lane, LANE), axis=1, keepdims=True)
        l2 = jnp.where(lane == i1, -jnp.inf, logits)
        v2 = jnp.max(l2, axis=1, keepdims=True)
        i2 = jnp.min(jnp.where(l2 == v2, lane, LANE), axis=1, keepdims=True)
        e = jnp.exp(v2 - v1)
        w1 = 1.0 / (1.0 + e)
        w2 = e / (1.0 + e)
        rw_ref[...] = jnp.where(lane == 0, w1, jnp.where(lane == 1, w2, 0.0))
        ri_ref[...] = jnp.where(lane == 0, i1, jnp.where(lane == 1, i2, 0))


class _Rows:
    def __init__(self, B, S, NC, D):
        self.B, self.S, self.NC, self.D = B, S, NC, D
        self.T = B * S + B * NC
        assert S % ROW_TILE == 0 and NC % ROW_TILE == 0
        self.nt = self.T // ROW_TILE

    def mod_row(self, i):
        n_lat = self.B * self.S // ROW_TILE
        return jnp.where(i < n_lat, i // (self.S // ROW_TILE), self.B)


def _row_spec(D):
    return pl.BlockSpec((ROW_TILE, D), lambda i: (i, 0))


def _vec_spec(D):
    return pl.BlockSpec((1, D), lambda i: (0, 0))


def _mod_spec(rows, j):
    return pl.BlockSpec((None, None, 1, rows.D), lambda i: (rows.mod_row(i), j, 0, 0))


def norm_mod(rows, x, g, mods, j_shift, j_scale):
    D = rows.D
    return pl.pallas_call(
        _norm_mod_kernel, grid=(rows.nt,),
        in_specs=[_row_spec(D), _vec_spec(D), _mod_spec(rows, j_shift), _mod_spec(rows, j_scale)],
        out_specs=_row_spec(D),
        out_shape=jax.ShapeDtypeStruct((rows.T, D), BF16),
        compiler_params=_cp(("parallel",)), name="norm_mod",
    )(x, g, mods, mods)


def post(rows, x, y, g_post, mods, j_gate, nxt=None, router=None, latent_only=False):
    D = rows.D
    T = rows.B * rows.S if latent_only else rows.T
    in_specs = [_row_spec(D), _row_spec(D), _vec_spec(D), _mod_spec(rows, j_gate)]
    args = [x, y, g_post, mods]
    out_specs = [_row_spec(D)]
    out_shape = [jax.ShapeDtypeStruct((T, D), F32)]
    if nxt is not None:
        g_next, mods_next, j_shift, j_scale = nxt
        in_specs += [_vec_spec(D), _mod_spec(rows, j_shift), _mod_spec(rows, j_scale)]
        args += [g_next, mods_next, mods_next]
        out_specs.append(_row_spec(D))
        out_shape.append(jax.ShapeDtypeStruct((T, D), BF16))
        if router is not None:
            in_specs.append(pl.BlockSpec((D, LANE), lambda i: (0, 0)))
            args.append(router)
            out_specs += [_row_spec(D), _row_spec(LANE), _row_spec(LANE)]
            out_shape += [jax.ShapeDtypeStruct((T, D), F32), jax.ShapeDtypeStruct((T, LANE), F32),
                          jax.ShapeDtypeStruct((T, LANE), jnp.int32)]
    kern = functools.partial(_post_kernel, with_next=nxt is not None, with_router=router is not None)
    return pl.pallas_call(
        kern, grid=(T // ROW_TILE,), in_specs=in_specs, out_specs=out_specs, out_shape=out_shape,
        compiler_params=_cp(("parallel",)), name="post",
    )(*args)


def _rot_pairs(x, q):
    lane = lax.broadcasted_iota(jnp.int32, x.shape, 1)
    n = x.shape[1]
    return jnp.where(lane % (2 * q) < q, -pltpu.roll(x, n - q, 1), pltpu.roll(x, q, 1))


def _axial_tables(S, rot_dim):
    rows = S // GRID_W
    r = jnp.repeat(jnp.arange(rows, dtype=F32), GRID_W)
    col = jnp.tile(jnp.arange(GRID_W, dtype=F32), rows)
    half = rot_dim // 2
    inv = ROPE_BASE ** (-jnp.arange(0, half, 2, dtype=F32) / half)
    ar, ac = r[:, None] * inv, col[:, None] * inv
    ang = jnp.concatenate([ar, ar, ac, ac], axis=-1)
    return jnp.cos(ang), jnp.sin(ang)


def _attn_rope_kernel(p_ref, cos_ref, sin_ref, o_ref, *, scale):
    cos, sin = cos_ref[...], sin_ref[...]
    nq = A_HEADS * 64 // LANE
    for s in range(nq + 1):
        xs = p_ref[:, s * LANE:(s + 1) * LANE].astype(F32)
        r = xs * cos + _rot_pairs(xs, 16) * sin
        if s < nq:
            r = r * scale
        o_ref[:, s * LANE:(s + 1) * LANE] = r.astype(o_ref.dtype)
    o_ref[:, (nq + 1) * LANE:] = p_ref[:, (nq + 1) * LANE:].astype(o_ref.dtype)


def attn_rope(rows, p1, cos, sin):
    W = 768
    return pl.pallas_call(
        functools.partial(_attn_rope_kernel, scale=64 ** -0.5 * LOG2E), grid=(rows.nt,),
        in_specs=[pl.BlockSpec((ROW_TILE, W), lambda i: (i, 0)), _row_spec(LANE), _row_spec(LANE)],
        out_specs=pl.BlockSpec((ROW_TILE, W), lambda i: (i, 0)),
        out_shape=jax.ShapeDtypeStruct((rows.T, W), BF16),
        compiler_params=_cp(("parallel",)), name="attn_rope",
    )(p1, cos, sin)


def _nt(a, b):
    return lax.dot_general(a, b, (((1,), (1,)), ((), ())), preferred_element_type=F32)


def _softmax_av(scores, values, sink=None):
    m = scores[0].max(axis=1, keepdims=True)
    for s in scores[1:]:
        m = jnp.maximum(m, s.max(axis=1, keepdims=True))
    if sink is not None:
        m = jnp.maximum(m, sink)
    den = jnp.exp2(sink - m) if sink is not None else 0.0
    acc = None
    for s, v in zip(scores, values):
        p = jnp.exp2(s - m)
        den = den + p.sum(axis=1, keepdims=True)
        pv = jnp.dot(p.astype(BF16), v, preferred_element_type=F32)
        acc = pv if acc is None else acc + pv
    return acc / den


def _window_kernel(sink_ref, q_ref, kp_ref, kc_ref, kn_ref, vp_ref, vc_ref, vn_ref, kx_ref, vx_ref, o_ref, *, nb):
    n = pl.program_id(1)
    groups = A_HEADS // A_KV_HEADS
    d = 64
    kb = jnp.concatenate([kp_ref[...], kc_ref[...], kn_ref[...]], axis=0)
    vb = jnp.concatenate([vp_ref[...], vc_ref[...], vn_ref[...]], axis=0)
    Q = groups * BLOCK
    qi = lax.broadcasted_iota(jnp.int32, (Q, 3 * BLOCK), 0) % BLOCK
    kj = lax.broadcasted_iota(jnp.int32, (Q, 3 * BLOCK), 1)
    k_abs = (n - 1) * BLOCK + kj
    valid = (jnp.abs(kj - BLOCK - qi) <= WINDOW) & (k_abs >= 0) & (k_abs < nb * BLOCK) & (n < nb)
    hrow = lax.broadcasted_iota(jnp.int32, (Q, 1), 0) // BLOCK
    outs = []
    for g in range(A_KV_HEADS):
        qg = jnp.concatenate([q_ref[:, (g * groups + h) * d:(g * groups + h + 1) * d] for h in range(groups)], axis=0)
        ksl = slice(g * d, (g + 1) * d)
        s_band = jnp.where(valid, _nt(qg, kb[:, ksl]), NEG)
        s_ctx = _nt(qg, kx_ref[:, ksl])
        sink = jnp.zeros((Q, 1), F32)
        for h in range(groups):
            sink = jnp.where(hrow == h, sink_ref[g * groups + h] * LOG2E, sink)
        o = _softmax_av([s_band, s_ctx], [vb[:, ksl], vx_ref[:, ksl]], sink)
        outs += [o[h * BLOCK:(h + 1) * BLOCK] for h in range(groups)]
    o_ref[...] = jnp.concatenate(outs, axis=1).astype(o_ref.dtype)


def window_attn(rows, qkv, sink):
    B, S, NC = rows.B, rows.S, rows.NC
    nb, ncb = S // BLOCK, NC // BLOCK
    cb = B * S // NC

    def q_block(b, n):
        return jnp.where(n < nb, b * nb + n, B * nb + b * ncb + (n - nb))

    def kv_spec(col, off):
        return pl.BlockSpec((BLOCK, LANE), lambda b, n, s: (b * nb + jnp.clip(n + off, 0, nb - 1), col))

    in_specs = [pl.BlockSpec((BLOCK, 512), lambda b, n, s: (q_block(b, n), 0)),
                kv_spec(4, -1), kv_spec(4, 0), kv_spec(4, 1), kv_spec(5, -1), kv_spec(5, 0), kv_spec(5, 1),
                pl.BlockSpec((NC, LANE), lambda b, n, s: (cb + b, 4)),
                pl.BlockSpec((NC, LANE), lambda b, n, s: (cb + b, 5))]
    return pl.pallas_call(
        functools.partial(_window_kernel, nb=nb),
        grid_spec=pltpu.PrefetchScalarGridSpec(
            num_scalar_prefetch=1, grid=(B, nb + ncb), in_specs=in_specs,
            out_specs=pl.BlockSpec((BLOCK, 512), lambda b, n, s: (q_block(b, n), 0))),
        out_shape=jax.ShapeDtypeStruct((rows.T, 512), BF16),
        compiler_params=_cp(("parallel", "parallel")), name="window_attn",
    )(sink, *([qkv] * 9))


MLA_SCALE = (MLA_NOPE + MLA_ROPE) ** -0.5 * LOG2E


def _mla_q_epilogue(acc, cos_ref, sin_ref):
    cos, sin = cos_ref[...], sin_ref[...]
    slabs = [acc[:, h * LANE:(h + 1) * LANE] for h in range(MLA_HEADS)]
    return jnp.concatenate([(s * cos + _rot_pairs(s, 8) * sin) * MLA_SCALE for s in slabs], axis=1)


def _mla_k_epilogue(acc, kr_ref, cos_ref, sin_ref):
    kr = kr_ref[...]
    kr = kr * cos_ref[...] + _rot_pairs(kr, 8) * sin_ref[...]
    return jnp.concatenate([acc[:, h * LANE:(h + 1) * LANE] + kr for h in range(MLA_HEADS)], axis=1)


def _mla_kernel(q_ref, kl_ref, kx_ref, vl_ref, vx_ref, o_ref, *, nq):
    def run(with_latent_keys):
        outs = []
        for j in range(MLA_HP):
            sl, vs = slice(j * LANE, (j + 1) * LANE), slice(j * 64, (j + 1) * 64)
            q = q_ref[:, sl]
            scores, values = [_nt(q, kx_ref[:, sl])], [vx_ref[:, vs]]
            if with_latent_keys:
                scores.insert(0, _nt(q, kl_ref[:, sl]))
                values.insert(0, vl_ref[:, vs])
            outs.append(_softmax_av(scores, values))
        o_ref[...] = jnp.concatenate(outs, axis=1).astype(o_ref.dtype)

    i = pl.program_id(2)
    pl.when(i < nq)(lambda: run(True))
    pl.when(i >= nq)(lambda: run(False))


def mla_attn(rows, qm, km, vm, tq=256):
    B, S, NC = rows.B, rows.S, rows.NC
    cb = B * S // NC
    nq, nqc = S // tq, NC // tq
    kw, vw = MLA_HP * LANE, MLA_HP * 64

    def q_block(b, i):
        return jnp.where(i < nq, b * nq + i, B * nq + b * nqc + (i - nq))

    return pl.pallas_call(
        functools.partial(_mla_kernel, nq=nq), grid=(B, MLA_HEADS // MLA_HP, nq + nqc),
        in_specs=[pl.BlockSpec((tq, kw), lambda b, h, i: (q_block(b, i), h)),
                  pl.BlockSpec((S, kw), lambda b, h, i: (b, h)),
                  pl.BlockSpec((NC, kw), lambda b, h, i: (cb + b, h)),
                  pl.BlockSpec((S, vw), lambda b, h, i: (b, h)),
                  pl.BlockSpec((NC, vw), lambda b, h, i: (cb + b, h))],
        out_specs=pl.BlockSpec((tq, vw), lambda b, h, i: (q_block(b, i), h)),
        out_shape=jax.ShapeDtypeStruct((rows.T, MLA_HEADS * 64), BF16),
        compiler_params=_cp(("parallel", "parallel", "parallel")), name="mla_attn",
    )(qm, km, km, vm, vm)


def _mla_norms_kernel(x0_ref, x1_ref, x2_ref, x3_ref, gq_ref, gkv_ref, q_ref, kv_ref, kr_ref):
    x = jnp.concatenate([x0_ref[...], x1_ref[...], x2_ref[...], x3_ref[...]], axis=1)
    q_ref[...] = _rms(x[:, 16:528], gq_ref[...]).astype(q_ref.dtype)
    kv_ref[...] = _rms(x[:, 528:784], gkv_ref[...]).astype(kv_ref.dtype)
    z = lambda w: jnp.zeros((x.shape[0], w), F32)
    kr_ref[...] = jnp.concatenate([z(MLA_NOPE), x[:, 784:816], z(LANE - MLA_NOPE - MLA_ROPE)], axis=1)


def mla_norms(rows, p2, gq, gkv):
    T = rows.T
    return pl.pallas_call(
        _mla_norms_kernel, grid=(rows.nt,),
        in_specs=[pl.BlockSpec((ROW_TILE, 256), functools.partial(lambda j, i: (i, PB_GATES // 256 + j), j))
                  for j in range(4)] + [_vec_spec(512), _vec_spec(256)],
        out_specs=[_row_spec(512), _row_spec(256), _row_spec(LANE)],
        out_shape=[jax.ShapeDtypeStruct((T, 512), BF16), jax.ShapeDtypeStruct((T, 256), BF16),
                   jax.ShapeDtypeStruct((T, LANE), F32)],
        compiler_params=_cp(("parallel",)), name="mla_norms",
    )(p2, p2, p2, p2, gq, gkv)


def _log_sigmoid(x):
    return jnp.minimum(x, 0.0) - jnp.log(1.0 + jnp.exp(-jnp.abs(x)))


def _mlstm_kernel(*refs, scale):
    nh = ML_HEADS
    per = 3 * nh + 1
    gb_ref, of_ref, ob_ref, c_ref, n_ref, m_ref = refs[2 * per:]

    @pl.when(pl.program_id(1) == 0)
    def _():
        c_ref[...] = jnp.zeros_like(c_ref)
        n_ref[...] = jnp.zeros_like(n_ref)
        m_ref[...] = jnp.zeros_like(m_ref)

    for d, o_ref in enumerate((of_ref, ob_ref)):
        r = refs[d * per:(d + 1) * per]
        _mlstm_chunk(r[0:nh], r[nh:2 * nh], r[2 * nh:3 * nh], r[3 * nh], gb_ref, o_ref,
                     c_ref.at[d], n_ref.at[d], m_ref.at[d], backward=d == 1, scale=scale)


def _mlstm_chunk(q_refs, k_refs, v_refs, g_ref, gb_ref, o_ref, c_ref, n_ref, m_ref, *, backward, scale):
    nh, L = ML_HEADS, ML_CHUNK
    t_i = lax.broadcasted_iota(jnp.int32, (L, L), 0)
    s_i = lax.broadcasted_iota(jnp.int32, (L, L), 1)
    mask = (s_i >= t_i) if backward else (s_i <= t_i)
    maskf = mask.astype(F32)
    gb = g_ref[...] + gb_ref[...]
    ls = _log_sigmoid(gb)
    gbt, lst = gb.T, ls.T
    cum_c = jnp.dot(maskf, ls, preferred_element_type=F32, precision=HI)
    cum_r = jnp.dot(lst, maskf.T, preferred_element_type=F32, precision=HI)
    last = 0 if backward else L - 1
    gi0, gf0 = (2 * nh, 3 * nh) if backward else (0, nh)
    for h in range(nh):
        gi, gf = gi0 + h, gf0 + h
        q = q_refs[h][...].astype(F32)
        kf = k_refs[h][...].astype(F32) * scale
        vf = v_refs[h][...].astype(F32)
        qb, kb = q.astype(BF16), kf.astype(BF16)
        cumc, cumr = cum_c[:, gf:gf + 1], cum_r[gf:gf + 1, :]
        li_r, li_c = gbt[gi:gi + 1, :], gb[:, gi:gi + 1]
        m11 = m_ref[h][0:1, 0:1]
        log_intra = jnp.where(mask, cumc - cumr + li_r, NEG)
        log_inter = cumc + m11
        m_t = jnp.maximum(log_inter, log_intra.max(axis=1, keepdims=True))
        w_inter = jnp.exp(log_inter - m_t)
        s = _nt(qb, kb) * jnp.exp(log_intra - m_t)
        cmat = c_ref[h]
        num = w_inter * _nt(qb, cmat.astype(BF16)) + jnp.dot(s.astype(BF16), vf.astype(BF16),
                                                             preferred_element_type=F32)
        den = w_inter * jnp.sum(q * n_ref[h][0:1, :], axis=1, keepdims=True) + s.sum(axis=1, keepdims=True)
        o_ref[:, h * L:(h + 1) * L] = num / jnp.maximum(jnp.abs(den), jnp.exp(-m_t))
        total = cumc[last:last + 1, :]
        log_w = total - cumc + li_c
        m_new = jnp.maximum(total + m11, log_w.max(axis=0, keepdims=True))
        decay = jnp.exp(total + m11 - m_new)
        w = jnp.exp(log_w - m_new)
        upd = lax.dot_general((vf * w).astype(BF16), kb, (((0,), (0,)), ((), ())), preferred_element_type=F32)
        c_ref[h] = decay * cmat + upd
        n_ref[h] = jnp.broadcast_to(decay * n_ref[h][0:1, :] + jnp.sum(w * kf, axis=0, keepdims=True), (8, L))
        m_ref[h] = jnp.broadcast_to(m_new, (8, LANE))


def mlstm_scan(rows, pa, pb, gbias):
    B, S, NC = rows.B, rows.S, rows.NC
    nh, L = ML_HEADS, ML_CHUNK
    ncx, ncl = NC // L, S // L
    first_ctx = B * S // L

    def row_block(backward, b, j):
        cx = (ncx - 1 - j) if backward else j
        cl = (ncl - 1 - (j - ncx)) if backward else (j - ncx)
        return jnp.where(j < ncx, first_ctx + b * ncx + cx, b * ncl + cl)

    def col_spec(backward, col):
        return pl.BlockSpec((L, LANE), lambda b, j: (row_block(backward, b, j), col))

    q0 = 2304 // LANE
    in_specs = []
    for backward in (False, True):
        in_specs += [col_spec(backward, q0 + c) for c in range(3 * nh)] + [col_spec(backward, PB_GATES // LANE)]
    in_specs.append(pl.BlockSpec((1, LANE), lambda b, j: (0, 0)))
    out = jax.ShapeDtypeStruct((rows.T, nh * L), F32)
    return pl.pallas_call(
        functools.partial(_mlstm_kernel, scale=L ** -0.5),
        grid=(B, ncx + ncl), in_specs=in_specs,
        out_specs=[pl.BlockSpec((L, nh * L), functools.partial(lambda bw, b, j: (row_block(bw, b, j), 0), bw))
                   for bw in (False, True)],
        out_shape=[out, out],
        scratch_shapes=[pltpu.VMEM((2, nh, L, L), F32), pltpu.VMEM((2, nh, 8, L), F32),
                        pltpu.VMEM((2, nh, 8, LANE), F32)],
        compiler_params=_cp(("parallel", "arbitrary")), name="mlstm_scan",
    )(*(([pa] * (3 * nh) + [pb]) * 2), gbias)


def _mlstm_finish_kernel(hf_ref, hb_ref, *refs):
    o_refs, g_ref, out_ref = refs[:ML_HEADS], refs[ML_HEADS], refs[ML_HEADS + 1]
    for h in range(ML_HEADS):
        sl = slice(h * ML_CHUNK, (h + 1) * ML_CHUNK)
        hn = _rms(hf_ref[:, sl] + hb_ref[:, sl], g_ref[:, sl])
        out_ref[:, sl] = (jax.nn.sigmoid(o_refs[h][...]) * hn).astype(out_ref.dtype)


def mlstm_finish(rows, hf, hb, p1, gnorm):
    W = ML_HEADS * ML_CHUNK
    o_specs = [pl.BlockSpec((ROW_TILE, LANE), functools.partial(lambda h, i: (i, h), h))
               for h in range(ML_HEADS)]
    return pl.pallas_call(
        _mlstm_finish_kernel, grid=(rows.nt,),
        in_specs=[_row_spec(W), _row_spec(W)] + o_specs + [_vec_spec(W)],
        out_specs=_row_spec(W),
        out_shape=jax.ShapeDtypeStruct((rows.T, W), BF16),
        compiler_params=_cp(("parallel",)), name="mlstm_finish",
    )(hf, hb, *([p1] * ML_HEADS), gnorm)


def _dft_gen_kernel(ca_ref, sa_ref, cb_ref, sb_ref, ca2_ref, sa2_ref, cb2_ref, sb2_ref, fwd_ref, inv_ref):
    tb, L = cb_ref.shape
    ca, sa, cb, sb = ca_ref[...], sa_ref[...], cb_ref[...], sb_ref[...]
    fwd_ref[0] = (ca * cb - sa * sb).astype(fwd_ref.dtype)
    fwd_ref[1] = (-(sa * cb + ca * sb)).astype(fwd_ref.dtype)
    ca, sa, cb, sb = ca2_ref[...], sa2_ref[...], cb2_ref[...], sb2_ref[...]
    c, s = (ca * cb - sa * sb).astype(inv_ref.dtype), (-(sa * cb + ca * sb)).astype(inv_ref.dtype)
    for j in range(L // tb):
        inv_ref[:, 2 * j * tb:(2 * j + 1) * tb] = c[:, j * tb:(j + 1) * tb]
        inv_ref[:, (2 * j + 1) * tb:(2 * j + 2) * tb] = s[:, j * tb:(j + 1) * tb]


def _dft_matrices(L, tb):
    nb = L // tb
    col = jnp.arange(L, dtype=jnp.int32)[None, :]
    r0 = (jnp.arange(nb, dtype=jnp.int32) * tb)[:, None]
    i = jnp.arange(tb, dtype=jnp.int32)[:, None]
    unit = math.pi / (2 * L)
    trig = lambda ph: (jnp.cos((ph % (4 * L)).astype(F32) * unit), jnp.sin((ph % (4 * L)).astype(F32) * unit))
    ca, sa = trig((2 * r0 + 1) * col)
    cb, sb = trig(2 * i * col)
    ca2, sa2 = trig((2 * col + 1) * r0)
    cb2, sb2 = trig((2 * col + 1) * i)
    blk = lambda: pl.BlockSpec((None, 1, L), lambda j: (j, 0, 0))
    shared = lambda: pl.BlockSpec((tb, L), lambda j: (0, 0))
    a3 = lambda a: a[:, None, :]
    fwd, inv = pl.pallas_call(
        _dft_gen_kernel, grid=(nb,),
        in_specs=[blk(), blk(), shared(), shared(), blk(), blk(), shared(), shared()],
        out_specs=[pl.BlockSpec((None, 2, tb, L), lambda j: (j, 0, 0, 0)),
                   pl.BlockSpec((tb, 2 * L), lambda j: (j, 0))],
        out_shape=[jax.ShapeDtypeStruct((nb, 2, tb, L), BF16), jax.ShapeDtypeStruct((L, 2 * L), BF16)],
        compiler_params=_cp(("parallel",)), name="hy_dft_gen",
    )(a3(ca), a3(sa), cb, sb, a3(ca2), a3(sa2), cb2, sb2)
    return fwd.reshape(2 * L, L), inv


def _filter_feats(L):
    t = jnp.arange(L, dtype=F32) / L
    kf = jnp.arange(1, HY_POS_FREQS + 1, dtype=F32)
    ang = 2.0 * math.pi * t[:, None] * kf
    feats = jnp.concatenate([t[:, None], jnp.sin(ang), jnp.cos(ang)], axis=-1)
    return jnp.pad(feats, ((0, 0), (0, LANE - feats.shape[1])))


CG = 256


def _short_conv_eo_kernel(p_ref, w_ref, o_ref, scr_ref):
    x = p_ref[...].astype(F32)
    L = x.shape[0]
    row = lax.broadcasted_iota(jnp.int32, x.shape, 0)
    prev = jnp.where(row == 0, 0.0, pltpu.roll(x, 1, 0))
    nxt = jnp.where(row == L - 1, 0.0, pltpu.roll(x, L - 1, 0))
    u = w_ref[0:1, :] * prev + w_ref[1:2, :] * x + w_ref[2:3, :] * nxt
    for j in range(CG // LANE):
        scr_ref[j] = u[:, j * LANE:(j + 1) * LANE]
        for par in range(2):
            o_ref[:, par * CG + j * LANE:par * CG + (j + 1) * LANE] = scr_ref[j, pl.ds(par, L // 2, stride=2), :]


def short_conv_eo(pa, short_w, L, row_block0, B):
    ng = 512 // CG
    return pl.pallas_call(
        _short_conv_eo_kernel, grid=(B, 3, ng),
        in_specs=[pl.BlockSpec((L, CG), lambda b, w, g: (row_block0 + b, 768 // CG + w * ng + g)),
                  pl.BlockSpec((HY_SHORT, CG), lambda b, w, g: (0, w * ng + g))],
        out_specs=pl.BlockSpec((None, L // 2, 2 * CG), lambda b, w, g: (w, 0, b * ng + g)),
        out_shape=jax.ShapeDtypeStruct((3, L // 2, B * 1024), F32),
        scratch_shapes=[pltpu.VMEM((CG // LANE, L, LANE), F32)],
        compiler_params=_cp(("parallel", "parallel", "parallel")), name="hy_short_conv",
    )(pa, short_w)


def _filter_eo_kernel(feat_ref, w1_ref, b1_ref, w2_ref, b2_ref, fr_ref, w3f_ref, w3b_ref, dl_ref, o_ref, hid_ref,
                      sf_ref, sb_ref):
    L = feat_ref.shape[0]
    H = L // 2

    @pl.when((pl.program_id(0) == 0) & (pl.program_id(1) == 0))
    def _():
        fr = fr_ref[...]
        h1 = jnp.sin(fr * (jnp.dot(feat_ref[...], w1_ref[...], preferred_element_type=F32, precision=HI)
                           + b1_ref[...]))
        hid_ref[...] = jnp.sin(fr * (jnp.dot(h1, w2_ref[...], preferred_element_type=F32, precision=HI)
                                     + b2_ref[...]))

    hid = hid_ref[...]
    t = lax.broadcasted_iota(jnp.int32, (L, 1), 0).astype(F32) / L
    dec = jnp.exp(-t * dl_ref[...])
    hf = jnp.dot(hid, w3f_ref[...], preferred_element_type=F32, precision=HI) * dec
    hb = jnp.dot(hid, w3b_ref[...], preferred_element_type=F32, precision=HI) * dec
    hb = jnp.where(lax.broadcasted_iota(jnp.int32, hb.shape, 0) == 0, 0.0, hb)
    inv = 1.0 / (jnp.sum(jnp.abs(hf), axis=0, keepdims=True) + jnp.sum(jnp.abs(hb), axis=0, keepdims=True))
    for part, (s_ref, hv) in enumerate(((sf_ref, hf * inv), (sb_ref, hb * inv))):
        for j in range(CG // LANE):
            s_ref[j] = hv[:, j * LANE:(j + 1) * LANE]
            for par in range(2):
                c0 = (part * 2 + par) * CG + j * LANE
                o_ref[:, c0:c0 + LANE] = s_ref[j, pl.ds(par, H, stride=2), :].astype(o_ref.dtype)


def hyena_filters_eo(feats, w1p, b1, w2, b2, fr, w3, deltas, L):
    hid = w2.shape[0]
    ng = 512 // CG
    full = lambda shape: pl.BlockSpec(shape, lambda o, g: (0, 0))
    return pl.pallas_call(
        _filter_eo_kernel, grid=(HY_ORDER, ng),
        in_specs=[full((L, LANE)), full((LANE, hid)), full((1, hid)), full((hid, hid)), full((1, hid)),
                  full((1, hid)),
                  pl.BlockSpec((hid, CG), lambda o, g: (0, (o * 2) * ng + g)),
                  pl.BlockSpec((hid, CG), lambda o, g: (0, (o * 2 + 1) * ng + g)),
                  pl.BlockSpec((1, CG), lambda o, g: (0, g))],
        out_specs=pl.BlockSpec((None, L // 2, 4 * CG), lambda o, g: (0, 0, o * ng + g)),
        out_shape=jax.ShapeDtypeStruct((1, L // 2, HY_ORDER * ng * 4 * CG), BF16),
        scratch_shapes=[pltpu.VMEM((L, hid), F32), pltpu.VMEM((CG // LANE, L, LANE), F32),
                        pltpu.VMEM((CG // LANE, L, LANE), F32)],
        compiler_params=_cp(("arbitrary", "arbitrary")), name="hy_filters",
    )(feats, w1p, b1, w2, b2, fr, w3, w3, deltas)


def _split_eo(acc, c_ref, s_ref):
    tb = acc.shape[0] // 2
    c, s = c_ref[...], s_ref[...]
    ere, eim, ore, oim = acc[:tb, :CG], acc[tb:, :CG], acc[:tb, CG:], acc[tb:, CG:]
    tre, tim = c * ore + s * oim, c * oim - s * ore
    return (ere + tre, eim + tim), (ere - tre, tim - eim)


def _filter_spec_epilogue(acc, c_ref, s_ref):
    res, ims = [], []
    for j in range(acc.shape[1] // (2 * CG)):
        (lre, lim), (hre, him) = _split_eo(acc[:, j * 2 * CG:(j + 1) * 2 * CG], c_ref, s_ref)
        res += [lre, hre]
        ims += [lim, him]
    return jnp.concatenate([jnp.concatenate(res, axis=1), jnp.concatenate(ims, axis=1)], axis=0)


def _cmul(a, b):
    return a[0] * b[0] - a[1] * b[1], a[0] * b[1] + a[1] * b[0]


def _signal_spec_epilogue(acc, c_ref, s_ref, gre_ref, gim_ref, *, scale):
    c, s = c_ref[...], s_ref[...]
    res, ims = [], []
    for g in range(acc.shape[1] // (2 * CG)):
        x_lo, x_hi = _split_eo(acc[:, g * 2 * CG:(g + 1) * 2 * CG], c_ref, s_ref)
        f = lambda ref, j: ref[:, (g * 4 + j) * CG:(g * 4 + j + 1) * CG]
        g_lo = (f(gre_ref, 0) + f(gre_ref, 2), f(gim_ref, 0) - f(gim_ref, 2))
        g_hi = (f(gre_ref, 1) + f(gre_ref, 3), f(gim_ref, 1) - f(gim_ref, 3))
        p_lo, p_hi = _cmul(x_lo, g_lo), _cmul(x_hi, g_hi)
        dre, dim = p_lo[0] - p_hi[0], p_lo[1] + p_hi[1]
        res += [p_lo[0] + p_hi[0], dre * c - dim * s]
        ims += [p_lo[1] - p_hi[1], dre * s + dim * c]
    return jnp.concatenate([jnp.concatenate(res, axis=1), jnp.concatenate(ims, axis=1)], axis=0) * scale


def _gated(acc, x_ref, z_ref, b_ref):
    return x_ref[...] * (acc + b_ref[...] * z_ref[...])


def _gate_eo_kernel(x_ref, c_ref, z_ref, b_ref, *refs, nl):
    o_ref, scr_ref = refs[-2:]
    i = pl.program_id(0)

    @pl.when(i < nl)
    def _():
        r = x_ref[...] * (c_ref[...] + b_ref[...] * z_ref[...])
        half = r.shape[0]
        for g in range(512 // CG):
            for j in range(CG // LANE):
                slab = g * (CG // LANE) + j
                for par in range(2):
                    c0 = (g * 2 + par) * CG + j * LANE
                    scr_ref[slab, pl.ds(par, half, stride=2), :] = r[:, c0:c0 + LANE]
                o_ref[:, slab * LANE:(slab + 1) * LANE] = scr_ref[slab].astype(o_ref.dtype)

    @pl.when(i >= nl)
    def _():
        o_ref[...] = jnp.zeros_like(o_ref)


def hyena_gate_eo(u3, conv, z1, bias_u, L, B, tl, T, row0, into):
    nl = L // tl
    th = tl // 2
    ci = lambda i: jnp.minimum(i, nl - 1)
    in_specs = [pl.BlockSpec((None, th, 1024), lambda i, b: (2, ci(i), b)),
                pl.BlockSpec((th, 1024), lambda i, b: (ci(i), b)),
                pl.BlockSpec((None, th, 1024), lambda i, b: (0, ci(i), b)),
                pl.BlockSpec((None, 1, 1024), lambda i, b: (HY_ORDER - 1, 0, 0))]
    args = [u3, conv, z1, bias_u]
    if into is not None:
        n_fill, aliases = 0, {4: 0}
        in_specs.append(pl.BlockSpec(memory_space=pl.ANY))
        args.append(into)
        out_spec = pl.BlockSpec((tl, 512), lambda i, b: (row0 // tl + b * nl + i, 0))
    else:
        assert row0 == 0
        n_fill, aliases = (T - B * L) // tl, {}
        out_spec = pl.BlockSpec((tl, 512), lambda i, b: (jnp.where(i < nl, b * nl + i, B * nl + (i - nl)), 0))
    return pl.pallas_call(
        functools.partial(_gate_eo_kernel, nl=nl), grid=(nl + n_fill, B), in_specs=in_specs, out_specs=out_spec,
        out_shape=jax.ShapeDtypeStruct((T, 512), BF16), input_output_aliases=aliases,
        scratch_shapes=[pltpu.VMEM((512 // LANE, tl, LANE), F32)],
        compiler_params=_cp(("arbitrary", "arbitrary")), name="hy_gate",
    )(*args)


def _hyena_consts(L):
    H = L // 2
    tb = min(H, ROW_TILE)
    fwd, inv = _dft_matrices(H, tb)
    w = (2 * jnp.arange(H, dtype=F32)[:, None] + 1.0) * (math.pi / (2 * L))
    twc, tws = jnp.broadcast_to(jnp.cos(w), (H, CG)), jnp.broadcast_to(jnp.sin(w), (H, CG))
    return fwd, inv, twc, tws, _filter_feats(L)


def hyena_seq_eo(pa, L, row_block0, B, consts, wts, T, into=None):
    fwd, inv, twc, tws, feats = consts
    short_w, w1p, b1, w2, b2, fr, w3, deltas, bias_u = wts
    H = L // 2
    tb = min(H, ROW_TILE)
    nkb = H // tb
    tm_i = min(H, 512)
    ng = 512 // CG
    NW = B * 1024
    u3 = short_conv_eo(pa, short_w, L, row_block0, B)
    filt = hyena_filters_eo(feats, w1p, b1, w2, b2, fr, w3, deltas, L)
    zeros_f = jnp.zeros((nkb,), jnp.int32)
    zeros_i = jnp.zeros((H // tm_i,), jnp.int32)
    tw = [(twc, (tb, CG), lambda n, m: (m, 0)), (tws, (tb, CG), lambda n, m: (m, 0))]
    gw = ng * 4 * CG
    gs = gmm(fwd, filt, zeros_f, tm=2 * tb, tn=gw, n_tiles=HY_ORDER, out_dtype=F32,
             epilogue=_filter_spec_epilogue, extra=tw, name="hy_dft_filter")
    z_arr, which_z = u3, 0
    out = None
    for o in range(HY_ORDER):
        g_blk = lambda comp, o=o: (gs, (tb, gw), lambda n, m: (2 * m + comp, o))
        y = gmm(fwd, z_arr, zeros_f + which_z, tm=2 * tb, tn=ng * 2 * CG, n_tiles=B, out_dtype=BF16,
                epilogue=functools.partial(_signal_spec_epilogue, scale=1.0 / L),
                extra=tw + [g_blk(0), g_blk(1)], name="hy_dft_fwd")
        y = y.reshape(1, 2 * H, NW)
        if o < HY_ORDER - 1:
            out = gmm(inv, y, zeros_i, tm=tm_i, tn=2 * CG, n_tiles=B * ng, out_dtype=F32, epilogue=_gated,
                      extra=[(u3, (None, tm_i, 2 * CG), functools.partial(lambda o, n, m: (1 + o, m, n), o)),
                             (z_arr, (None, tm_i, 2 * CG), functools.partial(lambda w, n, m: (w, m, n), which_z)),
                             (bias_u, (None, 1, 2 * CG), functools.partial(lambda o, n, m: (o, 0, n % ng), o))],
                      name="hy_dft_inv_gate").reshape(1, H, NW)
        else:
            conv = gmm(inv, y, zeros_i, tm=tm_i, tn=2 * CG, n_tiles=B * ng, out_dtype=F32, name="hy_dft_inv")
            out = hyena_gate_eo(u3, conv, z_arr, bias_u, L, B, min(L, 512), T, row_block0 * L, into)
        z_arr, which_z = out, 0
    return out


def _gather_kernel(idx_ref, src_ref, *refs, tm, k):
    if k > 1:
        w_ref, o_ref, buf_ref, sem = refs
    else:
        o_ref, buf_ref, sem = refs
    i, n = pl.program_id(0), pl.num_programs(0)
    M = n * tm

    def issue_tile(tile, slot):
        for j in range(k):
            def issue(r, carry):
                pltpu.make_async_copy(src_ref.at[pl.ds(idx_ref[j * M + tile * tm + r], 1)],
                                      buf_ref.at[slot, j, pl.ds(r, 1)], sem.at[slot]).start()
                return carry

            lax.fori_loop(0, tm, issue, 0, unroll=8)

    @pl.when(i == 0)
    def _():
        issue_tile(0, 0)

    @pl.when(i + 1 < n)
    def _():
        issue_tile(i + 1, (i + 1) % 2)

    slot = i % 2
    for j in range(k):
        pltpu.make_async_copy(src_ref.at[pl.ds(0, tm)], buf_ref.at[slot, j], sem.at[slot]).wait()
    if k > 1:
        w = w_ref[...]
        acc = w[:, 0:1] * buf_ref[slot, 0]
        for j in range(1, k):
            acc = acc + w[:, j:j + 1] * buf_ref[slot, j]
        o_ref[...] = acc.astype(o_ref.dtype)
    else:
        o_ref[...] = buf_ref[slot, 0].astype(o_ref.dtype)


def gather_rows(src, idx, tm, out_dtype, weights=None, k=1):
    M = idx.shape[0] // k
    D = src.shape[1]
    in_specs = [pl.BlockSpec(memory_space=pl.ANY)]
    args = [src]
    if k > 1:
        in_specs.append(pl.BlockSpec((tm, LANE), lambda i, idx: (i, 0)))
        args.append(weights)
    return pl.pallas_call(
        functools.partial(_gather_kernel, tm=tm, k=k),
        grid_spec=pltpu.PrefetchScalarGridSpec(
            num_scalar_prefetch=1, grid=(M // tm,), in_specs=in_specs,
            out_specs=pl.BlockSpec((tm, D), lambda i, idx: (i, 0)),
            scratch_shapes=[pltpu.VMEM((2, k, tm, D), src.dtype), pltpu.SemaphoreType.DMA((2,))]),
        out_shape=jax.ShapeDtypeStruct((M, D), out_dtype),
        compiler_params=_cp(("arbitrary",)), name="gather_rows" if k == 1 else "gather_combine",
    )(idx, *args)


def moe_ffn(rows, h_f32, rw, ri, w_gate, w_up, w_down, layer_moe, tm=512):
    T, D = rows.T, rows.D
    E = w_gate.shape[2]
    e_flat = ri[:, :TOP_K].T.reshape(-1)
    onehot = (e_flat[:, None] == jnp.arange(N_EXPERTS, dtype=jnp.int32)[None, :]).astype(jnp.int32)
    csum = jnp.cumsum(onehot, axis=0)
    rank = jnp.sum(csum * onehot, axis=1) - 1
    counts = csum[-1]
    padded = ((counts + tm - 1) // tm) * tm
    ends = jnp.cumsum(padded)
    starts = ends - padded
    dest = starts[e_flat] + rank
    m_pad = TOP_K * T + N_EXPERTS * tm
    token = jnp.tile(jnp.arange(T, dtype=jnp.int32), TOP_K)
    src_tok = jnp.zeros((m_pad,), jnp.int32).at[dest].set(token)
    tile_row = jnp.arange(m_pad // tm, dtype=jnp.int32) * tm
    gid = jnp.minimum(jnp.sum((tile_row[:, None] >= ends[None, :]).astype(jnp.int32), axis=1), N_EXPERTS - 1)
    gid = gid + layer_moe * N_EXPERTS
    used = (ends[-1:] // tm).astype(jnp.int32)

    xs = gather_rows(h_f32, src_tok, tm, BF16)
    tile = lambda n, m: (m, n)
    g = gmm(xs, w_gate, gid, tm=tm, tn=E // 2, n_tiles=2, out_dtype=BF16, epilogue=_silu, used=used,
            name="moe_gate")
    hh = gmm(xs, w_up, gid, tm=tm, tn=E // 2, n_tiles=2, out_dtype=BF16, epilogue=_times,
             extra=[(g, (tm, E // 2), tile)], used=used, name="moe_up")
    y = gmm(hh, w_down, gid, tm=tm, tn=D // 2, n_tiles=2, out_dtype=F32, used=used, name="moe_down")
    return gather_rows(y, dest, ROW_TILE, BF16, weights=rw, k=TOP_K)


def kernel(x, c, ctx, c_ctx, w_mod, b_mod, g_mix_pre, g_mix_post, g_ffn_pre, g_ffn_post, w_in, w_out, attn_sink,
           hy_short, hy_w1, hy_b1, hy_w2, hy_b2, hy_freq, hy_w3, hy_bias, ml_gate_bias, ml_norm, mla_q_norm,
           mla_w_uq, mla_kv_norm, mla_w_ukv, ffn_w_gate, ffn_w_up, ffn_w_down, moe_router, moe_w_gate, moe_w_up,
           moe_w_down):
    B, S, D = x.shape
    NC = ctx.shape[1]
    depth = w_mod.shape[0]
    rows = _Rows(B, S, NC, D)
    T = rows.T
    TM = 512 if T % 512 == 0 else ROW_TILE
    n_mt = T // TM
    ffn_dim = ffn_w_gate.shape[2]

    xs = jnp.concatenate([x.reshape(B * S, D), ctx.reshape(B * NC, D)], axis=0)

    cm = jnp.concatenate([c, c_ctx[None, :]], axis=0)
    cm = jnp.pad(jax.nn.silu(cm), ((0, 16 - (B + 1)), (0, 0))).astype(BF16)
    mod_all = gmm(jnp.tile(cm, (depth, 1)), w_mod, jnp.arange(depth, dtype=jnp.int32), tm=16, tn=1536,
                  n_tiles=N_MOD * D // 1536, out_dtype=F32, name="adaln")
    mods = (mod_all.reshape(depth, 16, N_MOD * D)[:, :B + 1] + b_mod[:, None, :]).reshape(depth, B + 1, N_MOD, 1, D)

    def rope_table(rot_dim, reps, lane0, width):
        cos, sin = _axial_tables(S, rot_dim)
        cos, sin = jnp.tile(cos, (1, reps)), jnp.tile(sin, (1, reps))
        padw = ((0, 0), (lane0, width - lane0 - cos.shape[1]))
        cos = jnp.pad(cos, padw, constant_values=1.0)
        sin = jnp.pad(sin, padw)
        cos = jnp.concatenate([jnp.tile(cos, (B, 1)), jnp.ones((B * NC, width), F32)], axis=0)
        sin = jnp.concatenate([jnp.tile(sin, (B, 1)), jnp.zeros((B * NC, width), F32)], axis=0)
        return cos, sin

    cos_a, sin_a = rope_table(64, 2, 0, LANE)
    cos_m, sin_m = rope_table(MLA_ROPE, 1, MLA_NOPE, LANE)
    hy_consts = {L: _hyena_consts(L) for L in (S, NC)}
    deltas = jnp.abs(jnp.linspace(math.log(HY_DECAY_TARGET) / HY_SLOW_DECAY,
                                  math.log(HY_DECAY_TARGET) / HY_FAST_DECAY, 512, dtype=F32))[None, :]

    w_in_t = jnp.swapaxes(w_in, 1, 2)
    wq = mla_w_uq.reshape(depth, -1, MLA_HEADS, MLA_NOPE + MLA_ROPE)
    wq = jnp.pad(wq, ((0, 0), (0, 0), (0, 0), (0, LANE - MLA_NOPE - MLA_ROPE))).reshape(depth, -1, MLA_HEADS * LANE)
    wkv = mla_w_ukv.reshape(depth, -1, MLA_HEADS, 2 * 64)
    wk = jnp.pad(wkv[..., :MLA_NOPE], ((0, 0), (0, 0), (0, 0), (0, LANE - MLA_NOPE)))
    wk = wk.reshape(depth, -1, MLA_HEADS * LANE)
    wv = wkv[..., MLA_NOPE:].reshape(depth, -1, MLA_HEADS * 64)
    w1p = jnp.pad(hy_w1, ((0, 0), (0, LANE - hy_w1.shape[1]), (0, 0)))
    bias_u = jnp.broadcast_to(hy_bias.reshape(depth, HY_ORDER, 512 // CG, 1, CG),
                              (depth, HY_ORDER, 512 // CG, 2, CG)).reshape(depth, HY_ORDER, 1, 1024)
    gbias = jnp.pad(ml_gate_bias.reshape(depth, 1, 4 * ML_HEADS), ((0, 0), (0, 0), (0, LANE - 4 * ML_HEADS)))
    router = jnp.pad(moe_router, ((0, 0), (0, 0), (0, LANE - N_EXPERTS)))
    n_moe = moe_w_gate.shape[0]
    mw_gate = moe_w_gate.reshape(n_moe * N_EXPERTS, D, -1)
    mw_up = moe_w_up.reshape(n_moe * N_EXPERTS, D, -1)
    mw_down = moe_w_down.reshape(n_moe * N_EXPERTS, -1, D)
    vec = lambda a, l: a[l][None, :]

    h = norm_mod(rows, xs, vec(g_mix_pre, 0), mods[0], 0, 1)
    for layer in range(depth):
        gid = jnp.full((n_mt,), layer, jnp.int32)
        ml = mods[layer]
        p1 = gmm(h, w_in_t, gid, tm=TM, tn=1280, n_tiles=PA_COLS // 1280, out_dtype=BF16, w_t=True, name="proj_in")
        p2 = gmm(h, w_in_t, gid, tm=TM, tn=768, n_tiles=2, n_off=PA_COLS // 768, out_dtype=F32, w_t=True,
                 name="proj_in_f32")

        qkv = attn_rope(rows, p1, cos_a, sin_a)
        a_all = window_attn(rows, qkv, attn_sink[layer])

        hy_w = (hy_short[layer], w1p[layer], vec(hy_b1, layer), hy_w2[layer], vec(hy_b2, layer),
                vec(hy_freq, layer), hy_w3[layer], deltas, bias_u[layer])
        b_l = hyena_seq_eo(p1, S, 0, B, hy_consts[S], hy_w, T)
        b_all = hyena_seq_eo(p1, NC, B * S // NC, B, hy_consts[NC], hy_w, T, into=b_l)

        hf, hb = mlstm_scan(rows, p1, p2, gbias[layer])
        m_all = mlstm_finish(rows, hf, hb, p2, vec(ml_norm, layer))

        qn, kvn, kr = mla_norms(rows, p2, vec(mla_q_norm, layer), vec(mla_kv_norm, layer))
        slab = lambda a: (a, (TM, LANE), lambda n, m: (m, 0))
        qm = gmm(qn, wq, gid, tm=TM, tn=MLA_HEADS * LANE, n_tiles=1, out_dtype=BF16, epilogue=_mla_q_epilogue,
                 extra=[slab(cos_m), slab(sin_m)], name="mla_uq")
        km = gmm(kvn, wk, gid, tm=TM, tn=MLA_HEADS * LANE, n_tiles=1, out_dtype=BF16, epilogue=_mla_k_epilogue,
                 extra=[slab(kr), slab(cos_m), slab(sin_m)], name="mla_uk")
        vm = gmm(kvn, wv, gid, tm=TM, tn=MLA_HEADS * 64, n_tiles=1, out_dtype=BF16, name="mla_uv")
        d_all = mla_attn(rows, qm, km, vm)

        y = gmm([a_all, b_all, m_all, d_all], w_out, gid, tm=TM, tn=1024, n_tiles=D // 1024, out_dtype=BF16,
                name="proj_out")

        i = layer // 2
        nxt = (vec(g_ffn_pre, layer), ml, 3, 4)
        if layer % 2 == 0:
            xs, h2 = post(rows, xs, y, vec(g_mix_post, layer), ml, 2, nxt=nxt)
            gi = jnp.full((n_mt,), i, jnp.int32)
            g = gmm(h2, ffn_w_gate, gi, tm=TM, tn=ffn_dim // 2, n_tiles=2, out_dtype=BF16, epilogue=_silu,
                    w_single_buffer=True, name="ffn_gate")
            hh = gmm(h2, ffn_w_up, gi, tm=TM, tn=ffn_dim // 2, n_tiles=2, out_dtype=BF16, epilogue=_times,
                     extra=[(g, (TM, ffn_dim // 2), lambda n, m: (m, n))], w_single_buffer=True, name="ffn_up")
            y2 = gmm(hh, ffn_w_down, gi, tm=TM, tn=512, n_tiles=D // 512, out_dtype=BF16, name="ffn_down")
        else:
            xs, h2, h2f, rw, ri = post(rows, xs, y, vec(g_mix_post, layer), ml, 2, nxt=nxt, router=router[i])
            y2 = moe_ffn(rows, h2f, rw, ri, mw_gate, mw_up, mw_down, i)
        if layer + 1 < depth:
            xs, h = post(rows, xs, y2, vec(g_ffn_post, layer), ml, 5,
                         nxt=(vec(g_mix_pre, layer + 1), mods[layer + 1], 0, 1))
        else:
            (xs,) = post(rows, xs, y2, vec(g_ffn_post, layer), ml, 5, latent_only=True)
    return xs.reshape(B, S, D)
```

```python
import functools
import math

import numpy as np
import jax
import jax.numpy as jnp
from jax import lax
from jax.experimental import pallas as pl
from jax.experimental.pallas import tpu as pltpu

F32 = jnp.float32
BF16 = jnp.bfloat16
HI = lax.Precision.HIGHEST

EPS = 1e-6
ROPE_BASE = 10000.0
GRID_W = 64
BLOCK = 128
WINDOW = 128
N_MOD = 6
A_HEADS, A_KV_HEADS = 8, 2
HY_ORDER, HY_SHORT, HY_POS_FREQS = 2, 3, 8
HY_DECAY_TARGET, HY_FAST_DECAY, HY_SLOW_DECAY = 1e-2, 0.3, 1.5
ML_HEADS, ML_CHUNK = 4, 128
MLA_HEADS, MLA_NOPE, MLA_ROPE = 8, 64, 32
N_EXPERTS, TOP_K = 8, 2

PA_COLS = 3840
PB_GATES = 4352 - PA_COLS
MLA_HP = 8
LANE = 128
ROW_TILE = 256
NEG = -1e30
LOG2E = math.log2(math.e)
VMEM_LIMIT = 56 * 1024 * 1024


def _cp(sem, vmem=VMEM_LIMIT):
    return pltpu.CompilerParams(dimension_semantics=sem, vmem_limit_bytes=vmem)


def _gmm_kernel(gid_ref, used_ref, *refs, n_a, n_w, cast, w_t, epilogue, n_extra, w_n, tn, n_off):
    a_refs, w_refs, rest = refs[:n_a], refs[n_a:n_a + n_w], refs[n_a + n_w:]
    extra_refs, rest = rest[:n_extra], rest[n_extra:]
    o_ref = rest[0]
    n, m = pl.program_id(0), pl.program_id(1)
    n_axis = 0 if w_t else 1
    if cast:
        wbf_refs = rest[1:1 + n_w]
        prev = gid_ref[jnp.maximum(m - 1, 0)]

        @pl.when((m == 0) | (gid_ref[m] != prev))
        def _():
            for w_ref, wbf_ref in zip(w_refs, wbf_refs):
                w = w_ref[...]
                if w_n is not None:
                    col = (n + n_off) * tn + lax.broadcasted_iota(jnp.int32, w.shape, n_axis)
                    w = jnp.where(col < w_n, w, 0.0)
                wbf_ref[...] = w.astype(BF16)

        srcs = wbf_refs
    else:
        srcs = w_refs

    @pl.when(m < used_ref[0])
    def _():
        accs = []
        for src in srcs:
            acc, lo = None, 0
            for a_ref in a_refs:
                k = a_ref.shape[1]
                if w_t:
                    part = lax.dot_general(a_ref[...], src[:, lo:lo + k], (((1,), (1,)), ((), ())),
                                           preferred_element_type=F32)
                else:
                    part = jnp.dot(a_ref[...], src[lo:lo + k, :], preferred_element_type=F32)
                acc = part if acc is None else acc + part
                lo += k
            accs.append(acc)
        acc = epilogue(*accs, *extra_refs) if epilogue is not None else accs[0]
        o_ref[...] = acc.astype(o_ref.dtype)

    @pl.when(m >= used_ref[0])
    def _():
        o_ref[...] = jnp.zeros_like(o_ref)


def _silu(acc):
    return acc * jax.nn.sigmoid(acc)


def _times(acc, g_ref):
    return acc * g_ref[...].astype(F32)


def _swiglu(gate_acc, up_acc):
    return gate_acc * jax.nn.sigmoid(gate_acc) * up_acc


def gmm(a, w, gid, *, tm, tn, n_tiles, out_dtype, n_off=0, epilogue=None, extra=(), w_t=False, used=None,
        w_single_buffer=False, name="gmm"):
    a_list = list(a) if isinstance(a, (list, tuple)) else [a]
    w_list = list(w) if isinstance(w, (list, tuple)) else [w]
    w = w_list[0]
    M = a_list[0].shape[0]
    k_axis, n_axis = (2, 1) if w_t else (1, 2)
    a_cols = [w.shape[k_axis]] if len(a_list) == 1 else [p.shape[1] for p in a_list]
    K = sum(a_cols)
    mt = M // tm
    assert mt * tm == M and w.shape[k_axis] == K and a_list[0].shape[1] >= a_cols[0] and gid.shape == (mt,)
    cast = w.dtype != BF16
    partial_n = (n_off + n_tiles) * tn > w.shape[n_axis]
    assert cast or not partial_n
    if used is None:
        used = jnp.full((1,), mt, jnp.int32)
    in_specs = [pl.BlockSpec((tm, k), lambda n, m, g, u: (m, 0)) for k in a_cols]
    w_mode = dict(pipeline_mode=pl.Buffered(1)) if w_single_buffer else {}
    for _ in w_list:
        if w_t:
            in_specs.append(pl.BlockSpec((None, tn, K), lambda n, m, g, u: (g[m], n + n_off, 0), **w_mode))
        else:
            in_specs.append(pl.BlockSpec((None, K, tn), lambda n, m, g, u: (g[m], 0, n + n_off), **w_mode))
    args = a_list + w_list
    for arr, block, imap in extra:
        in_specs.append(pl.BlockSpec(block, functools.partial(lambda f, n, m, g, u: f(n, m), imap)))
        args.append(arr)
    kern = functools.partial(_gmm_kernel, n_a=len(a_list), n_w=len(w_list), cast=cast, w_t=w_t, epilogue=epilogue,
                             n_extra=len(extra), w_n=w.shape[n_axis] if partial_n else None, tn=tn, n_off=n_off)
    return pl.pallas_call(
        kern,
        grid_spec=pltpu.PrefetchScalarGridSpec(
            num_scalar_prefetch=2, grid=(n_tiles, mt), in_specs=in_specs,
            out_specs=pl.BlockSpec((tm, tn), lambda n, m, g, u: (m, n)),
            scratch_shapes=[pltpu.VMEM((tn, K) if w_t else (K, tn), BF16)] * len(w_list) if cast else []),
        out_shape=jax.ShapeDtypeStruct((M, n_tiles * tn), out_dtype),
        compiler_params=_cp(("arbitrary", "arbitrary")),
        name=name,
    )(gid, used, *args)


def _rms(v, g):
    return v * lax.rsqrt(jnp.mean(v * v, axis=-1, keepdims=True) + EPS) * g


def _norm_mod_kernel(x_ref, g_ref, sh_ref, sc_ref, h_ref):
    h_ref[...] = (_rms(x_ref[...], g_ref[...]) * (1.0 + sc_ref[...]) + sh_ref[...]).astype(h_ref.dtype)


def _post_kernel(*refs, with_next, with_router):
    x_ref, y_ref, gp_ref, gate_ref = refs[:4]
    refs = refs[4:]
    xn = x_ref[...] + gate_ref[...] * _rms(y_ref[...].astype(F32), gp_ref[...])
    if not with_next:
        refs[0][...] = xn
        return
    gn_ref, sh_ref, sc_ref = refs[:3]
    refs = refs[3:]
    if with_router:
        r_ref, refs = refs[0], refs[1:]
    refs[0][...] = xn
    h = _rms(xn, gn_ref[...]) * (1.0 + sc_ref[...]) + sh_ref[...]
    refs[1][...] = h.astype(BF16)
    if with_router:
        hf_ref, rw_ref, ri_ref = refs[2:5]
        hf_ref[...] = h
        logits = jnp.dot(h, r_ref[...], preferred_element_type=F32, precision=HI)
        lane = lax.broadcasted_iota(jnp.int32, logits.shape, 1)
        logits = jnp.where(lane < N_EXPERTS, logits, -jnp.inf)
        v1 = jnp.max(logits, axis=1, keepdims=True)
        i1 = jnp.min(jnp.where(logits == v1, lane, LANE), axis=1, keepdims=True)
        l2 = jnp.where(lane == i1, -jnp.inf, logits)
        v2 = jnp.max(l2, axis=1, keepdims=True)
        i2 = jnp.min(jnp.where(l2 == v2, lane, LANE), axis=1, keepdims=True)
        e = jnp.exp(v2 - v1)
        w1 = 1.0 / (1.0 + e)
        w2 = e / (1.0 + e)
        rw_ref[...] = jnp.where(lane == 0, w1, jnp.where(lane == 1, w2, 0.0))
        ri_ref[...] = jnp.where(lane == 0, i1, jnp.where(lane == 1, i2, 0))


class _Rows:
    def __init__(self, B, S, NC, D):
        self.B, self.S, self.NC, self.D = B, S, NC, D
        self.T = B * S + B * NC
        assert S % ROW_TILE == 0 and NC % ROW_TILE == 0
        self.nt = self.T // ROW_TILE

    def mod_row(self, i):
        n_lat = self.B * self.S // ROW_TILE
        return jnp.where(i < n_lat, i // (self.S // ROW_TILE), self.B)


def _row_spec(D):
    return pl.BlockSpec((ROW_TILE, D), lambda i: (i, 0))


def _vec_spec(D):
    return pl.BlockSpec((1, D), lambda i: (0, 0))


def _mod_spec(rows, j):
    return pl.BlockSpec((None, None, 1, rows.D), lambda i: (rows.mod_row(i), j, 0, 0))


def norm_mod(rows, x, g, mods, j_shift, j_scale):
    D = rows.D
    return pl.pallas_call(
        _norm_mod_kernel, grid=(rows.nt,),
        in_specs=[_row_spec(D), _vec_spec(D), _mod_spec(rows, j_shift), _mod_spec(rows, j_scale)],
        out_specs=_row_spec(D),
        out_shape=jax.ShapeDtypeStruct((rows.T, D), BF16),
        compiler_params=_cp(("parallel",)), name="norm_mod",
    )(x, g, mods, mods)


def post(rows, x, y, g_post, mods, j_gate, nxt=None, router=None, latent_only=False):
    D = rows.D
    T = rows.B * rows.S if latent_only else rows.T
    in_specs = [_row_spec(D), _row_spec(D), _vec_spec(D), _mod_spec(rows, j_gate)]
    args = [x, y, g_post, mods]
    out_specs = [_row_spec(D)]
    out_shape = [jax.ShapeDtypeStruct((T, D), F32)]
    if nxt is not None:
        g_next, mods_next, j_shift, j_scale = nxt
        in_specs += [_vec_spec(D), _mod_spec(rows, j_shift), _mod_spec(rows, j_scale)]
        args += [g_next, mods_next, mods_next]
        out_specs.append(_row_spec(D))
        out_shape.append(jax.ShapeDtypeStruct((T, D), BF16))
        if router is not None:
            in_specs.append(pl.BlockSpec((D, LANE), lambda i: (0, 0)))
            args.append(router)
            out_specs += [_row_spec(D), _row_spec(LANE), _row_spec(LANE)]
            out_shape += [jax.ShapeDtypeStruct((T, D), F32), jax.ShapeDtypeStruct((T, LANE), F32),
                          jax.ShapeDtypeStruct((T, LANE), jnp.int32)]
    kern = functools.partial(_post_kernel, with_next=nxt is not None, with_router=router is not None)
    return pl.pallas_call(
        kern, grid=(T // ROW_TILE,), in_specs=in_specs, out_specs=out_specs, out_shape=out_shape,
        compiler_params=_cp(("parallel",)), name="post",
    )(*args)


def _rot_pairs(x, q):
    lane = lax.broadcasted_iota(jnp.int32, x.shape, 1)
    n = x.shape[1]
    return jnp.where(lane % (2 * q) < q, -pltpu.roll(x, n - q, 1), pltpu.roll(x, q, 1))


def _axial_tables(S, rot_dim):
    rows = S // GRID_W
    r = jnp.repeat(jnp.arange(rows, dtype=F32), GRID_W)
    col = jnp.tile(jnp.arange(GRID_W, dtype=F32), rows)
    half = rot_dim // 2
    inv = ROPE_BASE ** (-jnp.arange(0, half, 2, dtype=F32) / half)
    ar, ac = r[:, None] * inv, col[:, None] * inv
    ang = jnp.concatenate([ar, ar, ac, ac], axis=-1)
    return jnp.cos(ang), jnp.sin(ang)


def _attn_rope_kernel(p_ref, cos_ref, sin_ref, o_ref, *, scale):
    cos, sin = cos_ref[...], sin_ref[...]
    nq = A_HEADS * 64 // LANE
    for s in range(nq + 1):
        xs = p_ref[:, s * LANE:(s + 1) * LANE].astype(F32)
        r = xs * cos + _rot_pairs(xs, 16) * sin
        if s < nq:
            r = r * scale
        o_ref[:, s * LANE:(s + 1) * LANE] = r.astype(o_ref.dtype)
    o_ref[:, (nq + 1) * LANE:] = p_ref[:, (nq + 1) * LANE:].astype(o_ref.dtype)


def attn_rope(rows, p1, cos, sin):
    W = 768
    return pl.pallas_call(
        functools.partial(_attn_rope_kernel, scale=64 ** -0.5 * LOG2E), grid=(rows.nt,),
        in_specs=[pl.BlockSpec((ROW_TILE, W), lambda i: (i, 0)), _row_spec(LANE), _row_spec(LANE)],
        out_specs=pl.BlockSpec((ROW_TILE, W), lambda i: (i, 0)),
        out_shape=jax.ShapeDtypeStruct((rows.T, W), BF16),
        compiler_params=_cp(("parallel",)), name="attn_rope",
    )(p1, cos, sin)


def _nt(a, b):
    return lax.dot_general(a, b, (((1,), (1,)), ((), ())), preferred_element_type=F32)


def _softmax_av(scores, values, sink=None):
    m = scores[0].max(axis=1, keepdims=True)
    for s in scores[1:]:
        m = jnp.maximum(m, s.max(axis=1, keepdims=True))
    if sink is not None:
        m = jnp.maximum(m, sink)
    den = jnp.exp2(sink - m) if sink is not None else 0.0
    acc = None
    for s, v in zip(scores, values):
        p = jnp.exp2(s - m)
        den = den + p.sum(axis=1, keepdims=True)
        pv = jnp.dot(p.astype(BF16), v, preferred_element_type=F32)
        acc = pv if acc is None else acc + pv
    return acc / den


def _window_kernel(sink_ref, q_ref, kp_ref, kc_ref, kn_ref, vp_ref, vc_ref, vn_ref, kx_ref, vx_ref, o_ref, *, nb):
    n = pl.program_id(1)
    groups = A_HEADS // A_KV_HEADS
    d = 64
    kb = jnp.concatenate([kp_ref[...], kc_ref[...], kn_ref[...]], axis=0)
    vb = jnp.concatenate([vp_ref[...], vc_ref[...], vn_ref[...]], axis=0)
    Q = groups * BLOCK
    qi = lax.broadcasted_iota(jnp.int32, (Q, 3 * BLOCK), 0) % BLOCK
    kj = lax.broadcasted_iota(jnp.int32, (Q, 3 * BLOCK), 1)
    k_abs = (n - 1) * BLOCK + kj
    valid = (jnp.abs(kj - BLOCK - qi) <= WINDOW) & (k_abs >= 0) & (k_abs < nb * BLOCK) & (n < nb)
    hrow = lax.broadcasted_iota(jnp.int32, (Q, 1), 0) // BLOCK
    outs = []
    for g in range(A_KV_HEADS):
        qg = jnp.concatenate([q_ref[:, (g * groups + h) * d:(g * groups + h + 1) * d] for h in range(groups)], axis=0)
        ksl = slice(g * d, (g + 1) * d)
        s_band = jnp.where(valid, _nt(qg, kb[:, ksl]), NEG)
        s_ctx = _nt(qg, kx_ref[:, ksl])
        sink = jnp.zeros((Q, 1), F32)
        for h in range(groups):
            sink = jnp.where(hrow == h, sink_ref[g * groups + h] * LOG2E, sink)
        o = _softmax_av([s_band, s_ctx], [vb[:, ksl], vx_ref[:, ksl]], sink)
        outs += [o[h * BLOCK:(h + 1) * BLOCK] for h in range(groups)]
    o_ref[...] = jnp.concatenate(outs, axis=1).astype(o_ref.dtype)


def window_attn(rows, qkv, sink):
    B, S, NC = rows.B, rows.S, rows.NC
    nb, ncb = S // BLOCK, NC // BLOCK
    cb = B * S // NC

    def q_block(b, n):
        return jnp.where(n < nb, b * nb + n, B * nb + b * ncb + (n - nb))

    def kv_spec(col, off):
        return pl.BlockSpec((BLOCK, LANE), lambda b, n, s: (b * nb + jnp.clip(n + off, 0, nb - 1), col))

    in_specs = [pl.BlockSpec((BLOCK, 512), lambda b, n, s: (q_block(b, n), 0)),
                kv_spec(4, -1), kv_spec(4, 0), kv_spec(4, 1), kv_spec(5, -1), kv_spec(5, 0), kv_spec(5, 1),
                pl.BlockSpec((NC, LANE), lambda b, n, s: (cb + b, 4)),
                pl.BlockSpec((NC, LANE), lambda b, n, s: (cb + b, 5))]
    return pl.pallas_call(
        functools.partial(_window_kernel, nb=nb),
        grid_spec=pltpu.PrefetchScalarGridSpec(
            num_scalar_prefetch=1, grid=(B, nb + ncb), in_specs=in_specs,
            out_specs=pl.BlockSpec((BLOCK, 512), lambda b, n, s: (q_block(b, n), 0))),
        out_shape=jax.ShapeDtypeStruct((rows.T, 512), BF16),
        compiler_params=_cp(("parallel", "parallel")), name="window_attn",
    )(sink, *([qkv] * 9))


MLA_SCALE = (MLA_NOPE + MLA_ROPE) ** -0.5 * LOG2E


def _mla_q_epilogue(acc, cos_ref, sin_ref):
    cos, sin = cos_ref[...], sin_ref[...]
    slabs = [acc[:, h * LANE:(h + 1) * LANE] for h in range(MLA_HEADS)]
    return jnp.concatenate([(s * cos + _rot_pairs(s, 8) * sin) * MLA_SCALE for s in slabs], axis=1)


def _mla_k_epilogue(acc, kr_ref, cos_ref, sin_ref):
    kr = kr_ref[...]
    kr = kr * cos_ref[...] + _rot_pairs(kr, 8) * sin_ref[...]
    return jnp.concatenate([acc[:, h * LANE:(h + 1) * LANE] + kr for h in range(MLA_HEADS)], axis=1)


def _mla_kernel(q_ref, kl_ref, kx_ref, vl_ref, vx_ref, o_ref, *, nq):
    def run(with_latent_keys):
        outs = []
        for j in range(MLA_HP):
            sl, vs = slice(j * LANE, (j + 1) * LANE), slice(j * 64, (j + 1) * 64)
            q = q_ref[:, sl]
            scores, values = [_nt(q, kx_ref[:, sl])], [vx_ref[:, vs]]
            if with_latent_keys:
                scores.insert(0, _nt(q, kl_ref[:, sl]))
                values.insert(0, vl_ref[:, vs])
            outs.append(_softmax_av(scores, values))
        o_ref[...] = jnp.concatenate(outs, axis=1).astype(o_ref.dtype)

    i = pl.program_id(2)
    pl.when(i < nq)(lambda: run(True))
    pl.when(i >= nq)(lambda: run(False))


def mla_attn(rows, qm, km, vm, tq=256):
    B, S, NC = rows.B, rows.S, rows.NC
    cb = B * S // NC
    nq, nqc = S // tq, NC // tq
    kw, vw = MLA_HP * LANE, MLA_HP * 64

    def q_block(b, i):
        return jnp.where(i < nq, b * nq + i, B * nq + b * nqc + (i - nq))

    return pl.pallas_call(
        functools.partial(_mla_kernel, nq=nq), grid=(B, MLA_HEADS // MLA_HP, nq + nqc),
        in_specs=[pl.BlockSpec((tq, kw), lambda b, h, i: (q_block(b, i), h)),
                  pl.BlockSpec((S, kw), lambda b, h, i: (b, h)),
                  pl.BlockSpec((NC, kw), lambda b, h, i: (cb + b, h)),
                  pl.BlockSpec((S, vw), lambda b, h, i: (b, h)),
                  pl.BlockSpec((NC, vw), lambda b, h, i: (cb + b, h))],
        out_specs=pl.BlockSpec((tq, vw), lambda b, h, i: (q_block(b, i), h)),
        out_shape=jax.ShapeDtypeStruct((rows.T, MLA_HEADS * 64), BF16),
        compiler_params=_cp(("parallel", "parallel", "parallel")), name="mla_attn",
    )(qm, km, km, vm, vm)


def _mla_norms_kernel(x0_ref, x1_ref, x2_ref, x3_ref, gq_ref, gkv_ref, q_ref, kv_ref, kr_ref):
    x = jnp.concatenate([x0_ref[...], x1_ref[...], x2_ref[...], x3_ref[...]], axis=1)
    q_ref[...] = _rms(x[:, 16:528], gq_ref[...]).astype(q_ref.dtype)
    kv_ref[...] = _rms(x[:, 528:784], gkv_ref[...]).astype(kv_ref.dtype)
    z = lambda w: jnp.zeros((x.shape[0], w), F32)
    kr_ref[...] = jnp.concatenate([z(MLA_NOPE), x[:, 784:816], z(LANE - MLA_NOPE - MLA_ROPE)], axis=1)


def mla_norms(rows, p2, gq, gkv):
    T = rows.T
    return pl.pallas_call(
        _mla_norms_kernel, grid=(rows.nt,),
        in_specs=[pl.BlockSpec((ROW_TILE, 256), functools.partial(lambda j, i: (i, PB_GATES // 256 + j), j))
                  for j in range(4)] + [_vec_spec(512), _vec_spec(256)],
        out_specs=[_row_spec(512), _row_spec(256), _row_spec(LANE)],
        out_shape=[jax.ShapeDtypeStruct((T, 512), BF16), jax.ShapeDtypeStruct((T, 256), BF16),
                   jax.ShapeDtypeStruct((T, LANE), F32)],
        compiler_params=_cp(("parallel",)), name="mla_norms",
    )(p2, p2, p2, p2, gq, gkv)


def _log_sigmoid(x):
    return jnp.minimum(x, 0.0) - jnp.log(1.0 + jnp.exp(-jnp.abs(x)))


def _mlstm_kernel(*refs, scale):
    nh = ML_HEADS
    per = 3 * nh + 1
    gb_ref, of_ref, ob_ref, c_ref, n_ref, m_ref = refs[2 * per:]

    @pl.when(pl.program_id(1) == 0)
    def _():
        c_ref[...] = jnp.zeros_like(c_ref)
        n_ref[...] = jnp.zeros_like(n_ref)
        m_ref[...] = jnp.zeros_like(m_ref)

    for d, o_ref in enumerate((of_ref, ob_ref)):
        r = refs[d * per:(d + 1) * per]
        _mlstm_chunk(r[0:nh], r[nh:2 * nh], r[2 * nh:3 * nh], r[3 * nh], gb_ref, o_ref,
                     c_ref.at[d], n_ref.at[d], m_ref.at[d], backward=d == 1, scale=scale)


def _mlstm_chunk(q_refs, k_refs, v_refs, g_ref, gb_ref, o_ref, c_ref, n_ref, m_ref, *, backward, scale):
    nh, L = ML_HEADS, ML_CHUNK
    t_i = lax.broadcasted_iota(jnp.int32, (L, L), 0)
    s_i = lax.broadcasted_iota(jnp.int32, (L, L), 1)
    mask = (s_i >= t_i) if backward else (s_i <= t_i)
    maskf = mask.astype(F32)
    gb = g_ref[...] + gb_ref[...]
    ls = _log_sigmoid(gb)
    gbt, lst = gb.T, ls.T
    cum_c = jnp.dot(maskf, ls, preferred_element_type=F32, precision=HI)
    cum_r = jnp.dot(lst, maskf.T, preferred_element_type=F32, precision=HI)
    last = 0 if backward else L - 1
    gi0, gf0 = (2 * nh, 3 * nh) if backward else (0, nh)
    for h in range(nh):
        gi, gf = gi0 + h, gf0 + h
        q = q_refs[h][...].astype(F32)
        kf = k_refs[h][...].astype(F32) * scale
        vf = v_refs[h][...].astype(F32)
        qb, kb = q.astype(BF16), kf.astype(BF16)
        cumc, cumr = cum_c[:, gf:gf + 1], cum_r[gf:gf + 1, :]
        li_r, li_c = gbt[gi:gi + 1, :], gb[:, gi:gi + 1]
        m11 = m_ref[h][0:1, 0:1]
        log_intra = jnp.where(mask, cumc - cumr + li_r, NEG)
        log_inter = cumc + m11
        m_t = jnp.maximum(log_inter, log_intra.max(axis=1, keepdims=True))
        w_inter = jnp.exp(log_inter - m_t)
        s = _nt(qb, kb) * jnp.exp(log_intra - m_t)
        cmat = c_ref[h]
        num = w_inter * _nt(qb, cmat.astype(BF16)) + jnp.dot(s.astype(BF16), vf.astype(BF16),
                                                             preferred_element_type=F32)
        den = w_inter * jnp.sum(q * n_ref[h][0:1, :], axis=1, keepdims=True) + s.sum(axis=1, keepdims=True)
        o_ref[:, h * L:(h + 1) * L] = num / jnp.maximum(jnp.abs(den), jnp.exp(-m_t))
        total = cumc[last:last + 1, :]
        log_w = total - cumc + li_c
        m_new = jnp.maximum(total + m11, log_w.max(axis=0, keepdims=True))
        decay = jnp.exp(total + m11 - m_new)
        w = jnp.exp(log_w - m_new)
        upd = lax.dot_general((vf * w).astype(BF16), kb, (((0,), (0,)), ((), ())), preferred_element_type=F32)
        c_ref[h] = decay * cmat + upd
        n_ref[h] = jnp.broadcast_to(decay * n_ref[h][0:1, :] + jnp.sum(w * kf, axis=0, keepdims=True), (8, L))
        m_ref[h] = jnp.broadcast_to(m_new, (8, LANE))


def mlstm_scan(rows, pa, pb, gbias):
    B, S, NC = rows.B, rows.S, rows.NC
    nh, L = ML_HEADS, ML_CHUNK
    ncx, ncl = NC // L, S // L
    first_ctx = B * S // L

    def row_block(backward, b, j):
        cx = (ncx - 1 - j) if backward else j
        cl = (ncl - 1 - (j - ncx)) if backward else (j - ncx)
        return jnp.where(j < ncx, first_ctx + b * ncx + cx, b * ncl + cl)

    def col_spec(backward, col):
        return pl.BlockSpec((L, LANE), lambda b, j: (row_block(backward, b, j), col))

    q0 = 2304 // LANE
    in_specs = []
    for backward in (False, True):
        in_specs += [col_spec(backward, q0 + c) for c in range(3 * nh)] + [col_spec(backward, PB_GATES // LANE)]
    in_specs.append(pl.BlockSpec((1, LANE), lambda b, j: (0, 0)))
    out = jax.ShapeDtypeStruct((rows.T, nh * L), F32)
    return pl.pallas_call(
        functools.partial(_mlstm_kernel, scale=L ** -0.5),
        grid=(B, ncx + ncl), in_specs=in_specs,
        out_specs=[pl.BlockSpec((L, nh * L), functools.partial(lambda bw, b, j: (row_block(bw, b, j), 0), bw))
                   for bw in (False, True)],
        out_shape=[out, out],
        scratch_shapes=[pltpu.VMEM((2, nh, L, L), F32), pltpu.VMEM((2, nh, 8, L), F32),
                        pltpu.VMEM((2, nh, 8, LANE), F32)],
        compiler_params=_cp(("parallel", "arbitrary")), name="mlstm_scan",
    )(*(([pa] * (3 * nh) + [pb]) * 2), gbias)


def _mlstm_finish_kernel(hf_ref, hb_ref, *refs):
    o_refs, g_ref, out_ref = refs[:ML_HEADS], refs[ML_HEADS], refs[ML_HEADS + 1]
    for h in range(ML_HEADS):
        sl = slice(h * ML_CHUNK, (h + 1) * ML_CHUNK)
        hn = _rms(hf_ref[:, sl] + hb_ref[:, sl], g_ref[:, sl])
        out_ref[:, sl] = (jax.nn.sigmoid(o_refs[h][...]) * hn).astype(out_ref.dtype)


def mlstm_finish(rows, hf, hb, p1, gnorm):
    W = ML_HEADS * ML_CHUNK
    o_specs = [pl.BlockSpec((ROW_TILE, LANE), functools.partial(lambda h, i: (i, h), h))
               for h in range(ML_HEADS)]
    return pl.pallas_call(
        _mlstm_finish_kernel, grid=(rows.nt,),
        in_specs=[_row_spec(W), _row_spec(W)] + o_specs + [_vec_spec(W)],
        out_specs=_row_spec(W),
        out_shape=jax.ShapeDtypeStruct((rows.T, W), BF16),
        compiler_params=_cp(("parallel",)), name="mlstm_finish",
    )(hf, hb, *([p1] * ML_HEADS), gnorm)


def _dft_gen_kernel(ca_ref, sa_ref, cb_ref, sb_ref, ca2_ref, sa2_ref, cb2_ref, sb2_ref, fwd_ref, inv_ref):
    tb, L = cb_ref.shape
    ca, sa, cb, sb = ca_ref[...], sa_ref[...], cb_ref[...], sb_ref[...]
    fwd_ref[0] = (ca * cb - sa * sb).astype(fwd_ref.dtype)
    fwd_ref[1] = (-(sa * cb + ca * sb)).astype(fwd_ref.dtype)
    ca, sa, cb, sb = ca2_ref[...], sa2_ref[...], cb2_ref[...], sb2_ref[...]
    c, s = (ca * cb - sa * sb).astype(inv_ref.dtype), (-(sa * cb + ca * sb)).astype(inv_ref.dtype)
    for j in range(L // tb):
        inv_ref[:, 2 * j * tb:(2 * j + 1) * tb] = c[:, j * tb:(j + 1) * tb]
        inv_ref[:, (2 * j + 1) * tb:(2 * j + 2) * tb] = s[:, j * tb:(j + 1) * tb]


def _dft_matrices(L, tb):
    nb = L // tb
    col = jnp.arange(L, dtype=jnp.int32)[None, :]
    r0 = (jnp.arange(nb, dtype=jnp.int32) * tb)[:, None]
    i = jnp.arange(tb, dtype=jnp.int32)[:, None]
    unit = math.pi / (2 * L)
    trig = lambda ph: (jnp.cos((ph % (4 * L)).astype(F32) * unit), jnp.sin((ph % (4 * L)).astype(F32) * unit))
    ca, sa = trig((2 * r0 + 1) * col)
    cb, sb = trig(2 * i * col)
    ca2, sa2 = trig((2 * col + 1) * r0)
    cb2, sb2 = trig((2 * col + 1) * i)
    blk = lambda: pl.BlockSpec((None, 1, L), lambda j: (j, 0, 0))
    shared = lambda: pl.BlockSpec((tb, L), lambda j: (0, 0))
    a3 = lambda a: a[:, None, :]
    fwd, inv = pl.pallas_call(
        _dft_gen_kernel, grid=(nb,),
        in_specs=[blk(), blk(), shared(), shared(), blk(), blk(), shared(), shared()],
        out_specs=[pl.BlockSpec((None, 2, tb, L), lambda j: (j, 0, 0, 0)),
                   pl.BlockSpec((tb, 2 * L), lambda j: (j, 0))],
        out_shape=[jax.ShapeDtypeStruct((nb, 2, tb, L), BF16), jax.ShapeDtypeStruct((L, 2 * L), BF16)],
        compiler_params=_cp(("parallel",)), name="hy_dft_gen",
    )(a3(ca), a3(sa), cb, sb, a3(ca2), a3(sa2), cb2, sb2)
    return fwd.reshape(2 * L, L), inv


def _filter_feats(L):
    t = jnp.arange(L, dtype=F32) / L
    kf = jnp.arange(1, HY_POS_FREQS + 1, dtype=F32)
    ang = 2.0 * math.pi * t[:, None] * kf
    feats = jnp.concatenate([t[:, None], jnp.sin(ang), jnp.cos(ang)], axis=-1)
    return jnp.pad(feats, ((0, 0), (0, LANE - feats.shape[1])))


CG = 256


def _short_conv_eo_kernel(p_ref, w_ref, o_ref, scr_ref):
    x = p_ref[...].astype(F32)
    L = x.shape[0]
    row = lax.broadcasted_iota(jnp.int32, x.shape, 0)
    prev = jnp.where(row == 0, 0.0, pltpu.roll(x, 1, 0))
    nxt = jnp.where(row == L - 1, 0.0, pltpu.roll(x, L - 1, 0))
    u = w_ref[0:1, :] * prev + w_ref[1:2, :] * x + w_ref[2:3, :] * nxt
    for j in range(CG // LANE):
        scr_ref[j] = u[:, j * LANE:(j + 1) * LANE]
        for par in range(2):
            o_ref[:, par * CG + j * LANE:par * CG + (j + 1) * LANE] = scr_ref[j, pl.ds(par, L // 2, stride=2), :]


def short_conv_eo(pa, short_w, L, row_block0, B):
    ng = 512 // CG
    return pl.pallas_call(
        _short_conv_eo_kernel, grid=(B, 3, ng),
        in_specs=[pl.BlockSpec((L, CG), lambda b, w, g: (row_block0 + b, 768 // CG + w * ng + g)),
                  pl.BlockSpec((HY_SHORT, CG), lambda b, w, g: (0, w * ng + g))],
        out_specs=pl.BlockSpec((None, L // 2, 2 * CG), lambda b, w, g: (w, 0, b * ng + g)),
        out_shape=jax.ShapeDtypeStruct((3, L // 2, B * 1024), F32),
        scratch_shapes=[pltpu.VMEM((CG // LANE, L, LANE), F32)],
        compiler_params=_cp(("parallel", "parallel", "parallel")), name="hy_short_conv",
    )(pa, short_w)


def _filter_eo_kernel(feat_ref, w1_ref, b1_ref, w2_ref, b2_ref, fr_ref, w3f_ref, w3b_ref, dl_ref, o_ref, hid_ref,
                      sf_ref, sb_ref):
    L = feat_ref.shape[0]
    H = L // 2

    @pl.when((pl.program_id(0) == 0) & (pl.program_id(1) == 0))
    def _():
        fr = fr_ref[...]
        h1 = jnp.sin(fr * (jnp.dot(feat_ref[...], w1_ref[...], preferred_element_type=F32, precision=HI)
                           + b1_ref[...]))
        hid_ref[...] = jnp.sin(fr * (jnp.dot(h1, w2_ref[...], preferred_element_type=F32, precision=HI)
                                     + b2_ref[...]))

    hid = hid_ref[...]
    t = lax.broadcasted_iota(jnp.int32, (L, 1), 0).astype(F32) / L
    dec = jnp.exp(-t * dl_ref[...])
    hf = jnp.dot(hid, w3f_ref[...], preferred_element_type=F32, precision=HI) * dec
    hb = jnp.dot(hid, w3b_ref[...], preferred_element_type=F32, precision=HI) * dec
    hb = jnp.where(lax.broadcasted_iota(jnp.int32, hb.shape, 0) == 0, 0.0, hb)
    inv = 1.0 / (jnp.sum(jnp.abs(hf), axis=0, keepdims=True) + jnp.sum(jnp.abs(hb), axis=0, keepdims=True))
    for part, (s_ref, hv) in enumerate(((sf_ref, hf * inv), (sb_ref, hb * inv))):
        for j in range(CG // LANE):
            s_ref[j] = hv[:, j * LANE:(j + 1) * LANE]
            for par in range(2):
                c0 = (part * 2 + par) * CG + j * LANE
                o_ref[:, c0:c0 + LANE] = s_ref[j, pl.ds(par, H, stride=2), :].astype(o_ref.dtype)


def hyena_filters_eo(feats, w1p, b1, w2, b2, fr, w3, deltas, L):
    hid = w2.shape[0]
    ng = 512 // CG
    full = lambda shape: pl.BlockSpec(shape, lambda o, g: (0, 0))
    return pl.pallas_call(
        _filter_eo_kernel, grid=(HY_ORDER, ng),
        in_specs=[full((L, LANE)), full((LANE, hid)), full((1, hid)), full((hid, hid)), full((1, hid)),
                  full((1, hid)),
                  pl.BlockSpec((hid, CG), lambda o, g: (0, (o * 2) * ng + g)),
                  pl.BlockSpec((hid, CG), lambda o, g: (0, (o * 2 + 1) * ng + g)),
                  pl.BlockSpec((1, CG), lambda o, g: (0, g))],
        out_specs=pl.BlockSpec((None, L // 2, 4 * CG), lambda o, g: (0, 0, o * ng + g)),
        out_shape=jax.ShapeDtypeStruct((1, L // 2, HY_ORDER * ng * 4 * CG), BF16),
        scratch_shapes=[pltpu.VMEM((L, hid), F32), pltpu.VMEM((CG // LANE, L, LANE), F32),
                        pltpu.VMEM((CG // LANE, L, LANE), F32)],
        compiler_params=_cp(("arbitrary", "arbitrary")), name="hy_filters",
    )(feats, w1p, b1, w2, b2, fr, w3, w3, deltas)


def _split_eo(acc, c_ref, s_ref):
    tb = acc.shape[0] // 2
    c, s = c_ref[...], s_ref[...]
    ere, eim, ore, oim = acc[:tb, :CG], acc[tb:, :CG], acc[:tb, CG:], acc[tb:, CG:]
    tre, tim = c * ore + s * oim, c * oim - s * ore
    return (ere + tre, eim + tim), (ere - tre, tim - eim)


def _filter_spec_epilogue(acc, c_ref, s_ref):
    res, ims = [], []
    for j in range(acc.shape[1] // (2 * CG)):
        (lre, lim), (hre, him) = _split_eo(acc[:, j * 2 * CG:(j + 1) * 2 * CG], c_ref, s_ref)
        res += [lre, hre]
        ims += [lim, him]
    return jnp.concatenate([jnp.concatenate(res, axis=1), jnp.concatenate(ims, axis=1)], axis=0)


def _cmul(a, b):
    return a[0] * b[0] - a[1] * b[1], a[0] * b[1] + a[1] * b[0]


def _signal_spec_epilogue(acc, c_ref, s_ref, gre_ref, gim_ref, *, scale):
    c, s = c_ref[...], s_ref[...]
    res, ims = [], []
    for g in range(acc.shape[1] // (2 * CG)):
        x_lo, x_hi = _split_eo(acc[:, g * 2 * CG:(g + 1) * 2 * CG], c_ref, s_ref)
        f = lambda ref, j: ref[:, (g * 4 + j) * CG:(g * 4 + j + 1) * CG]
        g_lo = (f(gre_ref, 0) + f(gre_ref, 2), f(gim_ref, 0) - f(gim_ref, 2))
        g_hi = (f(gre_ref, 1) + f(gre_ref, 3), f(gim_ref, 1) - f(gim_ref, 3))
        p_lo, p_hi = _cmul(x_lo, g_lo), _cmul(x_hi, g_hi)
        dre, dim = p_lo[0] - p_hi[0], p_lo[1] + p_hi[1]
        res += [p_lo[0] + p_hi[0], dre * c - dim * s]
        ims += [p_lo[1] - p_hi[1], dre * s + dim * c]
    return jnp.concatenate([jnp.concatenate(res, axis=1), jnp.concatenate(ims, axis=1)], axis=0) * scale


def _gated(acc, x_ref, z_ref, b_ref):
    return x_ref[...] * (acc + b_ref[...] * z_ref[...])


def _gate_eo_kernel(x_ref, c_ref, z_ref, b_ref, *refs, nl):
    o_ref, scr_ref = refs[-2:]
    i = pl.program_id(0)

    @pl.when(i < nl)
    def _():
        r = x_ref[...] * (c_ref[...] + b_ref[...] * z_ref[...])
        half = r.shape[0]
        for g in range(512 // CG):
            for j in range(CG // LANE):
                slab = g * (CG // LANE) + j
                for par in range(2):
                    c0 = (g * 2 + par) * CG + j * LANE
                    scr_ref[slab, pl.ds(par, half, stride=2), :] = r[:, c0:c0 + LANE]
                o_ref[:, slab * LANE:(slab + 1) * LANE] = scr_ref[slab].astype(o_ref.dtype)

    @pl.when(i >= nl)
    def _():
        o_ref[...] = jnp.zeros_like(o_ref)


def hyena_gate_eo(u3, conv, z1, bias_u, L, B, tl, T, row0, into):
    nl = L // tl
    th = tl // 2
    ci = lambda i: jnp.minimum(i, nl - 1)
    in_specs = [pl.BlockSpec((None, th, 1024), lambda i, b: (2, ci(i), b)),
                pl.BlockSpec((th, 1024), lambda i, b: (ci(i), b)),
                pl.BlockSpec((None, th, 1024), lambda i, b: (0, ci(i), b)),
                pl.BlockSpec((None, 1, 1024), lambda i, b: (HY_ORDER - 1, 0, 0))]
    args = [u3, conv, z1, bias_u]
    if into is not None:
        n_fill, aliases = 0, {4: 0}
        in_specs.append(pl.BlockSpec(memory_space=pl.ANY))
        args.append(into)
        out_spec = pl.BlockSpec((tl, 512), lambda i, b: (row0 // tl + b * nl + i, 0))
    else:
        assert row0 == 0
        n_fill, aliases = (T - B * L) // tl, {}
        out_spec = pl.BlockSpec((tl, 512), lambda i, b: (jnp.where(i < nl, b * nl + i, B * nl + (i - nl)), 0))
    return pl.pallas_call(
        functools.partial(_gate_eo_kernel, nl=nl), grid=(nl + n_fill, B), in_specs=in_specs, out_specs=out_spec,
        out_shape=jax.ShapeDtypeStruct((T, 512), BF16), input_output_aliases=aliases,
        scratch_shapes=[pltpu.VMEM((512 // LANE, tl, LANE), F32)],
        compiler_params=_cp(("arbitrary", "arbitrary")), name="hy_gate",
    )(*args)


def _hyena_consts(L):
    H = L // 2
    tb = min(H, ROW_TILE)
    fwd, inv = _dft_matrices(H, tb)
    w = (2 * jnp.arange(H, dtype=F32)[:, None] + 1.0) * (math.pi / (2 * L))
    twc, tws = jnp.broadcast_to(jnp.cos(w), (H, CG)), jnp.broadcast_to(jnp.sin(w), (H, CG))
    return fwd, inv, twc, tws, _filter_feats(L)


def hyena_seq_eo(pa, L, row_block0, B, consts, wts, T, into=None):
    fwd, inv, twc, tws, feats = consts
    short_w, w1p, b1, w2, b2, fr, w3, deltas, bias_u = wts
    H = L // 2
    tb = min(H, ROW_TILE)
    nkb = H // tb
    tm_i = min(H, 512)
    ng = 512 // CG
    NW = B * 1024
    u3 = short_conv_eo(pa, short_w, L, row_block0, B)
    filt = hyena_filters_eo(feats, w1p, b1, w2, b2, fr, w3, deltas, L)
    zeros_f = jnp.zeros((nkb,), jnp.int32)
    zeros_i = jnp.zeros((H // tm_i,), jnp.int32)
    tw = [(twc, (tb, CG), lambda n, m: (m, 0)), (tws, (tb, CG), lambda n, m: (m, 0))]
    gw = ng * 4 * CG
    gs = gmm(fwd, filt, zeros_f, tm=2 * tb, tn=gw, n_tiles=HY_ORDER, out_dtype=F32,
             epilogue=_filter_spec_epilogue, extra=tw, name="hy_dft_filter")
    z_arr, which_z = u3, 0
    out = None
    for o in range(HY_ORDER):
        g_blk = lambda comp, o=o: (gs, (tb, gw), lambda n, m: (2 * m + comp, o))
        y = gmm(fwd, z_arr, zeros_f + which_z, tm=2 * tb, tn=ng * 2 * CG, n_tiles=B, out_dtype=BF16,
                epilogue=functools.partial(_signal_spec_epilogue, scale=1.0 / L),
                extra=tw + [g_blk(0), g_blk(1)], name="hy_dft_fwd")
        y = y.reshape(1, 2 * H, NW)
        if o < HY_ORDER - 1:
            out = gmm(inv, y, zeros_i, tm=tm_i, tn=2 * CG, n_tiles=B * ng, out_dtype=F32, epilogue=_gated,
                      extra=[(u3, (None, tm_i, 2 * CG), functools.partial(lambda o, n, m: (1 + o, m, n), o)),
                             (z_arr, (None, tm_i, 2 * CG), functools.partial(lambda w, n, m: (w, m, n), which_z)),
                             (bias_u, (None, 1, 2 * CG), functools.partial(lambda o, n, m: (o, 0, n % ng), o))],
                      name="hy_dft_inv_gate").reshape(1, H, NW)
        else:
            conv = gmm(inv, y, zeros_i, tm=tm_i, tn=2 * CG, n_tiles=B * ng, out_dtype=F32, name="hy_dft_inv")
            out = hyena_gate_eo(u3, conv, z_arr, bias_u, L, B, min(L, 512), T, row_block0 * L, into)
        z_arr, which_z = out, 0
    return out


def _gather_kernel(idx_ref, src_ref, *refs, tm, k):
    if k > 1:
        w_ref, o_ref, buf_ref, sem = refs
    else:
        o_ref, buf_ref, sem = refs
    i, n = pl.program_id(0), pl.num_programs(0)
    M = n * tm

    def issue_tile(tile, slot):
        for j in range(k):
            def issue(r, carry):
                pltpu.make_async_copy(src_ref.at[pl.ds(idx_ref[j * M + tile * tm + r], 1)],
                                      buf_ref.at[slot, j, pl.ds(r, 1)], sem.at[slot]).start()
                return carry

            lax.fori_loop(0, tm, issue, 0, unroll=8)

    @pl.when(i == 0)
    def _():
        issue_tile(0, 0)

    @pl.when(i + 1 < n)
    def _():
        issue_tile(i + 1, (i + 1) % 2)

    slot = i % 2
    for j in range(k):
        pltpu.make_async_copy(src_ref.at[pl.ds(0, tm)], buf_ref.at[slot, j], sem.at[slot]).wait()
    if k > 1:
        w = w_ref[...]
        acc = w[:, 0:1] * buf_ref[slot, 0]
        for j in range(1, k):
            acc = acc + w[:, j:j + 1] * buf_ref[slot, j]
        o_ref[...] = acc.astype(o_ref.dtype)
    else:
        o_ref[...] = buf_ref[slot, 0].astype(o_ref.dtype)


def gather_rows(src, idx, tm, out_dtype, weights=None, k=1):
    M = idx.shape[0] // k
    D = src.shape[1]
    in_specs = [pl.BlockSpec(memory_space=pl.ANY)]
    args = [src]
    if k > 1:
        in_specs.append(pl.BlockSpec((tm, LANE), lambda i, idx: (i, 0)))
        args.append(weights)
    return pl.pallas_call(
        functools.partial(_gather_kernel, tm=tm, k=k),
        grid_spec=pltpu.PrefetchScalarGridSpec(
            num_scalar_prefetch=1, grid=(M // tm,), in_specs=in_specs,
            out_specs=pl.BlockSpec((tm, D), lambda i, idx: (i, 0)),
            scratch_shapes=[pltpu.VMEM((2, k, tm, D), src.dtype), pltpu.SemaphoreType.DMA((2,))]),
        out_shape=jax.ShapeDtypeStruct((M, D), out_dtype),
        compiler_params=_cp(("arbitrary",)), name="gather_rows" if k == 1 else "gather_combine",
    )(idx, *args)


def moe_ffn(rows, h_f32, rw, ri, w_gate, w_up, w_down, layer_moe, tm=512):
    T, D = rows.T, rows.D
    E = w_gate.shape[2]
    e_flat = ri[:, :TOP_K].T.reshape(-1)
    onehot = (e_flat[:, None] == jnp.arange(N_EXPERTS, dtype=jnp.int32)[None, :]).astype(jnp.int32)
    csum = jnp.cumsum(onehot, axis=0)
    rank = jnp.sum(csum * onehot, axis=1) - 1
    counts = csum[-1]
    padded = ((counts + tm - 1) // tm) * tm
    ends = jnp.cumsum(padded)
    starts = ends - padded
    dest = starts[e_flat] + rank
    m_pad = TOP_K * T + N_EXPERTS * tm
    token = jnp.tile(jnp.arange(T, dtype=jnp.int32), TOP_K)
    src_tok = jnp.zeros((m_pad,), jnp.int32).at[dest].set(token)
    tile_row = jnp.arange(m_pad // tm, dtype=jnp.int32) * tm
    gid = jnp.minimum(jnp.sum((tile_row[:, None] >= ends[None, :]).astype(jnp.int32), axis=1), N_EXPERTS - 1)
    gid = gid + layer_moe * N_EXPERTS
    used = (ends[-1:] // tm).astype(jnp.int32)

    xs = gather_rows(h_f32, src_tok, tm, BF16)
    tn_h = 1024
    hh = gmm(xs, [w_gate, w_up], gid, tm=tm, tn=tn_h, n_tiles=-(-E // tn_h), out_dtype=BF16, epilogue=_swiglu,
             used=used, name="moe_gate_up")
    y = gmm(hh, w_down, gid, tm=tm, tn=D // 2, n_tiles=2, out_dtype=F32, used=used, name="moe_down")
    return gather_rows(y, dest, ROW_TILE, BF16, weights=rw, k=TOP_K)


def kernel(x, c, ctx, c_ctx, w_mod, b_mod, g_mix_pre, g_mix_post, g_ffn_pre, g_ffn_post, w_in, w_out, attn_sink,
           hy_short, hy_w1, hy_b1, hy_w2, hy_b2, hy_freq, hy_w3, hy_bias, ml_gate_bias, ml_norm, mla_q_norm,
           mla_w_uq, mla_kv_norm, mla_w_ukv, ffn_w_gate, ffn_w_up, ffn_w_down, moe_router, moe_w_gate, moe_w_up,
           moe_w_down):
    B, S, D = x.shape
    NC = ctx.shape[1]
    depth = w_mod.shape[0]
    rows = _Rows(B, S, NC, D)
    T = rows.T
    TM = 512 if T % 512 == 0 else ROW_TILE
    n_mt = T // TM
    ffn_dim = ffn_w_gate.shape[2]

    xs = jnp.concatenate([x.reshape(B * S, D), ctx.reshape(B * NC, D)], axis=0)

    cm = jnp.concatenate([c, c_ctx[None, :]], axis=0)
    cm = jnp.pad(jax.nn.silu(cm), ((0, 16 - (B + 1)), (0, 0))).astype(BF16)
    mod_all = gmm(jnp.tile(cm, (depth, 1)), w_mod, jnp.arange(depth, dtype=jnp.int32), tm=16, tn=1536,
                  n_tiles=N_MOD * D // 1536, out_dtype=F32, name="adaln")
    mods = (mod_all.reshape(depth, 16, N_MOD * D)[:, :B + 1] + b_mod[:, None, :]).reshape(depth, B + 1, N_MOD, 1, D)

    def rope_table(rot_dim, reps, lane0, width):
        cos, sin = _axial_tables(S, rot_dim)
        cos, sin = jnp.tile(cos, (1, reps)), jnp.tile(sin, (1, reps))
        padw = ((0, 0), (lane0, width - lane0 - cos.shape[1]))
        cos = jnp.pad(cos, padw, constant_values=1.0)
        sin = jnp.pad(sin, padw)
        cos = jnp.concatenate([jnp.tile(cos, (B, 1)), jnp.ones((B * NC, width), F32)], axis=0)
        sin = jnp.concatenate([jnp.tile(sin, (B, 1)), jnp.zeros((B * NC, width), F32)], axis=0)
        return cos, sin

    cos_a, sin_a = rope_table(64, 2, 0, LANE)
    cos_m, sin_m = rope_table(MLA_ROPE, 1, MLA_NOPE, LANE)
    hy_consts = {L: _hyena_consts(L) for L in (S, NC)}
    deltas = jnp.abs(jnp.linspace(math.log(HY_DECAY_TARGET) / HY_SLOW_DECAY,
                                  math.log(HY_DECAY_TARGET) / HY_FAST_DECAY, 512, dtype=F32))[None, :]

    w_in_t = jnp.swapaxes(w_in, 1, 2)
    wq = mla_w_uq.reshape(depth, -1, MLA_HEADS, MLA_NOPE + MLA_ROPE)
    wq = jnp.pad(wq, ((0, 0), (0, 0), (0, 0), (0, LANE - MLA_NOPE - MLA_ROPE))).reshape(depth, -1, MLA_HEADS * LANE)
    wkv = mla_w_ukv.reshape(depth, -1, MLA_HEADS, 2 * 64)
    wk = jnp.pad(wkv[..., :MLA_NOPE], ((0, 0), (0, 0), (0, 0), (0, LANE - MLA_NOPE)))
    wk = wk.reshape(depth, -1, MLA_HEADS * LANE)
    wv = wkv[..., MLA_NOPE:].reshape(depth, -1, MLA_HEADS * 64)
    w1p = jnp.pad(hy_w1, ((0, 0), (0, LANE - hy_w1.shape[1]), (0, 0)))
    bias_u = jnp.broadcast_to(hy_bias.reshape(depth, HY_ORDER, 512 // CG, 1, CG),
                              (depth, HY_ORDER, 512 // CG, 2, CG)).reshape(depth, HY_ORDER, 1, 1024)
    gbias = jnp.pad(ml_gate_bias.reshape(depth, 1, 4 * ML_HEADS), ((0, 0), (0, 0), (0, LANE - 4 * ML_HEADS)))
    router = jnp.pad(moe_router, ((0, 0), (0, 0), (0, LANE - N_EXPERTS)))
    n_moe = moe_w_gate.shape[0]
    mw_gate = moe_w_gate.reshape(n_moe * N_EXPERTS, D, -1)
    mw_up = moe_w_up.reshape(n_moe * N_EXPERTS, D, -1)
    mw_down = moe_w_down.reshape(n_moe * N_EXPERTS, -1, D)
    vec = lambda a, l: a[l][None, :]

    h = norm_mod(rows, xs, vec(g_mix_pre, 0), mods[0], 0, 1)
    for layer in range(depth):
        gid = jnp.full((n_mt,), layer, jnp.int32)
        ml = mods[layer]
        p1 = gmm(h, w_in_t, gid, tm=TM, tn=1280, n_tiles=PA_COLS // 1280, out_dtype=BF16, w_t=True, name="proj_in")
        p2 = gmm(h, w_in_t, gid, tm=TM, tn=768, n_tiles=2, n_off=PA_COLS // 768, out_dtype=F32, w_t=True,
                 name="proj_in_f32")

        qkv = attn_rope(rows, p1, cos_a, sin_a)
        a_all = window_attn(rows, qkv, attn_sink[layer])

        hy_w = (hy_short[layer], w1p[layer], vec(hy_b1, layer), hy_w2[layer], vec(hy_b2, layer),
                vec(hy_freq, layer), hy_w3[layer], deltas, bias_u[layer])
        b_l = hyena_seq_eo(p1, S, 0, B, hy_consts[S], hy_w, T)
        b_all = hyena_seq_eo(p1, NC, B * S // NC, B, hy_consts[NC], hy_w, T, into=b_l)

        hf, hb = mlstm_scan(rows, p1, p2, gbias[layer])
        m_all = mlstm_finish(rows, hf, hb, p2, vec(ml_norm, layer))

        qn, kvn, kr = mla_norms(rows, p2, vec(mla_q_norm, layer), vec(mla_kv_norm, layer))
        slab = lambda a: (a, (TM, LANE), lambda n, m: (m, 0))
        qm = gmm(qn, wq, gid, tm=TM, tn=MLA_HEADS * LANE, n_tiles=1, out_dtype=BF16, epilogue=_mla_q_epilogue,
                 extra=[slab(cos_m), slab(sin_m)], name="mla_uq")
        km = gmm(kvn, wk, gid, tm=TM, tn=MLA_HEADS * LANE, n_tiles=1, out_dtype=BF16, epilogue=_mla_k_epilogue,
                 extra=[slab(kr), slab(cos_m), slab(sin_m)], name="mla_uk")
        vm = gmm(kvn, wv, gid, tm=TM, tn=MLA_HEADS * 64, n_tiles=1, out_dtype=BF16, name="mla_uv")
        d_all = mla_attn(rows, qm, km, vm)

        y = gmm([a_all, b_all, m_all, d_all], w_out, gid, tm=TM, tn=1024, n_tiles=D // 1024, out_dtype=BF16,
                name="proj_out")

        i = layer // 2
        nxt = (vec(g_ffn_pre, layer), ml, 3, 4)
        if layer % 2 == 0:
            xs, h2 = post(rows, xs, y, vec(g_mix_post, layer), ml, 2, nxt=nxt)
            gi = jnp.full((n_mt,), i, jnp.int32)
            g = gmm(h2, ffn_w_gate, gi, tm=TM, tn=ffn_dim // 2, n_tiles=2, out_dtype=BF16, epilogue=_silu,
                    w_single_buffer=True, name="ffn_gate")
            hh = gmm(h2, ffn_w_up, gi, tm=TM, tn=ffn_dim // 2, n_tiles=2, out_dtype=BF16, epilogue=_times,
                     extra=[(g, (TM, ffn_dim // 2), lambda n, m: (m, n))], w_single_buffer=True, name="ffn_up")
            y2 = gmm(hh, ffn_w_down, gi, tm=TM, tn=512, n_tiles=D // 512, out_dtype=BF16, name="ffn_down")
        else:
            xs, h2, h2f, rw, ri = post(rows, xs, y, vec(g_mix_post, layer), ml, 2, nxt=nxt, router=router[i])
            y2 = moe_ffn(rows, h2f, rw, ri, mw_gate, mw_up, mw_down, i)
        if layer + 1 < depth:
            xs, h = post(rows, xs, y2, vec(g_ffn_post, layer), ml, 5,
                         nxt=(vec(g_mix_pre, layer + 1), mods[layer + 1], 0, 1))
        else:
            (xs,) = post(rows, xs, y2, vec(g_ffn_post, layer), ml, 5, latent_only=True)
    return xs.reshape(B, S, D)
```

```python
import functools
import math

import numpy as np
import jax
import jax.numpy as jnp
from jax import lax
from jax.experimental import pallas as pl
from jax.experimental.pallas import tpu as pltpu

F32 = jnp.float32
BF16 = jnp.bfloat16
HI = lax.Precision.HIGHEST

EPS = 1e-6
ROPE_BASE = 10000.0
GRID_W = 64
BLOCK = 128
WINDOW = 128
N_MOD = 6
A_HEADS, A_KV_HEADS = 8, 2
HY_ORDER, HY_SHORT, HY_POS_FREQS = 2, 3, 8
HY_DECAY_TARGET, HY_FAST_DECAY, HY_SLOW_DECAY = 1e-2, 0.3, 1.5
ML_HEADS, ML_CHUNK = 4, 128
MLA_HEADS, MLA_NOPE, MLA_ROPE = 8, 64, 32
N_EXPERTS, TOP_K = 8, 2

PA_COLS = 3840
PB_GATES = 4352 - PA_COLS
MLA_HP = 8
LANE = 128
ROW_TILE = 256
NEG = -1e30
LOG2E = math.log2(math.e)
VMEM_LIMIT = 56 * 1024 * 1024


def _cp(sem, vmem=VMEM_LIMIT):
    return pltpu.CompilerParams(dimension_semantics=sem, vmem_limit_bytes=vmem)


def _gmm_kernel(gid_ref, used_ref, *refs, n_a, n_w, cast, w_t, epilogue, n_extra, w_n, tn, n_off):
    a_refs, w_refs, rest = refs[:n_a], refs[n_a:n_a + n_w], refs[n_a + n_w:]
    extra_refs, rest = rest[:n_extra], rest[n_extra:]
    o_ref = rest[0]
    n, m = pl.program_id(0), pl.program_id(1)
    n_axis = 0 if w_t else 1
    if cast:
        wbf_refs = rest[1:1 + n_w]
        prev = gid_ref[jnp.maximum(m - 1, 0)]

        @pl.when((m == 0) | (gid_ref[m] != prev))
        def _():
            for w_ref, wbf_ref in zip(w_refs, wbf_refs):
                w = w_ref[...]
                if w_n is not None:
                    col = (n + n_off) * tn + lax.broadcasted_iota(jnp.int32, w.shape, n_axis)
                    w = jnp.where(col < w_n, w, 0.0)
                wbf_ref[...] = w.astype(BF16)

        srcs = wbf_refs
    else:
        srcs = w_refs

    @pl.when(m < used_ref[0])
    def _():
        accs = []
        for src in srcs:
            acc, lo = None, 0
            for a_ref in a_refs:
                k = a_ref.shape[1]
                if w_t:
                    part = lax.dot_general(a_ref[...], src[:, lo:lo + k], (((1,), (1,)), ((), ())),
                                           preferred_element_type=F32)
                else:
                    part = jnp.dot(a_ref[...], src[lo:lo + k, :], preferred_element_type=F32)
                acc = part if acc is None else acc + part
                lo += k
            accs.append(acc)
        acc = epilogue(*accs, *extra_refs) if epilogue is not None else accs[0]
        o_ref[...] = acc.astype(o_ref.dtype)

    @pl.when(m >= used_ref[0])
    def _():
        o_ref[...] = jnp.zeros_like(o_ref)


def _silu(acc):
    return acc * jax.nn.sigmoid(acc)


def _times(acc, g_ref):
    return acc * g_ref[...].astype(F32)


def gmm(a, w, gid, *, tm, tn, n_tiles, out_dtype, n_off=0, epilogue=None, extra=(), w_t=False, used=None,
        w_single_buffer=False, name="gmm"):
    a_list = list(a) if isinstance(a, (list, tuple)) else [a]
    w_list = list(w) if isinstance(w, (list, tuple)) else [w]
    w = w_list[0]
    M = a_list[0].shape[0]
    k_axis, n_axis = (2, 1) if w_t else (1, 2)
    a_cols = [w.shape[k_axis]] if len(a_list) == 1 else [p.shape[1] for p in a_list]
    K = sum(a_cols)
    mt = M // tm
    assert mt * tm == M and w.shape[k_axis] == K and a_list[0].shape[1] >= a_cols[0] and gid.shape == (mt,)
    cast = w.dtype != BF16
    partial_n = (n_off + n_tiles) * tn > w.shape[n_axis]
    assert cast or not partial_n
    if used is None:
        used = jnp.full((1,), mt, jnp.int32)
    in_specs = [pl.BlockSpec((tm, k), lambda n, m, g, u: (m, 0)) for k in a_cols]
    w_mode = dict(pipeline_mode=pl.Buffered(1)) if w_single_buffer else {}
    for _ in w_list:
        if w_t:
            in_specs.append(pl.BlockSpec((None, tn, K), lambda n, m, g, u: (g[m], n + n_off, 0), **w_mode))
        else:
            in_specs.append(pl.BlockSpec((None, K, tn), lambda n, m, g, u: (g[m], 0, n + n_off), **w_mode))
    args = a_list + w_list
    for arr, block, imap in extra:
        in_specs.append(pl.BlockSpec(block, functools.partial(lambda f, n, m, g, u: f(n, m), imap)))
        args.append(arr)
    kern = functools.partial(_gmm_kernel, n_a=len(a_list), n_w=len(w_list), cast=cast, w_t=w_t, epilogue=epilogue,
                             n_extra=len(extra), w_n=w.shape[n_axis] if partial_n else None, tn=tn, n_off=n_off)
    return pl.pallas_call(
        kern,
        grid_spec=pltpu.PrefetchScalarGridSpec(
            num_scalar_prefetch=2, grid=(n_tiles, mt), in_specs=in_specs,
            out_specs=pl.BlockSpec((tm, tn), lambda n, m, g, u: (m, n)),
            scratch_shapes=[pltpu.VMEM((tn, K) if w_t else (K, tn), BF16)] * len(w_list) if cast else []),
        out_shape=jax.ShapeDtypeStruct((M, n_tiles * tn), out_dtype),
        compiler_params=_cp(("arbitrary", "arbitrary")),
        name=name,
    )(gid, used, *args)


def _rms(v, g):
    return v * lax.rsqrt(jnp.mean(v * v, axis=-1, keepdims=True) + EPS) * g


def _norm_mod_kernel(x_ref, g_ref, sh_ref, sc_ref, h_ref):
    h_ref[...] = (_rms(x_ref[...], g_ref[...]) * (1.0 + sc_ref[...]) + sh_ref[...]).astype(h_ref.dtype)


def _post_kernel(*refs, with_next, with_router):
    x_ref, y_ref, gp_ref, gate_ref = refs[:4]
    refs = refs[4:]
    xn = x_ref[...] + gate_ref[...] * _rms(y_ref[...].astype(F32), gp_ref[...])
    if not with_next:
        refs[0][...] = xn
        return
    gn_ref, sh_ref, sc_ref = refs[:3]
    refs = refs[3:]
    if with_router:
        r_ref, refs = refs[0], refs[1:]
    refs[0][...] = xn
    h = _rms(xn, gn_ref[...]) * (1.0 + sc_ref[...]) + sh_ref[...]
    refs[1][...] = h.astype(BF16)
    if with_router:
        hf_ref, rw_ref, ri_ref = refs[2:5]
        hf_ref[...] = h
        logits = jnp.dot(h, r_ref[...], preferred_element_type=F32, precision=HI)
        lane = lax.broadcasted_iota(jnp.int32, logits.shape, 1)
        logits = jnp.where(lane < N_EXPERTS, logits, -jnp.inf)
        v1 = jnp.max(logits, axis=1, keepdims=True)
        i1 = jnp.min(jnp.where(logits == v1, lane, LANE), axis=1, keepdims=True)
        l2 = jnp.where(lane == i1, -jnp.inf, logits)
        v2 = jnp.max(l2, axis=1, keepdims=True)
        i2 = jnp.min(jnp.where(l2 == v2, lane, LANE), axis=1, keepdims=True)
        e = jnp.exp(v2 - v1)
        w1 = 1.0 / (1.0 + e)
        w2 = e / (1.0 + e)
        rw_ref[...] = jnp.where(lane == 0, w1, jnp.where(lane == 1, w2, 0.0))
        ri_ref[...] = jnp.where(lane == 0, i1, jnp.where(lane == 1, i2, 0))


class _Rows:
    def __init__(self, B, S, NC, D):
        self.B, self.S, self.NC, self.D = B, S, NC, D
        self.T = B * S + B * NC
        assert S % ROW_TILE == 0 and NC % ROW_TILE == 0
        self.nt = self.T // ROW_TILE

    def mod_row(self, i):
        n_lat = self.B * self.S // ROW_TILE
        return jnp.where(i < n_lat, i // (self.S // ROW_TILE), self.B)


def _row_spec(D):
    return pl.BlockSpec((ROW_TILE, D), lambda i: (i, 0))


def _vec_spec(D):
    return pl.BlockSpec((1, D), lambda i: (0, 0))


def _mod_spec(rows, j):
    return pl.BlockSpec((None, None, 1, rows.D), lambda i: (rows.mod_row(i), j, 0, 0))


def norm_mod(rows, x, g, mods, j_shift, j_scale):
    D = rows.D
    return pl.pallas_call(
        _norm_mod_kernel, grid=(rows.nt,),
        in_specs=[_row_spec(D), _vec_spec(D), _mod_spec(rows, j_shift), _mod_spec(rows, j_scale)],
        out_specs=_row_spec(D),
        out_shape=jax.ShapeDtypeStruct((rows.T, D), BF16),
        compiler_params=_cp(("parallel",)), name="norm_mod",
    )(x, g, mods, mods)


def post(rows, x, y, g_post, mods, j_gate, nxt=None, router=None, latent_only=False):
    D = rows.D
    T = rows.B * rows.S if latent_only else rows.T
    in_specs = [_row_spec(D), _row_spec(D), _vec_spec(D), _mod_spec(rows, j_gate)]
    args = [x, y, g_post, mods]
    out_specs = [_row_spec(D)]
    out_shape = [jax.ShapeDtypeStruct((T, D), F32)]
    if nxt is not None:
        g_next, mods_next, j_shift, j_scale = nxt
        in_specs += [_vec_spec(D), _mod_spec(rows, j_shift), _mod_spec(rows, j_scale)]
        args += [g_next, mods_next, mods_next]
        out_specs.append(_row_spec(D))
        out_shape.append(jax.ShapeDtypeStruct((T, D), BF16))
        if router is not None:
            in_specs.append(pl.BlockSpec((D, LANE), lambda i: (0, 0)))
            args.append(router)
            out_specs += [_row_spec(D), _row_spec(LANE), _row_spec(LANE)]
            out_shape += [jax.ShapeDtypeStruct((T, D), F32), jax.ShapeDtypeStruct((T, LANE), F32),
                          jax.ShapeDtypeStruct((T, LANE), jnp.int32)]
    kern = functools.partial(_post_kernel, with_next=nxt is not None, with_router=router is not None)
    return pl.pallas_call(
        kern, grid=(T // ROW_TILE,), in_specs=in_specs, out_specs=out_specs, out_shape=out_shape,
        compiler_params=_cp(("parallel",)), name="post",
    )(*args)


def _rot_pairs(x, q):
    lane = lax.broadcasted_iota(jnp.int32, x.shape, 1)
    n = x.shape[1]
    return jnp.where(lane % (2 * q) < q, -pltpu.roll(x, n - q, 1), pltpu.roll(x, q, 1))


def _axial_tables(S, rot_dim):
    rows = S // GRID_W
    r = jnp.repeat(jnp.arange(rows, dtype=F32), GRID_W)
    col = jnp.tile(jnp.arange(GRID_W, dtype=F32), rows)
    half = rot_dim // 2
    inv = ROPE_BASE ** (-jnp.arange(0, half, 2, dtype=F32) / half)
    ar, ac = r[:, None] * inv, col[:, None] * inv
    ang = jnp.concatenate([ar, ar, ac, ac], axis=-1)
    return jnp.cos(ang), jnp.sin(ang)


def _attn_rope_kernel(p_ref, cos_ref, sin_ref, o_ref, *, scale):
    cos, sin = cos_ref[...], sin_ref[...]
    nq = A_HEADS * 64 // LANE
    for s in range(nq + 1):
        xs = p_ref[:, s * LANE:(s + 1) * LANE].astype(F32)
        r = xs * cos + _rot_pairs(xs, 16) * sin
        if s < nq:
            r = r * scale
        o_ref[:, s * LANE:(s + 1) * LANE] = r.astype(o_ref.dtype)
    o_ref[:, (nq + 1) * LANE:] = p_ref[:, (nq + 1) * LANE:].astype(o_ref.dtype)


def attn_rope(rows, p1, cos, sin):
    W = 768
    return pl.pallas_call(
        functools.partial(_attn_rope_kernel, scale=64 ** -0.5 * LOG2E), grid=(rows.nt,),
        in_specs=[pl.BlockSpec((ROW_TILE, W), lambda i: (i, 0)), _row_spec(LANE), _row_spec(LANE)],
        out_specs=pl.BlockSpec((ROW_TILE, W), lambda i: (i, 0)),
        out_shape=jax.ShapeDtypeStruct((rows.T, W), BF16),
        compiler_params=_cp(("parallel",)), name="attn_rope",
    )(p1, cos, sin)


def _nt(a, b):
    return lax.dot_general(a, b, (((1,), (1,)), ((), ())), preferred_element_type=F32)


def _softmax_av(scores, values, sink=None):
    m = scores[0].max(axis=1, keepdims=True)
    for s in scores[1:]:
        m = jnp.maximum(m, s.max(axis=1, keepdims=True))
    if sink is not None:
        m = jnp.maximum(m, sink)
    den = jnp.exp2(sink - m) if sink is not None else 0.0
    acc = None
    for s, v in zip(scores, values):
        p = jnp.exp2(s - m)
        den = den + p.sum(axis=1, keepdims=True)
        pv = jnp.dot(p.astype(BF16), v, preferred_element_type=F32)
        acc = pv if acc is None else acc + pv
    return acc / den


def _window_kernel(sink_ref, q_ref, kp_ref, kc_ref, kn_ref, vp_ref, vc_ref, vn_ref, kx_ref, vx_ref, o_ref, *, nb):
    n = pl.program_id(1)
    groups = A_HEADS // A_KV_HEADS
    d = 64
    kb = jnp.concatenate([kp_ref[...], kc_ref[...], kn_ref[...]], axis=0)
    vb = jnp.concatenate([vp_ref[...], vc_ref[...], vn_ref[...]], axis=0)
    Q = groups * BLOCK
    qi = lax.broadcasted_iota(jnp.int32, (Q, 3 * BLOCK), 0) % BLOCK
    kj = lax.broadcasted_iota(jnp.int32, (Q, 3 * BLOCK), 1)
    k_abs = (n - 1) * BLOCK + kj
    valid = (jnp.abs(kj - BLOCK - qi) <= WINDOW) & (k_abs >= 0) & (k_abs < nb * BLOCK) & (n < nb)
    hrow = lax.broadcasted_iota(jnp.int32, (Q, 1), 0) // BLOCK
    outs = []
    for g in range(A_KV_HEADS):
        qg = jnp.concatenate([q_ref[:, (g * groups + h) * d:(g * groups + h + 1) * d] for h in range(groups)], axis=0)
        ksl = slice(g * d, (g + 1) * d)
        s_band = jnp.where(valid, _nt(qg, kb[:, ksl]), NEG)
        s_ctx = _nt(qg, kx_ref[:, ksl])
        sink = jnp.zeros((Q, 1), F32)
        for h in range(groups):
            sink = jnp.where(hrow == h, sink_ref[g * groups + h] * LOG2E, sink)
        o = _softmax_av([s_band, s_ctx], [vb[:, ksl], vx_ref[:, ksl]], sink)
        outs += [o[h * BLOCK:(h + 1) * BLOCK] for h in range(groups)]
    o_ref[...] = jnp.concatenate(outs, axis=1).astype(o_ref.dtype)


def window_attn(rows, qkv, sink):
    B, S, NC = rows.B, rows.S, rows.NC
    nb, ncb = S // BLOCK, NC // BLOCK
    cb = B * S // NC

    def q_block(b, n):
        return jnp.where(n < nb, b * nb + n, B * nb + b * ncb + (n - nb))

    def kv_spec(col, off):
        return pl.BlockSpec((BLOCK, LANE), lambda b, n, s: (b * nb + jnp.clip(n + off, 0, nb - 1), col))

    in_specs = [pl.BlockSpec((BLOCK, 512), lambda b, n, s: (q_block(b, n), 0)),
                kv_spec(4, -1), kv_spec(4, 0), kv_spec(4, 1), kv_spec(5, -1), kv_spec(5, 0), kv_spec(5, 1),
                pl.BlockSpec((NC, LANE), lambda b, n, s: (cb + b, 4)),
                pl.BlockSpec((NC, LANE), lambda b, n, s: (cb + b, 5))]
    return pl.pallas_call(
        functools.partial(_window_kernel, nb=nb),
        grid_spec=pltpu.PrefetchScalarGridSpec(
            num_scalar_prefetch=1, grid=(B, nb + ncb), in_specs=in_specs,
            out_specs=pl.BlockSpec((BLOCK, 512), lambda b, n, s: (q_block(b, n), 0))),
        out_shape=jax.ShapeDtypeStruct((rows.T, 512), BF16),
        compiler_params=_cp(("parallel", "parallel")), name="window_attn",
    )(sink, *([qkv] * 9))


MLA_SCALE = (MLA_NOPE + MLA_ROPE) ** -0.5 * LOG2E


def _mla_q_epilogue(acc, cos_ref, sin_ref):
    cos, sin = cos_ref[...], sin_ref[...]
    slabs = [acc[:, h * LANE:(h + 1) * LANE] for h in range(MLA_HEADS)]
    return jnp.concatenate([(s * cos + _rot_pairs(s, 8) * sin) * MLA_SCALE for s in slabs], axis=1)


def _mla_k_epilogue(acc, kr_ref, cos_ref, sin_ref):
    kr = kr_ref[...]
    kr = kr * cos_ref[...] + _rot_pairs(kr, 8) * sin_ref[...]
    return jnp.concatenate([acc[:, h * LANE:(h + 1) * LANE] + kr for h in range(MLA_HEADS)], axis=1)


def _mla_kernel(q_ref, kl_ref, kx_ref, vl_ref, vx_ref, o_ref, *, nq):
    def run(with_latent_keys):
        outs = []
        for j in range(MLA_HP):
            sl, vs = slice(j * LANE, (j + 1) * LANE), slice(j * 64, (j + 1) * 64)
            q = q_ref[:, sl]
            scores, values = [_nt(q, kx_ref[:, sl])], [vx_ref[:, vs]]
            if with_latent_keys:
                scores.insert(0, _nt(q, kl_ref[:, sl]))
                values.insert(0, vl_ref[:, vs])
            outs.append(_softmax_av(scores, values))
        o_ref[...] = jnp.concatenate(outs, axis=1).astype(o_ref.dtype)

    i = pl.program_id(2)
    pl.when(i < nq)(lambda: run(True))
    pl.when(i >= nq)(lambda: run(False))


def mla_attn(rows, qm, km, vm, tq=256):
    B, S, NC = rows.B, rows.S, rows.NC
    cb = B * S // NC
    nq, nqc = S // tq, NC // tq
    kw, vw = MLA_HP * LANE, MLA_HP * 64

    def q_block(b, i):
        return jnp.where(i < nq, b * nq + i, B * nq + b * nqc + (i - nq))

    return pl.pallas_call(
        functools.partial(_mla_kernel, nq=nq), grid=(B, MLA_HEADS // MLA_HP, nq + nqc),
        in_specs=[pl.BlockSpec((tq, kw), lambda b, h, i: (q_block(b, i), h)),
                  pl.BlockSpec((S, kw), lambda b, h, i: (b, h)),
                  pl.BlockSpec((NC, kw), lambda b, h, i: (cb + b, h)),
                  pl.BlockSpec((S, vw), lambda b, h, i: (b, h)),
                  pl.BlockSpec((NC, vw), lambda b, h, i: (cb + b, h))],
        out_specs=pl.BlockSpec((tq, vw), lambda b, h, i: (q_block(b, i), h)),
        out_shape=jax.ShapeDtypeStruct((rows.T, MLA_HEADS * 64), BF16),
        compiler_params=_cp(("parallel", "parallel", "parallel")), name="mla_attn",
    )(qm, km, km, vm, vm)


def _mla_norms_kernel(x0_ref, x1_ref, x2_ref, x3_ref, gq_ref, gkv_ref, q_ref, kv_ref, kr_ref):
    x = jnp.concatenate([x0_ref[...], x1_ref[...], x2_ref[...], x3_ref[...]], axis=1)
    q_ref[...] = _rms(x[:, 16:528], gq_ref[...]).astype(q_ref.dtype)
    kv_ref[...] = _rms(x[:, 528:784], gkv_ref[...]).astype(kv_ref.dtype)
    z = lambda w: jnp.zeros((x.shape[0], w), F32)
    kr_ref[...] = jnp.concatenate([z(MLA_NOPE), x[:, 784:816], z(LANE - MLA_NOPE - MLA_ROPE)], axis=1)


def mla_norms(rows, p2, gq, gkv):
    T = rows.T
    return pl.pallas_call(
        _mla_norms_kernel, grid=(rows.nt,),
        in_specs=[pl.BlockSpec((ROW_TILE, 256), functools.partial(lambda j, i: (i, PB_GATES // 256 + j), j))
                  for j in range(4)] + [_vec_spec(512), _vec_spec(256)],
        out_specs=[_row_spec(512), _row_spec(256), _row_spec(LANE)],
        out_shape=[jax.ShapeDtypeStruct((T, 512), BF16), jax.ShapeDtypeStruct((T, 256), BF16),
                   jax.ShapeDtypeStruct((T, LANE), F32)],
        compiler_params=_cp(("parallel",)), name="mla_norms",
    )(p2, p2, p2, p2, gq, gkv)


def _bf16_pieces(x):
    hi = x.astype(BF16)
    r = x - hi.astype(F32)
    mid = r.astype(BF16)
    return hi, mid, (r - mid.astype(F32)).astype(BF16)


def _log_sigmoid(x):
    return jnp.minimum(x, 0.0) - jnp.log(1.0 + jnp.exp(-jnp.abs(x)))


def _mlstm_kernel(*refs, scale):
    nh = ML_HEADS
    per = 3 * nh + 1
    gb_ref, of_ref, ob_ref, c_ref, n_ref, m_ref = refs[2 * per:]

    @pl.when(pl.program_id(1) == 0)
    def _():
        c_ref[...] = jnp.zeros_like(c_ref)
        n_ref[...] = jnp.zeros_like(n_ref)
        m_ref[...] = jnp.zeros_like(m_ref)

    for d, o_ref in enumerate((of_ref, ob_ref)):
        r = refs[d * per:(d + 1) * per]
        _mlstm_chunk(r[0:nh], r[nh:2 * nh], r[2 * nh:3 * nh], r[3 * nh], gb_ref, o_ref,
                     c_ref.at[d], n_ref.at[d], m_ref.at[d], backward=d == 1, scale=scale)


def _mlstm_chunk(q_refs, k_refs, v_refs, g_ref, gb_ref, o_ref, c_ref, n_ref, m_ref, *, backward, scale):
    nh, L = ML_HEADS, ML_CHUNK
    t_i = lax.broadcasted_iota(jnp.int32, (L, L), 0)
    s_i = lax.broadcasted_iota(jnp.int32, (L, L), 1)
    mask = (s_i >= t_i) if backward else (s_i <= t_i)
    gb = g_ref[...] + gb_ref[...]
    ls = _log_sigmoid(gb)
    gbt, lst = gb.T, ls.T
    mask_b = mask.astype(BF16)
    cum_c = sum(jnp.dot(mask_b, p, preferred_element_type=F32) for p in _bf16_pieces(ls))
    cum_r = sum(jnp.dot(p, mask_b.T, preferred_element_type=F32) for p in _bf16_pieces(lst))
    last = 0 if backward else L - 1
    gi0, gf0 = (2 * nh, 3 * nh) if backward else (0, nh)
    for h in range(nh):
        gi, gf = gi0 + h, gf0 + h
        q = q_refs[h][...].astype(F32)
        kf = k_refs[h][...].astype(F32) * scale
        vf = v_refs[h][...].astype(F32)
        qb, kb = q.astype(BF16), kf.astype(BF16)
        cumc, cumr = cum_c[:, gf:gf + 1], cum_r[gf:gf + 1, :]
        li_r, li_c = gbt[gi:gi + 1, :], gb[:, gi:gi + 1]
        m11 = m_ref[h][0:1, 0:1]
        log_intra = jnp.where(mask, cumc - cumr + li_r, NEG)
        log_inter = cumc + m11
        m_t = jnp.maximum(log_inter, log_intra.max(axis=1, keepdims=True))
        w_inter = jnp.exp(log_inter - m_t)
        s = _nt(qb, kb) * jnp.exp(log_intra - m_t)
        cmat = c_ref[h]
        num = w_inter * _nt(qb, cmat.astype(BF16)) + jnp.dot(s.astype(BF16), vf.astype(BF16),
                                                             preferred_element_type=F32)
        den = w_inter * jnp.sum(q * n_ref[h][0:1, :], axis=1, keepdims=True) + s.sum(axis=1, keepdims=True)
        o_ref[:, h * L:(h + 1) * L] = num / jnp.maximum(jnp.abs(den), jnp.exp(-m_t))
        total = cumc[last:last + 1, :]
        log_w = total - cumc + li_c
        m_new = jnp.maximum(total + m11, log_w.max(axis=0, keepdims=True))
        decay = jnp.exp(total + m11 - m_new)
        w = jnp.exp(log_w - m_new)
        upd = lax.dot_general((vf * w).astype(BF16), kb, (((0,), (0,)), ((), ())), preferred_element_type=F32)
        c_ref[h] = decay * cmat + upd
        n_ref[h] = jnp.broadcast_to(decay * n_ref[h][0:1, :] + jnp.sum(w * kf, axis=0, keepdims=True), (8, L))
        m_ref[h] = jnp.broadcast_to(m_new, (8, LANE))


def mlstm_scan(rows, pa, pb, gbias):
    B, S, NC = rows.B, rows.S, rows.NC
    nh, L = ML_HEADS, ML_CHUNK
    ncx, ncl = NC // L, S // L
    first_ctx = B * S // L

    def row_block(backward, b, j):
        cx = (ncx - 1 - j) if backward else j
        cl = (ncl - 1 - (j - ncx)) if backward else (j - ncx)
        return jnp.where(j < ncx, first_ctx + b * ncx + cx, b * ncl + cl)

    def col_spec(backward, col):
        return pl.BlockSpec((L, LANE), lambda b, j: (row_block(backward, b, j), col))

    q0 = 2304 // LANE
    in_specs = []
    for backward in (False, True):
        in_specs += [col_spec(backward, q0 + c) for c in range(3 * nh)] + [col_spec(backward, PB_GATES // LANE)]
    in_specs.append(pl.BlockSpec((1, LANE), lambda b, j: (0, 0)))
    out = jax.ShapeDtypeStruct((rows.T, nh * L), F32)
    return pl.pallas_call(
        functools.partial(_mlstm_kernel, scale=L ** -0.5),
        grid=(B, ncx + ncl), in_specs=in_specs,
        out_specs=[pl.BlockSpec((L, nh * L), functools.partial(lambda bw, b, j: (row_block(bw, b, j), 0), bw))
                   for bw in (False, True)],
        out_shape=[out, out],
        scratch_shapes=[pltpu.VMEM((2, nh, L, L), F32), pltpu.VMEM((2, nh, 8, L), F32),
                        pltpu.VMEM((2, nh, 8, LANE), F32)],
        compiler_params=_cp(("parallel", "arbitrary")), name="mlstm_scan",
    )(*(([pa] * (3 * nh) + [pb]) * 2), gbias)


def _mlstm_finish_kernel(hf_ref, hb_ref, *refs):
    o_refs, g_ref, out_ref = refs[:ML_HEADS], refs[ML_HEADS], refs[ML_HEADS + 1]
    for h in range(ML_HEADS):
        sl = slice(h * ML_CHUNK, (h + 1) * ML_CHUNK)
        hn = _rms(hf_ref[:, sl] + hb_ref[:, sl], g_ref[:, sl])
        out_ref[:, sl] = (jax.nn.sigmoid(o_refs[h][...]) * hn).astype(out_ref.dtype)


def mlstm_finish(rows, hf, hb, p1, gnorm):
    W = ML_HEADS * ML_CHUNK
    o_specs = [pl.BlockSpec((ROW_TILE, LANE), functools.partial(lambda h, i: (i, h), h))
               for h in range(ML_HEADS)]
    return pl.pallas_call(
        _mlstm_finish_kernel, grid=(rows.nt,),
        in_specs=[_row_spec(W), _row_spec(W)] + o_specs + [_vec_spec(W)],
        out_specs=_row_spec(W),
        out_shape=jax.ShapeDtypeStruct((rows.T, W), BF16),
        compiler_params=_cp(("parallel",)), name="mlstm_finish",
    )(hf, hb, *([p1] * ML_HEADS), gnorm)


def _dft_gen_kernel(ca_ref, sa_ref, cb_ref, sb_ref, ca2_ref, sa2_ref, cb2_ref, sb2_ref, fwd_ref, inv_ref):
    tb, L = cb_ref.shape
    ca, sa, cb, sb = ca_ref[...], sa_ref[...], cb_ref[...], sb_ref[...]
    fwd_ref[0] = (ca * cb - sa * sb).astype(fwd_ref.dtype)
    fwd_ref[1] = (-(sa * cb + ca * sb)).astype(fwd_ref.dtype)
    ca, sa, cb, sb = ca2_ref[...], sa2_ref[...], cb2_ref[...], sb2_ref[...]
    c, s = (ca * cb - sa * sb).astype(inv_ref.dtype), (-(sa * cb + ca * sb)).astype(inv_ref.dtype)
    for j in range(L // tb):
        inv_ref[:, 2 * j * tb:(2 * j + 1) * tb] = c[:, j * tb:(j + 1) * tb]
        inv_ref[:, (2 * j + 1) * tb:(2 * j + 2) * tb] = s[:, j * tb:(j + 1) * tb]


def _dft_matrices(L, tb):
    nb = L // tb
    col = jnp.arange(L, dtype=jnp.int32)[None, :]
    r0 = (jnp.arange(nb, dtype=jnp.int32) * tb)[:, None]
    i = jnp.arange(tb, dtype=jnp.int32)[:, None]
    unit = math.pi / (2 * L)
    trig = lambda ph: (jnp.cos((ph % (4 * L)).astype(F32) * unit), jnp.sin((ph % (4 * L)).astype(F32) * unit))
    ca, sa = trig((2 * r0 + 1) * col)
    cb, sb = trig(2 * i * col)
    ca2, sa2 = trig((2 * col + 1) * r0)
    cb2, sb2 = trig((2 * col + 1) * i)
    blk = lambda: pl.BlockSpec((None, 1, L), lambda j: (j, 0, 0))
    shared = lambda: pl.BlockSpec((tb, L), lambda j: (0, 0))
    a3 = lambda a: a[:, None, :]
    fwd, inv = pl.pallas_call(
        _dft_gen_kernel, grid=(nb,),
        in_specs=[blk(), blk(), shared(), shared(), blk(), blk(), shared(), shared()],
        out_specs=[pl.BlockSpec((None, 2, tb, L), lambda j: (j, 0, 0, 0)),
                   pl.BlockSpec((tb, 2 * L), lambda j: (j, 0))],
        out_shape=[jax.ShapeDtypeStruct((nb, 2, tb, L), BF16), jax.ShapeDtypeStruct((L, 2 * L), BF16)],
        compiler_params=_cp(("parallel",)), name="hy_dft_gen",
    )(a3(ca), a3(sa), cb, sb, a3(ca2), a3(sa2), cb2, sb2)
    return fwd.reshape(2 * L, L), inv


def _filter_feats(L):
    t = jnp.arange(L, dtype=F32) / L
    kf = jnp.arange(1, HY_POS_FREQS + 1, dtype=F32)
    ang = 2.0 * math.pi * t[:, None] * kf
    feats = jnp.concatenate([t[:, None], jnp.sin(ang), jnp.cos(ang)], axis=-1)
    return jnp.pad(feats, ((0, 0), (0, LANE - feats.shape[1])))


CG = 256


def _short_conv_eo_kernel(p_ref, w_ref, o_ref, scr_ref):
    x = p_ref[...].astype(F32)
    L = x.shape[0]
    row = lax.broadcasted_iota(jnp.int32, x.shape, 0)
    prev = jnp.where(row == 0, 0.0, pltpu.roll(x, 1, 0))
    nxt = jnp.where(row == L - 1, 0.0, pltpu.roll(x, L - 1, 0))
    u = w_ref[0:1, :] * prev + w_ref[1:2, :] * x + w_ref[2:3, :] * nxt
    for j in range(CG // LANE):
        scr_ref[j] = u[:, j * LANE:(j + 1) * LANE]
        for par in range(2):
            o_ref[:, par * CG + j * LANE:par * CG + (j + 1) * LANE] = scr_ref[j, pl.ds(par, L // 2, stride=2), :]


def short_conv_eo(pa, short_w, L, row_block0, B):
    ng = 512 // CG
    return pl.pallas_call(
        _short_conv_eo_kernel, grid=(B, 3, ng),
        in_specs=[pl.BlockSpec((L, CG), lambda b, w, g: (row_block0 + b, 768 // CG + w * ng + g)),
                  pl.BlockSpec((HY_SHORT, CG), lambda b, w, g: (0, w * ng + g))],
        out_specs=pl.BlockSpec((None, L // 2, 2 * CG), lambda b, w, g: (w, 0, b * ng + g)),
        out_shape=jax.ShapeDtypeStruct((3, L // 2, B * 1024), F32),
        scratch_shapes=[pltpu.VMEM((CG // LANE, L, LANE), F32)],
        compiler_params=_cp(("parallel", "parallel", "parallel")), name="hy_short_conv",
    )(pa, short_w)


def _filter_eo_kernel(feat_ref, w1_ref, b1_ref, w2_ref, b2_ref, fr_ref, w3f_ref, w3b_ref, dl_ref, o_ref, hid_ref,
                      sf_ref, sb_ref):
    L = feat_ref.shape[0]
    H = L // 2

    @pl.when((pl.program_id(0) == 0) & (pl.program_id(1) == 0))
    def _():
        fr = fr_ref[...]
        h1 = jnp.sin(fr * (jnp.dot(feat_ref[...], w1_ref[...], preferred_element_type=F32, precision=HI)
                           + b1_ref[...]))
        hid_ref[...] = jnp.sin(fr * (jnp.dot(h1, w2_ref[...], preferred_element_type=F32, precision=HI)
                                     + b2_ref[...]))

    hid = hid_ref[...]
    t = lax.broadcasted_iota(jnp.int32, (L, 1), 0).astype(F32) / L
    dec = jnp.exp(-t * dl_ref[...])
    hf = jnp.dot(hid, w3f_ref[...], preferred_element_type=F32, precision=HI) * dec
    hb = jnp.dot(hid, w3b_ref[...], preferred_element_type=F32, precision=HI) * dec
    hb = jnp.where(lax.broadcasted_iota(jnp.int32, hb.shape, 0) == 0, 0.0, hb)
    inv = 1.0 / (jnp.sum(jnp.abs(hf), axis=0, keepdims=True) + jnp.sum(jnp.abs(hb), axis=0, keepdims=True))
    for part, (s_ref, hv) in enumerate(((sf_ref, hf * inv), (sb_ref, hb * inv))):
        for j in range(CG // LANE):
            s_ref[j] = hv[:, j * LANE:(j + 1) * LANE]
            for par in range(2):
                c0 = (part * 2 + par) * CG + j * LANE
                o_ref[:, c0:c0 + LANE] = s_ref[j, pl.ds(par, H, stride=2), :].astype(o_ref.dtype)


def hyena_filters_eo(feats, w1p, b1, w2, b2, fr, w3, deltas, L):
    hid = w2.shape[0]
    ng = 512 // CG
    full = lambda shape: pl.BlockSpec(shape, lambda o, g: (0, 0))
    return pl.pallas_call(
        _filter_eo_kernel, grid=(HY_ORDER, ng),
        in_specs=[full((L, LANE)), full((LANE, hid)), full((1, hid)), full((hid, hid)), full((1, hid)),
                  full((1, hid)),
                  pl.BlockSpec((hid, CG), lambda o, g: (0, (o * 2) * ng + g)),
                  pl.BlockSpec((hid, CG), lambda o, g: (0, (o * 2 + 1) * ng + g)),
                  pl.BlockSpec((1, CG), lambda o, g: (0, g))],
        out_specs=pl.BlockSpec((None, L // 2, 4 * CG), lambda o, g: (0, 0, o * ng + g)),
        out_shape=jax.ShapeDtypeStruct((1, L // 2, HY_ORDER * ng * 4 * CG), BF16),
        scratch_shapes=[pltpu.VMEM((L, hid), F32), pltpu.VMEM((CG // LANE, L, LANE), F32),
                        pltpu.VMEM((CG // LANE, L, LANE), F32)],
        compiler_params=_cp(("arbitrary", "arbitrary")), name="hy_filters",
    )(feats, w1p, b1, w2, b2, fr, w3, w3, deltas)


def _split_eo(acc, c_ref, s_ref):
    tb = acc.shape[0] // 2
    c, s = c_ref[...], s_ref[...]
    ere, eim, ore, oim = acc[:tb, :CG], acc[tb:, :CG], acc[:tb, CG:], acc[tb:, CG:]
    tre, tim = c * ore + s * oim, c * oim - s * ore
    return (ere + tre, eim + tim), (ere - tre, tim - eim)


def _filter_spec_epilogue(acc, c_ref, s_ref):
    res, ims = [], []
    for j in range(acc.shape[1] // (2 * CG)):
        (lre, lim), (hre, him) = _split_eo(acc[:, j * 2 * CG:(j + 1) * 2 * CG], c_ref, s_ref)
        res += [lre, hre]
        ims += [lim, him]
    return jnp.concatenate([jnp.concatenate(res, axis=1), jnp.concatenate(ims, axis=1)], axis=0)


def _cmul(a, b):
    return a[0] * b[0] - a[1] * b[1], a[0] * b[1] + a[1] * b[0]


def _signal_spec_epilogue(acc, c_ref, s_ref, gre_ref, gim_ref, *, scale):
    c, s = c_ref[...], s_ref[...]
    res, ims = [], []
    for g in range(acc.shape[1] // (2 * CG)):
        x_lo, x_hi = _split_eo(acc[:, g * 2 * CG:(g + 1) * 2 * CG], c_ref, s_ref)
        f = lambda ref, j: ref[:, (g * 4 + j) * CG:(g * 4 + j + 1) * CG]
        g_lo = (f(gre_ref, 0) + f(gre_ref, 2), f(gim_ref, 0) - f(gim_ref, 2))
        g_hi = (f(gre_ref, 1) + f(gre_ref, 3), f(gim_ref, 1) - f(gim_ref, 3))
        p_lo, p_hi = _cmul(x_lo, g_lo), _cmul(x_hi, g_hi)
        dre, dim = p_lo[0] - p_hi[0], p_lo[1] + p_hi[1]
        res += [p_lo[0] + p_hi[0], dre * c - dim * s]
        ims += [p_lo[1] - p_hi[1], dre * s + dim * c]
    return jnp.concatenate([jnp.concatenate(res, axis=1), jnp.concatenate(ims, axis=1)], axis=0) * scale


def _gated(acc, x_ref, z_ref, b_ref):
    return x_ref[...] * (acc + b_ref[...] * z_ref[...])


def _gate_eo_kernel(x_ref, c_ref, z_ref, b_ref, *refs, nl):
    o_ref, scr_ref = refs[-2:]
    i = pl.program_id(0)

    @pl.when(i < nl)
    def _():
        r = x_ref[...] * (c_ref[...] + b_ref[...] * z_ref[...])
        half = r.shape[0]
        for g in range(512 // CG):
            for j in range(CG // LANE):
                slab = g * (CG // LANE) + j
                for par in range(2):
                    c0 = (g * 2 + par) * CG + j * LANE
                    scr_ref[slab, pl.ds(par, half, stride=2), :] = r[:, c0:c0 + LANE]
                o_ref[:, slab * LANE:(slab + 1) * LANE] = scr_ref[slab].astype(o_ref.dtype)

    @pl.when(i >= nl)
    def _():
        o_ref[...] = jnp.zeros_like(o_ref)


def hyena_gate_eo(u3, conv, z1, bias_u, L, B, tl, T, row0, into):
    nl = L // tl
    th = tl // 2
    ci = lambda i: jnp.minimum(i, nl - 1)
    in_specs = [pl.BlockSpec((None, th, 1024), lambda i, b: (2, ci(i), b)),
                pl.BlockSpec((th, 1024), lambda i, b: (ci(i), b)),
                pl.BlockSpec((None, th, 1024), lambda i, b: (0, ci(i), b)),
                pl.BlockSpec((None, 1, 1024), lambda i, b: (HY_ORDER - 1, 0, 0))]
    args = [u3, conv, z1, bias_u]
    if into is not None:
        n_fill, aliases = 0, {4: 0}
        in_specs.append(pl.BlockSpec(memory_space=pl.ANY))
        args.append(into)
        out_spec = pl.BlockSpec((tl, 512), lambda i, b: (row0 // tl + b * nl + i, 0))
    else:
        assert row0 == 0
        n_fill, aliases = (T - B * L) // tl, {}
        out_spec = pl.BlockSpec((tl, 512), lambda i, b: (jnp.where(i < nl, b * nl + i, B * nl + (i - nl)), 0))
    return pl.pallas_call(
        functools.partial(_gate_eo_kernel, nl=nl), grid=(nl + n_fill, B), in_specs=in_specs, out_specs=out_spec,
        out_shape=jax.ShapeDtypeStruct((T, 512), BF16), input_output_aliases=aliases,
        scratch_shapes=[pltpu.VMEM((512 // LANE, tl, LANE), F32)],
        compiler_params=_cp(("arbitrary", "arbitrary")), name="hy_gate",
    )(*args)


def _hyena_consts(L):
    H = L // 2
    tb = min(H, ROW_TILE)
    fwd, inv = _dft_matrices(H, tb)
    w = (2 * jnp.arange(H, dtype=F32)[:, None] + 1.0) * (math.pi / (2 * L))
    twc, tws = jnp.broadcast_to(jnp.cos(w), (H, CG)), jnp.broadcast_to(jnp.sin(w), (H, CG))
    return fwd, inv, twc, tws, _filter_feats(L)


def hyena_seq_eo(pa, L, row_block0, B, consts, wts, T, into=None):
    fwd, inv, twc, tws, feats = consts
    short_w, w1p, b1, w2, b2, fr, w3, deltas, bias_u = wts
    H = L // 2
    tb = min(H, ROW_TILE)
    nkb = H // tb
    tm_i = min(H, 512)
    ng = 512 // CG
    NW = B * 1024
    u3 = short_conv_eo(pa, short_w, L, row_block0, B)
    filt = hyena_filters_eo(feats, w1p, b1, w2, b2, fr, w3, deltas, L)
    zeros_f = jnp.zeros((nkb,), jnp.int32)
    zeros_i = jnp.zeros((H // tm_i,), jnp.int32)
    tw = [(twc, (tb, CG), lambda n, m: (m, 0)), (tws, (tb, CG), lambda n, m: (m, 0))]
    gw = ng * 4 * CG
    gs = gmm(fwd, filt, zeros_f, tm=2 * tb, tn=gw, n_tiles=HY_ORDER, out_dtype=F32,
             epilogue=_filter_spec_epilogue, extra=tw, name="hy_dft_filter")
    z_arr, which_z = u3, 0
    out = None
    for o in range(HY_ORDER):
        g_blk = lambda comp, o=o: (gs, (tb, gw), lambda n, m: (2 * m + comp, o))
        y = gmm(fwd, z_arr, zeros_f + which_z, tm=2 * tb, tn=ng * 2 * CG, n_tiles=B, out_dtype=BF16,
                epilogue=functools.partial(_signal_spec_epilogue, scale=1.0 / L),
                extra=tw + [g_blk(0), g_blk(1)], name="hy_dft_fwd")
        y = y.reshape(1, 2 * H, NW)
        if o < HY_ORDER - 1:
            out = gmm(inv, y, zeros_i, tm=tm_i, tn=2 * CG, n_tiles=B * ng, out_dtype=F32, epilogue=_gated,
                      extra=[(u3, (None, tm_i, 2 * CG), functools.partial(lambda o, n, m: (1 + o, m, n), o)),
                             (z_arr, (None, tm_i, 2 * CG), functools.partial(lambda w, n, m: (w, m, n), which_z)),
                             (bias_u, (None, 1, 2 * CG), functools.partial(lambda o, n, m: (o, 0, n % ng), o))],
                      name="hy_dft_inv_gate").reshape(1, H, NW)
        else:
            conv = gmm(inv, y, zeros_i, tm=tm_i, tn=2 * CG, n_tiles=B * ng, out_dtype=F32, name="hy_dft_inv")
            out = hyena_gate_eo(u3, conv, z_arr, bias_u, L, B, min(L, 512), T, row_block0 * L, into)
        z_arr, which_z = out, 0
    return out


def _gather_kernel(idx_ref, src_ref, *refs, tm, k):
    if k > 1:
        w_ref, o_ref, buf_ref, sem = refs
    else:
        o_ref, buf_ref, sem = refs
    i, n = pl.program_id(0), pl.num_programs(0)
    M = n * tm

    def issue_tile(tile, slot):
        for j in range(k):
            def issue(r, carry):
                pltpu.make_async_copy(src_ref.at[pl.ds(idx_ref[j * M + tile * tm + r], 1)],
                                      buf_ref.at[slot, j, pl.ds(r, 1)], sem.at[slot]).start()
                return carry

            lax.fori_loop(0, tm, issue, 0, unroll=8)

    @pl.when(i == 0)
    def _():
        issue_tile(0, 0)

    @pl.when(i + 1 < n)
    def _():
        issue_tile(i + 1, (i + 1) % 2)

    slot = i % 2
    for j in range(k):
        pltpu.make_async_copy(src_ref.at[pl.ds(0, tm)], buf_ref.at[slot, j], sem.at[slot]).wait()
    if k > 1:
        w = w_ref[...]
        acc = w[:, 0:1] * buf_ref[slot, 0]
        for j in range(1, k):
            acc = acc + w[:, j:j + 1] * buf_ref[slot, j]
        o_ref[...] = acc.astype(o_ref.dtype)
    else:
        o_ref[...] = buf_ref[slot, 0].astype(o_ref.dtype)


def gather_rows(src, idx, tm, out_dtype, weights=None, k=1):
    M = idx.shape[0] // k
    D = src.shape[1]
    in_specs = [pl.BlockSpec(memory_space=pl.ANY)]
    args = [src]
    if k > 1:
        in_specs.append(pl.BlockSpec((tm, LANE), lambda i, idx: (i, 0)))
        args.append(weights)
    return pl.pallas_call(
        functools.partial(_gather_kernel, tm=tm, k=k),
        grid_spec=pltpu.PrefetchScalarGridSpec(
            num_scalar_prefetch=1, grid=(M // tm,), in_specs=in_specs,
            out_specs=pl.BlockSpec((tm, D), lambda i, idx: (i, 0)),
            scratch_shapes=[pltpu.VMEM((2, k, tm, D), src.dtype), pltpu.SemaphoreType.DMA((2,))]),
        out_shape=jax.ShapeDtypeStruct((M, D), out_dtype),
        compiler_params=_cp(("arbitrary",)), name="gather_rows" if k == 1 else "gather_combine",
    )(idx, *args)


def moe_ffn(rows, h_f32, rw, ri, w_gate, w_up, w_down, layer_moe, tm=512):
    T, D = rows.T, rows.D
    E = w_gate.shape[2]
    e_flat = ri[:, :TOP_K].T.reshape(-1)
    onehot = (e_flat[:, None] == jnp.arange(N_EXPERTS, dtype=jnp.int32)[None, :]).astype(jnp.int32)
    csum = jnp.cumsum(onehot, axis=0)
    rank = jnp.sum(csum * onehot, axis=1) - 1
    counts = csum[-1]
    padded = ((counts + tm - 1) // tm) * tm
    ends = jnp.cumsum(padded)
    starts = ends - padded
    dest = starts[e_flat] + rank
    m_pad = TOP_K * T + N_EXPERTS * tm
    token = jnp.tile(jnp.arange(T, dtype=jnp.int32), TOP_K)
    src_tok = (jnp.arange(m_pad, dtype=jnp.int32) % T).at[dest].set(token)
    tile_row = jnp.arange(m_pad // tm, dtype=jnp.int32) * tm
    gid = jnp.minimum(jnp.sum((tile_row[:, None] >= ends[None, :]).astype(jnp.int32), axis=1), N_EXPERTS - 1)
    gid = gid + layer_moe * N_EXPERTS
    used = (ends[-1:] // tm).astype(jnp.int32)

    xs = gather_rows(h_f32, src_tok, tm, BF16)
    tile = lambda n, m: (m, n)
    g = gmm(xs, w_gate, gid, tm=tm, tn=E // 2, n_tiles=2, out_dtype=BF16, epilogue=_silu, used=used,
            name="moe_gate")
    hh = gmm(xs, w_up, gid, tm=tm, tn=E // 2, n_tiles=2, out_dtype=BF16, epilogue=_times,
             extra=[(g, (tm, E // 2), tile)], used=used, name="moe_up")
    y = gmm(hh, w_down, gid, tm=tm, tn=D // 2, n_tiles=2, out_dtype=F32, used=used, name="moe_down")
    return gather_rows(y, dest, ROW_TILE, BF16, weights=rw, k=TOP_K)


def kernel(x, c, ctx, c_ctx, w_mod, b_mod, g_mix_pre, g_mix_post, g_ffn_pre, g_ffn_post, w_in, w_out, attn_sink,
           hy_short, hy_w1, hy_b1, hy_w2, hy_b2, hy_freq, hy_w3, hy_bias, ml_gate_bias, ml_norm, mla_q_norm,
           mla_w_uq, mla_kv_norm, mla_w_ukv, ffn_w_gate, ffn_w_up, ffn_w_down, moe_router, moe_w_gate, moe_w_up,
           moe_w_down):
    B, S, D = x.shape
    NC = ctx.shape[1]
    depth = w_mod.shape[0]
    rows = _Rows(B, S, NC, D)
    T = rows.T
    TM = 512 if T % 512 == 0 else ROW_TILE
    n_mt = T // TM
    ffn_dim = ffn_w_gate.shape[2]

    xs = jnp.concatenate([x.reshape(B * S, D), ctx.reshape(B * NC, D)], axis=0)

    cm = jnp.concatenate([c, c_ctx[None, :]], axis=0)
    cm = jnp.pad(jax.nn.silu(cm), ((0, 16 - (B + 1)), (0, 0))).astype(BF16)
    mod_all = gmm(jnp.tile(cm, (depth, 1)), w_mod, jnp.arange(depth, dtype=jnp.int32), tm=16, tn=1536,
                  n_tiles=N_MOD * D // 1536, out_dtype=F32, name="adaln")
    mods = (mod_all.reshape(depth, 16, N_MOD * D)[:, :B + 1] + b_mod[:, None, :]).reshape(depth, B + 1, N_MOD, 1, D)

    def rope_table(rot_dim, reps, lane0, width):
        cos, sin = _axial_tables(S, rot_dim)
        cos, sin = jnp.tile(cos, (1, reps)), jnp.tile(sin, (1, reps))
        padw = ((0, 0), (lane0, width - lane0 - cos.shape[1]))
        cos = jnp.pad(cos, padw, constant_values=1.0)
        sin = jnp.pad(sin, padw)
        cos = jnp.concatenate([jnp.tile(cos, (B, 1)), jnp.ones((B * NC, width), F32)], axis=0)
        sin = jnp.concatenate([jnp.tile(sin, (B, 1)), jnp.zeros((B * NC, width), F32)], axis=0)
        return cos, sin

    cos_a, sin_a = rope_table(64, 2, 0, LANE)
    cos_m, sin_m = rope_table(MLA_ROPE, 1, MLA_NOPE, LANE)
    hy_consts = {L: _hyena_consts(L) for L in (S, NC)}
    deltas = jnp.abs(jnp.linspace(math.log(HY_DECAY_TARGET) / HY_SLOW_DECAY,
                                  math.log(HY_DECAY_TARGET) / HY_FAST_DECAY, 512, dtype=F32))[None, :]

    w_in_t = jnp.swapaxes(w_in, 1, 2)
    wq = mla_w_uq.reshape(depth, -1, MLA_HEADS, MLA_NOPE + MLA_ROPE)
    wq = jnp.pad(wq, ((0, 0), (0, 0), (0, 0), (0, LANE - MLA_NOPE - MLA_ROPE))).reshape(depth, -1, MLA_HEADS * LANE)
    wkv = mla_w_ukv.reshape(depth, -1, MLA_HEADS, 2 * 64)
    wk = jnp.pad(wkv[..., :MLA_NOPE], ((0, 0), (0, 0), (0, 0), (0, LANE - MLA_NOPE)))
    wk = wk.reshape(depth, -1, MLA_HEADS * LANE)
    wv = wkv[..., MLA_NOPE:].reshape(depth, -1, MLA_HEADS * 64)
    w1p = jnp.pad(hy_w1, ((0, 0), (0, LANE - hy_w1.shape[1]), (0, 0)))
    bias_u = jnp.broadcast_to(hy_bias.reshape(depth, HY_ORDER, 512 // CG, 1, CG),
                              (depth, HY_ORDER, 512 // CG, 2, CG)).reshape(depth, HY_ORDER, 1, 1024)
    gbias = jnp.pad(ml_gate_bias.reshape(depth, 1, 4 * ML_HEADS), ((0, 0), (0, 0), (0, LANE - 4 * ML_HEADS)))
    router = jnp.pad(moe_router, ((0, 0), (0, 0), (0, LANE - N_EXPERTS)))
    n_moe = moe_w_gate.shape[0]
    mw_gate = moe_w_gate.reshape(n_moe * N_EXPERTS, D, -1)
    mw_up = moe_w_up.reshape(n_moe * N_EXPERTS, D, -1)
    mw_down = moe_w_down.reshape(n_moe * N_EXPERTS, -1, D)
    vec = lambda a, l: a[l][None, :]

    h = norm_mod(rows, xs, vec(g_mix_pre, 0), mods[0], 0, 1)
    for layer in range(depth):
        gid = jnp.full((n_mt,), layer, jnp.int32)
        ml = mods[layer]
        p1 = gmm(h, w_in_t, gid, tm=TM, tn=1280, n_tiles=PA_COLS // 1280, out_dtype=BF16, w_t=True, name="proj_in")
        p2 = gmm(h, w_in_t, gid, tm=TM, tn=768, n_tiles=2, n_off=PA_COLS // 768, out_dtype=F32, w_t=True,
                 name="proj_in_f32")

        qkv = attn_rope(rows, p1, cos_a, sin_a)
        a_all = window_attn(rows, qkv, attn_sink[layer])

        hy_w = (hy_short[layer], w1p[layer], vec(hy_b1, layer), hy_w2[layer], vec(hy_b2, layer),
                vec(hy_freq, layer), hy_w3[layer], deltas, bias_u[layer])
        b_l = hyena_seq_eo(p1, S, 0, B, hy_consts[S], hy_w, T)
        b_all = hyena_seq_eo(p1, NC, B * S // NC, B, hy_consts[NC], hy_w, T, into=b_l)

        hf, hb = mlstm_scan(rows, p1, p2, gbias[layer])
        m_all = mlstm_finish(rows, hf, hb, p2, vec(ml_norm, layer))

        qn, kvn, kr = mla_norms(rows, p2, vec(mla_q_norm, layer), vec(mla_kv_norm, layer))
        slab = lambda a: (a, (TM, LANE), lambda n, m: (m, 0))
        qm = gmm(qn, wq, gid, tm=TM, tn=MLA_HEADS * LANE, n_tiles=1, out_dtype=BF16, epilogue=_mla_q_epilogue,
                 extra=[slab(cos_m), slab(sin_m)], name="mla_uq")
        km = gmm(kvn, wk, gid, tm=TM, tn=MLA_HEADS * LANE, n_tiles=1, out_dtype=BF16, epilogue=_mla_k_epilogue,
                 extra=[slab(kr), slab(cos_m), slab(sin_m)], name="mla_uk")
        vm = gmm(kvn, wv, gid, tm=TM, tn=MLA_HEADS * 64, n_tiles=1, out_dtype=BF16, name="mla_uv")
        d_all = mla_attn(rows, qm, km, vm)

        y = gmm([a_all, b_all, m_all, d_all], w_out, gid, tm=TM, tn=1024, n_tiles=D // 1024, out_dtype=BF16,
                name="proj_out")

        i = layer // 2
        nxt = (vec(g_ffn_pre, layer), ml, 3, 4)
        if layer % 2 == 0:
            xs, h2 = post(rows, xs, y, vec(g_mix_post, layer), ml, 2, nxt=nxt)
            gi = jnp.full((n_mt,), i, jnp.int32)
            g = gmm(h2, ffn_w_gate, gi, tm=TM, tn=ffn_dim // 2, n_tiles=2, out_dtype=BF16, epilogue=_silu,
                    w_single_buffer=True, name="ffn_gate")
            hh = gmm(h2, ffn_w_up, gi, tm=TM, tn=ffn_dim // 2, n_tiles=2, out_dtype=BF16, epilogue=_times,
                     extra=[(g, (TM, ffn_dim // 2), lambda n, m: (m, n))], w_single_buffer=True, name="ffn_up")
            y2 = gmm(hh, ffn_w_down, gi, tm=TM, tn=512, n_tiles=D // 512, out_dtype=BF16, name="ffn_down")
        else:
            xs, h2, h2f, rw, ri = post(rows, xs, y, vec(g_mix_post, layer), ml, 2, nxt=nxt, router=router[i])
            y2 = moe_ffn(rows, h2f, rw, ri, mw_gate, mw_up, mw_down, i)
        if layer + 1 < depth:
            xs, h = post(rows, xs, y2, vec(g_ffn_post, layer), ml, 5,
                         nxt=(vec(g_mix_pre, layer + 1), mods[layer + 1], 0, 1))
        else:
            (xs,) = post(rows, xs, y2, vec(g_ffn_post, layer), ml, 5, latent_only=True)
    return xs.reshape(B, S, D)
```

```python
import functools
import math

import numpy as np
import jax
import jax.numpy as jnp
from jax import lax
from jax.experimental import pallas as pl
from jax.experimental.pallas import tpu as pltpu

F32 = jnp.float32
BF16 = jnp.bfloat16
HI = lax.Precision.HIGHEST

EPS = 1e-6
ROPE_BASE = 10000.0
GRID_W = 64
BLOCK = 128
WINDOW = 128
N_MOD = 6
A_HEADS, A_KV_HEADS = 8, 2
HY_ORDER, HY_SHORT, HY_POS_FREQS = 2, 3, 8
HY_DECAY_TARGET, HY_FAST_DECAY, HY_SLOW_DECAY = 1e-2, 0.3, 1.5
ML_HEADS, ML_CHUNK = 4, 128
MLA_HEADS, MLA_NOPE, MLA_ROPE = 8, 64, 32
N_EXPERTS, TOP_K = 8, 2

PA_COLS = 3840
PB_GATES = 4352 - PA_COLS
MLA_HP = 8
LANE = 128
ROW_TILE = 256
NEG = -1e30
LOG2E = math.log2(math.e)
VMEM_LIMIT = 56 * 1024 * 1024


def _cp(sem, vmem=VMEM_LIMIT):
    return pltpu.CompilerParams(dimension_semantics=sem, vmem_limit_bytes=vmem)


def _gmm_kernel(gid_ref, used_ref, *refs, n_a, n_w, cast, w_t, epilogue, n_extra, w_n, tn, n_off):
    a_refs, w_refs, rest = refs[:n_a], refs[n_a:n_a + n_w], refs[n_a + n_w:]
    extra_refs, rest = rest[:n_extra], rest[n_extra:]
    o_ref = rest[0]
    n, m = pl.program_id(0), pl.program_id(1)
    n_axis = 0 if w_t else 1
    if cast:
        wbf_refs = rest[1:1 + n_w]
        prev = gid_ref[jnp.maximum(m - 1, 0)]

        @pl.when((m == 0) | (gid_ref[m] != prev))
        def _():
            for w_ref, wbf_ref in zip(w_refs, wbf_refs):
                w = w_ref[...]
                if w_n is not None:
                    col = (n + n_off) * tn + lax.broadcasted_iota(jnp.int32, w.shape, n_axis)
                    w = jnp.where(col < w_n, w, 0.0)
                wbf_ref[...] = w.astype(BF16)

        srcs = wbf_refs
    else:
        srcs = w_refs

    @pl.when(m < used_ref[0])
    def _():
        accs = []
        for src in srcs:
            acc, lo = None, 0
            for a_ref in a_refs:
                k = a_ref.shape[1]
                if w_t:
                    part = lax.dot_general(a_ref[...], src[:, lo:lo + k], (((1,), (1,)), ((), ())),
                                           preferred_element_type=F32)
                else:
                    part = jnp.dot(a_ref[...], src[lo:lo + k, :], preferred_element_type=F32)
                acc = part if acc is None else acc + part
                lo += k
            accs.append(acc)
        acc = epilogue(*accs, *extra_refs) if epilogue is not None else accs[0]
        o_ref[...] = acc.astype(o_ref.dtype)

    @pl.when(m >= used_ref[0])
    def _():
        o_ref[...] = jnp.zeros_like(o_ref)


def _silu(acc):
    return acc * jax.nn.sigmoid(acc)


def _times(acc, g_ref):
    return acc * g_ref[...].astype(F32)


def _swiglu(gate_acc, up_acc):
    return gate_acc * jax.nn.sigmoid(gate_acc) * up_acc


def gmm(a, w, gid, *, tm, tn, n_tiles, out_dtype, n_off=0, epilogue=None, extra=(), w_t=False, used=None,
        w_single_buffer=False, name="gmm"):
    a_list = list(a) if isinstance(a, (list, tuple)) else [a]
    w_list = list(w) if isinstance(w, (list, tuple)) else [w]
    w = w_list[0]
    M = a_list[0].shape[0]
    k_axis, n_axis = (2, 1) if w_t else (1, 2)
    a_cols = [w.shape[k_axis]] if len(a_list) == 1 else [p.shape[1] for p in a_list]
    K = sum(a_cols)
    mt = M // tm
    assert mt * tm == M and w.shape[k_axis] == K and a_list[0].shape[1] >= a_cols[0] and gid.shape == (mt,)
    cast = w.dtype != BF16
    partial_n = (n_off + n_tiles) * tn > w.shape[n_axis]
    assert cast or not partial_n
    if used is None:
        used = jnp.full((1,), mt, jnp.int32)
    in_specs = [pl.BlockSpec((tm, k), lambda n, m, g, u: (m, 0)) for k in a_cols]
    w_mode = dict(pipeline_mode=pl.Buffered(1)) if w_single_buffer else {}
    for _ in w_list:
        if w_t:
            in_specs.append(pl.BlockSpec((None, tn, K), lambda n, m, g, u: (g[m], n + n_off, 0), **w_mode))
        else:
            in_specs.append(pl.BlockSpec((None, K, tn), lambda n, m, g, u: (g[m], 0, n + n_off), **w_mode))
    args = a_list + w_list
    for arr, block, imap in extra:
        in_specs.append(pl.BlockSpec(block, functools.partial(lambda f, n, m, g, u: f(n, m), imap)))
        args.append(arr)
    kern = functools.partial(_gmm_kernel, n_a=len(a_list), n_w=len(w_list), cast=cast, w_t=w_t, epilogue=epilogue,
                             n_extra=len(extra), w_n=w.shape[n_axis] if partial_n else None, tn=tn, n_off=n_off)
    return pl.pallas_call(
        kern,
        grid_spec=pltpu.PrefetchScalarGridSpec(
            num_scalar_prefetch=2, grid=(n_tiles, mt), in_specs=in_specs,
            out_specs=pl.BlockSpec((tm, tn), lambda n, m, g, u: (m, n)),
            scratch_shapes=[pltpu.VMEM((tn, K) if w_t else (K, tn), BF16)] * len(w_list) if cast else []),
        out_shape=jax.ShapeDtypeStruct((M, n_tiles * tn), out_dtype),
        compiler_params=_cp(("arbitrary", "arbitrary")),
        name=name,
    )(gid, used, *args)


def _rms(v, g):
    return v * lax.rsqrt(jnp.mean(v * v, axis=-1, keepdims=True) + EPS) * g


def _norm_mod_kernel(x_ref, g_ref, sh_ref, sc_ref, h_ref):
    h_ref[...] = (_rms(x_ref[...], g_ref[...]) * (1.0 + sc_ref[...]) + sh_ref[...]).astype(h_ref.dtype)


def _post_kernel(*refs, with_next, with_router):
    x_ref, y_ref, gp_ref, gate_ref = refs[:4]
    refs = refs[4:]
    xn = x_ref[...] + gate_ref[...] * _rms(y_ref[...].astype(F32), gp_ref[...])
    if not with_next:
        refs[0][...] = xn
        return
    gn_ref, sh_ref, sc_ref = refs[:3]
    refs = refs[3:]
    if with_router:
        r_ref, refs = refs[0], refs[1:]
    refs[0][...] = xn
    h = _rms(xn, gn_ref[...]) * (1.0 + sc_ref[...]) + sh_ref[...]
    refs[1][...] = h.astype(BF16)
    if with_router:
        hf_ref, rw_ref, ri_ref = refs[2:5]
        hf_ref[...] = h
        logits = jnp.dot(h, r_ref[...], preferred_element_type=F32, precision=HI)
        lane = lax.broadcasted_iota(jnp.int32, logits.shape, 1)
        logits = jnp.where(lane < N_EXPERTS, logits, -jnp.inf)
        v1 = jnp.max(logits, axis=1, keepdims=True)
        i1 = jnp.min(jnp.where(logits == v1, lane, LANE), axis=1, keepdims=True)
        l2 = jnp.where(lane == i1, -jnp.inf, logits)
        v2 = jnp.max(l2, axis=1, keepdims=True)
        i2 = jnp.min(jnp.where(l2 == v2, lane, LANE), axis=1, keepdims=True)
        e = jnp.exp(v2 - v1)
        w1 = 1.0 / (1.0 + e)
        w2 = e / (1.0 + e)
        rw_ref[...] = jnp.where(lane == 0, w1, jnp.where(lane == 1, w2, 0.0))
        ri_ref[...] = jnp.where(lane == 0, i1, jnp.where(lane == 1, i2, 0))


class _Rows:
    def __init__(self, B, S, NC, D):
        self.B, self.S, self.NC, self.D = B, S, NC, D
        self.T = B * S + B * NC
        assert S % ROW_TILE == 0 and NC % ROW_TILE == 0
        self.nt = self.T // ROW_TILE

    def mod_row(self, i):
        n_lat = self.B * self.S // ROW_TILE
        return jnp.where(i < n_lat, i // (self.S // ROW_TILE), self.B)


def _row_spec(D):
    return pl.BlockSpec((ROW_TILE, D), lambda i: (i, 0))


def _vec_spec(D):
    return pl.BlockSpec((1, D), lambda i: (0, 0))


def _mod_spec(rows, j):
    return pl.BlockSpec((None, None, 1, rows.D), lambda i: (rows.mod_row(i), j, 0, 0))


def norm_mod(rows, x, g, mods, j_shift, j_scale):
    D = rows.D
    return pl.pallas_call(
        _norm_mod_kernel, grid=(rows.nt,),
        in_specs=[_row_spec(D), _vec_spec(D), _mod_spec(rows, j_shift), _mod_spec(rows, j_scale)],
        out_specs=_row_spec(D),
        out_shape=jax.ShapeDtypeStruct((rows.T, D), BF16),
        compiler_params=_cp(("parallel",)), name="norm_mod",
    )(x, g, mods, mods)


def post(rows, x, y, g_post, mods, j_gate, nxt=None, router=None, latent_only=False):
    D = rows.D
    T = rows.B * rows.S if latent_only else rows.T
    in_specs = [_row_spec(D), _row_spec(D), _vec_spec(D), _mod_spec(rows, j_gate)]
    args = [x, y, g_post, mods]
    out_specs = [_row_spec(D)]
    out_shape = [jax.ShapeDtypeStruct((T, D), F32)]
    if nxt is not None:
        g_next, mods_next, j_shift, j_scale = nxt
        in_specs += [_vec_spec(D), _mod_spec(rows, j_shift), _mod_spec(rows, j_scale)]
        args += [g_next, mods_next, mods_next]
        out_specs.append(_row_spec(D))
        out_shape.append(jax.ShapeDtypeStruct((T, D), BF16))
        if router is not None:
            in_specs.append(pl.BlockSpec((D, LANE), lambda i: (0, 0)))
            args.append(router)
            out_specs += [_row_spec(D), _row_spec(LANE), _row_spec(LANE)]
            out_shape += [jax.ShapeDtypeStruct((T, D), F32), jax.ShapeDtypeStruct((T, LANE), F32),
                          jax.ShapeDtypeStruct((T, LANE), jnp.int32)]
    kern = functools.partial(_post_kernel, with_next=nxt is not None, with_router=router is not None)
    return pl.pallas_call(
        kern, grid=(T // ROW_TILE,), in_specs=in_specs, out_specs=out_specs, out_shape=out_shape,
        compiler_params=_cp(("parallel",)), name="post",
    )(*args)


def _rot_pairs(x, q):
    lane = lax.broadcasted_iota(jnp.int32, x.shape, 1)
    n = x.shape[1]
    return jnp.where(lane % (2 * q) < q, -pltpu.roll(x, n - q, 1), pltpu.roll(x, q, 1))


def _axial_tables(S, rot_dim):
    rows = S // GRID_W
    r = jnp.repeat(jnp.arange(rows, dtype=F32), GRID_W)
    col = jnp.tile(jnp.arange(GRID_W, dtype=F32), rows)
    half = rot_dim // 2
    inv = ROPE_BASE ** (-jnp.arange(0, half, 2, dtype=F32) / half)
    ar, ac = r[:, None] * inv, col[:, None] * inv
    ang = jnp.concatenate([ar, ar, ac, ac], axis=-1)
    return jnp.cos(ang), jnp.sin(ang)


def _attn_rope_kernel(p_ref, cos_ref, sin_ref, o_ref, *, scale):
    cos, sin = cos_ref[...], sin_ref[...]
    nq = A_HEADS * 64 // LANE
    for s in range(nq + 1):
        xs = p_ref[:, s * LANE:(s + 1) * LANE].astype(F32)
        r = xs * cos + _rot_pairs(xs, 16) * sin
        if s < nq:
            r = r * scale
        o_ref[:, s * LANE:(s + 1) * LANE] = r.astype(o_ref.dtype)
    o_ref[:, (nq + 1) * LANE:] = p_ref[:, (nq + 1) * LANE:].astype(o_ref.dtype)


def attn_rope(rows, p1, cos, sin):
    W = 768
    return pl.pallas_call(
        functools.partial(_attn_rope_kernel, scale=64 ** -0.5 * LOG2E), grid=(rows.nt,),
        in_specs=[pl.BlockSpec((ROW_TILE, W), lambda i: (i, 0)), _row_spec(LANE), _row_spec(LANE)],
        out_specs=pl.BlockSpec((ROW_TILE, W), lambda i: (i, 0)),
        out_shape=jax.ShapeDtypeStruct((rows.T, W), BF16),
        compiler_params=_cp(("parallel",)), name="attn_rope",
    )(p1, cos, sin)


def _nt(a, b):
    return lax.dot_general(a, b, (((1,), (1,)), ((), ())), preferred_element_type=F32)


def _softmax_av(scores, values, sink=None):
    m = scores[0].max(axis=1, keepdims=True)
    for s in scores[1:]:
        m = jnp.maximum(m, s.max(axis=1, keepdims=True))
    if sink is not None:
        m = jnp.maximum(m, sink)
    den = jnp.exp2(sink - m) if sink is not None else 0.0
    acc = None
    for s, v in zip(scores, values):
        p = jnp.exp2(s - m)
        den = den + p.sum(axis=1, keepdims=True)
        pv = jnp.dot(p.astype(BF16), v, preferred_element_type=F32)
        acc = pv if acc is None else acc + pv
    return acc / den


def _window_kernel(sink_ref, q_ref, kp_ref, kc_ref, kn_ref, vp_ref, vc_ref, vn_ref, kx_ref, vx_ref, o_ref, *, nb):
    n = pl.program_id(1)
    groups = A_HEADS // A_KV_HEADS
    d = 64
    kb = jnp.concatenate([kp_ref[...], kc_ref[...], kn_ref[...]], axis=0)
    vb = jnp.concatenate([vp_ref[...], vc_ref[...], vn_ref[...]], axis=0)
    Q = groups * BLOCK
    qi = lax.broadcasted_iota(jnp.int32, (Q, 3 * BLOCK), 0) % BLOCK
    kj = lax.broadcasted_iota(jnp.int32, (Q, 3 * BLOCK), 1)
    k_abs = (n - 1) * BLOCK + kj
    valid = (jnp.abs(kj - BLOCK - qi) <= WINDOW) & (k_abs >= 0) & (k_abs < nb * BLOCK) & (n < nb)
    hrow = lax.broadcasted_iota(jnp.int32, (Q, 1), 0) // BLOCK
    outs = []
    for g in range(A_KV_HEADS):
        qg = jnp.concatenate([q_ref[:, (g * groups + h) * d:(g * groups + h + 1) * d] for h in range(groups)], axis=0)
        ksl = slice(g * d, (g + 1) * d)
        s_band = jnp.where(valid, _nt(qg, kb[:, ksl]), NEG)
        s_ctx = _nt(qg, kx_ref[:, ksl])
        sink = jnp.zeros((Q, 1), F32)
        for h in range(groups):
            sink = jnp.where(hrow == h, sink_ref[g * groups + h] * LOG2E, sink)
        o = _softmax_av([s_band, s_ctx], [vb[:, ksl], vx_ref[:, ksl]], sink)
        outs += [o[h * BLOCK:(h + 1) * BLOCK] for h in range(groups)]
    o_ref[...] = jnp.concatenate(outs, axis=1).astype(o_ref.dtype)


def window_attn(rows, qkv, sink):
    B, S, NC = rows.B, rows.S, rows.NC
    nb, ncb = S // BLOCK, NC // BLOCK
    cb = B * S // NC

    def q_block(b, n):
        return jnp.where(n < nb, b * nb + n, B * nb + b * ncb + (n - nb))

    def kv_spec(col, off):
        return pl.BlockSpec((BLOCK, LANE), lambda b, n, s: (b * nb + jnp.clip(n + off, 0, nb - 1), col))

    in_specs = [pl.BlockSpec((BLOCK, 512), lambda b, n, s: (q_block(b, n), 0)),
                kv_spec(4, -1), kv_spec(4, 0), kv_spec(4, 1), kv_spec(5, -1), kv_spec(5, 0), kv_spec(5, 1),
                pl.BlockSpec((NC, LANE), lambda b, n, s: (cb + b, 4)),
                pl.BlockSpec((NC, LANE), lambda b, n, s: (cb + b, 5))]
    return pl.pallas_call(
        functools.partial(_window_kernel, nb=nb),
        grid_spec=pltpu.PrefetchScalarGridSpec(
            num_scalar_prefetch=1, grid=(B, nb + ncb), in_specs=in_specs,
            out_specs=pl.BlockSpec((BLOCK, 512), lambda b, n, s: (q_block(b, n), 0))),
        out_shape=jax.ShapeDtypeStruct((rows.T, 512), BF16),
        compiler_params=_cp(("parallel", "parallel")), name="window_attn",
    )(sink, *([qkv] * 9))


MLA_SCALE = (MLA_NOPE + MLA_ROPE) ** -0.5 * LOG2E


def _mla_q_epilogue(acc, cos_ref, sin_ref):
    cos, sin = cos_ref[...], sin_ref[...]
    slabs = [acc[:, h * LANE:(h + 1) * LANE] for h in range(MLA_HEADS)]
    return jnp.concatenate([(s * cos + _rot_pairs(s, 8) * sin) * MLA_SCALE for s in slabs], axis=1)


def _mla_k_epilogue(acc, kr_ref, cos_ref, sin_ref):
    kr = kr_ref[...]
    kr = kr * cos_ref[...] + _rot_pairs(kr, 8) * sin_ref[...]
    return jnp.concatenate([acc[:, h * LANE:(h + 1) * LANE] + kr for h in range(MLA_HEADS)], axis=1)


def _mla_kernel(q_ref, kl_ref, kx_ref, vl_ref, vx_ref, o_ref, *, nq):
    def run(with_latent_keys):
        outs = []
        for j in range(MLA_HP):
            sl, vs = slice(j * LANE, (j + 1) * LANE), slice(j * 64, (j + 1) * 64)
            q = q_ref[:, sl]
            scores, values = [_nt(q, kx_ref[:, sl])], [vx_ref[:, vs]]
            if with_latent_keys:
                scores.insert(0, _nt(q, kl_ref[:, sl]))
                values.insert(0, vl_ref[:, vs])
            outs.append(_softmax_av(scores, values))
        o_ref[...] = jnp.concatenate(outs, axis=1).astype(o_ref.dtype)

    i = pl.program_id(2)
    pl.when(i < nq)(lambda: run(True))
    pl.when(i >= nq)(lambda: run(False))


def mla_attn(rows, qm, km, vm, tq=256):
    B, S, NC = rows.B, rows.S, rows.NC
    cb = B * S // NC
    nq, nqc = S // tq, NC // tq
    kw, vw = MLA_HP * LANE, MLA_HP * 64

    def q_block(b, i):
        return jnp.where(i < nq, b * nq + i, B * nq + b * nqc + (i - nq))

    return pl.pallas_call(
        functools.partial(_mla_kernel, nq=nq), grid=(B, MLA_HEADS // MLA_HP, nq + nqc),
        in_specs=[pl.BlockSpec((tq, kw), lambda b, h, i: (q_block(b, i), h)),
                  pl.BlockSpec((S, kw), lambda b, h, i: (b, h)),
                  pl.BlockSpec((NC, kw), lambda b, h, i: (cb + b, h)),
                  pl.BlockSpec((S, vw), lambda b, h, i: (b, h)),
                  pl.BlockSpec((NC, vw), lambda b, h, i: (cb + b, h))],
        out_specs=pl.BlockSpec((tq, vw), lambda b, h, i: (q_block(b, i), h)),
        out_shape=jax.ShapeDtypeStruct((rows.T, MLA_HEADS * 64), BF16),
        compiler_params=_cp(("parallel", "parallel", "parallel")), name="mla_attn",
    )(qm, km, km, vm, vm)


def _mla_norms_kernel(x0_ref, x1_ref, x2_ref, x3_ref, gq_ref, gkv_ref, q_ref, kv_ref, kr_ref):
    x = jnp.concatenate([x0_ref[...], x1_ref[...], x2_ref[...], x3_ref[...]], axis=1)
    q_ref[...] = _rms(x[:, 16:528], gq_ref[...]).astype(q_ref.dtype)
    kv_ref[...] = _rms(x[:, 528:784], gkv_ref[...]).astype(kv_ref.dtype)
    z = lambda w: jnp.zeros((x.shape[0], w), F32)
    kr_ref[...] = jnp.concatenate([z(MLA_NOPE), x[:, 784:816], z(LANE - MLA_NOPE - MLA_ROPE)], axis=1)


def mla_norms(rows, p2, gq, gkv):
    T = rows.T
    return pl.pallas_call(
        _mla_norms_kernel, grid=(rows.nt,),
        in_specs=[pl.BlockSpec((ROW_TILE, 256), functools.partial(lambda j, i: (i, PB_GATES // 256 + j), j))
                  for j in range(4)] + [_vec_spec(512), _vec_spec(256)],
        out_specs=[_row_spec(512), _row_spec(256), _row_spec(LANE)],
        out_shape=[jax.ShapeDtypeStruct((T, 512), BF16), jax.ShapeDtypeStruct((T, 256), BF16),
                   jax.ShapeDtypeStruct((T, LANE), F32)],
        compiler_params=_cp(("parallel",)), name="mla_norms",
    )(p2, p2, p2, p2, gq, gkv)


def _bf16_pieces(x):
    hi = x.astype(BF16)
    r = x - hi.astype(F32)
    mid = r.astype(BF16)
    return hi, mid, (r - mid.astype(F32)).astype(BF16)


def _log_sigmoid(x):
    return jnp.minimum(x, 0.0) - jnp.log(1.0 + jnp.exp(-jnp.abs(x)))


def _mlstm_kernel(*refs, scale):
    nh = ML_HEADS
    per = 3 * nh + 1
    gb_ref, of_ref, ob_ref, c_ref, n_ref, m_ref = refs[2 * per:]

    @pl.when(pl.program_id(1) == 0)
    def _():
        c_ref[...] = jnp.zeros_like(c_ref)
        n_ref[...] = jnp.zeros_like(n_ref)
        m_ref[...] = jnp.zeros_like(m_ref)

    for d, o_ref in enumerate((of_ref, ob_ref)):
        r = refs[d * per:(d + 1) * per]
        _mlstm_chunk(r[0:nh], r[nh:2 * nh], r[2 * nh:3 * nh], r[3 * nh], gb_ref, o_ref,
                     c_ref.at[d], n_ref.at[d], m_ref.at[d], backward=d == 1, scale=scale)


def _mlstm_chunk(q_refs, k_refs, v_refs, g_ref, gb_ref, o_ref, c_ref, n_ref, m_ref, *, backward, scale):
    nh, L = ML_HEADS, ML_CHUNK
    t_i = lax.broadcasted_iota(jnp.int32, (L, L), 0)
    s_i = lax.broadcasted_iota(jnp.int32, (L, L), 1)
    mask = (s_i >= t_i) if backward else (s_i <= t_i)
    gb = g_ref[...] + gb_ref[...]
    ls = _log_sigmoid(gb)
    gbt, lst = gb.T, ls.T
    mask_b = mask.astype(BF16)
    cum_c = sum(jnp.dot(mask_b, p, preferred_element_type=F32) for p in _bf16_pieces(ls))
    cum_r = sum(jnp.dot(p, mask_b.T, preferred_element_type=F32) for p in _bf16_pieces(lst))
    last = 0 if backward else L - 1
    gi0, gf0 = (2 * nh, 3 * nh) if backward else (0, nh)
    for h in range(nh):
        gi, gf = gi0 + h, gf0 + h
        q = q_refs[h][...].astype(F32)
        kf = k_refs[h][...].astype(F32) * scale
        vf = v_refs[h][...].astype(F32)
        qb, kb = q.astype(BF16), kf.astype(BF16)
        cumc, cumr = cum_c[:, gf:gf + 1], cum_r[gf:gf + 1, :]
        li_r, li_c = gbt[gi:gi + 1, :], gb[:, gi:gi + 1]
        m11 = m_ref[h][0:1, 0:1]
        log_intra = jnp.where(mask, cumc - cumr + li_r, NEG)
        log_inter = cumc + m11
        m_t = jnp.maximum(log_inter, log_intra.max(axis=1, keepdims=True))
        w_inter = jnp.exp(log_inter - m_t)
        s = _nt(qb, kb) * jnp.exp(log_intra - m_t)
        cmat = c_ref[h]
        num = w_inter * _nt(qb, cmat.astype(BF16)) + jnp.dot(s.astype(BF16), vf.astype(BF16),
                                                             preferred_element_type=F32)
        den = w_inter * jnp.sum(q * n_ref[h][0:1, :], axis=1, keepdims=True) + s.sum(axis=1, keepdims=True)
        o_ref[:, h * L:(h + 1) * L] = num / jnp.maximum(jnp.abs(den), jnp.exp(-m_t))
        total = cumc[last:last + 1, :]
        log_w = total - cumc + li_c
        m_new = jnp.maximum(total + m11, log_w.max(axis=0, keepdims=True))
        decay = jnp.exp(total + m11 - m_new)
        w = jnp.exp(log_w - m_new)
        upd = lax.dot_general((vf * w).astype(BF16), kb, (((0,), (0,)), ((), ())), preferred_element_type=F32)
        c_ref[h] = decay * cmat + upd
        n_ref[h] = jnp.broadcast_to(decay * n_ref[h][0:1, :] + jnp.sum(w * kf, axis=0, keepdims=True), (8, L))
        m_ref[h] = jnp.broadcast_to(m_new, (8, LANE))


def mlstm_scan(rows, pa, pb, gbias):
    B, S, NC = rows.B, rows.S, rows.NC
    nh, L = ML_HEADS, ML_CHUNK
    ncx, ncl = NC // L, S // L
    first_ctx = B * S // L

    def row_block(backward, b, j):
        cx = (ncx - 1 - j) if backward else j
        cl = (ncl - 1 - (j - ncx)) if backward else (j - ncx)
        return jnp.where(j < ncx, first_ctx + b * ncx + cx, b * ncl + cl)

    def col_spec(backward, col):
        return pl.BlockSpec((L, LANE), lambda b, j: (row_block(backward, b, j), col))

    q0 = 2304 // LANE
    in_specs = []
    for backward in (False, True):
        in_specs += [col_spec(backward, q0 + c) for c in range(3 * nh)] + [col_spec(backward, PB_GATES // LANE)]
    in_specs.append(pl.BlockSpec((1, LANE), lambda b, j: (0, 0)))
    out = jax.ShapeDtypeStruct((rows.T, nh * L), F32)
    return pl.pallas_call(
        functools.partial(_mlstm_kernel, scale=L ** -0.5),
        grid=(B, ncx + ncl), in_specs=in_specs,
        out_specs=[pl.BlockSpec((L, nh * L), functools.partial(lambda bw, b, j: (row_block(bw, b, j), 0), bw))
                   for bw in (False, True)],
        out_shape=[out, out],
        scratch_shapes=[pltpu.VMEM((2, nh, L, L), F32), pltpu.VMEM((2, nh, 8, L), F32),
                        pltpu.VMEM((2, nh, 8, LANE), F32)],
        compiler_params=_cp(("parallel", "arbitrary")), name="mlstm_scan",
    )(*(([pa] * (3 * nh) + [pb]) * 2), gbias)


def _mlstm_finish_kernel(hf_ref, hb_ref, *refs):
    o_refs, g_ref, out_ref = refs[:ML_HEADS], refs[ML_HEADS], refs[ML_HEADS + 1]
    for h in range(ML_HEADS):
        sl = slice(h * ML_CHUNK, (h + 1) * ML_CHUNK)
        hn = _rms(hf_ref[:, sl] + hb_ref[:, sl], g_ref[:, sl])
        out_ref[:, sl] = (jax.nn.sigmoid(o_refs[h][...]) * hn).astype(out_ref.dtype)


def mlstm_finish(rows, hf, hb, p1, gnorm):
    W = ML_HEADS * ML_CHUNK
    o_specs = [pl.BlockSpec((ROW_TILE, LANE), functools.partial(lambda h, i: (i, h), h))
               for h in range(ML_HEADS)]
    return pl.pallas_call(
        _mlstm_finish_kernel, grid=(rows.nt,),
        in_specs=[_row_spec(W), _row_spec(W)] + o_specs + [_vec_spec(W)],
        out_specs=_row_spec(W),
        out_shape=jax.ShapeDtypeStruct((rows.T, W), BF16),
        compiler_params=_cp(("parallel",)), name="mlstm_finish",
    )(hf, hb, *([p1] * ML_HEADS), gnorm)


def _dft_gen_kernel(ca_ref, sa_ref, cb_ref, sb_ref, ca2_ref, sa2_ref, cb2_ref, sb2_ref, fwd_ref, inv_ref):
    tb, L = cb_ref.shape
    ca, sa, cb, sb = ca_ref[...], sa_ref[...], cb_ref[...], sb_ref[...]
    fwd_ref[0] = (ca * cb - sa * sb).astype(fwd_ref.dtype)
    fwd_ref[1] = (-(sa * cb + ca * sb)).astype(fwd_ref.dtype)
    ca, sa, cb, sb = ca2_ref[...], sa2_ref[...], cb2_ref[...], sb2_ref[...]
    c, s = (ca * cb - sa * sb).astype(inv_ref.dtype), (-(sa * cb + ca * sb)).astype(inv_ref.dtype)
    for j in range(L // tb):
        inv_ref[:, 2 * j * tb:(2 * j + 1) * tb] = c[:, j * tb:(j + 1) * tb]
        inv_ref[:, (2 * j + 1) * tb:(2 * j + 2) * tb] = s[:, j * tb:(j + 1) * tb]


def _dft_matrices(L, tb):
    nb = L // tb
    col = jnp.arange(L, dtype=jnp.int32)[None, :]
    r0 = (jnp.arange(nb, dtype=jnp.int32) * tb)[:, None]
    i = jnp.arange(tb, dtype=jnp.int32)[:, None]
    unit = math.pi / (2 * L)
    trig = lambda ph: (jnp.cos((ph % (4 * L)).astype(F32) * unit), jnp.sin((ph % (4 * L)).astype(F32) * unit))
    ca, sa = trig((2 * r0 + 1) * col)
    cb, sb = trig(2 * i * col)
    ca2, sa2 = trig((2 * col + 1) * r0)
    cb2, sb2 = trig((2 * col + 1) * i)
    blk = lambda: pl.BlockSpec((None, 1, L), lambda j: (j, 0, 0))
    shared = lambda: pl.BlockSpec((tb, L), lambda j: (0, 0))
    a3 = lambda a: a[:, None, :]
    fwd, inv = pl.pallas_call(
        _dft_gen_kernel, grid=(nb,),
        in_specs=[blk(), blk(), shared(), shared(), blk(), blk(), shared(), shared()],
        out_specs=[pl.BlockSpec((None, 2, tb, L), lambda j: (j, 0, 0, 0)),
                   pl.BlockSpec((tb, 2 * L), lambda j: (j, 0))],
        out_shape=[jax.ShapeDtypeStruct((nb, 2, tb, L), BF16), jax.ShapeDtypeStruct((L, 2 * L), BF16)],
        compiler_params=_cp(("parallel",)), name="hy_dft_gen",
    )(a3(ca), a3(sa), cb, sb, a3(ca2), a3(sa2), cb2, sb2)
    return fwd.reshape(2 * L, L), inv


def _filter_feats(L):
    t = jnp.arange(L, dtype=F32) / L
    kf = jnp.arange(1, HY_POS_FREQS + 1, dtype=F32)
    ang = 2.0 * math.pi * t[:, None] * kf
    feats = jnp.concatenate([t[:, None], jnp.sin(ang), jnp.cos(ang)], axis=-1)
    return jnp.pad(feats, ((0, 0), (0, LANE - feats.shape[1])))


CG = 256


def _short_conv_eo_kernel(p_ref, w_ref, o_ref, scr_ref):
    x = p_ref[...].astype(F32)
    L = x.shape[0]
    row = lax.broadcasted_iota(jnp.int32, x.shape, 0)
    prev = jnp.where(row == 0, 0.0, pltpu.roll(x, 1, 0))
    nxt = jnp.where(row == L - 1, 0.0, pltpu.roll(x, L - 1, 0))
    u = w_ref[0:1, :] * prev + w_ref[1:2, :] * x + w_ref[2:3, :] * nxt
    for j in range(CG // LANE):
        scr_ref[j] = u[:, j * LANE:(j + 1) * LANE]
        for par in range(2):
            o_ref[:, par * CG + j * LANE:par * CG + (j + 1) * LANE] = scr_ref[j, pl.ds(par, L // 2, stride=2), :]


def short_conv_eo(pa, short_w, L, row_block0, B):
    ng = 512 // CG
    return pl.pallas_call(
        _short_conv_eo_kernel, grid=(B, 3, ng),
        in_specs=[pl.BlockSpec((L, CG), lambda b, w, g: (row_block0 + b, 768 // CG + w * ng + g)),
                  pl.BlockSpec((HY_SHORT, CG), lambda b, w, g: (0, w * ng + g))],
        out_specs=pl.BlockSpec((None, L // 2, 2 * CG), lambda b, w, g: (w, 0, b * ng + g)),
        out_shape=jax.ShapeDtypeStruct((3, L // 2, B * 1024), F32),
        scratch_shapes=[pltpu.VMEM((CG // LANE, L, LANE), F32)],
        compiler_params=_cp(("parallel", "parallel", "parallel")), name="hy_short_conv",
    )(pa, short_w)


def _filter_eo_kernel(feat_ref, w1_ref, b1_ref, w2_ref, b2_ref, fr_ref, w3f_ref, w3b_ref, dl_ref, o_ref, hid_ref,
                      sf_ref, sb_ref):
    L = feat_ref.shape[0]
    H = L // 2

    @pl.when((pl.program_id(0) == 0) & (pl.program_id(1) == 0))
    def _():
        fr = fr_ref[...]
        h1 = jnp.sin(fr * (jnp.dot(feat_ref[...], w1_ref[...], preferred_element_type=F32, precision=HI)
                           + b1_ref[...]))
        hid_ref[...] = jnp.sin(fr * (jnp.dot(h1, w2_ref[...], preferred_element_type=F32, precision=HI)
                                     + b2_ref[...]))

    hid = hid_ref[...]
    t = lax.broadcasted_iota(jnp.int32, (L, 1), 0).astype(F32) / L
    dec = jnp.exp(-t * dl_ref[...])
    hf = jnp.dot(hid, w3f_ref[...], preferred_element_type=F32, precision=HI) * dec
    hb = jnp.dot(hid, w3b_ref[...], preferred_element_type=F32, precision=HI) * dec
    hb = jnp.where(lax.broadcasted_iota(jnp.int32, hb.shape, 0) == 0, 0.0, hb)
    inv = 1.0 / (jnp.sum(jnp.abs(hf), axis=0, keepdims=True) + jnp.sum(jnp.abs(hb), axis=0, keepdims=True))
    for part, (s_ref, hv) in enumerate(((sf_ref, hf * inv), (sb_ref, hb * inv))):
        for j in range(CG // LANE):
            s_ref[j] = hv[:, j * LANE:(j + 1) * LANE]
            for par in range(2):
                c0 = (part * 2 + par) * CG + j * LANE
                o_ref[:, c0:c0 + LANE] = s_ref[j, pl.ds(par, H, stride=2), :].astype(o_ref.dtype)


def hyena_filters_eo(feats, w1p, b1, w2, b2, fr, w3, deltas, L):
    hid = w2.shape[0]
    ng = 512 // CG
    full = lambda shape: pl.BlockSpec(shape, lambda o, g: (0, 0))
    return pl.pallas_call(
        _filter_eo_kernel, grid=(HY_ORDER, ng),
        in_specs=[full((L, LANE)), full((LANE, hid)), full((1, hid)), full((hid, hid)), full((1, hid)),
                  full((1, hid)),
                  pl.BlockSpec((hid, CG), lambda o, g: (0, (o * 2) * ng + g)),
                  pl.BlockSpec((hid, CG), lambda o, g: (0, (o * 2 + 1) * ng + g)),
                  pl.BlockSpec((1, CG), lambda o, g: (0, g))],
        out_specs=pl.BlockSpec((None, L // 2, 4 * CG), lambda o, g: (0, 0, o * ng + g)),
        out_shape=jax.ShapeDtypeStruct((1, L // 2, HY_ORDER * ng * 4 * CG), BF16),
        scratch_shapes=[pltpu.VMEM((L, hid), F32), pltpu.VMEM((CG // LANE, L, LANE), F32),
                        pltpu.VMEM((CG // LANE, L, LANE), F32)],
        compiler_params=_cp(("arbitrary", "arbitrary")), name="hy_filters",
    )(feats, w1p, b1, w2, b2, fr, w3, w3, deltas)


def _split_eo(acc, c_ref, s_ref):
    tb = acc.shape[0] // 2
    c, s = c_ref[...], s_ref[...]
    ere, eim, ore, oim = acc[:tb, :CG], acc[tb:, :CG], acc[:tb, CG:], acc[tb:, CG:]
    tre, tim = c * ore + s * oim, c * oim - s * ore
    return (ere + tre, eim + tim), (ere - tre, tim - eim)


def _filter_spec_epilogue(acc, c_ref, s_ref):
    res, ims = [], []
    for j in range(acc.shape[1] // (2 * CG)):
        (lre, lim), (hre, him) = _split_eo(acc[:, j * 2 * CG:(j + 1) * 2 * CG], c_ref, s_ref)
        res += [lre, hre]
        ims += [lim, him]
    return jnp.concatenate([jnp.concatenate(res, axis=1), jnp.concatenate(ims, axis=1)], axis=0)


def _cmul(a, b):
    return a[0] * b[0] - a[1] * b[1], a[0] * b[1] + a[1] * b[0]


def _signal_spec_epilogue(acc, c_ref, s_ref, gre_ref, gim_ref, *, scale):
    c, s = c_ref[...], s_ref[...]
    res, ims = [], []
    for g in range(acc.shape[1] // (2 * CG)):
        x_lo, x_hi = _split_eo(acc[:, g * 2 * CG:(g + 1) * 2 * CG], c_ref, s_ref)
        f = lambda ref, j: ref[:, (g * 4 + j) * CG:(g * 4 + j + 1) * CG]
        g_lo = (f(gre_ref, 0) + f(gre_ref, 2), f(gim_ref, 0) - f(gim_ref, 2))
        g_hi = (f(gre_ref, 1) + f(gre_ref, 3), f(gim_ref, 1) - f(gim_ref, 3))
        p_lo, p_hi = _cmul(x_lo, g_lo), _cmul(x_hi, g_hi)
        dre, dim = p_lo[0] - p_hi[0], p_lo[1] + p_hi[1]
        res += [p_lo[0] + p_hi[0], dre * c - dim * s]
        ims += [p_lo[1] - p_hi[1], dre * s + dim * c]
    return jnp.concatenate([jnp.concatenate(res, axis=1), jnp.concatenate(ims, axis=1)], axis=0) * scale


def _gated(acc, x_ref, z_ref, b_ref):
    return x_ref[...] * (acc + b_ref[...] * z_ref[...])


def _gate_eo_kernel(x_ref, c_ref, z_ref, b_ref, *refs, nl):
    o_ref, scr_ref = refs[-2:]
    i = pl.program_id(0)

    @pl.when(i < nl)
    def _():
        r = x_ref[...] * (c_ref[...] + b_ref[...] * z_ref[...])
        half = r.shape[0]
        for g in range(512 // CG):
            for j in range(CG // LANE):
                slab = g * (CG // LANE) + j
                for par in range(2):
                    c0 = (g * 2 + par) * CG + j * LANE
                    scr_ref[slab, pl.ds(par, half, stride=2), :] = r[:, c0:c0 + LANE]
                o_ref[:, slab * LANE:(slab + 1) * LANE] = scr_ref[slab].astype(o_ref.dtype)

    @pl.when(i >= nl)
    def _():
        o_ref[...] = jnp.zeros_like(o_ref)


def hyena_gate_eo(u3, conv, z1, bias_u, L, B, tl, T, row0, into):
    nl = L // tl
    th = tl // 2
    ci = lambda i: jnp.minimum(i, nl - 1)
    in_specs = [pl.BlockSpec((None, th, 1024), lambda i, b: (2, ci(i), b)),
                pl.BlockSpec((th, 1024), lambda i, b: (ci(i), b)),
                pl.BlockSpec((None, th, 1024), lambda i, b: (0, ci(i), b)),
                pl.BlockSpec((None, 1, 1024), lambda i, b: (HY_ORDER - 1, 0, 0))]
    args = [u3, conv, z1, bias_u]
    if into is not None:
        n_fill, aliases = 0, {4: 0}
        in_specs.append(pl.BlockSpec(memory_space=pl.ANY))
        args.append(into)
        out_spec = pl.BlockSpec((tl, 512), lambda i, b: (row0 // tl + b * nl + i, 0))
    else:
        assert row0 == 0
        n_fill, aliases = (T - B * L) // tl, {}
        out_spec = pl.BlockSpec((tl, 512), lambda i, b: (jnp.where(i < nl, b * nl + i, B * nl + (i - nl)), 0))
    return pl.pallas_call(
        functools.partial(_gate_eo_kernel, nl=nl), grid=(nl + n_fill, B), in_specs=in_specs, out_specs=out_spec,
        out_shape=jax.ShapeDtypeStruct((T, 512), BF16), input_output_aliases=aliases,
        scratch_shapes=[pltpu.VMEM((512 // LANE, tl, LANE), F32)],
        compiler_params=_cp(("arbitrary", "arbitrary")), name="hy_gate",
    )(*args)


def _hyena_consts(L):
    H = L // 2
    tb = min(H, ROW_TILE)
    fwd, inv = _dft_matrices(H, tb)
    w = (2 * jnp.arange(H, dtype=F32)[:, None] + 1.0) * (math.pi / (2 * L))
    twc, tws = jnp.broadcast_to(jnp.cos(w), (H, CG)), jnp.broadcast_to(jnp.sin(w), (H, CG))
    return fwd, inv, twc, tws, _filter_feats(L)


def hyena_seq_eo(pa, L, row_block0, B, consts, wts, T, into=None):
    fwd, inv, twc, tws, feats = consts
    short_w, w1p, b1, w2, b2, fr, w3, deltas, bias_u = wts
    H = L // 2
    tb = min(H, ROW_TILE)
    nkb = H // tb
    tm_i = min(H, 512)
    ng = 512 // CG
    NW = B * 1024
    u3 = short_conv_eo(pa, short_w, L, row_block0, B)
    filt = hyena_filters_eo(feats, w1p, b1, w2, b2, fr, w3, deltas, L)
    zeros_f = jnp.zeros((nkb,), jnp.int32)
    zeros_i = jnp.zeros((H // tm_i,), jnp.int32)
    tw = [(twc, (tb, CG), lambda n, m: (m, 0)), (tws, (tb, CG), lambda n, m: (m, 0))]
    gw = ng * 4 * CG
    gs = gmm(fwd, filt, zeros_f, tm=2 * tb, tn=gw, n_tiles=HY_ORDER, out_dtype=F32,
             epilogue=_filter_spec_epilogue, extra=tw, name="hy_dft_filter")
    z_arr, which_z = u3, 0
    out = None
    for o in range(HY_ORDER):
        g_blk = lambda comp, o=o: (gs, (tb, gw), lambda n, m: (2 * m + comp, o))
        y = gmm(fwd, z_arr, zeros_f + which_z, tm=2 * tb, tn=ng * 2 * CG, n_tiles=B, out_dtype=BF16,
                epilogue=functools.partial(_signal_spec_epilogue, scale=1.0 / L),
                extra=tw + [g_blk(0), g_blk(1)], name="hy_dft_fwd")
        y = y.reshape(1, 2 * H, NW)
        if o < HY_ORDER - 1:
            out = gmm(inv, y, zeros_i, tm=tm_i, tn=2 * CG, n_tiles=B * ng, out_dtype=F32, epilogue=_gated,
                      extra=[(u3, (None, tm_i, 2 * CG), functools.partial(lambda o, n, m: (1 + o, m, n), o)),
                             (z_arr, (None, tm_i, 2 * CG), functools.partial(lambda w, n, m: (w, m, n), which_z)),
                             (bias_u, (None, 1, 2 * CG), functools.partial(lambda o, n, m: (o, 0, n % ng), o))],
                      name="hy_dft_inv_gate").reshape(1, H, NW)
        else:
            conv = gmm(inv, y, zeros_i, tm=tm_i, tn=2 * CG, n_tiles=B * ng, out_dtype=F32, name="hy_dft_inv")
            out = hyena_gate_eo(u3, conv, z_arr, bias_u, L, B, min(L, 512), T, row_block0 * L, into)
        z_arr, which_z = out, 0
    return out


def _gather_kernel(idx_ref, src_ref, *refs, tm, k):
    if k > 1:
        w_ref, o_ref, buf_ref, sem = refs
    else:
        o_ref, buf_ref, sem = refs
    i, n = pl.program_id(0), pl.num_programs(0)
    M = n * tm

    def issue_tile(tile, slot):
        for j in range(k):
            def issue(r, carry):
                pltpu.make_async_copy(src_ref.at[pl.ds(idx_ref[j * M + tile * tm + r], 1)],
                                      buf_ref.at[slot, j, pl.ds(r, 1)], sem.at[slot]).start()
                return carry

            lax.fori_loop(0, tm, issue, 0, unroll=8)

    @pl.when(i == 0)
    def _():
        issue_tile(0, 0)

    @pl.when(i + 1 < n)
    def _():
        issue_tile(i + 1, (i + 1) % 2)

    slot = i % 2
    for j in range(k):
        pltpu.make_async_copy(src_ref.at[pl.ds(0, tm)], buf_ref.at[slot, j], sem.at[slot]).wait()
    if k > 1:
        w = w_ref[...]
        acc = w[:, 0:1] * buf_ref[slot, 0]
        for j in range(1, k):
            acc = acc + w[:, j:j + 1] * buf_ref[slot, j]
        o_ref[...] = acc.astype(o_ref.dtype)
    else:
        o_ref[...] = buf_ref[slot, 0].astype(o_ref.dtype)


def gather_rows(src, idx, tm, out_dtype, weights=None, k=1):
    M = idx.shape[0] // k
    D = src.shape[1]
    in_specs = [pl.BlockSpec(memory_space=pl.ANY)]
    args = [src]
    if k > 1:
        in_specs.append(pl.BlockSpec((tm, LANE), lambda i, idx: (i, 0)))
        args.append(weights)
    return pl.pallas_call(
        functools.partial(_gather_kernel, tm=tm, k=k),
        grid_spec=pltpu.PrefetchScalarGridSpec(
            num_scalar_prefetch=1, grid=(M // tm,), in_specs=in_specs,
            out_specs=pl.BlockSpec((tm, D), lambda i, idx: (i, 0)),
            scratch_shapes=[pltpu.VMEM((2, k, tm, D), src.dtype), pltpu.SemaphoreType.DMA((2,))]),
        out_shape=jax.ShapeDtypeStruct((M, D), out_dtype),
        compiler_params=_cp(("arbitrary",)), name="gather_rows" if k == 1 else "gather_combine",
    )(idx, *args)


def moe_ffn(rows, h_f32, rw, ri, w_gate, w_up, w_down, layer_moe, tm=512):
    T, D = rows.T, rows.D
    E = w_gate.shape[2]
    e_flat = ri[:, :TOP_K].T.reshape(-1)
    onehot = (e_flat[:, None] == jnp.arange(N_EXPERTS, dtype=jnp.int32)[None, :]).astype(jnp.int32)
    csum = jnp.cumsum(onehot, axis=0)
    rank = jnp.sum(csum * onehot, axis=1) - 1
    counts = csum[-1]
    padded = ((counts + tm - 1) // tm) * tm
    ends = jnp.cumsum(padded)
    starts = ends - padded
    dest = starts[e_flat] + rank
    m_pad = TOP_K * T + N_EXPERTS * tm
    token = jnp.tile(jnp.arange(T, dtype=jnp.int32), TOP_K)
    src_tok = (jnp.arange(m_pad, dtype=jnp.int32) % T).at[dest].set(token)
    tile_row = jnp.arange(m_pad // tm, dtype=jnp.int32) * tm
    gid = jnp.minimum(jnp.sum((tile_row[:, None] >= ends[None, :]).astype(jnp.int32), axis=1), N_EXPERTS - 1)
    gid = gid + layer_moe * N_EXPERTS
    used = (ends[-1:] // tm).astype(jnp.int32)

    xs = gather_rows(h_f32, src_tok, tm, BF16)
    tn_h = 1024
    hh = gmm(xs, [w_gate, w_up], gid, tm=tm, tn=tn_h, n_tiles=-(-E // tn_h), out_dtype=BF16, epilogue=_swiglu,
             used=used, name="moe_gate_up")
    y = gmm(hh, w_down, gid, tm=tm, tn=D // 2, n_tiles=2, out_dtype=F32, used=used, name="moe_down")
    return gather_rows(y, dest, ROW_TILE, BF16, weights=rw, k=TOP_K)


def kernel(x, c, ctx, c_ctx, w_mod, b_mod, g_mix_pre, g_mix_post, g_ffn_pre, g_ffn_post, w_in, w_out, attn_sink,
           hy_short, hy_w1, hy_b1, hy_w2, hy_b2, hy_freq, hy_w3, hy_bias, ml_gate_bias, ml_norm, mla_q_norm,
           mla_w_uq, mla_kv_norm, mla_w_ukv, ffn_w_gate, ffn_w_up, ffn_w_down, moe_router, moe_w_gate, moe_w_up,
           moe_w_down):
    B, S, D = x.shape
    NC = ctx.shape[1]
    depth = w_mod.shape[0]
    rows = _Rows(B, S, NC, D)
    T = rows.T
    TM = 512 if T % 512 == 0 else ROW_TILE
    n_mt = T // TM
    ffn_dim = ffn_w_gate.shape[2]

    xs = jnp.concatenate([x.reshape(B * S, D), ctx.reshape(B * NC, D)], axis=0)

    cm = jnp.concatenate([c, c_ctx[None, :]], axis=0)
    cm = jnp.pad(jax.nn.silu(cm), ((0, 16 - (B + 1)), (0, 0))).astype(BF16)
    mod_all = gmm(jnp.tile(cm, (depth, 1)), w_mod, jnp.arange(depth, dtype=jnp.int32), tm=16, tn=1536,
                  n_tiles=N_MOD * D // 1536, out_dtype=F32, name="adaln")
    mods = (mod_all.reshape(depth, 16, N_MOD * D)[:, :B + 1] + b_mod[:, None, :]).reshape(depth, B + 1, N_MOD, 1, D)

    def rope_table(rot_dim, reps, lane0, width):
        cos, sin = _axial_tables(S, rot_dim)
        cos, sin = jnp.tile(cos, (1, reps)), jnp.tile(sin, (1, reps))
        padw = ((0, 0), (lane0, width - lane0 - cos.shape[1]))
        cos = jnp.pad(cos, padw, constant_values=1.0)
        sin = jnp.pad(sin, padw)
        cos = jnp.concatenate([jnp.tile(cos, (B, 1)), jnp.ones((B * NC, width), F32)], axis=0)
        sin = jnp.concatenate([jnp.tile(sin, (B, 1)), jnp.zeros((B * NC, width), F32)], axis=0)
        return cos, sin

    cos_a, sin_a = rope_table(64, 2, 0, LANE)
    cos_m, sin_m = rope_table(MLA_ROPE, 1, MLA_NOPE, LANE)
    hy_consts = {L: _hyena_consts(L) for L in (S, NC)}
    deltas = jnp.abs(jnp.linspace(math.log(HY_DECAY_TARGET) / HY_SLOW_DECAY,
                                  math.log(HY_DECAY_TARGET) / HY_FAST_DECAY, 512, dtype=F32))[None, :]

    w_in_t = jnp.swapaxes(w_in, 1, 2)
    wq = mla_w_uq.reshape(depth, -1, MLA_HEADS, MLA_NOPE + MLA_ROPE)
    wq = jnp.pad(wq, ((0, 0), (0, 0), (0, 0), (0, LANE - MLA_NOPE - MLA_ROPE))).reshape(depth, -1, MLA_HEADS * LANE)
    wkv = mla_w_ukv.reshape(depth, -1, MLA_HEADS, 2 * 64)
    wk = jnp.pad(wkv[..., :MLA_NOPE], ((0, 0), (0, 0), (0, 0), (0, LANE - MLA_NOPE)))
    wk = wk.reshape(depth, -1, MLA_HEADS * LANE)
    wv = wkv[..., MLA_NOPE:].reshape(depth, -1, MLA_HEADS * 64)
    w1p = jnp.pad(hy_w1, ((0, 0), (0, LANE - hy_w1.shape[1]), (0, 0)))
    bias_u = jnp.broadcast_to(hy_bias.reshape(depth, HY_ORDER, 512 // CG, 1, CG),
                              (depth, HY_ORDER, 512 // CG, 2, CG)).reshape(depth, HY_ORDER, 1, 1024)
    gbias = jnp.pad(ml_gate_bias.reshape(depth, 1, 4 * ML_HEADS), ((0, 0), (0, 0), (0, LANE - 4 * ML_HEADS)))
    router = jnp.pad(moe_router, ((0, 0), (0, 0), (0, LANE - N_EXPERTS)))
    n_moe = moe_w_gate.shape[0]
    mw_gate = moe_w_gate.reshape(n_moe * N_EXPERTS, D, -1)
    mw_up = moe_w_up.reshape(n_moe * N_EXPERTS, D, -1)
    mw_down = moe_w_down.reshape(n_moe * N_EXPERTS, -1, D)
    vec = lambda a, l: a[l][None, :]

    h = norm_mod(rows, xs, vec(g_mix_pre, 0), mods[0], 0, 1)
    for layer in range(depth):
        gid = jnp.full((n_mt,), layer, jnp.int32)
        ml = mods[layer]
        p1 = gmm(h, w_in_t, gid, tm=TM, tn=1280, n_tiles=PA_COLS // 1280, out_dtype=BF16, w_t=True, name="proj_in")
        p2 = gmm(h, w_in_t, gid, tm=TM, tn=768, n_tiles=2, n_off=PA_COLS // 768, out_dtype=F32, w_t=True,
                 name="proj_in_f32")

        qkv = attn_rope(rows, p1, cos_a, sin_a)
        a_all = window_attn(rows, qkv, attn_sink[layer])

        hy_w = (hy_short[layer], w1p[layer], vec(hy_b1, layer), hy_w2[layer], vec(hy_b2, layer),
                vec(hy_freq, layer), hy_w3[layer], deltas, bias_u[layer])
        b_l = hyena_seq_eo(p1, S, 0, B, hy_consts[S], hy_w, T)
        b_all = hyena_seq_eo(p1, NC, B * S // NC, B, hy_consts[NC], hy_w, T, into=b_l)

        hf, hb = mlstm_scan(rows, p1, p2, gbias[layer])
        m_all = mlstm_finish(rows, hf, hb, p2, vec(ml_norm, layer))

        qn, kvn, kr = mla_norms(rows, p2, vec(mla_q_norm, layer), vec(mla_kv_norm, layer))
        slab = lambda a: (a, (TM, LANE), lambda n, m: (m, 0))
        qm = gmm(qn, wq, gid, tm=TM, tn=MLA_HEADS * LANE, n_tiles=1, out_dtype=BF16, epilogue=_mla_q_epilogue,
                 extra=[slab(cos_m), slab(sin_m)], name="mla_uq")
        km = gmm(kvn, wk, gid, tm=TM, tn=MLA_HEADS * LANE, n_tiles=1, out_dtype=BF16, epilogue=_mla_k_epilogue,
                 extra=[slab(kr), slab(cos_m), slab(sin_m)], name="mla_uk")
        vm = gmm(kvn, wv, gid, tm=TM, tn=MLA_HEADS * 64, n_tiles=1, out_dtype=BF16, name="mla_uv")
        d_all = mla_attn(rows, qm, km, vm)

        y = gmm([a_all, b_all, m_all, d_all], w_out, gid, tm=TM, tn=1024, n_tiles=D // 1024, out_dtype=BF16,
                name="proj_out")

        i = layer // 2
        nxt = (vec(g_ffn_pre, layer), ml, 3, 4)
        if layer % 2 == 0:
            xs, h2 = post(rows, xs, y, vec(g_mix_post, layer), ml, 2, nxt=nxt)
            gi = jnp.full((n_mt,), i, jnp.int32)
            g = gmm(h2, ffn_w_gate, gi, tm=TM, tn=ffn_dim // 2, n_tiles=2, out_dtype=BF16, epilogue=_silu,
                    w_single_buffer=True, name="ffn_gate")
            hh = gmm(h2, ffn_w_up, gi, tm=TM, tn=ffn_dim // 2, n_tiles=2, out_dtype=BF16, epilogue=_times,
                     extra=[(g, (TM, ffn_dim // 2), lambda n, m: (m, n))], w_single_buffer=True, name="ffn_up")
            y2 = gmm(hh, ffn_w_down, gi, tm=TM, tn=512, n_tiles=D // 512, out_dtype=BF16, name="ffn_down")
        else:
            xs, h2, h2f, rw, ri = post(rows, xs, y, vec(g_mix_post, layer), ml, 2, nxt=nxt, router=router[i])
            y2 = moe_ffn(rows, h2f, rw, ri, mw_gate, mw_up, mw_down, i)
        if layer + 1 < depth:
            xs, h = post(rows, xs, y2, vec(g_ffn_post, layer), ml, 5,
                         nxt=(vec(g_mix_pre, layer + 1), mods[layer + 1], 0, 1))
        else:
            (xs,) = post(rows, xs, y2, vec(g_ffn_post, layer), ml, 5, latent_only=True)
    return xs.reshape(B, S, D)
```

```python
import functools
import math

import numpy as np
import jax
import jax.numpy as jnp
from jax import lax
from jax.experimental import pallas as pl
from jax.experimental.pallas import tpu as pltpu

F32 = jnp.float32
BF16 = jnp.bfloat16
HI = lax.Precision.HIGHEST

EPS = 1e-6
ROPE_BASE = 10000.0
GRID_W = 64
BLOCK = 128
WINDOW = 128
N_MOD = 6
A_HEADS, A_KV_HEADS = 8, 2
HY_ORDER, HY_SHORT, HY_POS_FREQS = 2, 3, 8
HY_DECAY_TARGET, HY_FAST_DECAY, HY_SLOW_DECAY = 1e-2, 0.3, 1.5
ML_HEADS, ML_CHUNK = 4, 128
MLA_HEADS, MLA_NOPE, MLA_ROPE = 8, 64, 32
N_EXPERTS, TOP_K = 8, 2

PA_COLS = 3840
PB_GATES = 4352 - PA_COLS
MLA_HP = 8
LANE = 128
ROW_TILE = 256
NEG = -1e30
LOG2E = math.log2(math.e)
VMEM_LIMIT = 56 * 1024 * 1024


def _cp(sem, vmem=VMEM_LIMIT):
    return pltpu.CompilerParams(dimension_semantics=sem, vmem_limit_bytes=vmem)


def _gmm_kernel(gid_ref, used_ref, *refs, n_a, n_w, cast, w_t, epilogue, n_extra, w_n, tn, n_off):
    a_refs, w_refs, rest = refs[:n_a], refs[n_a:n_a + n_w], refs[n_a + n_w:]
    extra_refs, rest = rest[:n_extra], rest[n_extra:]
    o_ref = rest[0]
    n, m = pl.program_id(0), pl.program_id(1)
    n_axis = 0 if w_t else 1
    if cast:
        wbf_refs = rest[1:1 + n_w]
        prev = gid_ref[jnp.maximum(m - 1, 0)]

        @pl.when((m == 0) | (gid_ref[m] != prev))
        def _():
            for w_ref, wbf_ref in zip(w_refs, wbf_refs):
                w = w_ref[...]
                if w_n is not None:
                    col = (n + n_off) * tn + lax.broadcasted_iota(jnp.int32, w.shape, n_axis)
                    w = jnp.where(col < w_n, w, 0.0)
                wbf_ref[...] = w.astype(BF16)

        srcs = wbf_refs
    else:
        srcs = w_refs

    @pl.when(m < used_ref[0])
    def _():
        accs = []
        for src in srcs:
            acc, lo = None, 0
            for a_ref in a_refs:
                k = a_ref.shape[1]
                if w_t:
                    part = lax.dot_general(a_ref[...], src[:, lo:lo + k], (((1,), (1,)), ((), ())),
                                           preferred_element_type=F32)
                else:
                    part = jnp.dot(a_ref[...], src[lo:lo + k, :], preferred_element_type=F32)
                acc = part if acc is None else acc + part
                lo += k
            accs.append(acc)
        acc = epilogue(*accs, *extra_refs) if epilogue is not None else accs[0]
        o_ref[...] = acc.astype(o_ref.dtype)

    @pl.when(m >= used_ref[0])
    def _():
        o_ref[...] = jnp.zeros_like(o_ref)


def _silu(acc):
    return acc * jax.nn.sigmoid(acc)


def _times(acc, g_ref):
    return acc * g_ref[...].astype(F32)


def _swiglu(gate_acc, up_acc):
    return gate_acc * jax.nn.sigmoid(gate_acc) * up_acc


def gmm(a, w, gid, *, tm, tn, n_tiles, out_dtype, n_off=0, epilogue=None, extra=(), w_t=False, used=None,
        w_single_buffer=False, name="gmm"):
    a_list = list(a) if isinstance(a, (list, tuple)) else [a]
    w_list = list(w) if isinstance(w, (list, tuple)) else [w]
    w = w_list[0]
    M = a_list[0].shape[0]
    k_axis, n_axis = (2, 1) if w_t else (1, 2)
    a_cols = [w.shape[k_axis]] if len(a_list) == 1 else [p.shape[1] for p in a_list]
    K = sum(a_cols)
    mt = M // tm
    assert mt * tm == M and w.shape[k_axis] == K and a_list[0].shape[1] >= a_cols[0] and gid.shape == (mt,)
    cast = w.dtype != BF16
    partial_n = (n_off + n_tiles) * tn > w.shape[n_axis]
    assert cast or not partial_n
    if used is None:
        used = jnp.full((1,), mt, jnp.int32)
    in_specs = [pl.BlockSpec((tm, k), lambda n, m, g, u: (m, 0)) for k in a_cols]
    w_mode = dict(pipeline_mode=pl.Buffered(1)) if w_single_buffer else {}
    for _ in w_list:
        if w_t:
            in_specs.append(pl.BlockSpec((None, tn, K), lambda n, m, g, u: (g[m], n + n_off, 0), **w_mode))
        else:
            in_specs.append(pl.BlockSpec((None, K, tn), lambda n, m, g, u: (g[m], 0, n + n_off), **w_mode))
    args = a_list + w_list
    for arr, block, imap in extra:
        in_specs.append(pl.BlockSpec(block, functools.partial(lambda f, n, m, g, u: f(n, m), imap)))
        args.append(arr)
    kern = functools.partial(_gmm_kernel, n_a=len(a_list), n_w=len(w_list), cast=cast, w_t=w_t, epilogue=epilogue,
                             n_extra=len(extra), w_n=w.shape[n_axis] if partial_n else None, tn=tn, n_off=n_off)
    return pl.pallas_call(
        kern,
        grid_spec=pltpu.PrefetchScalarGridSpec(
            num_scalar_prefetch=2, grid=(n_tiles, mt), in_specs=in_specs,
            out_specs=pl.BlockSpec((tm, tn), lambda n, m, g, u: (m, n)),
            scratch_shapes=[pltpu.VMEM((tn, K) if w_t else (K, tn), BF16)] * len(w_list) if cast else []),
        out_shape=jax.ShapeDtypeStruct((M, n_tiles * tn), out_dtype),
        compiler_params=_cp(("arbitrary", "arbitrary")),
        name=name,
    )(gid, used, *args)


def _rms(v, g):
    return v * lax.rsqrt(jnp.mean(v * v, axis=-1, keepdims=True) + EPS) * g


def _norm_mod_kernel(x_ref, g_ref, sh_ref, sc_ref, h_ref):
    h_ref[...] = (_rms(x_ref[...], g_ref[...]) * (1.0 + sc_ref[...]) + sh_ref[...]).astype(h_ref.dtype)


def _post_kernel(*refs, with_next, with_router):
    x_ref, y_ref, gp_ref, gate_ref = refs[:4]
    refs = refs[4:]
    xn = x_ref[...] + gate_ref[...] * _rms(y_ref[...].astype(F32), gp_ref[...])
    if not with_next:
        refs[0][...] = xn
        return
    gn_ref, sh_ref, sc_ref = refs[:3]
    refs = refs[3:]
    if with_router:
        r_ref, refs = refs[0], refs[1:]
    refs[0][...] = xn
    h = _rms(xn, gn_ref[...]) * (1.0 + sc_ref[...]) + sh_ref[...]
    refs[1][...] = h.astype(BF16)
    if with_router:
        hf_ref, rw_ref, ri_ref = refs[2:5]
        hf_ref[...] = h
        logits = jnp.dot(h, r_ref[...], preferred_element_type=F32, precision=HI)
        lane = lax.broadcasted_iota(jnp.int32, logits.shape, 1)
        logits = jnp.where(lane < N_EXPERTS, logits, -jnp.inf)
        v1 = jnp.max(logits, axis=1, keepdims=True)
        i1 = jnp.min(jnp.where(logits == v1, lane, LANE), axis=1, keepdims=True)
        l2 = jnp.where(lane == i1, -jnp.inf, logits)
        v2 = jnp.max(l2, axis=1, keepdims=True)
        i2 = jnp.min(jnp.where(l2 == v2, lane, LANE), axis=1, keepdims=True)
        e = jnp.exp(v2 - v1)
        w1 = 1.0 / (1.0 + e)
        w2 = e / (1.0 + e)
        rw_ref[...] = jnp.where(lane == 0, w1, jnp.where(lane == 1, w2, 0.0))
        ri_ref[...] = jnp.where(lane == 0, i1, jnp.where(lane == 1, i2, 0))


class _Rows:
    def __init__(self, B, S, NC, D):
        self.B, self.S, self.NC, self.D = B, S, NC, D
        self.T = B * S + B * NC
        assert S % ROW_TILE == 0 and NC % ROW_TILE == 0
        self.nt = self.T // ROW_TILE

    def mod_row(self, i):
        n_lat = self.B * self.S // ROW_TILE
        return jnp.where(i < n_lat, i // (self.S // ROW_TILE), self.B)


def _row_spec(D):
    return pl.BlockSpec((ROW_TILE, D), lambda i: (i, 0))


def _vec_spec(D):
    return pl.BlockSpec((1, D), lambda i: (0, 0))


def _mod_spec(rows, j):
    return pl.BlockSpec((None, None, 1, rows.D), lambda i: (rows.mod_row(i), j, 0, 0))


def norm_mod(rows, x, g, mods, j_shift, j_scale):
    D = rows.D
    return pl.pallas_call(
        _norm_mod_kernel, grid=(rows.nt,),
        in_specs=[_row_spec(D), _vec_spec(D), _mod_spec(rows, j_shift), _mod_spec(rows, j_scale)],
        out_specs=_row_spec(D),
        out_shape=jax.ShapeDtypeStruct((rows.T, D), BF16),
        compiler_params=_cp(("parallel",)), name="norm_mod",
    )(x, g, mods, mods)


def post(rows, x, y, g_post, mods, j_gate, nxt=None, router=None, latent_only=False):
    D = rows.D
    T = rows.B * rows.S if latent_only else rows.T
    in_specs = [_row_spec(D), _row_spec(D), _vec_spec(D), _mod_spec(rows, j_gate)]
    args = [x, y, g_post, mods]
    out_specs = [_row_spec(D)]
    out_shape = [jax.ShapeDtypeStruct((T, D), F32)]
    if nxt is not None:
        g_next, mods_next, j_shift, j_scale = nxt
        in_specs += [_vec_spec(D), _mod_spec(rows, j_shift), _mod_spec(rows, j_scale)]
        args += [g_next, mods_next, mods_next]
        out_specs.append(_row_spec(D))
        out_shape.append(jax.ShapeDtypeStruct((T, D), BF16))
        if router is not None:
            in_specs.append(pl.BlockSpec((D, LANE), lambda i: (0, 0)))
            args.append(router)
            out_specs += [_row_spec(D), _row_spec(LANE), _row_spec(LANE)]
            out_shape += [jax.ShapeDtypeStruct((T, D), F32), jax.ShapeDtypeStruct((T, LANE), F32),
                          jax.ShapeDtypeStruct((T, LANE), jnp.int32)]
    kern = functools.partial(_post_kernel, with_next=nxt is not None, with_router=router is not None)
    return pl.pallas_call(
        kern, grid=(T // ROW_TILE,), in_specs=in_specs, out_specs=out_specs, out_shape=out_shape,
        compiler_params=_cp(("parallel",)), name="post",
    )(*args)


def _rot_pairs(x, q):
    lane = lax.broadcasted_iota(jnp.int32, x.shape, 1)
    n = x.shape[1]
    return jnp.where(lane % (2 * q) < q, -pltpu.roll(x, n - q, 1), pltpu.roll(x, q, 1))


def _axial_tables(S, rot_dim):
    rows = S // GRID_W
    r = jnp.repeat(jnp.arange(rows, dtype=F32), GRID_W)
    col = jnp.tile(jnp.arange(GRID_W, dtype=F32), rows)
    half = rot_dim // 2
    inv = ROPE_BASE ** (-jnp.arange(0, half, 2, dtype=F32) / half)
    ar, ac = r[:, None] * inv, col[:, None] * inv
    ang = jnp.concatenate([ar, ar, ac, ac], axis=-1)
    return jnp.cos(ang), jnp.sin(ang)


def _attn_rope_kernel(p_ref, cos_ref, sin_ref, o_ref, *, scale):
    cos, sin = cos_ref[...], sin_ref[...]
    nq = A_HEADS * 64 // LANE
    for s in range(nq + 1):
        xs = p_ref[:, s * LANE:(s + 1) * LANE].astype(F32)
        r = xs * cos + _rot_pairs(xs, 16) * sin
        if s < nq:
            r = r * scale
        o_ref[:, s * LANE:(s + 1) * LANE] = r.astype(o_ref.dtype)
    o_ref[:, (nq + 1) * LANE:] = p_ref[:, (nq + 1) * LANE:].astype(o_ref.dtype)


def attn_rope(rows, p1, cos, sin):
    W = 768
    return pl.pallas_call(
        functools.partial(_attn_rope_kernel, scale=64 ** -0.5 * LOG2E), grid=(rows.nt,),
        in_specs=[pl.BlockSpec((ROW_TILE, W), lambda i: (i, 0)), _row_spec(LANE), _row_spec(LANE)],
        out_specs=pl.BlockSpec((ROW_TILE, W), lambda i: (i, 0)),
        out_shape=jax.ShapeDtypeStruct((rows.T, W), BF16),
        compiler_params=_cp(("parallel",)), name="attn_rope",
    )(p1, cos, sin)


def _nt(a, b):
    return lax.dot_general(a, b, (((1,), (1,)), ((), ())), preferred_element_type=F32)


def _softmax_av(scores, values, sink=None):
    m = scores[0].max(axis=1, keepdims=True)
    for s in scores[1:]:
        m = jnp.maximum(m, s.max(axis=1, keepdims=True))
    if sink is not None:
        m = jnp.maximum(m, sink)
    den = jnp.exp2(sink - m) if sink is not None else 0.0
    acc = None
    for s, v in zip(scores, values):
        p = jnp.exp2(s - m)
        den = den + p.sum(axis=1, keepdims=True)
        pv = jnp.dot(p.astype(BF16), v, preferred_element_type=F32)
        acc = pv if acc is None else acc + pv
    return acc / den


def _window_kernel(sink_ref, q_ref, kp_ref, kc_ref, kn_ref, vp_ref, vc_ref, vn_ref, kx_ref, vx_ref, o_ref, *, nb):
    n = pl.program_id(1)
    groups = A_HEADS // A_KV_HEADS
    d = 64
    kb = jnp.concatenate([kp_ref[...], kc_ref[...], kn_ref[...]], axis=0)
    vb = jnp.concatenate([vp_ref[...], vc_ref[...], vn_ref[...]], axis=0)
    Q = groups * BLOCK
    qi = lax.broadcasted_iota(jnp.int32, (Q, 3 * BLOCK), 0) % BLOCK
    kj = lax.broadcasted_iota(jnp.int32, (Q, 3 * BLOCK), 1)
    k_abs = (n - 1) * BLOCK + kj
    valid = (jnp.abs(kj - BLOCK - qi) <= WINDOW) & (k_abs >= 0) & (k_abs < nb * BLOCK) & (n < nb)
    hrow = lax.broadcasted_iota(jnp.int32, (Q, 1), 0) // BLOCK
    outs = []
    for g in range(A_KV_HEADS):
        qg = jnp.concatenate([q_ref[:, (g * groups + h) * d:(g * groups + h + 1) * d] for h in range(groups)], axis=0)
        ksl = slice(g * d, (g + 1) * d)
        s_band = jnp.where(valid, _nt(qg, kb[:, ksl]), NEG)
        s_ctx = _nt(qg, kx_ref[:, ksl])
        sink = jnp.zeros((Q, 1), F32)
        for h in range(groups):
            sink = jnp.where(hrow == h, sink_ref[g * groups + h] * LOG2E, sink)
        o = _softmax_av([s_band, s_ctx], [vb[:, ksl], vx_ref[:, ksl]], sink)
        outs += [o[h * BLOCK:(h + 1) * BLOCK] for h in range(groups)]
    o_ref[...] = jnp.concatenate(outs, axis=1).astype(o_ref.dtype)


def window_attn(rows, qkv, sink):
    B, S, NC = rows.B, rows.S, rows.NC
    nb, ncb = S // BLOCK, NC // BLOCK
    cb = B * S // NC

    def q_block(b, n):
        return jnp.where(n < nb, b * nb + n, B * nb + b * ncb + (n - nb))

    def kv_spec(col, off):
        return pl.BlockSpec((BLOCK, LANE), lambda b, n, s: (b * nb + jnp.clip(n + off, 0, nb - 1), col))

    in_specs = [pl.BlockSpec((BLOCK, 512), lambda b, n, s: (q_block(b, n), 0)),
                kv_spec(4, -1), kv_spec(4, 0), kv_spec(4, 1), kv_spec(5, -1), kv_spec(5, 0), kv_spec(5, 1),
                pl.BlockSpec((NC, LANE), lambda b, n, s: (cb + b, 4)),
                pl.BlockSpec((NC, LANE), lambda b, n, s: (cb + b, 5))]
    return pl.pallas_call(
        functools.partial(_window_kernel, nb=nb),
        grid_spec=pltpu.PrefetchScalarGridSpec(
            num_scalar_prefetch=1, grid=(B, nb + ncb), in_specs=in_specs,
            out_specs=pl.BlockSpec((BLOCK, 512), lambda b, n, s: (q_block(b, n), 0))),
        out_shape=jax.ShapeDtypeStruct((rows.T, 512), BF16),
        compiler_params=_cp(("parallel", "parallel")), name="window_attn",
    )(sink, *([qkv] * 9))


MLA_SCALE = (MLA_NOPE + MLA_ROPE) ** -0.5 * LOG2E


def _mla_q_epilogue(acc, cos_ref, sin_ref):
    cos, sin = cos_ref[...], sin_ref[...]
    slabs = [acc[:, h * LANE:(h + 1) * LANE] for h in range(MLA_HEADS)]
    return jnp.concatenate([(s * cos + _rot_pairs(s, 8) * sin) * MLA_SCALE for s in slabs], axis=1)


def _mla_k_epilogue(acc, kr_ref, cos_ref, sin_ref):
    kr = kr_ref[...]
    kr = kr * cos_ref[...] + _rot_pairs(kr, 8) * sin_ref[...]
    return jnp.concatenate([acc[:, h * LANE:(h + 1) * LANE] + kr for h in range(MLA_HEADS)], axis=1)


def _mla_kernel(q_ref, kl_ref, kx_ref, vl_ref, vx_ref, o_ref, *, nq):
    def run(with_latent_keys):
        outs = []
        for j in range(MLA_HP):
            sl, vs = slice(j * LANE, (j + 1) * LANE), slice(j * 64, (j + 1) * 64)
            q = q_ref[:, sl]
            scores, values = [_nt(q, kx_ref[:, sl])], [vx_ref[:, vs]]
            if with_latent_keys:
                scores.insert(0, _nt(q, kl_ref[:, sl]))
                values.insert(0, vl_ref[:, vs])
            outs.append(_softmax_av(scores, values))
        o_ref[...] = jnp.concatenate(outs, axis=1).astype(o_ref.dtype)

    i = pl.program_id(2)
    pl.when(i < nq)(lambda: run(True))
    pl.when(i >= nq)(lambda: run(False))


def mla_attn(rows, qm, km, vm, tq=256):
    B, S, NC = rows.B, rows.S, rows.NC
    cb = B * S // NC
    nq, nqc = S // tq, NC // tq
    kw, vw = MLA_HP * LANE, MLA_HP * 64

    def q_block(b, i):
        return jnp.where(i < nq, b * nq + i, B * nq + b * nqc + (i - nq))

    return pl.pallas_call(
        functools.partial(_mla_kernel, nq=nq), grid=(B, MLA_HEADS // MLA_HP, nq + nqc),
        in_specs=[pl.BlockSpec((tq, kw), lambda b, h, i: (q_block(b, i), h)),
                  pl.BlockSpec((S, kw), lambda b, h, i: (b, h)),
                  pl.BlockSpec((NC, kw), lambda b, h, i: (cb + b, h)),
                  pl.BlockSpec((S, vw), lambda b, h, i: (b, h)),
                  pl.BlockSpec((NC, vw), lambda b, h, i: (cb + b, h))],
        out_specs=pl.BlockSpec((tq, vw), lambda b, h, i: (q_block(b, i), h)),
        out_shape=jax.ShapeDtypeStruct((rows.T, MLA_HEADS * 64), BF16),
        compiler_params=_cp(("parallel", "parallel", "parallel")), name="mla_attn",
    )(qm, km, km, vm, vm)


def _mla_norms_kernel(x0_ref, x1_ref, x2_ref, x3_ref, gq_ref, gkv_ref, q_ref, kv_ref, kr_ref):
    x = jnp.concatenate([x0_ref[...], x1_ref[...], x2_ref[...], x3_ref[...]], axis=1)
    q_ref[...] = _rms(x[:, 16:528], gq_ref[...]).astype(q_ref.dtype)
    kv_ref[...] = _rms(x[:, 528:784], gkv_ref[...]).astype(kv_ref.dtype)
    z = lambda w: jnp.zeros((x.shape[0], w), F32)
    kr_ref[...] = jnp.concatenate([z(MLA_NOPE), x[:, 784:816], z(LANE - MLA_NOPE - MLA_ROPE)], axis=1)


def mla_norms(rows, p2, gq, gkv):
    T = rows.T
    return pl.pallas_call(
        _mla_norms_kernel, grid=(rows.nt,),
        in_specs=[pl.BlockSpec((ROW_TILE, 256), functools.partial(lambda j, i: (i, PB_GATES // 256 + j), j))
                  for j in range(4)] + [_vec_spec(512), _vec_spec(256)],
        out_specs=[_row_spec(512), _row_spec(256), _row_spec(LANE)],
        out_shape=[jax.ShapeDtypeStruct((T, 512), BF16), jax.ShapeDtypeStruct((T, 256), BF16),
                   jax.ShapeDtypeStruct((T, LANE), F32)],
        compiler_params=_cp(("parallel",)), name="mla_norms",
    )(p2, p2, p2, p2, gq, gkv)


def _bf16_pieces(x):
    hi = x.astype(BF16)
    r = x - hi.astype(F32)
    mid = r.astype(BF16)
    return hi, mid, (r - mid.astype(F32)).astype(BF16)


def _log_sigmoid(x):
    return jnp.minimum(x, 0.0) - jnp.log(1.0 + jnp.exp(-jnp.abs(x)))


def _mlstm_kernel(*refs, scale):
    nh = ML_HEADS
    per = 3 * nh + 1
    gb_ref, of_ref, ob_ref, c_ref, n_ref, m_ref = refs[2 * per:]

    @pl.when(pl.program_id(1) == 0)
    def _():
        c_ref[...] = jnp.zeros_like(c_ref)
        n_ref[...] = jnp.zeros_like(n_ref)
        m_ref[...] = jnp.zeros_like(m_ref)

    for d, o_ref in enumerate((of_ref, ob_ref)):
        r = refs[d * per:(d + 1) * per]
        _mlstm_chunk(r[0:nh], r[nh:2 * nh], r[2 * nh:3 * nh], r[3 * nh], gb_ref, o_ref,
                     c_ref.at[d], n_ref.at[d], m_ref.at[d], backward=d == 1, scale=scale)


def _mlstm_chunk(q_refs, k_refs, v_refs, g_ref, gb_ref, o_ref, c_ref, n_ref, m_ref, *, backward, scale):
    nh, L = ML_HEADS, ML_CHUNK
    t_i = lax.broadcasted_iota(jnp.int32, (L, L), 0)
    s_i = lax.broadcasted_iota(jnp.int32, (L, L), 1)
    mask = (s_i >= t_i) if backward else (s_i <= t_i)
    gb = g_ref[...] + gb_ref[...]
    ls = _log_sigmoid(gb)
    gbt, lst = gb.T, ls.T
    mask_b = mask.astype(BF16)
    cum_c = sum(jnp.dot(mask_b, p, preferred_element_type=F32) for p in _bf16_pieces(ls))
    cum_r = sum(jnp.dot(p, mask_b.T, preferred_element_type=F32) for p in _bf16_pieces(lst))
    last = 0 if backward else L - 1
    gi0, gf0 = (2 * nh, 3 * nh) if backward else (0, nh)
    for h in range(nh):
        gi, gf = gi0 + h, gf0 + h
        q = q_refs[h][...].astype(F32)
        kf = k_refs[h][...].astype(F32) * scale
        vf = v_refs[h][...].astype(F32)
        qb, kb = q.astype(BF16), kf.astype(BF16)
        cumc, cumr = cum_c[:, gf:gf + 1], cum_r[gf:gf + 1, :]
        li_r, li_c = gbt[gi:gi + 1, :], gb[:, gi:gi + 1]
        m11 = m_ref[h][0:1, 0:1]
        log_intra = jnp.where(mask, cumc - cumr + li_r, NEG)
        log_inter = cumc + m11
        m_t = jnp.maximum(log_inter, log_intra.max(axis=1, keepdims=True))
        w_inter = jnp.exp(log_inter - m_t)
        s = _nt(qb, kb) * jnp.exp(log_intra - m_t)
        cmat = c_ref[h]
        num = w_inter * _nt(qb, cmat.astype(BF16)) + jnp.dot(s.astype(BF16), vf.astype(BF16),
                                                             preferred_element_type=F32)
        den = w_inter * jnp.sum(q * n_ref[h][0:1, :], axis=1, keepdims=True) + s.sum(axis=1, keepdims=True)
        o_ref[:, h * L:(h + 1) * L] = num / jnp.maximum(jnp.abs(den), jnp.exp(-m_t))
        total = cumc[last:last + 1, :]
        log_w = total - cumc + li_c
        m_new = jnp.maximum(total + m11, log_w.max(axis=0, keepdims=True))
        decay = jnp.exp(total + m11 - m_new)
        w = jnp.exp(log_w - m_new)
        upd = lax.dot_general((vf * w).astype(BF16), kb, (((0,), (0,)), ((), ())), preferred_element_type=F32)
        c_ref[h] = decay * cmat + upd
        n_ref[h] = jnp.broadcast_to(decay * n_ref[h][0:1, :] + jnp.sum(w * kf, axis=0, keepdims=True), (8, L))
        m_ref[h] = jnp.broadcast_to(m_new, (8, LANE))


def mlstm_scan(rows, pa, pb, gbias):
    B, S, NC = rows.B, rows.S, rows.NC
    nh, L = ML_HEADS, ML_CHUNK
    ncx, ncl = NC // L, S // L
    first_ctx = B * S // L

    def row_block(backward, b, j):
        cx = (ncx - 1 - j) if backward else j
        cl = (ncl - 1 - (j - ncx)) if backward else (j - ncx)
        return jnp.where(j < ncx, first_ctx + b * ncx + cx, b * ncl + cl)

    def col_spec(backward, col):
        return pl.BlockSpec((L, LANE), lambda b, j: (row_block(backward, b, j), col))

    q0 = 2304 // LANE
    in_specs = []
    for backward in (False, True):
        in_specs += [col_spec(backward, q0 + c) for c in range(3 * nh)] + [col_spec(backward, PB_GATES // LANE)]
    in_specs.append(pl.BlockSpec((1, LANE), lambda b, j: (0, 0)))
    out = jax.ShapeDtypeStruct((rows.T, nh * L), F32)
    return pl.pallas_call(
        functools.partial(_mlstm_kernel, scale=L ** -0.5),
        grid=(B, ncx + ncl), in_specs=in_specs,
        out_specs=[pl.BlockSpec((L, nh * L), functools.partial(lambda bw, b, j: (row_block(bw, b, j), 0), bw))
                   for bw in (False, True)],
        out_shape=[out, out],
        scratch_shapes=[pltpu.VMEM((2, nh, L, L), F32), pltpu.VMEM((2, nh, 8, L), F32),
                        pltpu.VMEM((2, nh, 8, LANE), F32)],
        compiler_params=_cp(("parallel", "arbitrary")), name="mlstm_scan",
    )(*(([pa] * (3 * nh) + [pb]) * 2), gbias)


def _mlstm_finish_kernel(hf_ref, hb_ref, *refs):
    o_refs, g_ref, out_ref = refs[:ML_HEADS], refs[ML_HEADS], refs[ML_HEADS + 1]
    for h in range(ML_HEADS):
        sl = slice(h * ML_CHUNK, (h + 1) * ML_CHUNK)
        hn = _rms(hf_ref[:, sl] + hb_ref[:, sl], g_ref[:, sl])
        out_ref[:, sl] = (jax.nn.sigmoid(o_refs[h][...]) * hn).astype(out_ref.dtype)


def mlstm_finish(rows, hf, hb, p1, gnorm):
    W = ML_HEADS * ML_CHUNK
    o_specs = [pl.BlockSpec((ROW_TILE, LANE), functools.partial(lambda h, i: (i, h), h))
               for h in range(ML_HEADS)]
    return pl.pallas_call(
        _mlstm_finish_kernel, grid=(rows.nt,),
        in_specs=[_row_spec(W), _row_spec(W)] + o_specs + [_vec_spec(W)],
        out_specs=_row_spec(W),
        out_shape=jax.ShapeDtypeStruct((rows.T, W), BF16),
        compiler_params=_cp(("parallel",)), name="mlstm_finish",
    )(hf, hb, *([p1] * ML_HEADS), gnorm)


def _dft_gen_kernel(ca_ref, sa_ref, cb_ref, sb_ref, ca2_ref, sa2_ref, cb2_ref, sb2_ref, fwd_ref, inv_ref):
    tb, L = cb_ref.shape
    ca, sa, cb, sb = ca_ref[...], sa_ref[...], cb_ref[...], sb_ref[...]
    fwd_ref[0] = (ca * cb - sa * sb).astype(fwd_ref.dtype)
    fwd_ref[1] = (-(sa * cb + ca * sb)).astype(fwd_ref.dtype)
    ca, sa, cb, sb = ca2_ref[...], sa2_ref[...], cb2_ref[...], sb2_ref[...]
    c, s = (ca * cb - sa * sb).astype(inv_ref.dtype), (-(sa * cb + ca * sb)).astype(inv_ref.dtype)
    for j in range(L // tb):
        inv_ref[:, 2 * j * tb:(2 * j + 1) * tb] = c[:, j * tb:(j + 1) * tb]
        inv_ref[:, (2 * j + 1) * tb:(2 * j + 2) * tb] = s[:, j * tb:(j + 1) * tb]


def _dft_matrices(L, tb):
    nb = L // tb
    col = jnp.arange(L, dtype=jnp.int32)[None, :]
    r0 = (jnp.arange(nb, dtype=jnp.int32) * tb)[:, None]
    i = jnp.arange(tb, dtype=jnp.int32)[:, None]
    unit = math.pi / (2 * L)
    trig = lambda ph: (jnp.cos((ph % (4 * L)).astype(F32) * unit), jnp.sin((ph % (4 * L)).astype(F32) * unit))
    ca, sa = trig((2 * r0 + 1) * col)
    cb, sb = trig(2 * i * col)
    ca2, sa2 = trig((2 * col + 1) * r0)
    cb2, sb2 = trig((2 * col + 1) * i)
    blk = lambda: pl.BlockSpec((None, 1, L), lambda j: (j, 0, 0))
    shared = lambda: pl.BlockSpec((tb, L), lambda j: (0, 0))
    a3 = lambda a: a[:, None, :]
    fwd, inv = pl.pallas_call(
        _dft_gen_kernel, grid=(nb,),
        in_specs=[blk(), blk(), shared(), shared(), blk(), blk(), shared(), shared()],
        out_specs=[pl.BlockSpec((None, 2, tb, L), lambda j: (j, 0, 0, 0)),
                   pl.BlockSpec((tb, 2 * L), lambda j: (j, 0))],
        out_shape=[jax.ShapeDtypeStruct((nb, 2, tb, L), BF16), jax.ShapeDtypeStruct((L, 2 * L), BF16)],
        compiler_params=_cp(("parallel",)), name="hy_dft_gen",
    )(a3(ca), a3(sa), cb, sb, a3(ca2), a3(sa2), cb2, sb2)
    return fwd.reshape(2 * L, L), inv


def _filter_feats(L):
    t = jnp.arange(L, dtype=F32) / L
    kf = jnp.arange(1, HY_POS_FREQS + 1, dtype=F32)
    ang = 2.0 * math.pi * t[:, None] * kf
    feats = jnp.concatenate([t[:, None], jnp.sin(ang), jnp.cos(ang)], axis=-1)
    return jnp.pad(feats, ((0, 0), (0, LANE - feats.shape[1])))


CG = 256


def _short_conv_eo_kernel(p_ref, w_ref, o_ref, scr_ref):
    x = p_ref[...].astype(F32)
    L = x.shape[0]
    row = lax.broadcasted_iota(jnp.int32, x.shape, 0)
    prev = jnp.where(row == 0, 0.0, pltpu.roll(x, 1, 0))
    nxt = jnp.where(row == L - 1, 0.0, pltpu.roll(x, L - 1, 0))
    u = w_ref[0:1, :] * prev + w_ref[1:2, :] * x + w_ref[2:3, :] * nxt
    for j in range(CG // LANE):
        scr_ref[j] = u[:, j * LANE:(j + 1) * LANE]
        for par in range(2):
            o_ref[:, par * CG + j * LANE:par * CG + (j + 1) * LANE] = scr_ref[j, pl.ds(par, L // 2, stride=2), :]


def short_conv_eo(pa, short_w, L, row_block0, B):
    ng = 512 // CG
    return pl.pallas_call(
        _short_conv_eo_kernel, grid=(B, 3, ng),
        in_specs=[pl.BlockSpec((L, CG), lambda b, w, g: (row_block0 + b, 768 // CG + w * ng + g)),
                  pl.BlockSpec((HY_SHORT, CG), lambda b, w, g: (0, w * ng + g))],
        out_specs=pl.BlockSpec((None, L // 2, 2 * CG), lambda b, w, g: (w, 0, b * ng + g)),
        out_shape=jax.ShapeDtypeStruct((3, L // 2, B * 1024), F32),
        scratch_shapes=[pltpu.VMEM((CG // LANE, L, LANE), F32)],
        compiler_params=_cp(("parallel", "parallel", "parallel")), name="hy_short_conv",
    )(pa, short_w)


def _filter_eo_kernel(feat_ref, w1_ref, b1_ref, w2_ref, b2_ref, fr_ref, w3f_ref, w3b_ref, dl_ref, o_ref, hid_ref,
                      sf_ref, sb_ref):
    L = feat_ref.shape[0]
    H = L // 2

    @pl.when((pl.program_id(0) == 0) & (pl.program_id(1) == 0))
    def _():
        fr = fr_ref[...]
        h1 = jnp.sin(fr * (jnp.dot(feat_ref[...], w1_ref[...], preferred_element_type=F32, precision=HI)
                           + b1_ref[...]))
        hid_ref[...] = jnp.sin(fr * (jnp.dot(h1, w2_ref[...], preferred_element_type=F32, precision=HI)
                                     + b2_ref[...]))

    hid = hid_ref[...]
    t = lax.broadcasted_iota(jnp.int32, (L, 1), 0).astype(F32) / L
    dec = jnp.exp(-t * dl_ref[...])
    hf = jnp.dot(hid, w3f_ref[...], preferred_element_type=F32, precision=HI) * dec
    hb = jnp.dot(hid, w3b_ref[...], preferred_element_type=F32, precision=HI) * dec
    hb = jnp.where(lax.broadcasted_iota(jnp.int32, hb.shape, 0) == 0, 0.0, hb)
    inv = 1.0 / (jnp.sum(jnp.abs(hf), axis=0, keepdims=True) + jnp.sum(jnp.abs(hb), axis=0, keepdims=True))
    for part, (s_ref, hv) in enumerate(((sf_ref, hf * inv), (sb_ref, hb * inv))):
        for j in range(CG // LANE):
            s_ref[j] = hv[:, j * LANE:(j + 1) * LANE]
            for par in range(2):
                c0 = (part * 2 + par) * CG + j * LANE
                o_ref[:, c0:c0 + LANE] = s_ref[j, pl.ds(par, H, stride=2), :].astype(o_ref.dtype)


def hyena_filters_eo(feats, w1p, b1, w2, b2, fr, w3, deltas, L):
    hid = w2.shape[0]
    ng = 512 // CG
    full = lambda shape: pl.BlockSpec(shape, lambda o, g: (0, 0))
    return pl.pallas_call(
        _filter_eo_kernel, grid=(HY_ORDER, ng),
        in_specs=[full((L, LANE)), full((LANE, hid)), full((1, hid)), full((hid, hid)), full((1, hid)),
                  full((1, hid)),
                  pl.BlockSpec((hid, CG), lambda o, g: (0, (o * 2) * ng + g)),
                  pl.BlockSpec((hid, CG), lambda o, g: (0, (o * 2 + 1) * ng + g)),
                  pl.BlockSpec((1, CG), lambda o, g: (0, g))],
        out_specs=pl.BlockSpec((None, L // 2, 4 * CG), lambda o, g: (0, 0, o * ng + g)),
        out_shape=jax.ShapeDtypeStruct((1, L // 2, HY_ORDER * ng * 4 * CG), BF16),
        scratch_shapes=[pltpu.VMEM((L, hid), F32), pltpu.VMEM((CG // LANE, L, LANE), F32),
                        pltpu.VMEM((CG // LANE, L, LANE), F32)],
        compiler_params=_cp(("arbitrary", "arbitrary")), name="hy_filters",
    )(feats, w1p, b1, w2, b2, fr, w3, w3, deltas)


def _split_eo(acc, c_ref, s_ref):
    tb = acc.shape[0] // 2
    c, s = c_ref[...], s_ref[...]
    ere, eim, ore, oim = acc[:tb, :CG], acc[tb:, :CG], acc[:tb, CG:], acc[tb:, CG:]
    tre, tim = c * ore + s * oim, c * oim - s * ore
    return (ere + tre, eim + tim), (ere - tre, tim - eim)


def _filter_spec_epilogue(acc, c_ref, s_ref):
    res, ims = [], []
    for j in range(acc.shape[1] // (2 * CG)):
        (lre, lim), (hre, him) = _split_eo(acc[:, j * 2 * CG:(j + 1) * 2 * CG], c_ref, s_ref)
        res += [lre, hre]
        ims += [lim, him]
    return jnp.concatenate([jnp.concatenate(res, axis=1), jnp.concatenate(ims, axis=1)], axis=0)


def _cmul(a, b):
    return a[0] * b[0] - a[1] * b[1], a[0] * b[1] + a[1] * b[0]


def _signal_spec_epilogue(acc, c_ref, s_ref, gre_ref, gim_ref, *, scale):
    c, s = c_ref[...], s_ref[...]
    res, ims = [], []
    for g in range(acc.shape[1] // (2 * CG)):
        x_lo, x_hi = _split_eo(acc[:, g * 2 * CG:(g + 1) * 2 * CG], c_ref, s_ref)
        f = lambda ref, j: ref[:, (g * 4 + j) * CG:(g * 4 + j + 1) * CG]
        g_lo = (f(gre_ref, 0) + f(gre_ref, 2), f(gim_ref, 0) - f(gim_ref, 2))
        g_hi = (f(gre_ref, 1) + f(gre_ref, 3), f(gim_ref, 1) - f(gim_ref, 3))
        p_lo, p_hi = _cmul(x_lo, g_lo), _cmul(x_hi, g_hi)
        dre, dim = p_lo[0] - p_hi[0], p_lo[1] + p_hi[1]
        res += [p_lo[0] + p_hi[0], dre * c - dim * s]
        ims += [p_lo[1] - p_hi[1], dre * s + dim * c]
    return jnp.concatenate([jnp.concatenate(res, axis=1), jnp.concatenate(ims, axis=1)], axis=0) * scale


def _gated(acc, x_ref, z_ref, b_ref):
    return x_ref[...] * (acc + b_ref[...] * z_ref[...])


def _gate_eo_kernel(x_ref, c_ref, z_ref, b_ref, *refs, nl):
    o_ref, scr_ref = refs[-2:]
    i = pl.program_id(0)

    @pl.when(i < nl)
    def _():
        r = x_ref[...] * (c_ref[...] + b_ref[...] * z_ref[...])
        half = r.shape[0]
        for g in range(512 // CG):
            for j in range(CG // LANE):
                slab = g * (CG // LANE) + j
                for par in range(2):
                    c0 = (g * 2 + par) * CG + j * LANE
                    scr_ref[slab, pl.ds(par, half, stride=2), :] = r[:, c0:c0 + LANE]
                o_ref[:, slab * LANE:(slab + 1) * LANE] = scr_ref[slab].astype(o_ref.dtype)

    @pl.when(i >= nl)
    def _():
        o_ref[...] = jnp.zeros_like(o_ref)


def hyena_gate_eo(u3, conv, z1, bias_u, L, B, tl, T, row0, into):
    nl = L // tl
    th = tl // 2
    ci = lambda i: jnp.minimum(i, nl - 1)
    in_specs = [pl.BlockSpec((None, th, 1024), lambda i, b: (2, ci(i), b)),
                pl.BlockSpec((th, 1024), lambda i, b: (ci(i), b)),
                pl.BlockSpec((None, th, 1024), lambda i, b: (0, ci(i), b)),
                pl.BlockSpec((None, 1, 1024), lambda i, b: (HY_ORDER - 1, 0, 0))]
    args = [u3, conv, z1, bias_u]
    if into is not None:
        n_fill, aliases = 0, {4: 0}
        in_specs.append(pl.BlockSpec(memory_space=pl.ANY))
        args.append(into)
        out_spec = pl.BlockSpec((tl, 512), lambda i, b: (row0 // tl + b * nl + i, 0))
    else:
        assert row0 == 0
        n_fill, aliases = (T - B * L) // tl, {}
        out_spec = pl.BlockSpec((tl, 512), lambda i, b: (jnp.where(i < nl, b * nl + i, B * nl + (i - nl)), 0))
    return pl.pallas_call(
        functools.partial(_gate_eo_kernel, nl=nl), grid=(nl + n_fill, B), in_specs=in_specs, out_specs=out_spec,
        out_shape=jax.ShapeDtypeStruct((T, 512), BF16), input_output_aliases=aliases,
        scratch_shapes=[pltpu.VMEM((512 // LANE, tl, LANE), F32)],
        compiler_params=_cp(("arbitrary", "arbitrary")), name="hy_gate",
    )(*args)


def _hyena_consts(L):
    H = L // 2
    tb = min(H, ROW_TILE)
    fwd, inv = _dft_matrices(H, tb)
    w = (2 * jnp.arange(H, dtype=F32)[:, None] + 1.0) * (math.pi / (2 * L))
    twc, tws = jnp.broadcast_to(jnp.cos(w), (H, CG)), jnp.broadcast_to(jnp.sin(w), (H, CG))
    return fwd, inv, twc, tws, _filter_feats(L)


def hyena_seq_eo(pa, L, row_block0, B, consts, wts, T, into=None):
    fwd, inv, twc, tws, feats = consts
    short_w, w1p, b1, w2, b2, fr, w3, deltas, bias_u = wts
    H = L // 2
    tb = min(H, ROW_TILE)
    nkb = H // tb
    tm_i = min(H, 512)
    ng = 512 // CG
    NW = B * 1024
    u3 = short_conv_eo(pa, short_w, L, row_block0, B)
    filt = hyena_filters_eo(feats, w1p, b1, w2, b2, fr, w3, deltas, L)
    zeros_f = jnp.zeros((nkb,), jnp.int32)
    zeros_i = jnp.zeros((H // tm_i,), jnp.int32)
    tw = [(twc, (tb, CG), lambda n, m: (m, 0)), (tws, (tb, CG), lambda n, m: (m, 0))]
    gw = ng * 4 * CG
    gs = gmm(fwd, filt, zeros_f, tm=2 * tb, tn=gw, n_tiles=HY_ORDER, out_dtype=F32,
             epilogue=_filter_spec_epilogue, extra=tw, name="hy_dft_filter")
    z_arr, which_z = u3, 0
    out = None
    for o in range(HY_ORDER):
        g_blk = lambda comp, o=o: (gs, (tb, gw), lambda n, m: (2 * m + comp, o))
        y = gmm(fwd, z_arr, zeros_f + which_z, tm=2 * tb, tn=ng * 2 * CG, n_tiles=B, out_dtype=BF16,
                epilogue=functools.partial(_signal_spec_epilogue, scale=1.0 / L),
                extra=tw + [g_blk(0), g_blk(1)], name="hy_dft_fwd")
        y = y.reshape(1, 2 * H, NW)
        if o < HY_ORDER - 1:
            out = gmm(inv, y, zeros_i, tm=tm_i, tn=2 * CG, n_tiles=B * ng, out_dtype=F32, epilogue=_gated,
                      extra=[(u3, (None, tm_i, 2 * CG), functools.partial(lambda o, n, m: (1 + o, m, n), o)),
                             (z_arr, (None, tm_i, 2 * CG), functools.partial(lambda w, n, m: (w, m, n), which_z)),
                             (bias_u, (None, 1, 2 * CG), functools.partial(lambda o, n, m: (o, 0, n % ng), o))],
                      name="hy_dft_inv_gate").reshape(1, H, NW)
        else:
            conv = gmm(inv, y, zeros_i, tm=tm_i, tn=2 * CG, n_tiles=B * ng, out_dtype=F32, name="hy_dft_inv")
            out = hyena_gate_eo(u3, conv, z_arr, bias_u, L, B, min(L, 512), T, row_block0 * L, into)
        z_arr, which_z = out, 0
    return out


def _gather_kernel(idx_ref, src_ref, *refs, tm, k):
    if k > 1:
        w_ref, o_ref, buf_ref, sem = refs
    else:
        o_ref, buf_ref, sem = refs
    i, n = pl.program_id(0), pl.num_programs(0)
    M = n * tm

    def issue_tile(tile, slot):
        for j in range(k):
            def issue(r2, carry):
                for pr in range(2):
                    r = 2 * r2 + pr
                    pltpu.make_async_copy(src_ref.at[pl.ds(idx_ref[j * M + tile * tm + r], 1)],
                                          buf_ref.at[slot, j, pl.ds(r, 1)], sem.at[slot]).start(priority=pr)
                return carry

            lax.fori_loop(0, tm // 2, issue, 0, unroll=4)

    @pl.when(i == 0)
    def _():
        issue_tile(0, 0)

    @pl.when(i + 1 < n)
    def _():
        issue_tile(i + 1, (i + 1) % 2)

    slot = i % 2
    for j in range(k):
        pltpu.make_async_copy(src_ref.at[pl.ds(0, tm)], buf_ref.at[slot, j], sem.at[slot]).wait()
    if k > 1:
        w = w_ref[...]
        acc = w[:, 0:1] * buf_ref[slot, 0]
        for j in range(1, k):
            acc = acc + w[:, j:j + 1] * buf_ref[slot, j]
        o_ref[...] = acc.astype(o_ref.dtype)
    else:
        o_ref[...] = buf_ref[slot, 0].astype(o_ref.dtype)


def gather_rows(src, idx, tm, out_dtype, weights=None, k=1):
    M = idx.shape[0] // k
    D = src.shape[1]
    in_specs = [pl.BlockSpec(memory_space=pl.ANY)]
    args = [src]
    if k > 1:
        in_specs.append(pl.BlockSpec((tm, LANE), lambda i, idx: (i, 0)))
        args.append(weights)
    return pl.pallas_call(
        functools.partial(_gather_kernel, tm=tm, k=k),
        grid_spec=pltpu.PrefetchScalarGridSpec(
            num_scalar_prefetch=1, grid=(M // tm,), in_specs=in_specs,
            out_specs=pl.BlockSpec((tm, D), lambda i, idx: (i, 0)),
            scratch_shapes=[pltpu.VMEM((2, k, tm, D), src.dtype), pltpu.SemaphoreType.DMA((2,))]),
        out_shape=jax.ShapeDtypeStruct((M, D), out_dtype),
        compiler_params=_cp(("arbitrary",)), name="gather_rows" if k == 1 else "gather_combine",
    )(idx, *args)


def moe_ffn(rows, h_f32, rw, ri, w_gate, w_up, w_down, layer_moe, tm=512):
    T, D = rows.T, rows.D
    E = w_gate.shape[2]
    e_flat = ri[:, :TOP_K].T.reshape(-1)
    onehot = (e_flat[:, None] == jnp.arange(N_EXPERTS, dtype=jnp.int32)[None, :]).astype(jnp.int32)
    csum = jnp.cumsum(onehot, axis=0)
    rank = jnp.sum(csum * onehot, axis=1) - 1
    counts = csum[-1]
    padded = ((counts + tm - 1) // tm) * tm
    ends = jnp.cumsum(padded)
    starts = ends - padded
    dest = starts[e_flat] + rank
    m_pad = TOP_K * T + N_EXPERTS * tm
    token = jnp.tile(jnp.arange(T, dtype=jnp.int32), TOP_K)
    src_tok = (jnp.arange(m_pad, dtype=jnp.int32) % T).at[dest].set(token)
    tile_row = jnp.arange(m_pad // tm, dtype=jnp.int32) * tm
    gid = jnp.minimum(jnp.sum((tile_row[:, None] >= ends[None, :]).astype(jnp.int32), axis=1), N_EXPERTS - 1)
    gid = gid + layer_moe * N_EXPERTS
    used = (ends[-1:] // tm).astype(jnp.int32)

    xs = gather_rows(h_f32, src_tok, tm, BF16)
    tn_h = 1024
    hh = gmm(xs, [w_gate, w_up], gid, tm=tm, tn=tn_h, n_tiles=-(-E // tn_h), out_dtype=BF16, epilogue=_swiglu,
             used=used, name="moe_gate_up")
    y = gmm(hh, w_down, gid, tm=tm, tn=D // 2, n_tiles=2, out_dtype=F32, used=used, name="moe_down")
    return gather_rows(y, dest, ROW_TILE, BF16, weights=rw, k=TOP_K)


def kernel(x, c, ctx, c_ctx, w_mod, b_mod, g_mix_pre, g_mix_post, g_ffn_pre, g_ffn_post, w_in, w_out, attn_sink,
           hy_short, hy_w1, hy_b1, hy_w2, hy_b2, hy_freq, hy_w3, hy_bias, ml_gate_bias, ml_norm, mla_q_norm,
           mla_w_uq, mla_kv_norm, mla_w_ukv, ffn_w_gate, ffn_w_up, ffn_w_down, moe_router, moe_w_gate, moe_w_up,
           moe_w_down):
    B, S, D = x.shape
    NC = ctx.shape[1]
    depth = w_mod.shape[0]
    rows = _Rows(B, S, NC, D)
    T = rows.T
    TM = 512 if T % 512 == 0 else ROW_TILE
    n_mt = T // TM
    ffn_dim = ffn_w_gate.shape[2]

    xs = jnp.concatenate([x.reshape(B * S, D), ctx.reshape(B * NC, D)], axis=0)

    cm = jnp.concatenate([c, c_ctx[None, :]], axis=0)
    cm = jnp.pad(jax.nn.silu(cm), ((0, 16 - (B + 1)), (0, 0))).astype(BF16)
    mod_all = gmm(jnp.tile(cm, (depth, 1)), w_mod, jnp.arange(depth, dtype=jnp.int32), tm=16, tn=1536,
                  n_tiles=N_MOD * D // 1536, out_dtype=F32, name="adaln")
    mods = (mod_all.reshape(depth, 16, N_MOD * D)[:, :B + 1] + b_mod[:, None, :]).reshape(depth, B + 1, N_MOD, 1, D)

    def rope_table(rot_dim, reps, lane0, width):
        cos, sin = _axial_tables(S, rot_dim)
        cos, sin = jnp.tile(cos, (1, reps)), jnp.tile(sin, (1, reps))
        padw = ((0, 0), (lane0, width - lane0 - cos.shape[1]))
        cos = jnp.pad(cos, padw, constant_values=1.0)
        sin = jnp.pad(sin, padw)
        cos = jnp.concatenate([jnp.tile(cos, (B, 1)), jnp.ones((B * NC, width), F32)], axis=0)
        sin = jnp.concatenate([jnp.tile(sin, (B, 1)), jnp.zeros((B * NC, width), F32)], axis=0)
        return cos, sin

    cos_a, sin_a = rope_table(64, 2, 0, LANE)
    cos_m, sin_m = rope_table(MLA_ROPE, 1, MLA_NOPE, LANE)
    hy_consts = {L: _hyena_consts(L) for L in (S, NC)}
    deltas = jnp.abs(jnp.linspace(math.log(HY_DECAY_TARGET) / HY_SLOW_DECAY,
                                  math.log(HY_DECAY_TARGET) / HY_FAST_DECAY, 512, dtype=F32))[None, :]

    w_in_t = jnp.swapaxes(w_in, 1, 2)
    wq = mla_w_uq.reshape(depth, -1, MLA_HEADS, MLA_NOPE + MLA_ROPE)
    wq = jnp.pad(wq, ((0, 0), (0, 0), (0, 0), (0, LANE - MLA_NOPE - MLA_ROPE))).reshape(depth, -1, MLA_HEADS * LANE)
    wkv = mla_w_ukv.reshape(depth, -1, MLA_HEADS, 2 * 64)
    wk = jnp.pad(wkv[..., :MLA_NOPE], ((0, 0), (0, 0), (0, 0), (0, LANE - MLA_NOPE)))
    wk = wk.reshape(depth, -1, MLA_HEADS * LANE)
    wv = wkv[..., MLA_NOPE:].reshape(depth, -1, MLA_HEADS * 64)
    w1p = jnp.pad(hy_w1, ((0, 0), (0, LANE - hy_w1.shape[1]), (0, 0)))
    bias_u = jnp.broadcast_to(hy_bias.reshape(depth, HY_ORDER, 512 // CG, 1, CG),
                              (depth, HY_ORDER, 512 // CG, 2, CG)).reshape(depth, HY_ORDER, 1, 1024)
    gbias = jnp.pad(ml_gate_bias.reshape(depth, 1, 4 * ML_HEADS), ((0, 0), (0, 0), (0, LANE - 4 * ML_HEADS)))
    router = jnp.pad(moe_router, ((0, 0), (0, 0), (0, LANE - N_EXPERTS)))
    n_moe = moe_w_gate.shape[0]
    mw_gate = moe_w_gate.reshape(n_moe * N_EXPERTS, D, -1)
    mw_up = moe_w_up.reshape(n_moe * N_EXPERTS, D, -1)
    mw_down = moe_w_down.reshape(n_moe * N_EXPERTS, -1, D)
    vec = lambda a, l: a[l][None, :]

    h = norm_mod(rows, xs, vec(g_mix_pre, 0), mods[0], 0, 1)
    for layer in range(depth):
        gid = jnp.full((n_mt,), layer, jnp.int32)
        ml = mods[layer]
        p1 = gmm(h, w_in_t, gid, tm=TM, tn=1280, n_tiles=PA_COLS // 1280, out_dtype=BF16, w_t=True, name="proj_in")
        p2 = gmm(h, w_in_t, gid, tm=TM, tn=768, n_tiles=2, n_off=PA_COLS // 768, out_dtype=F32, w_t=True,
                 name="proj_in_f32")

        qkv = attn_rope(rows, p1, cos_a, sin_a)
        a_all = window_attn(rows, qkv, attn_sink[layer])

        hy_w = (hy_short[layer], w1p[layer], vec(hy_b1, layer), hy_w2[layer], vec(hy_b2, layer),
                vec(hy_freq, layer), hy_w3[layer], deltas, bias_u[layer])
        b_l = hyena_seq_eo(p1, S, 0, B, hy_consts[S], hy_w, T)
        b_all = hyena_seq_eo(p1, NC, B * S // NC, B, hy_consts[NC], hy_w, T, into=b_l)

        hf, hb = mlstm_scan(rows, p1, p2, gbias[layer])
        m_all = mlstm_finish(rows, hf, hb, p2, vec(ml_norm, layer))

        qn, kvn, kr = mla_norms(rows, p2, vec(mla_q_norm, layer), vec(mla_kv_norm, layer))
        slab = lambda a: (a, (TM, LANE), lambda n, m: (m, 0))
        qm = gmm(qn, wq, gid, tm=TM, tn=MLA_HEADS * LANE, n_tiles=1, out_dtype=BF16, epilogue=_mla_q_epilogue,
                 extra=[slab(cos_m), slab(sin_m)], name="mla_uq")
        km = gmm(kvn, wk, gid, tm=TM, tn=MLA_HEADS * LANE, n_tiles=1, out_dtype=BF16, epilogue=_mla_k_epilogue,
                 extra=[slab(kr), slab(cos_m), slab(sin_m)], name="mla_uk")
        vm = gmm(kvn, wv, gid, tm=TM, tn=MLA_HEADS * 64, n_tiles=1, out_dtype=BF16, name="mla_uv")
        d_all = mla_attn(rows, qm, km, vm)

        y = gmm([a_all, b_all, m_all, d_all], w_out, gid, tm=TM, tn=1024, n_tiles=D // 1024, out_dtype=BF16,
                name="proj_out")

        i = layer // 2
        nxt = (vec(g_ffn_pre, layer), ml, 3, 4)
        if layer % 2 == 0:
            xs, h2 = post(rows, xs, y, vec(g_mix_post, layer), ml, 2, nxt=nxt)
            gi = jnp.full((n_mt,), i, jnp.int32)
            g = gmm(h2, ffn_w_gate, gi, tm=TM, tn=ffn_dim // 2, n_tiles=2, out_dtype=BF16, epilogue=_silu,
                    w_single_buffer=True, name="ffn_gate")
            hh = gmm(h2, ffn_w_up, gi, tm=TM, tn=ffn_dim // 2, n_tiles=2, out_dtype=BF16, epilogue=_times,
                     extra=[(g, (TM, ffn_dim // 2), lambda n, m: (m, n))], w_single_buffer=True, name="ffn_up")
            y2 = gmm(hh, ffn_w_down, gi, tm=TM, tn=512, n_tiles=D // 512, out_dtype=BF16, name="ffn_down")
        else:
            xs, h2, h2f, rw, ri = post(rows, xs, y, vec(g_mix_post, layer), ml, 2, nxt=nxt, router=router[i])
            y2 = moe_ffn(rows, h2f, rw, ri, mw_gate, mw_up, mw_down, i)
        if layer + 1 < depth:
            xs, h = post(rows, xs, y2, vec(g_ffn_post, layer), ml, 5,
                         nxt=(vec(g_mix_pre, layer + 1), mods[layer + 1], 0, 1))
        else:
            (xs,) = post(rows, xs, y2, vec(g_ffn_post, layer), ml, 5, latent_only=True)
    return xs.reshape(B, S, D)
```
